```python
import numpy as np
import jax, jax.numpy as jnp
from jax import lax


D_MODEL = 1024
BATCH = 32
SEQ = 2048
DEPTH = 4

N_MIXERS = 3
GRID_W = 64
PLE_DIM = 256
RMS_EPS = 1e-6
LN_EPS = 1e-5
NEG = -1e30
HEAD_DIM = 64
MIX_WIDTH = 1024
A_WIDTH = 1024
A_GROUPS = 8
A_CHUNK = 128
B_HEADS = 16
B_KV_HEADS = 4
B_WINDOW = 128
B_BLOCK = 128
C_HEADS = 16
C_WIN_ROWS = 8
C_WIN_COLS = 16
C_QCOLS = 16
C_KCOLS = C_QCOLS + C_WIN_COLS
N_EXPERTS = 16
EXPERT_FF = 1024
EC_CAPACITY_FACTOR = 2
N_A = len(range(0, DEPTH, N_MIXERS))
N_B = len(range(1, DEPTH, N_MIXERS))
N_C = len(range(2, DEPTH, N_MIXERS))

kernel_name = 'hybrid_gmlp_swa_natten_ec_encoder'


def rms_norm(x, g):
    xf = x.astype(jnp.float32)
    y = xf * lax.rsqrt(jnp.mean(xf * xf, axis=-1, keepdims=True) + RMS_EPS)
    return (y * g.astype(jnp.float32)).astype(x.dtype)


def layer_norm(x, g):
    xf = x.astype(jnp.float32)
    mu = jnp.mean(xf, axis=-1, keepdims=True)
    var = jnp.mean(jnp.square(xf - mu), axis=-1, keepdims=True)
    return ((xf - mu) * lax.rsqrt(var + LN_EPS) * g.astype(jnp.float32)).astype(x.dtype)


def mixer_a(xn, w_in, vnorm_g, w_s, b_s):
    b_, s_, _ = xn.shape
    z = jax.nn.gelu(xn @ w_in)
    u, v = jnp.split(z, 2, axis=-1)
    v = layer_norm(v, vnorm_g)
    v = v.reshape(b_, s_ // A_CHUNK, A_CHUNK, A_GROUPS, A_WIDTH // A_GROUPS)
    s = jnp.einsum('gts,bnsgc->bntgc', w_s, v) + b_s.T[:, :, None]
    return u * s.reshape(b_, s_, A_WIDTH)


def alibi_slopes(n):
    return np.array([2.0 ** (-8.0 * (h + 1) / n) for h in range(n)], dtype=np.float32)


def mixer_b(xn, w_in, qn_g, kn_g, sink):
    b_, s_, _ = xn.shape
    grp = B_HEADS // B_KV_HEADS
    qkv = xn @ w_in
    q, k, v = jnp.split(qkv, [B_HEADS * HEAD_DIM, (B_HEADS + B_KV_HEADS) * HEAD_DIM], axis=-1)
    q = rms_norm(q.reshape(b_, s_, B_KV_HEADS, grp, HEAD_DIM), qn_g)
    k = rms_norm(k.reshape(b_, s_, B_KV_HEADS, HEAD_DIM), kn_g)
    v = v.reshape(b_, s_, B_KV_HEADS, HEAD_DIM)
    pad = ((0, 0), (B_BLOCK, B_BLOCK), (0, 0), (0, 0))
    kp = jnp.pad(k, pad)
    vp = jnp.pad(v, pad)
    span = 3 * B_BLOCK
    rel = np.arange(span)[None, :] - B_BLOCK - np.arange(B_BLOCK)[:, None]
    in_window = jnp.asarray(np.abs(rel) <= B_WINDOW)
    alibi = jnp.asarray((-alibi_slopes(B_HEADS).reshape(B_KV_HEADS, grp, 1, 1)
                         * np.abs(rel)[None, None]).astype(np.float32))
    sink_f = sink.astype(jnp.float32).reshape(B_KV_HEADS, grp)[None, :, :, None]
    scale = HEAD_DIM ** -0.5

    def block(n):
        t0 = n * B_BLOCK
        qb = lax.dynamic_slice_in_dim(q, t0, B_BLOCK, axis=1)
        kb = lax.dynamic_slice_in_dim(kp, t0, span, axis=1)
        vb = lax.dynamic_slice_in_dim(vp, t0, span, axis=1)
        s = jnp.einsum('bqhgd,bkhd->bhgqk', qb, kb,
                       preferred_element_type=jnp.float32) * scale + alibi
        kpos = t0 - B_BLOCK + jnp.arange(span)
        valid = in_window & ((kpos >= 0) & (kpos < s_))[None, :]
        s = jnp.where(valid, s, NEG)
        m = jnp.maximum(s.max(axis=-1), sink_f)
        pe = jnp.exp(s - m[..., None])
        denom = pe.sum(axis=-1) + jnp.exp(sink_f - m)
        o = jnp.einsum('bhgqk,bkhd->bqhgd', pe, vb.astype(jnp.float32))
        o = o / denom.transpose(0, 3, 1, 2)[..., None]
        return o.astype(xn.dtype)

    out = lax.map(block, jnp.arange(s_ // B_BLOCK))
    return out.transpose(1, 0, 2, 3, 4, 5).reshape(b_, s_, B_HEADS * HEAD_DIM)


def natten_tables(w):
    ncb = w // C_QCOLS
    c = np.arange(w).reshape(ncb, C_QCOLS)
    cs = np.clip(c - C_WIN_COLS // 2, 0, w - C_WIN_COLS)
    kb = np.clip(np.arange(ncb) * C_QCOLS - C_WIN_COLS // 2, 0, w - C_KCOLS)
    kcol = kb[:, None] + np.arange(C_KCOLS)[None, :]
    colmask = (kcol[:, None, :] >= cs[:, :, None]) & (kcol[:, None, :] < cs[:, :, None] + C_WIN_COLS)
    coloff = np.clip(kcol[:, None, :] - c[:, :, None] + C_WIN_COLS - 1, 0, 2 * C_WIN_COLS - 2)
    return kcol, colmask, coloff


def mixer_c(xn, w_in, qn_g, kn_g, rpb):
    b_, s_, _ = xn.shape
    rows = s_ // GRID_W
    kr_n = min(C_WIN_ROWS, rows)
    ncb = GRID_W // C_QCOLS
    qkv = xn @ w_in
    q, k, v = jnp.split(qkv, 3, axis=-1)
    q = rms_norm(q.reshape(b_, rows, GRID_W, C_HEADS, HEAD_DIM), qn_g)
    k = rms_norm(k.reshape(b_, rows, GRID_W, C_HEADS, HEAD_DIM), kn_g)
    v = v.reshape(b_, rows, GRID_W, C_HEADS, HEAD_DIM)
    kcol, colmask, coloff = natten_tables(GRID_W)
    colmask_j = jnp.asarray(colmask)[None, None, :, :, None, :]
    rpb_f = rpb.astype(jnp.float32)
    scale = HEAD_DIM ** -0.5

    def row(r):
        rs = jnp.clip(r - kr_n // 2, 0, rows - kr_n)
        qr = lax.dynamic_index_in_dim(q, r, axis=1, keepdims=False)
        qr = qr.reshape(b_, ncb, C_QCOLS, C_HEADS, HEAD_DIM)
        kr = lax.dynamic_slice_in_dim(k, rs, kr_n, axis=1)[:, :, kcol]
        vr = lax.dynamic_slice_in_dim(v, rs, kr_n, axis=1)[:, :, kcol]
        s = jnp.einsum('bnqhd,brnkhd->bhnqrk', qr, kr,
                       preferred_element_type=jnp.float32) * scale
        row_idx = rs + jnp.arange(kr_n) - r + C_WIN_ROWS - 1
        bias = rpb_f[:, row_idx][:, :, coloff]
        s = s + bias.transpose(0, 2, 3, 1, 4)[None]
        s = jnp.where(colmask_j, s, NEG)
        s = s.reshape(b_, C_HEADS, ncb, C_QCOLS, kr_n * C_KCOLS)
        pr = jax.nn.softmax(s, axis=-1).reshape(b_, C_HEADS, ncb, C_QCOLS, kr_n, C_KCOLS)
        o = jnp.einsum('bhnqrk,brnkhd->bnqhd', pr, vr.astype(jnp.float32))
        return o.reshape(b_, GRID_W, C_HEADS * HEAD_DIM).astype(xn.dtype)

    out = lax.map(row, jnp.arange(rows))
    return out.transpose(1, 0, 2, 3).reshape(b_, s_, C_HEADS * HEAD_DIM)


def expert_choice_ffn(xn, router_w, w_gate, w_up, w_down):
    b_, s_, _ = xn.shape
    cap = max(1, EC_CAPACITY_FACTOR * s_ // N_EXPERTS)
    aff = jax.nn.softmax((xn @ router_w).astype(jnp.float32), axis=-1)
    gates, idx = lax.top_k(aff.transpose(0, 2, 1), cap)
    bidx = jnp.arange(b_)[:, None, None]
    xs = xn[bidx, idx]
    hdn = jax.nn.silu(jnp.einsum('becd,edf->becf', xs, w_gate)) * jnp.einsum('becd,edf->becf', xs, w_up)
    y = jnp.einsum('becf,efd->becd', hdn, w_down) * gates[..., None].astype(xn.dtype)
    return jnp.zeros_like(xn).at[bidx, idx].add(y)


def setup_inputs(seed: int = 0) -> dict:
    key = jax.random.key(seed)
    ks = iter(jax.random.split(key, 32))
    f32 = jnp.float32
    nrm = lambda shape, s: jax.random.normal(next(ks), shape, f32) * s
    gain = lambda shape: 1.0 + nrm(shape, 0.05)
    d = D_MODEL
    return {
        'x': nrm((BATCH, SEQ, d), 1.0),
        'p': nrm((DEPTH, BATCH, SEQ, PLE_DIM), 1.0),
        'norm_mix_g': gain((DEPTH, d)),
        'norm_ffn_g': gain((DEPTH, d)),
        'w_out': nrm((DEPTH, MIX_WIDTH, d), MIX_WIDTH ** -0.5),
        'router_w': nrm((DEPTH, d, N_EXPERTS), d ** -0.5),
        'exp_w_gate': nrm((DEPTH, N_EXPERTS, d, EXPERT_FF), d ** -0.5),
        'exp_w_up': nrm((DEPTH, N_EXPERTS, d, EXPERT_FF), d ** -0.5),
        'exp_w_down': nrm((DEPTH, N_EXPERTS, EXPERT_FF, d), EXPERT_FF ** -0.5),
        'ple_norm_g': gain((DEPTH, d)),
        'ple_gate_w': nrm((DEPTH, d, d), d ** -0.5),
        'ple_proj_w': nrm((DEPTH, PLE_DIM, d), PLE_DIM ** -0.5),
        'a_w_in': nrm((N_A, d, 2 * A_WIDTH), d ** -0.5),
        'a_vnorm_g': gain((N_A, A_WIDTH)),
        'a_w_s': nrm((N_A, A_GROUPS, A_CHUNK, A_CHUNK), A_CHUNK ** -0.5),
        'a_b_s': 1.0 + nrm((N_A, A_GROUPS, A_CHUNK), 0.1),
        'b_w_in': nrm((N_B, d, (B_HEADS + 2 * B_KV_HEADS) * HEAD_DIM), d ** -0.5),
        'b_qnorm_g': gain((N_B, HEAD_DIM)),
        'b_knorm_g': gain((N_B, HEAD_DIM)),
        'b_sink': nrm((N_B, B_HEADS), 0.5),
        'c_w_in': nrm((N_C, d, 3 * C_HEADS * HEAD_DIM), d ** -0.5),
        'c_qnorm_g': gain((N_C, HEAD_DIM)),
        'c_knorm_g': gain((N_C, HEAD_DIM)),
        'c_rpb': nrm((N_C, C_HEADS, 2 * C_WIN_ROWS - 1, 2 * C_WIN_COLS - 1), 0.1),
    }


def reference(x, p, norm_mix_g, norm_ffn_g, w_out, router_w, exp_w_gate, exp_w_up, exp_w_down,
              ple_norm_g, ple_gate_w, ple_proj_w, a_w_in, a_vnorm_g, a_w_s, a_b_s,
              b_w_in, b_qnorm_g, b_knorm_g, b_sink, c_w_in, c_qnorm_g, c_knorm_g, c_rpb):
    h = x
    for i in range(DEPTH):
        kind = i % N_MIXERS
        j = i // N_MIXERS
        xn = rms_norm(h, norm_mix_g[i])
        if kind == 0:
            mix = mixer_a(xn, a_w_in[j], a_vnorm_g[j], a_w_s[j], a_b_s[j])
        elif kind == 1:
            mix = mixer_b(xn, b_w_in[j], b_qnorm_g[j], b_knorm_g[j], b_sink[j])
        else:
            mix = mixer_c(xn, c_w_in[j], c_qnorm_g[j], c_knorm_g[j], c_rpb[j])
        h = h + mix @ w_out[i]
        h = h + expert_choice_ffn(rms_norm(h, norm_ffn_g[i]), router_w[i],
                                  exp_w_gate[i], exp_w_up[i], exp_w_down[i])
        gate = jax.nn.sigmoid(rms_norm(h, ple_norm_g[i]) @ ple_gate_w[i])
        h = h + gate * (p[i] @ ple_proj_w[i])
    return h
```

```python
import functools

import numpy as np
import jax
import jax.numpy as jnp
from jax import lax
from jax.experimental import pallas as pl
from jax.experimental.pallas import tpu as pltpu

F32 = jnp.float32
BF16 = jnp.bfloat16

RMS_EPS = 1e-6
LN_EPS = 1e-5
NEG = -1e30
HEAD_DIM = 64
GRID_W = 64
A_GROUPS = 8
A_CHUNK = 128
B_HEADS = 16
B_KV_HEADS = 4
B_BLOCK = 128
C_HEADS = 16
C_WIN_ROWS = 8
C_WIN_COLS = 16
N_EXPERTS = 16
EC_CAPACITY_FACTOR = 2
N_MIXERS = 3

LANES = 128
MXU_DIM = 256
VMEM_LIMIT_BYTES = 60 * 1024 * 1024


def _cparams(*sem):
    return pltpu.CompilerParams(dimension_semantics=sem, vmem_limit_bytes=VMEM_LIMIT_BYTES)


def _rms(xf, g):
    return xf * lax.rsqrt(jnp.mean(xf * xf, axis=-1, keepdims=True) + RMS_EPS) * g


def _dot(a, b):
    return jnp.dot(a, b, preferred_element_type=F32)


def _dot_nt(a, b):
    return lax.dot_general(a, b, (((1,), (1,)), ((), ())), preferred_element_type=F32)


def _full_spec(shape):
    nd = len(shape)
    return pl.BlockSpec(shape, lambda *_: (0,) * nd)


def _mixer_a_kernel(h_ref, g_ref, win_ref, vg_ref, ws_ref, bias_ref, wout_ref, o_ref, mix_ref):
    tm = h_ref.shape[0]
    width = vg_ref.shape[1]
    gw = width // A_GROUPS
    h = h_ref[...]
    xn = _rms(h, g_ref[...]).astype(BF16)
    z = _dot(xn, win_ref[...])
    z = 0.5 * z * (1.0 + jnp.tanh(np.sqrt(2.0 / np.pi).astype(np.float32)
                                  * (z + 0.044715 * (z * z * z))))
    u = z[:, :width]
    v = z[:, width:]
    mu = jnp.mean(v, axis=-1, keepdims=True)
    vc = v - mu
    var = jnp.mean(vc * vc, axis=-1, keepdims=True)
    vn = (vc * lax.rsqrt(var + LN_EPS) * vg_ref[...]).astype(BF16)
    for c in range(tm // A_CHUNK):
        rows = slice(c * A_CHUNK, (c + 1) * A_CHUNK)
        for g in range(A_GROUPS):
            cols = slice(g * gw, (g + 1) * gw)
            s = _dot(ws_ref[g], vn[rows, cols]) + bias_ref[:, cols]
            mix_ref[rows, cols] = (u[rows, cols] * s).astype(BF16)
    o_ref[...] = h + _dot(mix_ref[...], wout_ref[...])


def _mixer_a(h2, g, w_in, vnorm_g, w_s, b_s, w_out, tm=256):
    t, d = h2.shape
    width = vnorm_g.shape[0]
    gw = width // A_GROUPS
    bias = jnp.repeat(b_s.T.astype(F32), gw, axis=1)
    return pl.pallas_call(
        _mixer_a_kernel,
        out_shape=jax.ShapeDtypeStruct((t, d), F32),
        grid=(t // tm,),
        in_specs=[
            pl.BlockSpec((tm, d), lambda i: (i, 0)),
            _full_spec((1, d)),
            _full_spec((d, 2 * width)),
            _full_spec((1, width)),
            _full_spec((A_GROUPS, A_CHUNK, A_CHUNK)),
            _full_spec((A_CHUNK, width)),
            _full_spec((width, d)),
        ],
        out_specs=pl.BlockSpec((tm, d), lambda i: (i, 0)),
        scratch_shapes=[pltpu.VMEM((tm, width), BF16)],
        compiler_params=_cparams("parallel"),
        name="mixer_a",
    )(h2, g.reshape(1, d), w_in.astype(BF16), vnorm_g.reshape(1, width), w_s.astype(BF16), bias,
      w_out.astype(BF16))


def _norm_proj_kernel(h_ref, g_ref, w_ref, hg_ref, bd_ref, o_ref, *, n_norm_cols):
    xn = _rms(h_ref[...], g_ref[...]).astype(BF16)
    acc = _dot(xn, w_ref[...])
    for j in range(n_norm_cols // MXU_DIM):
        cols = slice(j * MXU_DIM, (j + 1) * MXU_DIM)
        blk = acc[:, cols]
        ss = _dot((blk * blk).astype(BF16), bd_ref[...])
        o_ref[:, cols] = (blk * lax.rsqrt(ss * (1.0 / HEAD_DIM) + RMS_EPS) * hg_ref[:, cols]).astype(BF16)
    o_ref[:, n_norm_cols:] = acc[:, n_norm_cols:].astype(BF16)


def _norm_proj(h2, g, w, head_gain, n_norm_cols, tm=512):
    t, d = h2.shape
    n = w.shape[1]
    blockdiag = jnp.asarray(np.kron(np.eye(MXU_DIM // HEAD_DIM), np.ones((HEAD_DIM, HEAD_DIM))), BF16)
    return pl.pallas_call(
        functools.partial(_norm_proj_kernel, n_norm_cols=n_norm_cols),
        out_shape=jax.ShapeDtypeStruct((t, n), BF16),
        grid=(t // tm,),
        in_specs=[
            pl.BlockSpec((tm, d), lambda i: (i, 0)),
            _full_spec((1, d)),
            _full_spec((d, n)),
            _full_spec((1, n_norm_cols)),
            _full_spec((MXU_DIM, MXU_DIM)),
        ],
        out_specs=pl.BlockSpec((tm, n), lambda i: (i, 0)),
        compiler_params=_cparams("parallel"),
        name="norm_proj",
    )(h2, g.reshape(1, d), w.astype(BF16), head_gain.reshape(1, n_norm_cols).astype(F32), blockdiag)


def _proj_residual_kernel(a_ref, w_ref, h_ref, o_ref):
    o_ref[...] = h_ref[...] + _dot(a_ref[...], w_ref[...])


def _proj_residual(a2, w, h2, tm=512):
    t, d = h2.shape
    k = a2.shape[1]
    return pl.pallas_call(
        _proj_residual_kernel,
        out_shape=jax.ShapeDtypeStruct((t, d), F32),
        grid=(t // tm,),
        in_specs=[
            pl.BlockSpec((tm, k), lambda i: (i, 0)),
            _full_spec((k, d)),
            pl.BlockSpec((tm, d), lambda i: (i, 0)),
        ],
        out_specs=pl.BlockSpec((tm, d), lambda i: (i, 0)),
        compiler_params=_cparams("parallel"),
        name="proj_residual",
    )(a2, w.astype(BF16), h2)


def _alibi_slopes(n):
    return np.array([2.0 ** (-8.0 * (h + 1) / n) for h in range(n)], dtype=np.float32)


def _attn_b_tables():
    span = 3 * B_BLOCK
    rel = np.arange(span)[None, :] - B_BLOCK - np.arange(B_BLOCK)[:, None]
    in_window = np.abs(rel) <= B_BLOCK
    alibi = (-_alibi_slopes(B_HEADS)[:, None, None] * np.abs(rel)[None]).astype(np.float32)
    kblk = np.arange(span) // B_BLOCK
    tabs = []
    for kind in range(3):
        valid = in_window & ~((kind == 0) & (kblk == 0))[None, :] & ~((kind == 2) & (kblk == 2))[None, :]
        tabs.append(np.where(valid[None], alibi, np.float32(NEG)))
    return np.stack(tabs).astype(np.float32)


def _pair_split(x, lane, own_half):
    rolled = pltpu.roll(x, HEAD_DIM, axis=1)
    zero = jnp.zeros_like(x)
    if own_half == 0:
        return jnp.where(lane < HEAD_DIM, x, zero), jnp.where(lane >= HEAD_DIM, rolled, zero)
    return jnp.where(lane < HEAD_DIM, rolled, zero), jnp.where(lane >= HEAD_DIM, x, zero)


def _attn_b_kernel(sink_ref, q_ref, kp_ref, kc_ref, kn_ref, vp_ref, vc_ref, vn_ref, tab_ref, o_ref):
    span = 3 * B_BLOCK
    grp = B_HEADS // B_KV_HEADS
    lane = lax.broadcasted_iota(jnp.int32, (1, LANES), 1)
    for kh in range(B_KV_HEADS):
        tile = slice((kh // 2) * LANES, (kh // 2 + 1) * LANES)
        kt = jnp.concatenate([kp_ref[:, tile], kc_ref[:, tile], kn_ref[:, tile]], axis=0)
        vt = jnp.concatenate([vp_ref[:, tile], vc_ref[:, tile], vn_ref[:, tile]], axis=0)
        k_lo, k_hi = _pair_split(kt.astype(F32), lane, kh % 2)
        v_lo, v_hi = _pair_split(vt.astype(F32), lane, kh % 2)
        kk = jnp.concatenate([k_lo, k_hi], axis=0).astype(BF16)
        vv = jnp.concatenate([v_lo, v_hi], axis=0).astype(BF16)
        for half in range(grp // 2):
            qt = kh * (grp // 2) + half
            cols = slice(qt * LANES, (qt + 1) * LANES)
            s = _dot_nt(q_ref[:, cols], kk)
            ps, ds = [], []
            for ch in range(2):
                head = 2 * qt + ch
                sh = s[:, ch * span:(ch + 1) * span] + tab_ref[head]
                sink = sink_ref[head]
                m = jnp.maximum(jnp.max(sh, axis=-1, keepdims=True), sink)
                pe = jnp.exp(sh - m)
                ds.append(jnp.sum(pe, axis=-1, keepdims=True) + jnp.exp(sink - m))
                ps.append(pe.astype(BF16))
            o2 = _dot(jnp.concatenate(ps, axis=1), vv)
            o_ref[:, cols] = (o2 / jnp.where(lane < HEAD_DIM, ds[0], ds[1])).astype(o_ref.dtype)


def _attn_b(qkv3, sink):
    b, s, _ = qkv3.shape
    nb = s // B_BLOCK
    qw = B_HEADS * HEAD_DIM
    kvw = B_KV_HEADS * HEAD_DIM
    kcol = qw // kvw
    vcol = kcol + 1
    tabs = jnp.asarray(_attn_b_tables())
    prev = lambda n: jnp.maximum(n - 1, 0)
    nxt = lambda n: jnp.minimum(n + 1, nb - 1)
    kind = lambda n: jnp.where(n == 0, 0, jnp.where(n == nb - 1, 2, 1))
    kv = lambda col, f: pl.BlockSpec((None, B_BLOCK, kvw), lambda bi, n: (bi, f(n), col))
    same = lambda n: n
    return pl.pallas_call(
        _attn_b_kernel,
        out_shape=jax.ShapeDtypeStruct((b, s, qw), BF16),
        grid=(b, nb),
        in_specs=[
            pl.BlockSpec(memory_space=pltpu.SMEM),
            pl.BlockSpec((None, B_BLOCK, qw), lambda bi, n: (bi, n, 0)),
            kv(kcol, prev), kv(kcol, same), kv(kcol, nxt),
            kv(vcol, prev), kv(vcol, same), kv(vcol, nxt),
            pl.BlockSpec((None, B_HEADS, B_BLOCK, 3 * B_BLOCK), lambda bi, n: (kind(n), 0, 0, 0)),
        ],
        out_specs=pl.BlockSpec((None, B_BLOCK, qw), lambda bi, n: (bi, n, 0)),
        compiler_params=_cparams("parallel", "arbitrary"),
        name="attn_b",
    )(sink.astype(F32), qkv3, qkv3, qkv3, qkv3, qkv3, qkv3, qkv3, tabs)


def _attn_c_table(rpb, rows):
    w = GRID_W
    kr_n = min(C_WIN_ROWS, rows)
    c = np.arange(w)
    cs = np.clip(c - C_WIN_COLS // 2, 0, w - C_WIN_COLS)
    kcol = np.arange(w)
    colmask = (kcol[None, :] >= cs[:, None]) & (kcol[None, :] < cs[:, None] + C_WIN_COLS)
    coloff = np.clip(kcol[None, :] - c[:, None] + C_WIN_COLS - 1, 0, 2 * C_WIN_COLS - 2)
    n_off = 2 * C_WIN_ROWS - kr_n
    row_idx = np.arange(n_off)[:, None] + np.arange(kr_n)[None, :]
    bias = rpb.astype(F32)[:, row_idx][:, :, :, coloff]
    bias = jnp.where(jnp.asarray(colmask)[None, None, None], bias, NEG)
    return bias.transpose(1, 0, 3, 2, 4).reshape(n_off, rpb.shape[0], w, kr_n * w)


def _attn_c_kernel(q_ref, k_ref, v_ref, tab_ref, o_ref):
    nk = k_ref.shape[1] * k_ref.shape[2]
    lane = lax.broadcasted_iota(jnp.int32, (1, LANES), 1)
    for t in range(C_HEADS // 2):
        cols = slice(t * LANES, (t + 1) * LANES)
        kt = k_ref[0, :, :, cols].reshape(nk, LANES)
        vt = v_ref[0, :, :, cols].reshape(nk, LANES)
        zero = jnp.zeros_like(kt)
        kk = jnp.concatenate([jnp.where(lane < HEAD_DIM, kt, zero), jnp.where(lane >= HEAD_DIM, kt, zero)], axis=0)
        vv = jnp.concatenate([jnp.where(lane < HEAD_DIM, vt, zero), jnp.where(lane >= HEAD_DIM, vt, zero)], axis=0)
        s = _dot_nt(q_ref[:, cols], kk)
        ps, ds = [], []
        for ch in range(2):
            sh = s[:, ch * nk:(ch + 1) * nk] + tab_ref[2 * t + ch]
            m = jnp.max(sh, axis=-1, keepdims=True)
            pe = jnp.exp(sh - m)
            ds.append(jnp.sum(pe, axis=-1, keepdims=True))
            ps.append(pe.astype(BF16))
        o2 = _dot(jnp.concatenate(ps, axis=1), vv)
        o_ref[:, cols] = (o2 / jnp.where(lane < HEAD_DIM, ds[0], ds[1])).astype(o_ref.dtype)


def _attn_c(qkv4, rpb):
    b, rows, w, _ = qkv4.shape
    hw = C_HEADS * HEAD_DIM
    kr_n = min(C_WIN_ROWS, rows)
    table = _attn_c_table(rpb, rows)
    row_start = lambda r: jnp.clip(r - kr_n // 2, 0, rows - kr_n)
    el = pl.Element
    return pl.pallas_call(
        _attn_c_kernel,
        out_shape=jax.ShapeDtypeStruct((b, rows, w, hw), BF16),
        grid=(rows, b),
        in_specs=[
            pl.BlockSpec((None, None, w, hw), lambda r, bi: (bi, r, 0, 0)),
            pl.BlockSpec((el(1), el(kr_n), el(w), el(hw)), lambda r, bi: (bi, row_start(r), 0, hw)),
            pl.BlockSpec((el(1), el(kr_n), el(w), el(hw)), lambda r, bi: (bi, row_start(r), 0, 2 * hw)),
            pl.BlockSpec((None, C_HEADS, w, kr_n * w),
                         lambda r, bi: (row_start(r) - r + C_WIN_ROWS - 1, 0, 0, 0)),
        ],
        out_specs=pl.BlockSpec((None, None, w, hw), lambda r, bi: (bi, r, 0, 0)),
        compiler_params=_cparams("arbitrary", "arbitrary"),
        name="attn_c",
    )(qkv4, qkv4, qkv4, table)


def _lane_cumsum(x, tri):
    e, s = x.shape
    off = jnp.zeros((e, 1), F32)
    outs = []
    for j in range(s // LANES):
        blk = x[:, j * LANES:(j + 1) * LANES]
        outs.append(_dot(blk.astype(BF16), tri) + off)
        off = off + jnp.sum(blk, axis=1, keepdims=True)
    return jnp.concatenate(outs, axis=1)


def _route_kernel(h_ref, g_ref, rwt_ref, tri_ref, xn_ref, slotr_ref, gater_ref, slotc_ref, *, cap, chunk):
    s = h_ref.shape[0]
    n_e = rwt_ref.shape[0]
    logits = []
    for c in range(s // chunk):
        rows = slice(c * chunk, (c + 1) * chunk)
        xn = _rms(h_ref[rows, :], g_ref[...])
        xn_ref[rows, :] = xn.astype(BF16)
        logits.append(lax.dot_general(rwt_ref[...], xn, (((1,), (1,)), ((), ())),
                                      precision=lax.Precision.HIGHEST, preferred_element_type=F32))
    lg = jnp.concatenate(logits, axis=1)
    ex = jnp.exp(lg - jnp.max(lg, axis=0, keepdims=True))
    aff = ex / jnp.sum(ex, axis=0, keepdims=True)
    bits = pltpu.bitcast(aff, jnp.int32)
    thr = jnp.zeros((n_e, 1), jnp.int32)
    for bit in range(30, -1, -1):
        cand = thr | (1 << bit)
        cnt = jnp.sum(jnp.where(bits >= cand, 1.0, 0.0), axis=1, keepdims=True)
        thr = jnp.where(cnt >= cap, cand, thr)
    gt = bits > thr
    eq = bits == thr
    need = cap - jnp.sum(jnp.where(gt, 1.0, 0.0), axis=1, keepdims=True)
    eqf = jnp.where(eq, 1.0, 0.0)
    eq_rank = _lane_cumsum(eqf, tri_ref[...]) - eqf
    sel = gt | (eq & (eq_rank < need))
    self_ = jnp.where(sel, 1.0, 0.0)
    slot = jnp.where(sel, _lane_cumsum(self_, tri_ref[...]) - self_, -1.0)
    slotr_ref[...] = slot.astype(jnp.int32)
    gater_ref[...] = jnp.where(sel, aff, 0.0)
    padded = jnp.concatenate([slot, jnp.full((LANES - n_e, s), -1.0, F32)], axis=0)
    for j in range(s // LANES):
        slotc_ref[j * LANES:(j + 1) * LANES, :] = padded[:, j * LANES:(j + 1) * LANES].T.astype(BF16)


def _route(h3, g, router_w, cap):
    b, s, d = h3.shape
    n_e = router_w.shape[1]
    tri = jnp.asarray(np.triu(np.ones((LANES, LANES))), BF16)
    return pl.pallas_call(
        functools.partial(_route_kernel, cap=cap, chunk=256),
        out_shape=(
            jax.ShapeDtypeStruct((b, s, d), BF16),
            jax.ShapeDtypeStruct((b, n_e, s), jnp.int32),
            jax.ShapeDtypeStruct((b, n_e, s), F32),
            jax.ShapeDtypeStruct((b, s, LANES), BF16),
        ),
        grid=(b,),
        in_specs=[
            pl.BlockSpec((None, s, d), lambda i: (i, 0, 0)),
            _full_spec((1, d)),
            _full_spec((n_e, d)),
            _full_spec((LANES, LANES)),
        ],
        out_specs=(
            pl.BlockSpec((None, s, d), lambda i: (i, 0, 0)),
            pl.BlockSpec((None, n_e, s), lambda i: (i, 0, 0)),
            pl.BlockSpec((None, n_e, s), lambda i: (i, 0, 0)),
            pl.BlockSpec((None, s, LANES), lambda i: (i, 0, 0)),
        ),
        compiler_params=_cparams("parallel"),
        name="route",
    )(h3, g.reshape(1, d), router_w.T.astype(F32), tri)


def _ffn_kernel(xn_ref, slotr_ref, gater_ref, slotc_ref, wg_ref, wu_ref, wd_ref, o_ref, *, cap):
    e = pl.program_id(1)
    s = xn_ref.shape[0]

    @pl.when(e == 0)
    def _():
        o_ref[...] = jnp.zeros_like(o_ref)

    match = lax.broadcasted_iota(jnp.int32, (cap, s), 0) == slotr_ref[...]
    onehot = jnp.where(match, 1.0, 0.0).astype(BF16)
    gate = jnp.sum(jnp.where(match, gater_ref[...], 0.0), axis=1, keepdims=True)
    xs = _dot(onehot, xn_ref[...]).astype(BF16)
    hg = _dot(xs, wg_ref[...])
    hu = _dot(xs, wu_ref[...])
    hdn = (hg * jax.nn.sigmoid(hg) * hu).astype(BF16)
    y = (_dot(hdn, wd_ref[...]) * gate).astype(BF16)
    pick = jnp.where(lax.broadcasted_iota(jnp.int32, (LANES, cap), 0) == e, 1.0, 0.0).astype(BF16)
    slot_b = _dot(slotc_ref[...], pick)
    lane_id = lax.broadcasted_iota(jnp.int32, (s, cap), 1).astype(F32)
    onehot_t = jnp.where(slot_b == lane_id, 1.0, 0.0).astype(BF16)
    o_ref[...] += _dot(onehot_t, y)


def _expert_ffn(xn3, slot_row, gate_row, slot_col, w_gate, w_up, w_down, cap):
    b, s, d = xn3.shape
    n_e, _, ff = w_gate.shape
    return pl.pallas_call(
        functools.partial(_ffn_kernel, cap=cap),
        out_shape=jax.ShapeDtypeStruct((b, s, d), F32),
        grid=(b, n_e),
        in_specs=[
            pl.BlockSpec((None, s, d), lambda i, e: (i, 0, 0)),
            pl.BlockSpec((None, 1, s), lambda i, e: (i * n_e + e, 0, 0)),
            pl.BlockSpec((None, 1, s), lambda i, e: (i * n_e + e, 0, 0)),
            pl.BlockSpec((None, s, LANES), lambda i, e: (i, 0, 0)),
            pl.BlockSpec((None, d, ff), lambda i, e: (e, 0, 0)),
            pl.BlockSpec((None, d, ff), lambda i, e: (e, 0, 0)),
            pl.BlockSpec((None, ff, d), lambda i, e: (e, 0, 0)),
        ],
        out_specs=pl.BlockSpec((None, s, d), lambda i, e: (i, 0, 0)),
        compiler_params=_cparams("parallel", "arbitrary"),
        name="expert_ffn",
    )(xn3, slot_row.reshape(b * n_e, 1, s), gate_row.reshape(b * n_e, 1, s), slot_col,
      w_gate.astype(BF16), w_up.astype(BF16), w_down.astype(BF16))


def _ple_kernel(h_ref, d_ref, p_ref, g_ref, wg_ref, wp_ref, o_ref):
    h = h_ref[...] + d_ref[...]
    xn = _rms(h, g_ref[...]).astype(BF16)
    gate = jax.nn.sigmoid(_dot(xn, wg_ref[...]))
    o_ref[...] = h + gate * _dot(p_ref[...].astype(BF16), wp_ref[...])


def _ple(h2, delta2, p2, g, w_gate, w_proj, tm=512):
    t, d = h2.shape
    pd = p2.shape[1]
    row = lambda width: pl.BlockSpec((tm, width), lambda i: (i, 0))
    return pl.pallas_call(
        _ple_kernel,
        out_shape=jax.ShapeDtypeStruct((t, d), F32),
        grid=(t // tm,),
        in_specs=[row(d), row(d), row(pd), _full_spec((1, d)), _full_spec((d, d)), _full_spec((pd, d))],
        out_specs=row(d),
        compiler_params=_cparams("parallel"),
        name="ple",
    )(h2, delta2, p2, g.reshape(1, d), w_gate.astype(BF16), w_proj.astype(BF16))


def kernel(x, p, norm_mix_g, norm_ffn_g, w_out, router_w, exp_w_gate, exp_w_up, exp_w_down, ple_norm_g, ple_gate_w, ple_proj_w, a_w_in, a_vnorm_g, a_w_s, a_b_s, b_w_in, b_qnorm_g, b_knorm_g, b_sink, c_w_in, c_qnorm_g, c_knorm_g, c_rpb):
    b, s, d = x.shape
    depth = norm_mix_g.shape[0]
    t = b * s
    cap = max(1, EC_CAPACITY_FACTOR * s // N_EXPERTS)
    scale = HEAD_DIM ** -0.5
    h = x.reshape(t, d)
    for i in range(depth):
        kind = i % N_MIXERS
        j = i // N_MIXERS
        if kind == 0:
            h = _mixer_a(h, norm_mix_g[i], a_w_in[j], a_vnorm_g[j], a_w_s[j], a_b_s[j], w_out[i])
        elif kind == 1:
            gain = jnp.concatenate([jnp.tile(b_qnorm_g[j] * scale, B_HEADS), jnp.tile(b_knorm_g[j], B_KV_HEADS)])
            qkv = _norm_proj(h, norm_mix_g[i], b_w_in[j], gain, (B_HEADS + B_KV_HEADS) * HEAD_DIM)
            mix = _attn_b(qkv.reshape(b, s, -1), b_sink[j])
            h = _proj_residual(mix.reshape(t, -1), w_out[i], h)
        else:
            gain = jnp.concatenate([jnp.tile(c_qnorm_g[j] * scale, C_HEADS), jnp.tile(c_knorm_g[j], C_HEADS)])
            qkv = _norm_proj(h, norm_mix_g[i], c_w_in[j], gain, 2 * C_HEADS * HEAD_DIM)
            mix = _attn_c(qkv.reshape(b, s // GRID_W, GRID_W, -1), c_rpb[j])
            h = _proj_residual(mix.reshape(t, -1), w_out[i], h)
        xn, slot_row, gate_row, slot_col = _route(h.reshape(b, s, d), norm_ffn_g[i], router_w[i], cap)
        delta = _expert_ffn(xn, slot_row, gate_row, slot_col, exp_w_gate[i], exp_w_up[i], exp_w_down[i], cap)
        h = _ple(h, delta.reshape(t, d), p[i].reshape(t, -1), ple_norm_g[i], ple_gate_w[i], ple_proj_w[i])
    return h.reshape(b, s, d)
```

```python
import functools

import numpy as np
import jax
import jax.numpy as jnp
from jax import lax
from jax.experimental import pallas as pl
from jax.experimental.pallas import tpu as pltpu

F32 = jnp.float32
BF16 = jnp.bfloat16

RMS_EPS = 1e-6
LN_EPS = 1e-5
NEG = -1e30
HEAD_DIM = 64
GRID_W = 64
A_GROUPS = 8
A_CHUNK = 128
B_HEADS = 16
B_KV_HEADS = 4
B_BLOCK = 128
C_HEADS = 16
C_WIN_ROWS = 8
C_WIN_COLS = 16
N_EXPERTS = 16
EC_CAPACITY_FACTOR = 2
N_MIXERS = 3

LANES = 128
MXU_DIM = 256
VMEM_LIMIT_BYTES = 60 * 1024 * 1024


def _cparams(*sem):
    return pltpu.CompilerParams(dimension_semantics=sem, vmem_limit_bytes=VMEM_LIMIT_BYTES)


def _rms(xf, g):
    return xf * lax.rsqrt(jnp.mean(xf * xf, axis=-1, keepdims=True) + RMS_EPS) * g


def _dot(a, b):
    return jnp.dot(a, b, preferred_element_type=F32)


def _dot_nt(a, b):
    return lax.dot_general(a, b, (((1,), (1,)), ((), ())), preferred_element_type=F32)


def _full_spec(shape):
    nd = len(shape)
    return pl.BlockSpec(shape, lambda *_: (0,) * nd)


def _mixer_a_kernel(h_ref, g_ref, win_ref, vg_ref, ws_ref, bias_ref, wout_ref, o_ref, mix_ref):
    tm = h_ref.shape[0]
    width = vg_ref.shape[1]
    gw = width // A_GROUPS
    h = h_ref[...]
    xn = _rms(h, g_ref[...]).astype(BF16)
    z = _dot(xn, win_ref[...])
    z = 0.5 * z * (1.0 + jnp.tanh(np.sqrt(2.0 / np.pi).astype(np.float32)
                                  * (z + 0.044715 * (z * z * z))))
    u = z[:, :width]
    v = z[:, width:]
    mu = jnp.mean(v, axis=-1, keepdims=True)
    vc = v - mu
    var = jnp.mean(vc * vc, axis=-1, keepdims=True)
    vn = (vc * lax.rsqrt(var + LN_EPS) * vg_ref[...]).astype(BF16)
    for c in range(tm // A_CHUNK):
        rows = slice(c * A_CHUNK, (c + 1) * A_CHUNK)
        for g in range(A_GROUPS):
            cols = slice(g * gw, (g + 1) * gw)
            s = _dot(ws_ref[g], vn[rows, cols]) + bias_ref[:, cols]
            mix_ref[rows, cols] = (u[rows, cols] * s).astype(BF16)
    o_ref[...] = h + _dot(mix_ref[...], wout_ref[...])


def _mixer_a(h2, g, w_in, vnorm_g, w_s, b_s, w_out, tm=256):
    t, d = h2.shape
    width = vnorm_g.shape[0]
    gw = width // A_GROUPS
    bias = jnp.repeat(b_s.T.astype(F32), gw, axis=1)
    return pl.pallas_call(
        _mixer_a_kernel,
        out_shape=jax.ShapeDtypeStruct((t, d), F32),
        grid=(t // tm,),
        in_specs=[
            pl.BlockSpec((tm, d), lambda i: (i, 0)),
            _full_spec((1, d)),
            _full_spec((d, 2 * width)),
            _full_spec((1, width)),
            _full_spec((A_GROUPS, A_CHUNK, A_CHUNK)),
            _full_spec((A_CHUNK, width)),
            _full_spec((width, d)),
        ],
        out_specs=pl.BlockSpec((tm, d), lambda i: (i, 0)),
        scratch_shapes=[pltpu.VMEM((tm, width), BF16)],
        compiler_params=_cparams("parallel"),
        name="mixer_a",
    )(h2, g.reshape(1, d), w_in.astype(BF16), vnorm_g.reshape(1, width), w_s.astype(BF16), bias,
      w_out.astype(BF16))


def _norm_proj_kernel(h_ref, g_ref, w_ref, hg_ref, bd_ref, o_ref, *, n_norm_cols):
    xn = _rms(h_ref[...], g_ref[...]).astype(BF16)
    acc = _dot(xn, w_ref[...])
    for j in range(n_norm_cols // MXU_DIM):
        cols = slice(j * MXU_DIM, (j + 1) * MXU_DIM)
        blk = acc[:, cols]
        ss = _dot((blk * blk).astype(BF16), bd_ref[...])
        o_ref[:, cols] = (blk * lax.rsqrt(ss * (1.0 / HEAD_DIM) + RMS_EPS) * hg_ref[:, cols]).astype(BF16)
    o_ref[:, n_norm_cols:] = acc[:, n_norm_cols:].astype(BF16)


def _norm_proj(h2, g, w, head_gain, n_norm_cols, tm=512):
    t, d = h2.shape
    n = w.shape[1]
    blockdiag = jnp.asarray(np.kron(np.eye(MXU_DIM // HEAD_DIM), np.ones((HEAD_DIM, HEAD_DIM))), BF16)
    return pl.pallas_call(
        functools.partial(_norm_proj_kernel, n_norm_cols=n_norm_cols),
        out_shape=jax.ShapeDtypeStruct((t, n), BF16),
        grid=(t // tm,),
        in_specs=[
            pl.BlockSpec((tm, d), lambda i: (i, 0)),
            _full_spec((1, d)),
            _full_spec((d, n)),
            _full_spec((1, n_norm_cols)),
            _full_spec((MXU_DIM, MXU_DIM)),
        ],
        out_specs=pl.BlockSpec((tm, n), lambda i: (i, 0)),
        compiler_params=_cparams("parallel"),
        name="norm_proj",
    )(h2, g.reshape(1, d), w.astype(BF16), head_gain.reshape(1, n_norm_cols).astype(F32), blockdiag)


def _proj_residual_kernel(a_ref, w_ref, h_ref, o_ref):
    o_ref[...] = h_ref[...] + _dot(a_ref[...], w_ref[...])


def _proj_residual(a2, w, h2, tm=512):
    t, d = h2.shape
    k = a2.shape[1]
    return pl.pallas_call(
        _proj_residual_kernel,
        out_shape=jax.ShapeDtypeStruct((t, d), F32),
        grid=(t // tm,),
        in_specs=[
            pl.BlockSpec((tm, k), lambda i: (i, 0)),
            _full_spec((k, d)),
            pl.BlockSpec((tm, d), lambda i: (i, 0)),
        ],
        out_specs=pl.BlockSpec((tm, d), lambda i: (i, 0)),
        compiler_params=_cparams("parallel"),
        name="proj_residual",
    )(a2, w.astype(BF16), h2)


def _alibi_slopes(n):
    return np.array([2.0 ** (-8.0 * (h + 1) / n) for h in range(n)], dtype=np.float32)


def _attn_b_tables():
    span = 3 * B_BLOCK
    rel = np.arange(span)[None, :] - B_BLOCK - np.arange(B_BLOCK)[:, None]
    in_window = np.abs(rel) <= B_BLOCK
    alibi = (-_alibi_slopes(B_HEADS)[:, None, None] * np.abs(rel)[None]).astype(np.float32)
    kblk = np.arange(span) // B_BLOCK
    tabs = []
    for kind in range(3):
        valid = in_window & ~((kind == 0) & (kblk == 0))[None, :] & ~((kind == 2) & (kblk == 2))[None, :]
        tabs.append(np.where(valid[None], alibi, np.float32(NEG)))
    return np.stack(tabs).astype(np.float32)


def _pair_split(x, lane, own_half):
    rolled = pltpu.roll(x, HEAD_DIM, axis=1)
    zero = jnp.zeros_like(x)
    if own_half == 0:
        return jnp.where(lane < HEAD_DIM, x, zero), jnp.where(lane >= HEAD_DIM, rolled, zero)
    return jnp.where(lane < HEAD_DIM, rolled, zero), jnp.where(lane >= HEAD_DIM, x, zero)


def _attn_b_kernel(sink_ref, q_ref, kp_ref, kc_ref, kn_ref, vp_ref, vc_ref, vn_ref, tab_ref, o_ref):
    span = 3 * B_BLOCK
    grp = B_HEADS // B_KV_HEADS
    lane = lax.broadcasted_iota(jnp.int32, (1, LANES), 1)
    for kh in range(B_KV_HEADS):
        tile = slice((kh // 2) * LANES, (kh // 2 + 1) * LANES)
        kt = jnp.concatenate([kp_ref[:, tile], kc_ref[:, tile], kn_ref[:, tile]], axis=0)
        vt = jnp.concatenate([vp_ref[:, tile], vc_ref[:, tile], vn_ref[:, tile]], axis=0)
        k_lo, k_hi = _pair_split(kt.astype(F32), lane, kh % 2)
        v_lo, v_hi = _pair_split(vt.astype(F32), lane, kh % 2)
        kk = jnp.concatenate([k_lo, k_hi], axis=0).astype(BF16)
        vv = jnp.concatenate([v_lo, v_hi], axis=0).astype(BF16)
        for half in range(grp // 2):
            qt = kh * (grp // 2) + half
            cols = slice(qt * LANES, (qt + 1) * LANES)
            s = _dot_nt(q_ref[:, cols], kk)
            ps, ds = [], []
            for ch in range(2):
                head = 2 * qt + ch
                sh = s[:, ch * span:(ch + 1) * span] + tab_ref[head]
                sink = sink_ref[head]
                m = jnp.maximum(jnp.max(sh, axis=-1, keepdims=True), sink)
                pe = jnp.exp(sh - m)
                ds.append(jnp.sum(pe, axis=-1, keepdims=True) + jnp.exp(sink - m))
                ps.append(pe.astype(BF16))
            o2 = _dot(jnp.concatenate(ps, axis=1), vv)
            o_ref[:, cols] = (o2 / jnp.where(lane < HEAD_DIM, ds[0], ds[1])).astype(o_ref.dtype)


def _attn_b(qkv3, sink):
    b, s, _ = qkv3.shape
    nb = s // B_BLOCK
    qw = B_HEADS * HEAD_DIM
    kvw = B_KV_HEADS * HEAD_DIM
    kcol = qw // kvw
    vcol = kcol + 1
    tabs = jnp.asarray(_attn_b_tables())
    prev = lambda n: jnp.maximum(n - 1, 0)
    nxt = lambda n: jnp.minimum(n + 1, nb - 1)
    kind = lambda n: jnp.where(n == 0, 0, jnp.where(n == nb - 1, 2, 1))
    kv = lambda col, f: pl.BlockSpec((None, B_BLOCK, kvw), lambda bi, n: (bi, f(n), col))
    same = lambda n: n
    return pl.pallas_call(
        _attn_b_kernel,
        out_shape=jax.ShapeDtypeStruct((b, s, qw), BF16),
        grid=(b, nb),
        in_specs=[
            pl.BlockSpec(memory_space=pltpu.SMEM),
            pl.BlockSpec((None, B_BLOCK, qw), lambda bi, n: (bi, n, 0)),
            kv(kcol, prev), kv(kcol, same), kv(kcol, nxt),
            kv(vcol, prev), kv(vcol, same), kv(vcol, nxt),
            pl.BlockSpec((None, B_HEADS, B_BLOCK, 3 * B_BLOCK), lambda bi, n: (kind(n), 0, 0, 0)),
        ],
        out_specs=pl.BlockSpec((None, B_BLOCK, qw), lambda bi, n: (bi, n, 0)),
        compiler_params=_cparams("parallel", "arbitrary"),
        name="attn_b",
    )(sink.astype(F32), qkv3, qkv3, qkv3, qkv3, qkv3, qkv3, qkv3, tabs)


def _attn_c_table(rpb, rows):
    w = GRID_W
    kr_n = min(C_WIN_ROWS, rows)
    c = np.arange(w)
    cs = np.clip(c - C_WIN_COLS // 2, 0, w - C_WIN_COLS)
    kcol = np.arange(w)
    colmask = (kcol[None, :] >= cs[:, None]) & (kcol[None, :] < cs[:, None] + C_WIN_COLS)
    coloff = np.clip(kcol[None, :] - c[:, None] + C_WIN_COLS - 1, 0, 2 * C_WIN_COLS - 2)
    n_off = 2 * C_WIN_ROWS - kr_n
    row_idx = np.arange(n_off)[:, None] + np.arange(kr_n)[None, :]
    bias = rpb.astype(F32)[:, row_idx][:, :, :, coloff]
    bias = jnp.where(jnp.asarray(colmask)[None, None, None], bias, NEG)
    return bias.transpose(1, 0, 3, 2, 4).reshape(n_off, rpb.shape[0] // 2, 2 * w, kr_n * w)


def _attn_c_kernel(q_ref, k_ref, v_ref, tab_ref, o_ref):
    nk = k_ref.shape[1] * k_ref.shape[2]
    nq = q_ref.shape[0]
    lane = lax.broadcasted_iota(jnp.int32, (1, LANES), 1)
    low = lane < HEAD_DIM
    scores = []
    for t in range(C_HEADS // 2):
        cols = slice(t * LANES, (t + 1) * LANES)
        q2 = q_ref[:, cols]
        zero = jnp.zeros_like(q2)
        qq = jnp.concatenate([jnp.where(low, q2, zero), jnp.where(low, zero, q2)], axis=0)
        kt = k_ref[0, :, :, cols].reshape(nk, LANES)
        scores.append(_dot_nt(qq, kt) + tab_ref[t])
    sh = jnp.concatenate(scores, axis=0)
    pe = jnp.exp(sh - jnp.max(sh, axis=-1, keepdims=True))
    denom = jnp.sum(pe, axis=-1, keepdims=True)
    pe = pe.astype(BF16)
    for t in range(C_HEADS // 2):
        cols = slice(t * LANES, (t + 1) * LANES)
        rows = slice(2 * t * nq, 2 * (t + 1) * nq)
        vt = v_ref[0, :, :, cols].reshape(nk, LANES)
        o = _dot(pe[rows], vt) / denom[rows]
        o_ref[:, cols] = jnp.where(low, o[:nq], o[nq:]).astype(o_ref.dtype)


def _attn_c(qkv4, rpb):
    b, rows, w, _ = qkv4.shape
    hw = C_HEADS * HEAD_DIM
    kr_n = min(C_WIN_ROWS, rows)
    table = _attn_c_table(rpb, rows)
    row_start = lambda r: jnp.clip(r - kr_n // 2, 0, rows - kr_n)
    el = pl.Element
    return pl.pallas_call(
        _attn_c_kernel,
        out_shape=jax.ShapeDtypeStruct((b, rows, w, hw), BF16),
        grid=(rows, b),
        in_specs=[
            pl.BlockSpec((None, None, w, hw), lambda r, bi: (bi, r, 0, 0)),
            pl.BlockSpec((el(1), el(kr_n), el(w), el(hw)), lambda r, bi: (bi, row_start(r), 0, hw)),
            pl.BlockSpec((el(1), el(kr_n), el(w), el(hw)), lambda r, bi: (bi, row_start(r), 0, 2 * hw)),
            pl.BlockSpec((None, C_HEADS // 2, 2 * w, kr_n * w),
                         lambda r, bi: (row_start(r) - r + C_WIN_ROWS - 1, 0, 0, 0)),
        ],
        out_specs=pl.BlockSpec((None, None, w, hw), lambda r, bi: (bi, r, 0, 0)),
        compiler_params=_cparams("arbitrary", "arbitrary"),
        name="attn_c",
    )(qkv4, qkv4, qkv4, table)


def _lane_cumsum(x, tri):
    e, s = x.shape
    off = jnp.zeros((e, 1), F32)
    outs = []
    for j in range(s // LANES):
        blk = x[:, j * LANES:(j + 1) * LANES]
        outs.append(_dot(blk.astype(BF16), tri) + off)
        off = off + jnp.sum(blk, axis=1, keepdims=True)
    return jnp.concatenate(outs, axis=1)


def _route_kernel(h_ref, g_ref, rwt_ref, tri_ref, xn_ref, slotr_ref, gater_ref, slotc_ref, *, cap, chunk):
    s = h_ref.shape[0]
    n_e = rwt_ref.shape[0]
    logits = []
    for c in range(s // chunk):
        rows = slice(c * chunk, (c + 1) * chunk)
        xn = _rms(h_ref[rows, :], g_ref[...])
        xn_ref[rows, :] = xn.astype(BF16)
        logits.append(lax.dot_general(rwt_ref[...], xn, (((1,), (1,)), ((), ())),
                                      precision=lax.Precision.HIGHEST, preferred_element_type=F32))
    lg = jnp.concatenate(logits, axis=1)
    ex = jnp.exp(lg - jnp.max(lg, axis=0, keepdims=True))
    aff = ex / jnp.sum(ex, axis=0, keepdims=True)
    bits = pltpu.bitcast(aff, jnp.int32)
    thr = jnp.zeros((n_e, 1), jnp.int32)
    for bit in range(30, -1, -1):
        cand = thr | (1 << bit)
        cnt = jnp.sum(jnp.where(bits >= cand, 1.0, 0.0), axis=1, keepdims=True)
        thr = jnp.where(cnt >= cap, cand, thr)
    gt = bits > thr
    eq = bits == thr
    need = cap - jnp.sum(jnp.where(gt, 1.0, 0.0), axis=1, keepdims=True)
    eqf = jnp.where(eq, 1.0, 0.0)
    eq_rank = _lane_cumsum(eqf, tri_ref[...]) - eqf
    sel = gt | (eq & (eq_rank < need))
    self_ = jnp.where(sel, 1.0, 0.0)
    slot = jnp.where(sel, _lane_cumsum(self_, tri_ref[...]) - self_, -1.0)
    slotr_ref[...] = slot.astype(jnp.int32)
    gater_ref[...] = jnp.where(sel, aff, 0.0)
    padded = jnp.concatenate([slot, jnp.full((LANES - n_e, s), -1.0, F32)], axis=0)
    for j in range(s // LANES):
        slotc_ref[j * LANES:(j + 1) * LANES, :] = padded[:, j * LANES:(j + 1) * LANES].T.astype(BF16)


def _route(h3, g, router_w, cap):
    b, s, d = h3.shape
    n_e = router_w.shape[1]
    tri = jnp.asarray(np.triu(np.ones((LANES, LANES))), BF16)
    return pl.pallas_call(
        functools.partial(_route_kernel, cap=cap, chunk=256),
        out_shape=(
            jax.ShapeDtypeStruct((b, s, d), BF16),
            jax.ShapeDtypeStruct((b, n_e, s), jnp.int32),
            jax.ShapeDtypeStruct((b, n_e, s), F32),
            jax.ShapeDtypeStruct((b, s, LANES), BF16),
        ),
        grid=(b,),
        in_specs=[
            pl.BlockSpec((None, s, d), lambda i: (i, 0, 0)),
            _full_spec((1, d)),
            _full_spec((n_e, d)),
            _full_spec((LANES, LANES)),
        ],
        out_specs=(
            pl.BlockSpec((None, s, d), lambda i: (i, 0, 0)),
            pl.BlockSpec((None, n_e, s), lambda i: (i, 0, 0)),
            pl.BlockSpec((None, n_e, s), lambda i: (i, 0, 0)),
            pl.BlockSpec((None, s, LANES), lambda i: (i, 0, 0)),
        ),
        compiler_params=_cparams("parallel"),
        name="route",
    )(h3, g.reshape(1, d), router_w.T.astype(F32), tri)


def _ffn_kernel(xn_ref, slotr_ref, gater_ref, slotc_ref, wg_ref, wu_ref, wd_ref, o_ref, *, cap):
    e = pl.program_id(1)
    s = xn_ref.shape[0]

    @pl.when(e == 0)
    def _():
        o_ref[...] = jnp.zeros_like(o_ref)

    match = lax.broadcasted_iota(jnp.int32, (cap, s), 0) == slotr_ref[...]
    onehot = jnp.where(match, 1.0, 0.0).astype(BF16)
    gate = jnp.sum(jnp.where(match, gater_ref[...], 0.0), axis=1, keepdims=True)
    xs = _dot(onehot, xn_ref[...]).astype(BF16)
    hg = _dot(xs, wg_ref[...])
    hu = _dot(xs, wu_ref[...])
    hdn = (hg * jax.nn.sigmoid(hg) * hu).astype(BF16)
    y = (_dot(hdn, wd_ref[...]) * gate).astype(BF16)
    pick = jnp.where(lax.broadcasted_iota(jnp.int32, (LANES, cap), 0) == e, 1.0, 0.0).astype(BF16)
    slot_b = _dot(slotc_ref[...], pick)
    lane_id = lax.broadcasted_iota(jnp.int32, (s, cap), 1).astype(F32)
    onehot_t = jnp.where(slot_b == lane_id, 1.0, 0.0).astype(BF16)
    o_ref[...] += _dot(onehot_t, y)


def _expert_ffn(xn3, slot_row, gate_row, slot_col, w_gate, w_up, w_down, cap):
    b, s, d = xn3.shape
    n_e, _, ff = w_gate.shape
    return pl.pallas_call(
        functools.partial(_ffn_kernel, cap=cap),
        out_shape=jax.ShapeDtypeStruct((b, s, d), F32),
        grid=(b, n_e),
        in_specs=[
            pl.BlockSpec((None, s, d), lambda i, e: (i, 0, 0)),
            pl.BlockSpec((None, 1, s), lambda i, e: (i * n_e + e, 0, 0)),
            pl.BlockSpec((None, 1, s), lambda i, e: (i * n_e + e, 0, 0)),
            pl.BlockSpec((None, s, LANES), lambda i, e: (i, 0, 0)),
            pl.BlockSpec((None, d, ff), lambda i, e: (e, 0, 0)),
            pl.BlockSpec((None, d, ff), lambda i, e: (e, 0, 0)),
            pl.BlockSpec((None, ff, d), lambda i, e: (e, 0, 0)),
        ],
        out_specs=pl.BlockSpec((None, s, d), lambda i, e: (i, 0, 0)),
        compiler_params=_cparams("parallel", "arbitrary"),
        name="expert_ffn",
    )(xn3, slot_row.reshape(b * n_e, 1, s), gate_row.reshape(b * n_e, 1, s), slot_col,
      w_gate.astype(BF16), w_up.astype(BF16), w_down.astype(BF16))


def _ple_kernel(h_ref, d_ref, p_ref, g_ref, wg_ref, wp_ref, o_ref):
    h = h_ref[...] + d_ref[...]
    xn = _rms(h, g_ref[...]).astype(BF16)
    gate = jax.nn.sigmoid(_dot(xn, wg_ref[...]))
    o_ref[...] = h + gate * _dot(p_ref[...].astype(BF16), wp_ref[...])


def _ple(h2, delta2, p2, g, w_gate, w_proj, tm=512):
    t, d = h2.shape
    pd = p2.shape[1]
    row = lambda width: pl.BlockSpec((tm, width), lambda i: (i, 0))
    return pl.pallas_call(
        _ple_kernel,
        out_shape=jax.ShapeDtypeStruct((t, d), F32),
        grid=(t // tm,),
        in_specs=[row(d), row(d), row(pd), _full_spec((1, d)), _full_spec((d, d)), _full_spec((pd, d))],
        out_specs=row(d),
        compiler_params=_cparams("parallel"),
        name="ple",
    )(h2, delta2, p2, g.reshape(1, d), w_gate.astype(BF16), w_proj.astype(BF16))


def kernel(x, p, norm_mix_g, norm_ffn_g, w_out, router_w, exp_w_gate, exp_w_up, exp_w_down, ple_norm_g, ple_gate_w, ple_proj_w, a_w_in, a_vnorm_g, a_w_s, a_b_s, b_w_in, b_qnorm_g, b_knorm_g, b_sink, c_w_in, c_qnorm_g, c_knorm_g, c_rpb):
    b, s, d = x.shape
    depth = norm_mix_g.shape[0]
    t = b * s
    cap = max(1, EC_CAPACITY_FACTOR * s // N_EXPERTS)
    scale = HEAD_DIM ** -0.5
    h = x.reshape(t, d)
    for i in range(depth):
        kind = i % N_MIXERS
        j = i // N_MIXERS
        if kind == 0:
            h = _mixer_a(h, norm_mix_g[i], a_w_in[j], a_vnorm_g[j], a_w_s[j], a_b_s[j], w_out[i])
        elif kind == 1:
            gain = jnp.concatenate([jnp.tile(b_qnorm_g[j] * scale, B_HEADS), jnp.tile(b_knorm_g[j], B_KV_HEADS)])
            qkv = _norm_proj(h, norm_mix_g[i], b_w_in[j], gain, (B_HEADS + B_KV_HEADS) * HEAD_DIM)
            mix = _attn_b(qkv.reshape(b, s, -1), b_sink[j])
            h = _proj_residual(mix.reshape(t, -1), w_out[i], h)
        else:
            gain = jnp.concatenate([jnp.tile(c_qnorm_g[j] * scale, C_HEADS), jnp.tile(c_knorm_g[j], C_HEADS)])
            qkv = _norm_proj(h, norm_mix_g[i], c_w_in[j], gain, 2 * C_HEADS * HEAD_DIM)
            mix = _attn_c(qkv.reshape(b, s // GRID_W, GRID_W, -1), c_rpb[j])
            h = _proj_residual(mix.reshape(t, -1), w_out[i], h)
        xn, slot_row, gate_row, slot_col = _route(h.reshape(b, s, d), norm_ffn_g[i], router_w[i], cap)
        delta = _expert_ffn(xn, slot_row, gate_row, slot_col, exp_w_gate[i], exp_w_up[i], exp_w_down[i], cap)
        h = _ple(h, delta.reshape(t, d), p[i].reshape(t, -1), ple_norm_g[i], ple_gate_w[i], ple_proj_w[i])
    return h.reshape(b, s, d)
```

```python
import functools

import numpy as np
import jax
import jax.numpy as jnp
from jax import lax
from jax.experimental import pallas as pl
from jax.experimental.pallas import tpu as pltpu

F32 = jnp.float32
BF16 = jnp.bfloat16

RMS_EPS = 1e-6
LN_EPS = 1e-5
NEG = -1e30
HEAD_DIM = 64
GRID_W = 64
A_GROUPS = 8
A_CHUNK = 128
B_HEADS = 16
B_KV_HEADS = 4
B_BLOCK = 128
C_HEADS = 16
C_WIN_ROWS = 8
C_WIN_COLS = 16
N_EXPERTS = 16
EC_CAPACITY_FACTOR = 2
N_MIXERS = 3

LANES = 128
MXU_DIM = 256
VMEM_LIMIT_BYTES = 60 * 1024 * 1024

TOKEN_TILE = 256
CHUNK = 16
GROUP = MXU_DIM
CHUNKS_PER_GROUP = GROUP // CHUNK


def _cparams(*sem):
    return pltpu.CompilerParams(dimension_semantics=sem, vmem_limit_bytes=VMEM_LIMIT_BYTES)


def _rms(xf, g):
    return xf * lax.rsqrt(jnp.mean(xf * xf, axis=-1, keepdims=True) + RMS_EPS) * g


def _dot(a, b):
    return jnp.dot(a, b, preferred_element_type=F32)


def _dot_nt(a, b):
    return lax.dot_general(a, b, (((1,), (1,)), ((), ())), preferred_element_type=F32)


def _full_spec(shape):
    nd = len(shape)
    return pl.BlockSpec(shape, lambda *_: (0,) * nd)


def _mixer_a_kernel(h_ref, g_ref, win_ref, vg_ref, ws_ref, bias_ref, wout_ref, o_ref, mix_ref):
    tm = h_ref.shape[0]
    width = vg_ref.shape[1]
    gw = width // A_GROUPS
    h = h_ref[...]
    xn = _rms(h, g_ref[...]).astype(BF16)
    z = _dot(xn, win_ref[...])
    z = 0.5 * z * (1.0 + jnp.tanh(np.sqrt(2.0 / np.pi).astype(np.float32)
                                  * (z + 0.044715 * (z * z * z))))
    u = z[:, :width]
    v = z[:, width:]
    mu = jnp.mean(v, axis=-1, keepdims=True)
    vc = v - mu
    var = jnp.mean(vc * vc, axis=-1, keepdims=True)
    vn = (vc * lax.rsqrt(var + LN_EPS) * vg_ref[...]).astype(BF16)
    for c in range(tm // A_CHUNK):
        rows = slice(c * A_CHUNK, (c + 1) * A_CHUNK)
        for g in range(A_GROUPS):
            cols = slice(g * gw, (g + 1) * gw)
            s = _dot(ws_ref[g], vn[rows, cols]) + bias_ref[:, cols]
            mix_ref[rows, cols] = (u[rows, cols] * s).astype(BF16)
    o_ref[...] = h + _dot(mix_ref[...], wout_ref[...])


def _mixer_a(h2, g, w_in, vnorm_g, w_s, b_s, w_out, tm=256):
    t, d = h2.shape
    width = vnorm_g.shape[0]
    gw = width // A_GROUPS
    bias = jnp.repeat(b_s.T.astype(F32), gw, axis=1)
    return pl.pallas_call(
        _mixer_a_kernel,
        out_shape=jax.ShapeDtypeStruct((t, d), F32),
        grid=(t // tm,),
        in_specs=[
            pl.BlockSpec((tm, d), lambda i: (i, 0)),
            _full_spec((1, d)),
            _full_spec((d, 2 * width)),
            _full_spec((1, width)),
            _full_spec((A_GROUPS, A_CHUNK, A_CHUNK)),
            _full_spec((A_CHUNK, width)),
            _full_spec((width, d)),
        ],
        out_specs=pl.BlockSpec((tm, d), lambda i: (i, 0)),
        scratch_shapes=[pltpu.VMEM((tm, width), BF16)],
        compiler_params=_cparams("parallel"),
        name="mixer_a",
    )(h2, g.reshape(1, d), w_in.astype(BF16), vnorm_g.reshape(1, width), w_s.astype(BF16), bias,
      w_out.astype(BF16))


def _norm_proj_kernel(h_ref, g_ref, w_ref, hg_ref, bd_ref, o_ref, *, n_norm_cols):
    xn = _rms(h_ref[...], g_ref[...]).astype(BF16)
    acc = _dot(xn, w_ref[...])
    for j in range(n_norm_cols // MXU_DIM):
        cols = slice(j * MXU_DIM, (j + 1) * MXU_DIM)
        blk = acc[:, cols]
        ss = _dot((blk * blk).astype(BF16), bd_ref[...])
        o_ref[:, cols] = (blk * lax.rsqrt(ss * (1.0 / HEAD_DIM) + RMS_EPS) * hg_ref[:, cols]).astype(BF16)
    o_ref[:, n_norm_cols:] = acc[:, n_norm_cols:].astype(BF16)


def _norm_proj(h2, g, w, head_gain, n_norm_cols, tm=512):
    t, d = h2.shape
    n = w.shape[1]
    blockdiag = jnp.asarray(np.kron(np.eye(MXU_DIM // HEAD_DIM), np.ones((HEAD_DIM, HEAD_DIM))), BF16)
    return pl.pallas_call(
        functools.partial(_norm_proj_kernel, n_norm_cols=n_norm_cols),
        out_shape=jax.ShapeDtypeStruct((t, n), BF16),
        grid=(t // tm,),
        in_specs=[
            pl.BlockSpec((tm, d), lambda i: (i, 0)),
            _full_spec((1, d)),
            _full_spec((d, n)),
            _full_spec((1, n_norm_cols)),
            _full_spec((MXU_DIM, MXU_DIM)),
        ],
        out_specs=pl.BlockSpec((tm, n), lambda i: (i, 0)),
        compiler_params=_cparams("parallel"),
        name="norm_proj",
    )(h2, g.reshape(1, d), w.astype(BF16), head_gain.reshape(1, n_norm_cols).astype(F32), blockdiag)


def _proj_residual_kernel(a_ref, w_ref, h_ref, o_ref):
    o_ref[...] = h_ref[...] + _dot(a_ref[...], w_ref[...])


def _proj_residual(a2, w, h2, tm=512):
    t, d = h2.shape
    k = a2.shape[1]
    return pl.pallas_call(
        _proj_residual_kernel,
        out_shape=jax.ShapeDtypeStruct((t, d), F32),
        grid=(t // tm,),
        in_specs=[
            pl.BlockSpec((tm, k), lambda i: (i, 0)),
            _full_spec((k, d)),
            pl.BlockSpec((tm, d), lambda i: (i, 0)),
        ],
        out_specs=pl.BlockSpec((tm, d), lambda i: (i, 0)),
        compiler_params=_cparams("parallel"),
        name="proj_residual",
    )(a2, w.astype(BF16), h2)


def _alibi_slopes(n):
    return np.array([2.0 ** (-8.0 * (h + 1) / n) for h in range(n)], dtype=np.float32)


def _attn_b_tables():
    span = 3 * B_BLOCK
    rel = np.arange(span)[None, :] - B_BLOCK - np.arange(B_BLOCK)[:, None]
    in_window = np.abs(rel) <= B_BLOCK
    alibi = (-_alibi_slopes(B_HEADS)[:, None, None] * np.abs(rel)[None]).astype(np.float32)
    kblk = np.arange(span) // B_BLOCK
    tabs = []
    for kind in range(3):
        valid = in_window & ~((kind == 0) & (kblk == 0))[None, :] & ~((kind == 2) & (kblk == 2))[None, :]
        tabs.append(np.where(valid[None], alibi, np.float32(NEG)))
    return np.stack(tabs).astype(np.float32)


def _pair_split(x, lane, own_half):
    rolled = pltpu.roll(x, HEAD_DIM, axis=1)
    zero = jnp.zeros_like(x)
    if own_half == 0:
        return jnp.where(lane < HEAD_DIM, x, zero), jnp.where(lane >= HEAD_DIM, rolled, zero)
    return jnp.where(lane < HEAD_DIM, rolled, zero), jnp.where(lane >= HEAD_DIM, x, zero)


def _attn_b_kernel(sink_ref, q_ref, kp_ref, kc_ref, kn_ref, vp_ref, vc_ref, vn_ref, tab_ref, o_ref):
    span = 3 * B_BLOCK
    grp = B_HEADS // B_KV_HEADS
    lane = lax.broadcasted_iota(jnp.int32, (1, LANES), 1)
    for kh in range(B_KV_HEADS):
        tile = slice((kh // 2) * LANES, (kh // 2 + 1) * LANES)
        kt = jnp.concatenate([kp_ref[:, tile], kc_ref[:, tile], kn_ref[:, tile]], axis=0)
        vt = jnp.concatenate([vp_ref[:, tile], vc_ref[:, tile], vn_ref[:, tile]], axis=0)
        k_lo, k_hi = _pair_split(kt.astype(F32), lane, kh % 2)
        v_lo, v_hi = _pair_split(vt.astype(F32), lane, kh % 2)
        kk = jnp.concatenate([k_lo, k_hi], axis=0).astype(BF16)
        vv = jnp.concatenate([v_lo, v_hi], axis=0).astype(BF16)
        for half in range(grp // 2):
            qt = kh * (grp // 2) + half
            cols = slice(qt * LANES, (qt + 1) * LANES)
            s = _dot_nt(q_ref[:, cols], kk)
            ps, ds = [], []
            for ch in range(2):
                head = 2 * qt + ch
                sh = s[:, ch * span:(ch + 1) * span] + tab_ref[head]
                sink = sink_ref[head]
                m = jnp.maximum(jnp.max(sh, axis=-1, keepdims=True), sink)
                pe = jnp.exp(sh - m)
                ds.append(jnp.sum(pe, axis=-1, keepdims=True) + jnp.exp(sink - m))
                ps.append(pe.astype(BF16))
            o2 = _dot(jnp.concatenate(ps, axis=1), vv)
            o_ref[:, cols] = (o2 / jnp.where(lane < HEAD_DIM, ds[0], ds[1])).astype(o_ref.dtype)


def _attn_b(qkv3, sink):
    b, s, _ = qkv3.shape
    nb = s // B_BLOCK
    qw = B_HEADS * HEAD_DIM
    kvw = B_KV_HEADS * HEAD_DIM
    kcol = qw // kvw
    vcol = kcol + 1
    tabs = jnp.asarray(_attn_b_tables())
    prev = lambda n: jnp.maximum(n - 1, 0)
    nxt = lambda n: jnp.minimum(n + 1, nb - 1)
    kind = lambda n: jnp.where(n == 0, 0, jnp.where(n == nb - 1, 2, 1))
    kv = lambda col, f: pl.BlockSpec((None, B_BLOCK, kvw), lambda bi, n: (bi, f(n), col))
    same = lambda n: n
    return pl.pallas_call(
        _attn_b_kernel,
        out_shape=jax.ShapeDtypeStruct((b, s, qw), BF16),
        grid=(b, nb),
        in_specs=[
            pl.BlockSpec(memory_space=pltpu.SMEM),
            pl.BlockSpec((None, B_BLOCK, qw), lambda bi, n: (bi, n, 0)),
            kv(kcol, prev), kv(kcol, same), kv(kcol, nxt),
            kv(vcol, prev), kv(vcol, same), kv(vcol, nxt),
            pl.BlockSpec((None, B_HEADS, B_BLOCK, 3 * B_BLOCK), lambda bi, n: (kind(n), 0, 0, 0)),
        ],
        out_specs=pl.BlockSpec((None, B_BLOCK, qw), lambda bi, n: (bi, n, 0)),
        compiler_params=_cparams("parallel", "arbitrary"),
        name="attn_b",
    )(sink.astype(F32), qkv3, qkv3, qkv3, qkv3, qkv3, qkv3, qkv3, tabs)


def _attn_c_table(rpb, rows):
    w = GRID_W
    kr_n = min(C_WIN_ROWS, rows)
    c = np.arange(w)
    cs = np.clip(c - C_WIN_COLS // 2, 0, w - C_WIN_COLS)
    kcol = np.arange(w)
    colmask = (kcol[None, :] >= cs[:, None]) & (kcol[None, :] < cs[:, None] + C_WIN_COLS)
    coloff = np.clip(kcol[None, :] - c[:, None] + C_WIN_COLS - 1, 0, 2 * C_WIN_COLS - 2)
    n_off = 2 * C_WIN_ROWS - kr_n
    row_idx = np.arange(n_off)[:, None] + np.arange(kr_n)[None, :]
    bias = rpb.astype(F32)[:, row_idx][:, :, :, coloff]
    bias = jnp.where(jnp.asarray(colmask)[None, None, None], bias, NEG)
    return bias.transpose(1, 0, 3, 2, 4).reshape(n_off, rpb.shape[0] // 2, 2 * w, kr_n * w)


def _attn_c_kernel(q_ref, k_ref, v_ref, tab_ref, o_ref):
    nk = k_ref.shape[1] * k_ref.shape[2]
    nq = q_ref.shape[0]
    lane = lax.broadcasted_iota(jnp.int32, (1, LANES), 1)
    low = lane < HEAD_DIM
    scores = []
    for t in range(C_HEADS // 2):
        cols = slice(t * LANES, (t + 1) * LANES)
        q2 = q_ref[:, cols]
        zero = jnp.zeros_like(q2)
        qq = jnp.concatenate([jnp.where(low, q2, zero), jnp.where(low, zero, q2)], axis=0)
        kt = k_ref[0, :, :, cols].reshape(nk, LANES)
        scores.append(_dot_nt(qq, kt) + tab_ref[t])
    sh = jnp.concatenate(scores, axis=0)
    pe = jnp.exp(sh - jnp.max(sh, axis=-1, keepdims=True))
    denom = jnp.sum(pe, axis=-1, keepdims=True)
    pe = pe.astype(BF16)
    for t in range(C_HEADS // 2):
        cols = slice(t * LANES, (t + 1) * LANES)
        rows = slice(2 * t * nq, 2 * (t + 1) * nq)
        vt = v_ref[0, :, :, cols].reshape(nk, LANES)
        o = _dot(pe[rows], vt) / denom[rows]
        o_ref[:, cols] = jnp.where(low, o[:nq], o[nq:]).astype(o_ref.dtype)


def _attn_c(qkv4, rpb):
    b, rows, w, _ = qkv4.shape
    hw = C_HEADS * HEAD_DIM
    kr_n = min(C_WIN_ROWS, rows)
    table = _attn_c_table(rpb, rows)
    row_start = lambda r: jnp.clip(r - kr_n // 2, 0, rows - kr_n)
    el = pl.Element
    return pl.pallas_call(
        _attn_c_kernel,
        out_shape=jax.ShapeDtypeStruct((b, rows, w, hw), BF16),
        grid=(rows, b),
        in_specs=[
            pl.BlockSpec((None, None, w, hw), lambda r, bi: (bi, r, 0, 0)),
            pl.BlockSpec((el(1), el(kr_n), el(w), el(hw)), lambda r, bi: (bi, row_start(r), 0, hw)),
            pl.BlockSpec((el(1), el(kr_n), el(w), el(hw)), lambda r, bi: (bi, row_start(r), 0, 2 * hw)),
            pl.BlockSpec((None, C_HEADS // 2, 2 * w, kr_n * w),
                         lambda r, bi: (row_start(r) - r + C_WIN_ROWS - 1, 0, 0, 0)),
        ],
        out_specs=pl.BlockSpec((None, None, w, hw), lambda r, bi: (bi, r, 0, 0)),
        compiler_params=_cparams("arbitrary", "arbitrary"),
        name="attn_c",
    )(qkv4, qkv4, qkv4, table)


def _lane_cumsum(x, tri):
    e, s = x.shape
    off = jnp.zeros((e, 1), F32)
    outs = []
    for j in range(s // LANES):
        blk = x[:, j * LANES:(j + 1) * LANES]
        outs.append(_dot(blk.astype(BF16), tri) + off)
        off = off + jnp.sum(blk, axis=1, keepdims=True)
    return jnp.concatenate(outs, axis=1)


def _route_kernel(h_ref, g_ref, rwt_ref, tri_ref, xn_ref, slotr_ref, slotc_ref, gatec_ref, cnt_ref,
                  *, cap, chunk, tile):
    s = h_ref.shape[0]
    n_e = rwt_ref.shape[0]
    logits = []
    for c in range(s // chunk):
        rows = slice(c * chunk, (c + 1) * chunk)
        xn = _rms(h_ref[rows, :], g_ref[...])
        xn_ref[rows, :] = xn.astype(BF16)
        logits.append(lax.dot_general(rwt_ref[...], xn, (((1,), (1,)), ((), ())),
                                      precision=lax.Precision.HIGHEST, preferred_element_type=F32))
    lg = jnp.concatenate(logits, axis=1)
    ex = jnp.exp(lg - jnp.max(lg, axis=0, keepdims=True))
    aff = ex / jnp.sum(ex, axis=0, keepdims=True)
    bits = pltpu.bitcast(aff, jnp.int32)
    thr = jnp.zeros((n_e, 1), jnp.int32)
    for bit in range(30, -1, -1):
        cand = thr | (1 << bit)
        cnt = jnp.sum(jnp.where(bits >= cand, 1.0, 0.0), axis=1, keepdims=True)
        thr = jnp.where(cnt >= cap, cand, thr)
    gt = bits > thr
    eq = bits == thr
    need = cap - jnp.sum(jnp.where(gt, 1.0, 0.0), axis=1, keepdims=True)
    eqf = jnp.where(eq, 1.0, 0.0)
    eq_rank = _lane_cumsum(eqf, tri_ref[...]) - eqf
    sel = gt | (eq & (eq_rank < need))
    self_ = jnp.where(sel, 1.0, 0.0)
    slot = jnp.where(sel, _lane_cumsum(self_, tri_ref[...]) - self_, -1.0)
    lane = lax.broadcasted_iota(jnp.int32, (n_e, LANES), 1)
    counts = jnp.zeros((n_e, LANES), F32)
    for i in range(s // tile):
        counts = jnp.where(lane == i, jnp.sum(self_[:, i * tile:(i + 1) * tile], axis=1, keepdims=True), counts)
    cnt_ref[...] = counts.astype(jnp.int32)
    pad_rows = LANES - n_e
    slot_p = jnp.concatenate([slot, jnp.full((pad_rows, s), -1.0, F32)], axis=0)
    gate_p = jnp.concatenate([jnp.where(sel, aff, 0.0), jnp.zeros((pad_rows, s), F32)], axis=0)
    slotr_ref[...] = slot_p.astype(BF16)
    for j in range(s // LANES):
        cols = slice(j * LANES, (j + 1) * LANES)
        slotc_ref[cols, :] = slot_p[:, cols].T.astype(BF16)
        gatec_ref[cols, :] = gate_p[:, cols].T.astype(BF16)


def _route(h3, g, router_w, cap):
    b, s, d = h3.shape
    n_e = router_w.shape[1]
    tri = jnp.asarray(np.triu(np.ones((LANES, LANES))), BF16)
    per_seq = lambda *shape: pl.BlockSpec((None,) + shape, lambda i: (i,) + (0,) * len(shape))
    return pl.pallas_call(
        functools.partial(_route_kernel, cap=cap, chunk=256, tile=TOKEN_TILE),
        out_shape=(
            jax.ShapeDtypeStruct((b, s, d), BF16),
            jax.ShapeDtypeStruct((b, LANES, s), BF16),
            jax.ShapeDtypeStruct((b, s, LANES), BF16),
            jax.ShapeDtypeStruct((b, s, LANES), BF16),
            jax.ShapeDtypeStruct((b, n_e, LANES), jnp.int32),
        ),
        grid=(b,),
        in_specs=[per_seq(s, d), _full_spec((1, d)), _full_spec((n_e, d)), _full_spec((LANES, LANES))],
        out_specs=(per_seq(s, d), per_seq(LANES, s), per_seq(s, LANES), per_seq(s, LANES), per_seq(n_e, LANES)),
        compiler_params=_cparams("parallel"),
        name="route",
    )(h3, g.reshape(1, d), router_w.T.astype(F32), tri)


def _chunk_tables(cnt, cap):
    n_e = cnt.shape[1]
    kmax = n_e * (TOKEN_TILE // CHUNK + 1)
    kmax = -(-kmax // CHUNKS_PER_GROUP) * CHUNKS_PER_GROUP
    c0 = jnp.cumsum(cnt, axis=2) - cnt
    a0 = c0 // CHUNK
    a1 = jnp.where(cnt > 0, (c0 + cnt + CHUNK - 1) // CHUNK, a0)
    nch = (a1 - a0).transpose(0, 2, 1)
    a0 = a0.transpose(0, 2, 1)
    pos_end = jnp.cumsum(nch, axis=2)
    pos = pos_end - nch
    total = pos_end[..., -1]
    k = jnp.arange(kmax, dtype=jnp.int32)
    e_of_k = jnp.sum((k[None, None, :, None] >= pos_end[:, :, None, :]).astype(jnp.int32), axis=-1)
    e_of_k = jnp.minimum(e_of_k, n_e - 1)
    q = jnp.take_along_axis(a0, e_of_k, axis=2) + (k - jnp.take_along_axis(pos, e_of_k, axis=2))
    valid = k[None, None, :] < total[..., None]
    e_tab = jnp.where(valid, e_of_k, 0)
    slot_tab = jnp.where(valid, q * CHUNK, -(1 << 20))
    dst_tab = jnp.where(valid, e_of_k * cap + q * CHUNK, 0)
    groups = (total + CHUNKS_PER_GROUP - 1) // CHUNKS_PER_GROUP
    return jnp.concatenate([e_tab, slot_tab, dst_tab, groups[..., None]], axis=-1).astype(jnp.int32), kmax


def _gather_kernel(tab_ref, xn_ref, slotr_ref, xs_ref, *, kmax):
    @pl.when(pl.program_id(1) == 0)
    def _():
        xs_ref[...] = jnp.zeros_like(xs_ref)

    lane = lax.broadcasted_iota(jnp.int32, (CHUNK, LANES), 1)
    sub = lax.broadcasted_iota(jnp.int32, (CHUNK, LANES), 0)

    def group(g, carry):
        base = g * CHUNKS_PER_GROUP
        pick, want = [], []
        for c in range(CHUNKS_PER_GROUP):
            pick.append(jnp.where(lane == tab_ref[0, base + c], 1.0, 0.0).astype(BF16))
            want.append((sub + tab_ref[0, kmax + base + c]).astype(F32))
        slot_b = _dot(jnp.concatenate(pick, axis=0), slotr_ref[...])
        want = jnp.concatenate(want, axis=0)
        want = jnp.concatenate([want] * (slot_b.shape[1] // LANES), axis=1)
        onehot = jnp.where(slot_b == want, 1.0, 0.0).astype(BF16)
        rows = _dot(onehot, xn_ref[...]).astype(BF16)
        for c in range(CHUNKS_PER_GROUP):
            dst = pl.ds(pl.multiple_of(tab_ref[0, 2 * kmax + base + c], CHUNK), CHUNK)
            xs_ref[dst, :] = xs_ref[dst, :] + rows[c * CHUNK:(c + 1) * CHUNK, :]
        return carry

    lax.fori_loop(0, tab_ref[0, 3 * kmax], group, 0)


def _expert_kernel(xs_ref, wg_ref, wu_ref, wd_ref, y_ref):
    nb, cap, d = xs_ref.shape
    xs = xs_ref[...].reshape(nb * cap, d)
    hg = _dot(xs, wg_ref[...])
    hu = _dot(xs, wu_ref[...])
    hdn = (hg * jax.nn.sigmoid(hg) * hu).astype(BF16)
    y_ref[...] = _dot(hdn, wd_ref[...]).astype(BF16).reshape(nb, cap, d)


def _scatter_kernel(tab_ref, y_ref, slotc_ref, gatec_ref, o_ref, yg_ref, *, kmax):
    o_ref[...] = jnp.zeros_like(o_ref)
    lane = lax.broadcasted_iota(jnp.int32, (1, GROUP), 1)
    sub = lax.broadcasted_iota(jnp.int32, (LANES, GROUP), 0)

    def group(g, carry):
        base = g * CHUNKS_PER_GROUP
        expert = jnp.zeros((1, GROUP), jnp.int32)
        want = jnp.zeros((1, GROUP), jnp.int32)
        for c in range(CHUNKS_PER_GROUP):
            in_chunk = (lane >= c * CHUNK) & (lane < (c + 1) * CHUNK)
            expert = jnp.where(in_chunk, tab_ref[0, base + c], expert)
            want = jnp.where(in_chunk, tab_ref[0, kmax + base + c] + lane - c * CHUNK, want)
            src = pl.ds(pl.multiple_of(tab_ref[0, 2 * kmax + base + c], CHUNK), CHUNK)
            yg_ref[c * CHUNK:(c + 1) * CHUNK, :] = y_ref[src, :]
        pick = jnp.where(sub == expert, 1.0, 0.0).astype(BF16)
        slot_b = _dot(slotc_ref[...], pick)
        gate_b = _dot(gatec_ref[...], pick)
        weights = jnp.where(slot_b == want.astype(F32), gate_b, 0.0).astype(BF16)
        o_ref[...] += _dot(weights, yg_ref[...])
        return carry

    lax.fori_loop(0, tab_ref[0, 3 * kmax], group, 0)


def _expert_ffn(xn3, slot_row, slot_col, gate_col, cnt, w_gate, w_up, w_down, cap, seqs_per_step=4):
    b, s, d = xn3.shape
    n_e, _, ff = w_gate.shape
    nt = s // TOKEN_TILE
    tables, kmax = _chunk_tables(cnt[:, :, :nt], cap)
    tables = tables.reshape(b * nt, 1, 3 * kmax + 1)
    tab_spec = pl.BlockSpec((None, 1, 3 * kmax + 1), lambda i, t: (i * nt + t, 0, 0), memory_space=pltpu.SMEM)
    xs = pl.pallas_call(
        functools.partial(_gather_kernel, kmax=kmax),
        out_shape=jax.ShapeDtypeStruct((b, n_e * cap, d), BF16),
        grid=(b, nt),
        in_specs=[
            tab_spec,
            pl.BlockSpec((None, TOKEN_TILE, d), lambda i, t: (i, t, 0)),
            pl.BlockSpec((None, LANES, TOKEN_TILE), lambda i, t: (i, 0, t)),
        ],
        out_specs=pl.BlockSpec((None, n_e * cap, d), lambda i, t: (i, 0, 0)),
        compiler_params=_cparams("parallel", "arbitrary"),
        name="expert_gather",
    )(tables, xn3, slot_row)
    nb = seqs_per_step
    y = pl.pallas_call(
        _expert_kernel,
        out_shape=jax.ShapeDtypeStruct((b, n_e, cap, d), BF16),
        grid=(n_e, b // nb),
        in_specs=[
            pl.BlockSpec((nb, None, cap, d), lambda e, i: (i, e, 0, 0)),
            pl.BlockSpec((None, d, ff), lambda e, i: (e, 0, 0)),
            pl.BlockSpec((None, d, ff), lambda e, i: (e, 0, 0)),
            pl.BlockSpec((None, ff, d), lambda e, i: (e, 0, 0)),
        ],
        out_specs=pl.BlockSpec((nb, None, cap, d), lambda e, i: (i, e, 0, 0)),
        compiler_params=_cparams("parallel", "parallel"),
        name="expert_mlp",
    )(xs.reshape(b, n_e, cap, d), w_gate.astype(BF16), w_up.astype(BF16), w_down.astype(BF16))
    return pl.pallas_call(
        functools.partial(_scatter_kernel, kmax=kmax),
        out_shape=jax.ShapeDtypeStruct((b, s, d), F32),
        grid=(b, nt),
        in_specs=[
            tab_spec,
            pl.BlockSpec((None, n_e * cap, d), lambda i, t: (i, 0, 0)),
            pl.BlockSpec((None, TOKEN_TILE, LANES), lambda i, t: (i, t, 0)),
            pl.BlockSpec((None, TOKEN_TILE, LANES), lambda i, t: (i, t, 0)),
        ],
        out_specs=pl.BlockSpec((None, TOKEN_TILE, d), lambda i, t: (i, t, 0)),
        scratch_shapes=[pltpu.VMEM((GROUP, d), BF16)],
        compiler_params=_cparams("parallel", "arbitrary"),
        name="expert_scatter",
    )(tables, y.reshape(b, n_e * cap, d), slot_col, gate_col)


def _ple_kernel(h_ref, d_ref, p_ref, g_ref, wg_ref, wp_ref, o_ref):
    h = h_ref[...] + d_ref[...]
    xn = _rms(h, g_ref[...]).astype(BF16)
    gate = jax.nn.sigmoid(_dot(xn, wg_ref[...]))
    o_ref[...] = h + gate * _dot(p_ref[...].astype(BF16), wp_ref[...])


def _ple(h2, delta2, p2, g, w_gate, w_proj, tm=512):
    t, d = h2.shape
    pd = p2.shape[1]
    row = lambda width: pl.BlockSpec((tm, width), lambda i: (i, 0))
    return pl.pallas_call(
        _ple_kernel,
        out_shape=jax.ShapeDtypeStruct((t, d), F32),
        grid=(t // tm,),
        in_specs=[row(d), row(d), row(pd), _full_spec((1, d)), _full_spec((d, d)), _full_spec((pd, d))],
        out_specs=row(d),
        compiler_params=_cparams("parallel"),
        name="ple",
    )(h2, delta2, p2, g.reshape(1, d), w_gate.astype(BF16), w_proj.astype(BF16))


def kernel(x, p, norm_mix_g, norm_ffn_g, w_out, router_w, exp_w_gate, exp_w_up, exp_w_down, ple_norm_g, ple_gate_w, ple_proj_w, a_w_in, a_vnorm_g, a_w_s, a_b_s, b_w_in, b_qnorm_g, b_knorm_g, b_sink, c_w_in, c_qnorm_g, c_knorm_g, c_rpb):
    b, s, d = x.shape
    depth = norm_mix_g.shape[0]
    t = b * s
    cap = max(1, EC_CAPACITY_FACTOR * s // N_EXPERTS)
    scale = HEAD_DIM ** -0.5
    h = x.reshape(t, d)
    for i in range(depth):
        kind = i % N_MIXERS
        j = i // N_MIXERS
        if kind == 0:
            h = _mixer_a(h, norm_mix_g[i], a_w_in[j], a_vnorm_g[j], a_w_s[j], a_b_s[j], w_out[i])
        elif kind == 1:
            gain = jnp.concatenate([jnp.tile(b_qnorm_g[j] * scale, B_HEADS), jnp.tile(b_knorm_g[j], B_KV_HEADS)])
            qkv = _norm_proj(h, norm_mix_g[i], b_w_in[j], gain, (B_HEADS + B_KV_HEADS) * HEAD_DIM)
            mix = _attn_b(qkv.reshape(b, s, -1), b_sink[j])
            h = _proj_residual(mix.reshape(t, -1), w_out[i], h)
        else:
            gain = jnp.concatenate([jnp.tile(c_qnorm_g[j] * scale, C_HEADS), jnp.tile(c_knorm_g[j], C_HEADS)])
            qkv = _norm_proj(h, norm_mix_g[i], c_w_in[j], gain, 2 * C_HEADS * HEAD_DIM)
            mix = _attn_c(qkv.reshape(b, s // GRID_W, GRID_W, -1), c_rpb[j])
            h = _proj_residual(mix.reshape(t, -1), w_out[i], h)
        xn, slot_row, slot_col, gate_col, cnt = _route(h.reshape(b, s, d), norm_ffn_g[i], router_w[i], cap)
        delta = _expert_ffn(xn, slot_row, slot_col, gate_col, cnt, exp_w_gate[i], exp_w_up[i], exp_w_down[i], cap)
        h = _ple(h, delta.reshape(t, d), p[i].reshape(t, -1), ple_norm_g[i], ple_gate_w[i], ple_proj_w[i])
    return h.reshape(b, s, d)
```

```python
import functools

import numpy as np
import jax
import jax.numpy as jnp
from jax import lax
from jax.experimental import pallas as pl
from jax.experimental.pallas import tpu as pltpu

F32 = jnp.float32
BF16 = jnp.bfloat16

RMS_EPS = 1e-6
LN_EPS = 1e-5
NEG = -1e30
HEAD_DIM = 64
GRID_W = 64
A_GROUPS = 8
A_CHUNK = 128
B_HEADS = 16
B_KV_HEADS = 4
B_BLOCK = 128
C_HEADS = 16
C_WIN_ROWS = 8
C_WIN_COLS = 16
N_EXPERTS = 16
EC_CAPACITY_FACTOR = 2
N_MIXERS = 3

LANES = 128
MXU_DIM = 256
VMEM_LIMIT_BYTES = 60 * 1024 * 1024

TOKEN_TILE = 256
CHUNK = 16
GROUP = 2 * MXU_DIM
CHUNKS_PER_GROUP = GROUP // CHUNK


def _cparams(*sem):
    return pltpu.CompilerParams(dimension_semantics=sem, vmem_limit_bytes=VMEM_LIMIT_BYTES)


def _rms(xf, g):
    return xf * lax.rsqrt(jnp.mean(xf * xf, axis=-1, keepdims=True) + RMS_EPS) * g


def _dot(a, b):
    return jnp.dot(a, b, preferred_element_type=F32)


def _dot_nt(a, b):
    return lax.dot_general(a, b, (((1,), (1,)), ((), ())), preferred_element_type=F32)


def _full_spec(shape):
    nd = len(shape)
    return pl.BlockSpec(shape, lambda *_: (0,) * nd)


def _mixer_a_kernel(h_ref, g_ref, win_ref, vg_ref, ws_ref, bias_ref, wout_ref, o_ref, mix_ref):
    tm = h_ref.shape[0]
    width = vg_ref.shape[1]
    gw = width // A_GROUPS
    h = h_ref[...]
    xn = _rms(h, g_ref[...]).astype(BF16)
    z = _dot(xn, win_ref[...])
    z = 0.5 * z * (1.0 + jnp.tanh(np.sqrt(2.0 / np.pi).astype(np.float32)
                                  * (z + 0.044715 * (z * z * z))))
    u = z[:, :width]
    v = z[:, width:]
    mu = jnp.mean(v, axis=-1, keepdims=True)
    vc = v - mu
    var = jnp.mean(vc * vc, axis=-1, keepdims=True)
    vn = (vc * lax.rsqrt(var + LN_EPS) * vg_ref[...]).astype(BF16)
    for c in range(tm // A_CHUNK):
        rows = slice(c * A_CHUNK, (c + 1) * A_CHUNK)
        for g in range(A_GROUPS):
            cols = slice(g * gw, (g + 1) * gw)
            s = _dot(ws_ref[g], vn[rows, cols]) + bias_ref[:, cols]
            mix_ref[rows, cols] = (u[rows, cols] * s).astype(BF16)
    o_ref[...] = h + _dot(mix_ref[...], wout_ref[...])


def _mixer_a(h2, g, w_in, vnorm_g, w_s, b_s, w_out, tm=256):
    t, d = h2.shape
    width = vnorm_g.shape[0]
    gw = width // A_GROUPS
    bias = jnp.repeat(b_s.T.astype(F32), gw, axis=1)
    return pl.pallas_call(
        _mixer_a_kernel,
        out_shape=jax.ShapeDtypeStruct((t, d), F32),
        grid=(t // tm,),
        in_specs=[
            pl.BlockSpec((tm, d), lambda i: (i, 0)),
            _full_spec((1, d)),
            _full_spec((d, 2 * width)),
            _full_spec((1, width)),
            _full_spec((A_GROUPS, A_CHUNK, A_CHUNK)),
            _full_spec((A_CHUNK, width)),
            _full_spec((width, d)),
        ],
        out_specs=pl.BlockSpec((tm, d), lambda i: (i, 0)),
        scratch_shapes=[pltpu.VMEM((tm, width), BF16)],
        compiler_params=_cparams("parallel"),
        name="mixer_a",
    )(h2, g.reshape(1, d), w_in.astype(BF16), vnorm_g.reshape(1, width), w_s.astype(BF16), bias,
      w_out.astype(BF16))


def _norm_proj_kernel(h_ref, g_ref, w_ref, hg_ref, bd_ref, o_ref, *, n_norm_cols):
    xn = _rms(h_ref[...], g_ref[...]).astype(BF16)
    acc = _dot(xn, w_ref[...])
    for j in range(n_norm_cols // MXU_DIM):
        cols = slice(j * MXU_DIM, (j + 1) * MXU_DIM)
        blk = acc[:, cols]
        ss = _dot((blk * blk).astype(BF16), bd_ref[...])
        o_ref[:, cols] = (blk * lax.rsqrt(ss * (1.0 / HEAD_DIM) + RMS_EPS) * hg_ref[:, cols]).astype(BF16)
    o_ref[:, n_norm_cols:] = acc[:, n_norm_cols:].astype(BF16)


def _norm_proj(h2, g, w, head_gain, n_norm_cols, tm=512):
    t, d = h2.shape
    n = w.shape[1]
    blockdiag = jnp.asarray(np.kron(np.eye(MXU_DIM // HEAD_DIM), np.ones((HEAD_DIM, HEAD_DIM))), BF16)
    return pl.pallas_call(
        functools.partial(_norm_proj_kernel, n_norm_cols=n_norm_cols),
        out_shape=jax.ShapeDtypeStruct((t, n), BF16),
        grid=(t // tm,),
        in_specs=[
            pl.BlockSpec((tm, d), lambda i: (i, 0)),
            _full_spec((1, d)),
            _full_spec((d, n)),
            _full_spec((1, n_norm_cols)),
            _full_spec((MXU_DIM, MXU_DIM)),
        ],
        out_specs=pl.BlockSpec((tm, n), lambda i: (i, 0)),
        compiler_params=_cparams("parallel"),
        name="norm_proj",
    )(h2, g.reshape(1, d), w.astype(BF16), head_gain.reshape(1, n_norm_cols).astype(F32), blockdiag)


def _proj_residual_kernel(a_ref, w_ref, h_ref, o_ref):
    o_ref[...] = h_ref[...] + _dot(a_ref[...], w_ref[...])


def _proj_residual(a2, w, h2, tm=512):
    t, d = h2.shape
    k = a2.shape[1]
    return pl.pallas_call(
        _proj_residual_kernel,
        out_shape=jax.ShapeDtypeStruct((t, d), F32),
        grid=(t // tm,),
        in_specs=[
            pl.BlockSpec((tm, k), lambda i: (i, 0)),
            _full_spec((k, d)),
            pl.BlockSpec((tm, d), lambda i: (i, 0)),
        ],
        out_specs=pl.BlockSpec((tm, d), lambda i: (i, 0)),
        compiler_params=_cparams("parallel"),
        name="proj_residual",
    )(a2, w.astype(BF16), h2)


def _alibi_slopes(n):
    return np.array([2.0 ** (-8.0 * (h + 1) / n) for h in range(n)], dtype=np.float32)


def _attn_b_tables():
    span = 3 * B_BLOCK
    rel = np.arange(span)[None, :] - B_BLOCK - np.arange(B_BLOCK)[:, None]
    in_window = np.abs(rel) <= B_BLOCK
    alibi = (-_alibi_slopes(B_HEADS)[:, None, None] * np.abs(rel)[None]).astype(np.float32)
    kblk = np.arange(span) // B_BLOCK
    tabs = []
    for kind in range(3):
        valid = in_window & ~((kind == 0) & (kblk == 0))[None, :] & ~((kind == 2) & (kblk == 2))[None, :]
        tabs.append(np.where(valid[None], alibi, np.float32(NEG)))
    return np.stack(tabs).astype(np.float32)


def _pair_split(x, lane, own_half):
    rolled = pltpu.roll(x, HEAD_DIM, axis=1)
    zero = jnp.zeros_like(x)
    if own_half == 0:
        return jnp.where(lane < HEAD_DIM, x, zero), jnp.where(lane >= HEAD_DIM, rolled, zero)
    return jnp.where(lane < HEAD_DIM, rolled, zero), jnp.where(lane >= HEAD_DIM, x, zero)


def _attn_b_kernel(sink_ref, q_ref, kp_ref, kc_ref, kn_ref, vp_ref, vc_ref, vn_ref, tab_ref, o_ref):
    span = 3 * B_BLOCK
    grp = B_HEADS // B_KV_HEADS
    lane = lax.broadcasted_iota(jnp.int32, (1, LANES), 1)
    for kh in range(B_KV_HEADS):
        tile = slice((kh // 2) * LANES, (kh // 2 + 1) * LANES)
        kt = jnp.concatenate([kp_ref[:, tile], kc_ref[:, tile], kn_ref[:, tile]], axis=0)
        vt = jnp.concatenate([vp_ref[:, tile], vc_ref[:, tile], vn_ref[:, tile]], axis=0)
        k_lo, k_hi = _pair_split(kt.astype(F32), lane, kh % 2)
        v_lo, v_hi = _pair_split(vt.astype(F32), lane, kh % 2)
        kk = jnp.concatenate([k_lo, k_hi], axis=0).astype(BF16)
        vv = jnp.concatenate([v_lo, v_hi], axis=0).astype(BF16)
        for half in range(grp // 2):
            qt = kh * (grp // 2) + half
            cols = slice(qt * LANES, (qt + 1) * LANES)
            s = _dot_nt(q_ref[:, cols], kk)
            ps, ds = [], []
            for ch in range(2):
                head = 2 * qt + ch
                sh = s[:, ch * span:(ch + 1) * span] + tab_ref[head]
                sink = sink_ref[head]
                m = jnp.maximum(jnp.max(sh, axis=-1, keepdims=True), sink)
                pe = jnp.exp(sh - m)
                ds.append(jnp.sum(pe, axis=-1, keepdims=True) + jnp.exp(sink - m))
                ps.append(pe.astype(BF16))
            o2 = _dot(jnp.concatenate(ps, axis=1), vv)
            o_ref[:, cols] = (o2 / jnp.where(lane < HEAD_DIM, ds[0], ds[1])).astype(o_ref.dtype)


def _attn_b(qkv3, sink):
    b, s, _ = qkv3.shape
    nb = s // B_BLOCK
    qw = B_HEADS * HEAD_DIM
    kvw = B_KV_HEADS * HEAD_DIM
    kcol = qw // kvw
    vcol = kcol + 1
    tabs = jnp.asarray(_attn_b_tables())
    prev = lambda n: jnp.maximum(n - 1, 0)
    nxt = lambda n: jnp.minimum(n + 1, nb - 1)
    kind = lambda n: jnp.where(n == 0, 0, jnp.where(n == nb - 1, 2, 1))
    kv = lambda col, f: pl.BlockSpec((None, B_BLOCK, kvw), lambda bi, n: (bi, f(n), col))
    same = lambda n: n
    return pl.pallas_call(
        _attn_b_kernel,
        out_shape=jax.ShapeDtypeStruct((b, s, qw), BF16),
        grid=(b, nb),
        in_specs=[
            pl.BlockSpec(memory_space=pltpu.SMEM),
            pl.BlockSpec((None, B_BLOCK, qw), lambda bi, n: (bi, n, 0)),
            kv(kcol, prev), kv(kcol, same), kv(kcol, nxt),
            kv(vcol, prev), kv(vcol, same), kv(vcol, nxt),
            pl.BlockSpec((None, B_HEADS, B_BLOCK, 3 * B_BLOCK), lambda bi, n: (kind(n), 0, 0, 0)),
        ],
        out_specs=pl.BlockSpec((None, B_BLOCK, qw), lambda bi, n: (bi, n, 0)),
        compiler_params=_cparams("parallel", "arbitrary"),
        name="attn_b",
    )(sink.astype(F32), qkv3, qkv3, qkv3, qkv3, qkv3, qkv3, qkv3, tabs)


def _attn_c_table(rpb, rows):
    w = GRID_W
    kr_n = min(C_WIN_ROWS, rows)
    c = np.arange(w)
    cs = np.clip(c - C_WIN_COLS // 2, 0, w - C_WIN_COLS)
    kcol = np.arange(w)
    colmask = (kcol[None, :] >= cs[:, None]) & (kcol[None, :] < cs[:, None] + C_WIN_COLS)
    coloff = np.clip(kcol[None, :] - c[:, None] + C_WIN_COLS - 1, 0, 2 * C_WIN_COLS - 2)
    n_off = 2 * C_WIN_ROWS - kr_n
    row_idx = np.arange(n_off)[:, None] + np.arange(kr_n)[None, :]
    bias = rpb.astype(F32)[:, row_idx][:, :, :, coloff]
    bias = jnp.where(jnp.asarray(colmask)[None, None, None], bias, NEG)
    return bias.transpose(1, 0, 3, 2, 4).reshape(n_off, rpb.shape[0] // 2, 2 * w, kr_n * w)


def _attn_c_kernel(q_ref, k_ref, v_ref, tab_ref, o_ref):
    nk = k_ref.shape[1] * k_ref.shape[2]
    nq = q_ref.shape[0]
    lane = lax.broadcasted_iota(jnp.int32, (1, LANES), 1)
    low = lane < HEAD_DIM
    scores = []
    for t in range(C_HEADS // 2):
        cols = slice(t * LANES, (t + 1) * LANES)
        q2 = q_ref[:, cols]
        zero = jnp.zeros_like(q2)
        qq = jnp.concatenate([jnp.where(low, q2, zero), jnp.where(low, zero, q2)], axis=0)
        kt = k_ref[0, :, :, cols].reshape(nk, LANES)
        scores.append(_dot_nt(qq, kt) + tab_ref[t])
    sh = jnp.concatenate(scores, axis=0)
    pe = jnp.exp(sh - jnp.max(sh, axis=-1, keepdims=True))
    denom = jnp.sum(pe, axis=-1, keepdims=True)
    pe = pe.astype(BF16)
    for t in range(C_HEADS // 2):
        cols = slice(t * LANES, (t + 1) * LANES)
        rows = slice(2 * t * nq, 2 * (t + 1) * nq)
        vt = v_ref[0, :, :, cols].reshape(nk, LANES)
        o = _dot(pe[rows], vt) / denom[rows]
        o_ref[:, cols] = jnp.where(low, o[:nq], o[nq:]).astype(o_ref.dtype)


def _attn_c(qkv4, rpb):
    b, rows, w, _ = qkv4.shape
    hw = C_HEADS * HEAD_DIM
    kr_n = min(C_WIN_ROWS, rows)
    table = _attn_c_table(rpb, rows)
    row_start = lambda r: jnp.clip(r - kr_n // 2, 0, rows - kr_n)
    el = pl.Element
    return pl.pallas_call(
        _attn_c_kernel,
        out_shape=jax.ShapeDtypeStruct((b, rows, w, hw), BF16),
        grid=(rows, b),
        in_specs=[
            pl.BlockSpec((None, None, w, hw), lambda r, bi: (bi, r, 0, 0)),
            pl.BlockSpec((el(1), el(kr_n), el(w), el(hw)), lambda r, bi: (bi, row_start(r), 0, hw)),
            pl.BlockSpec((el(1), el(kr_n), el(w), el(hw)), lambda r, bi: (bi, row_start(r), 0, 2 * hw)),
            pl.BlockSpec((None, C_HEADS // 2, 2 * w, kr_n * w),
                         lambda r, bi: (row_start(r) - r + C_WIN_ROWS - 1, 0, 0, 0)),
        ],
        out_specs=pl.BlockSpec((None, None, w, hw), lambda r, bi: (bi, r, 0, 0)),
        compiler_params=_cparams("arbitrary", "arbitrary"),
        name="attn_c",
    )(qkv4, qkv4, qkv4, table)


def _lane_cumsum(x, tri):
    e, s = x.shape
    off = jnp.zeros((e, 1), F32)
    outs = []
    for j in range(s // LANES):
        blk = x[:, j * LANES:(j + 1) * LANES]
        outs.append(_dot(blk.astype(BF16), tri) + off)
        off = off + jnp.sum(blk, axis=1, keepdims=True)
    return jnp.concatenate(outs, axis=1)


def _route_kernel(h_ref, g_ref, rwt_ref, tri_ref, xn_ref, slotr_ref, slotc_ref, gatec_ref, cnt_ref,
                  *, cap, chunk, tile):
    s = h_ref.shape[0]
    n_e = rwt_ref.shape[0] // 2
    logits = []
    for c in range(s // chunk):
        rows = slice(c * chunk, (c + 1) * chunk)
        xn = _rms(h_ref[rows, :], g_ref[...])
        xn_hi = xn.astype(BF16)
        xn_ref[rows, :] = xn_hi
        xn_lo = (xn - xn_hi.astype(F32)).astype(BF16)
        both = _dot_nt(rwt_ref[...], xn_hi)
        logits.append(both[:n_e] + both[n_e:] + _dot_nt(rwt_ref[:n_e, :], xn_lo))
    lg = jnp.concatenate(logits, axis=1)
    ex = jnp.exp(lg - jnp.max(lg, axis=0, keepdims=True))
    aff = ex / jnp.sum(ex, axis=0, keepdims=True)
    bits = pltpu.bitcast(aff, jnp.int32)
    thr = jnp.zeros((n_e, 1), jnp.int32)
    for shift in range(27, -1, -3):
        digit = jnp.zeros((n_e, 1), jnp.int32)
        for j in range(1, 8):
            cnt = jnp.sum(jnp.where(bits >= (thr | (j << shift)), 1.0, 0.0), axis=1, keepdims=True)
            digit = digit + jnp.where(cnt >= cap, 1, 0)
        thr = thr | (digit * (1 << shift))
    gt = bits > thr
    eq = bits == thr
    need = cap - jnp.sum(jnp.where(gt, 1.0, 0.0), axis=1, keepdims=True)
    eqf = jnp.where(eq, 1.0, 0.0)
    eq_rank = _lane_cumsum(eqf, tri_ref[...]) - eqf
    sel = gt | (eq & (eq_rank < need))
    self_ = jnp.where(sel, 1.0, 0.0)
    slot = jnp.where(sel, _lane_cumsum(self_, tri_ref[...]) - self_, -1.0)
    lane = lax.broadcasted_iota(jnp.int32, (n_e, LANES), 1)
    counts = jnp.zeros((n_e, LANES), F32)
    for i in range(s // tile):
        counts = jnp.where(lane == i, jnp.sum(self_[:, i * tile:(i + 1) * tile], axis=1, keepdims=True), counts)
    cnt_ref[...] = counts.astype(jnp.int32)
    pad_rows = LANES - n_e
    slot_p = jnp.concatenate([slot, jnp.full((pad_rows, s), -1.0, F32)], axis=0)
    gate_p = jnp.concatenate([jnp.where(sel, aff, 0.0), jnp.zeros((pad_rows, s), F32)], axis=0)
    slotr_ref[...] = slot_p.astype(BF16)
    for j in range(s // LANES):
        cols = slice(j * LANES, (j + 1) * LANES)
        slotc_ref[cols, :] = slot_p[:, cols].T.astype(BF16)
        gatec_ref[cols, :] = gate_p[:, cols].T.astype(BF16)


def _route(h3, g, router_w, cap):
    b, s, d = h3.shape
    n_e = router_w.shape[1]
    tri = jnp.asarray(np.triu(np.ones((LANES, LANES))), BF16)
    rw_t = router_w.T.astype(F32)
    rw_hi = rw_t.astype(BF16)
    rw_lo = (rw_t - rw_hi.astype(F32)).astype(BF16)
    per_seq = lambda *shape: pl.BlockSpec((None,) + shape, lambda i: (i,) + (0,) * len(shape))
    return pl.pallas_call(
        functools.partial(_route_kernel, cap=cap, chunk=256, tile=TOKEN_TILE),
        out_shape=(
            jax.ShapeDtypeStruct((b, s, d), BF16),
            jax.ShapeDtypeStruct((b, LANES, s), BF16),
            jax.ShapeDtypeStruct((b, s, LANES), BF16),
            jax.ShapeDtypeStruct((b, s, LANES), BF16),
            jax.ShapeDtypeStruct((b, n_e, LANES), jnp.int32),
        ),
        grid=(b,),
        in_specs=[per_seq(s, d), _full_spec((1, d)), _full_spec((2 * n_e, d)), _full_spec((LANES, LANES))],
        out_specs=(per_seq(s, d), per_seq(LANES, s), per_seq(s, LANES), per_seq(s, LANES), per_seq(n_e, LANES)),
        compiler_params=_cparams("parallel"),
        name="route",
    )(h3, g.reshape(1, d), jnp.concatenate([rw_hi, rw_lo], axis=0), tri)


def _chunk_tables(cnt, cap):
    n_e = cnt.shape[1]
    kmax = n_e * (TOKEN_TILE // CHUNK + 1)
    kmax = -(-kmax // CHUNKS_PER_GROUP) * CHUNKS_PER_GROUP
    c0 = jnp.cumsum(cnt, axis=2) - cnt
    a0 = c0 // CHUNK
    a1 = jnp.where(cnt > 0, (c0 + cnt + CHUNK - 1) // CHUNK, a0)
    nch = (a1 - a0).transpose(0, 2, 1)
    a0 = a0.transpose(0, 2, 1)
    pos_end = jnp.cumsum(nch, axis=2)
    pos = pos_end - nch
    total = pos_end[..., -1]
    k = jnp.arange(kmax, dtype=jnp.int32)
    e_of_k = jnp.sum((k[None, None, :, None] >= pos_end[:, :, None, :]).astype(jnp.int32), axis=-1)
    e_of_k = jnp.minimum(e_of_k, n_e - 1)
    is_e = e_of_k[..., None] == jnp.arange(n_e, dtype=jnp.int32)
    q = k + jnp.sum(jnp.where(is_e, (a0 - pos)[:, :, None, :], 0), axis=-1)
    valid = k[None, None, :] < total[..., None]
    e_tab = jnp.where(valid, e_of_k, 0)
    slot_tab = jnp.where(valid, q * CHUNK, -(1 << 20))
    dst_tab = jnp.where(valid, e_of_k * cap + q * CHUNK, 0)
    groups = (total + CHUNKS_PER_GROUP - 1) // CHUNKS_PER_GROUP
    return jnp.concatenate([e_tab, slot_tab, dst_tab, groups[..., None]], axis=-1).astype(jnp.int32), kmax


def _gather_kernel(tab_ref, xn_ref, slotr_ref, xs_ref, *, kmax):
    @pl.when(pl.program_id(1) == 0)
    def _():
        xs_ref[...] = jnp.zeros_like(xs_ref)

    lane = lax.broadcasted_iota(jnp.int32, (CHUNK, LANES), 1)
    sub = lax.broadcasted_iota(jnp.int32, (CHUNK, LANES), 0)

    def group(g, carry):
        base = g * CHUNKS_PER_GROUP
        pick, want = [], []
        for c in range(CHUNKS_PER_GROUP):
            pick.append(jnp.where(lane == tab_ref[0, base + c], 1.0, 0.0).astype(BF16))
            want.append((sub + tab_ref[0, kmax + base + c]).astype(F32))
        slot_b = _dot(jnp.concatenate(pick, axis=0), slotr_ref[...])
        want = jnp.concatenate(want, axis=0)
        want = jnp.concatenate([want] * (slot_b.shape[1] // LANES), axis=1)
        onehot = jnp.where(slot_b == want, 1.0, 0.0).astype(BF16)
        rows = _dot(onehot, xn_ref[...]).astype(BF16)
        dsts = [pl.ds(pl.multiple_of(tab_ref[0, 2 * kmax + base + c], CHUNK), CHUNK) for c in range(CHUNKS_PER_GROUP)]
        sums = [xs_ref[dsts[c], :] + rows[c * CHUNK:(c + 1) * CHUNK, :] for c in range(CHUNKS_PER_GROUP)]
        for c in reversed(range(CHUNKS_PER_GROUP)):
            xs_ref[dsts[c], :] = sums[c]
        return carry

    lax.fori_loop(0, tab_ref[0, 3 * kmax], group, 0)


def _expert_kernel(xs_ref, wg_ref, wu_ref, wd_ref, y_ref):
    nb, cap, d = xs_ref.shape
    xs = xs_ref[...].reshape(nb * cap, d)
    hg = _dot(xs, wg_ref[...])
    hu = _dot(xs, wu_ref[...])
    hdn = (hg * jax.nn.sigmoid(hg) * hu).astype(BF16)
    y_ref[...] = _dot(hdn, wd_ref[...]).astype(BF16).reshape(nb, cap, d)


def _scatter_kernel(tab_ref, y_ref, slotc_ref, gatec_ref, o_ref, yg_ref, *, kmax):
    o_ref[...] = jnp.zeros_like(o_ref)
    lane = lax.broadcasted_iota(jnp.int32, (1, GROUP), 1)
    sub = lax.broadcasted_iota(jnp.int32, (LANES, GROUP), 0)

    def group(g, carry):
        base = g * CHUNKS_PER_GROUP
        expert = jnp.zeros((1, GROUP), jnp.int32)
        want = jnp.zeros((1, GROUP), jnp.int32)
        for c in range(CHUNKS_PER_GROUP):
            in_chunk = (lane >= c * CHUNK) & (lane < (c + 1) * CHUNK)
            expert = jnp.where(in_chunk, tab_ref[0, base + c], expert)
            want = jnp.where(in_chunk, tab_ref[0, kmax + base + c] + lane - c * CHUNK, want)
            src = pl.ds(pl.multiple_of(tab_ref[0, 2 * kmax + base + c], CHUNK), CHUNK)
            yg_ref[c * CHUNK:(c + 1) * CHUNK, :] = y_ref[src, :]
        pick = jnp.where(sub == expert, 1.0, 0.0).astype(BF16)
        slot_b = _dot(slotc_ref[...], pick)
        gate_b = _dot(gatec_ref[...], pick)
        weights = jnp.where(slot_b == want.astype(F32), gate_b, 0.0).astype(BF16)
        o_ref[...] += _dot(weights, yg_ref[...])
        return carry

    lax.fori_loop(0, tab_ref[0, 3 * kmax], group, 0)


def _expert_ffn(xn3, slot_row, slot_col, gate_col, cnt, w_gate, w_up, w_down, cap, seqs_per_step=4):
    b, s, d = xn3.shape
    n_e, _, ff = w_gate.shape
    nt = s // TOKEN_TILE
    tables, kmax = _chunk_tables(cnt[:, :, :nt], cap)
    tables = tables.reshape(b * nt, 1, 3 * kmax + 1)
    tab_spec = pl.BlockSpec((None, 1, 3 * kmax + 1), lambda i, t: (i * nt + t, 0, 0), memory_space=pltpu.SMEM)
    xs = pl.pallas_call(
        functools.partial(_gather_kernel, kmax=kmax),
        out_shape=jax.ShapeDtypeStruct((b, n_e * cap, d), BF16),
        grid=(b, nt),
        in_specs=[
            tab_spec,
            pl.BlockSpec((None, TOKEN_TILE, d), lambda i, t: (i, t, 0)),
            pl.BlockSpec((None, LANES, TOKEN_TILE), lambda i, t: (i, 0, t)),
        ],
        out_specs=pl.BlockSpec((None, n_e * cap, d), lambda i, t: (i, 0, 0)),
        compiler_params=_cparams("parallel", "arbitrary"),
        name="expert_gather",
    )(tables, xn3, slot_row)
    nb = seqs_per_step
    y = pl.pallas_call(
        _expert_kernel,
        out_shape=jax.ShapeDtypeStruct((b, n_e, cap, d), BF16),
        grid=(n_e, b // nb),
        in_specs=[
            pl.BlockSpec((nb, None, cap, d), lambda e, i: (i, e, 0, 0)),
            pl.BlockSpec((None, d, ff), lambda e, i: (e, 0, 0)),
            pl.BlockSpec((None, d, ff), lambda e, i: (e, 0, 0)),
            pl.BlockSpec((None, ff, d), lambda e, i: (e, 0, 0)),
        ],
        out_specs=pl.BlockSpec((nb, None, cap, d), lambda e, i: (i, e, 0, 0)),
        compiler_params=_cparams("parallel", "parallel"),
        name="expert_mlp",
    )(xs.reshape(b, n_e, cap, d), w_gate.astype(BF16), w_up.astype(BF16), w_down.astype(BF16))
    return pl.pallas_call(
        functools.partial(_scatter_kernel, kmax=kmax),
        out_shape=jax.ShapeDtypeStruct((b, s, d), F32),
        grid=(b, nt),
        in_specs=[
            tab_spec,
            pl.BlockSpec((None, n_e * cap, d), lambda i, t: (i, 0, 0)),
            pl.BlockSpec((None, TOKEN_TILE, LANES), lambda i, t: (i, t, 0)),
            pl.BlockSpec((None, TOKEN_TILE, LANES), lambda i, t: (i, t, 0)),
        ],
        out_specs=pl.BlockSpec((None, TOKEN_TILE, d), lambda i, t: (i, t, 0)),
        scratch_shapes=[pltpu.VMEM((GROUP, d), BF16)],
        compiler_params=_cparams("parallel", "arbitrary"),
        name="expert_scatter",
    )(tables, y.reshape(b, n_e * cap, d), slot_col, gate_col)


def _ple_kernel(h_ref, d_ref, p_ref, g_ref, wg_ref, wp_ref, o_ref):
    h = h_ref[...] + d_ref[...]
    xn = _rms(h, g_ref[...]).astype(BF16)
    gate = jax.nn.sigmoid(_dot(xn, wg_ref[...]))
    o_ref[...] = h + gate * _dot(p_ref[...].astype(BF16), wp_ref[...])


def _ple(h2, delta2, p3, layer, g, w_gate, w_proj, tm=512):
    t, d = h2.shape
    pd = p3.shape[2]
    row = lambda width: pl.BlockSpec((tm, width), lambda i: (i, 0))
    return pl.pallas_call(
        _ple_kernel,
        out_shape=jax.ShapeDtypeStruct((t, d), F32),
        grid=(t // tm,),
        in_specs=[row(d), row(d), pl.BlockSpec((None, tm, pd), lambda i: (layer, i, 0)),
                  _full_spec((1, d)), _full_spec((d, d)), _full_spec((pd, d))],
        out_specs=row(d),
        compiler_params=_cparams("parallel"),
        name="ple",
    )(h2, delta2, p3, g.reshape(1, d), w_gate.astype(BF16), w_proj.astype(BF16))


def kernel(x, p, norm_mix_g, norm_ffn_g, w_out, router_w, exp_w_gate, exp_w_up, exp_w_down, ple_norm_g, ple_gate_w, ple_proj_w, a_w_in, a_vnorm_g, a_w_s, a_b_s, b_w_in, b_qnorm_g, b_knorm_g, b_sink, c_w_in, c_qnorm_g, c_knorm_g, c_rpb):
    b, s, d = x.shape
    depth = norm_mix_g.shape[0]
    t = b * s
    cap = max(1, EC_CAPACITY_FACTOR * s // N_EXPERTS)
    scale = HEAD_DIM ** -0.5
    h = x.reshape(t, d)
    for i in range(depth):
        kind = i % N_MIXERS
        j = i // N_MIXERS
        if kind == 0:
            h = _mixer_a(h, norm_mix_g[i], a_w_in[j], a_vnorm_g[j], a_w_s[j], a_b_s[j], w_out[i])
        elif kind == 1:
            gain = jnp.concatenate([jnp.tile(b_qnorm_g[j] * scale, B_HEADS), jnp.tile(b_knorm_g[j], B_KV_HEADS)])
            qkv = _norm_proj(h, norm_mix_g[i], b_w_in[j], gain, (B_HEADS + B_KV_HEADS) * HEAD_DIM)
            mix = _attn_b(qkv.reshape(b, s, -1), b_sink[j])
            h = _proj_residual(mix.reshape(t, -1), w_out[i], h)
        else:
            gain = jnp.concatenate([jnp.tile(c_qnorm_g[j] * scale, C_HEADS), jnp.tile(c_knorm_g[j], C_HEADS)])
            qkv = _norm_proj(h, norm_mix_g[i], c_w_in[j], gain, 2 * C_HEADS * HEAD_DIM)
            mix = _attn_c(qkv.reshape(b, s // GRID_W, GRID_W, -1), c_rpb[j])
            h = _proj_residual(mix.reshape(t, -1), w_out[i], h)
        xn, slot_row, slot_col, gate_col, cnt = _route(h.reshape(b, s, d), norm_ffn_g[i], router_w[i], cap)
        delta = _expert_ffn(xn, slot_row, slot_col, gate_col, cnt, exp_w_gate[i], exp_w_up[i], exp_w_down[i], cap)
        h = _ple(h, delta.reshape(t, d), p.reshape(depth, t, -1), i, ple_norm_g[i], ple_gate_w[i], ple_proj_w[i])
    return h.reshape(b, s, d)
```

```python
import functools

import numpy as np
import jax
import jax.numpy as jnp
from jax import lax
from jax.experimental import pallas as pl
from jax.experimental.pallas import tpu as pltpu

F32 = jnp.float32
BF16 = jnp.bfloat16

RMS_EPS = 1e-6
LN_EPS = 1e-5
NEG = -1e30
HEAD_DIM = 64
GRID_W = 64
A_GROUPS = 8
A_CHUNK = 128
B_HEADS = 16
B_KV_HEADS = 4
B_BLOCK = 128
C_HEADS = 16
C_WIN_ROWS = 8
C_WIN_COLS = 16
N_EXPERTS = 16
EC_CAPACITY_FACTOR = 2
N_MIXERS = 3

LANES = 128
MXU_DIM = 256
VMEM_LIMIT_BYTES = 60 * 1024 * 1024

TOKEN_TILE = 256
CHUNK = 16
GROUP = 2 * MXU_DIM
CHUNKS_PER_GROUP = GROUP // CHUNK


def _cparams(*sem):
    return pltpu.CompilerParams(dimension_semantics=sem, vmem_limit_bytes=VMEM_LIMIT_BYTES)


def _rms(xf, g):
    return xf * lax.rsqrt(jnp.mean(xf * xf, axis=-1, keepdims=True) + RMS_EPS) * g


def _dot(a, b):
    return jnp.dot(a, b, preferred_element_type=F32)


def _dot_nt(a, b):
    return lax.dot_general(a, b, (((1,), (1,)), ((), ())), preferred_element_type=F32)


def _full_spec(shape):
    nd = len(shape)
    return pl.BlockSpec(shape, lambda *_: (0,) * nd)


def _mixer_a_kernel(h_ref, g_ref, win_ref, vg_ref, ws_ref, bias_ref, wout_ref, o_ref, mix_ref):
    tm = h_ref.shape[0]
    width = vg_ref.shape[1]
    gw = width // A_GROUPS
    h = h_ref[...]
    xn = _rms(h, g_ref[...]).astype(BF16)
    z = _dot(xn, win_ref[...])
    z = 0.5 * z * (1.0 + jnp.tanh(np.sqrt(2.0 / np.pi).astype(np.float32)
                                  * (z + 0.044715 * (z * z * z))))
    u = z[:, :width]
    v = z[:, width:]
    mu = jnp.mean(v, axis=-1, keepdims=True)
    vc = v - mu
    var = jnp.mean(vc * vc, axis=-1, keepdims=True)
    vn = (vc * lax.rsqrt(var + LN_EPS) * vg_ref[...]).astype(BF16)
    for c in range(tm // A_CHUNK):
        rows = slice(c * A_CHUNK, (c + 1) * A_CHUNK)
        for g in range(A_GROUPS):
            cols = slice(g * gw, (g + 1) * gw)
            s = _dot(ws_ref[g], vn[rows, cols]) + bias_ref[:, cols]
            mix_ref[rows, cols] = (u[rows, cols] * s).astype(BF16)
    o_ref[...] = h + _dot(mix_ref[...], wout_ref[...])


def _mixer_a(h2, g, w_in, vnorm_g, w_s, b_s, w_out, tm=256):
    t, d = h2.shape
    width = vnorm_g.shape[0]
    gw = width // A_GROUPS
    bias = jnp.repeat(b_s.T.astype(F32), gw, axis=1)
    return pl.pallas_call(
        _mixer_a_kernel,
        out_shape=jax.ShapeDtypeStruct((t, d), F32),
        grid=(t // tm,),
        in_specs=[
            pl.BlockSpec((tm, d), lambda i: (i, 0)),
            _full_spec((1, d)),
            _full_spec((d, 2 * width)),
            _full_spec((1, width)),
            _full_spec((A_GROUPS, A_CHUNK, A_CHUNK)),
            _full_spec((A_CHUNK, width)),
            _full_spec((width, d)),
        ],
        out_specs=pl.BlockSpec((tm, d), lambda i: (i, 0)),
        scratch_shapes=[pltpu.VMEM((tm, width), BF16)],
        compiler_params=_cparams("parallel"),
        name="mixer_a",
    )(h2, g.reshape(1, d), w_in.astype(BF16), vnorm_g.reshape(1, width), w_s.astype(BF16), bias,
      w_out.astype(BF16))


def _norm_proj_kernel(h_ref, g_ref, w_ref, hg_ref, bd_ref, o_ref, *, n_norm_cols):
    xn = _rms(h_ref[...], g_ref[...]).astype(BF16)
    acc = _dot(xn, w_ref[...])
    for j in range(n_norm_cols // MXU_DIM):
        cols = slice(j * MXU_DIM, (j + 1) * MXU_DIM)
        blk = acc[:, cols]
        ss = _dot((blk * blk).astype(BF16), bd_ref[...])
        o_ref[:, cols] = (blk * lax.rsqrt(ss * (1.0 / HEAD_DIM) + RMS_EPS) * hg_ref[:, cols]).astype(BF16)
    o_ref[:, n_norm_cols:] = acc[:, n_norm_cols:].astype(BF16)


def _norm_proj(h2, g, w, head_gain, n_norm_cols, tm=512):
    t, d = h2.shape
    n = w.shape[1]
    blockdiag = jnp.asarray(np.kron(np.eye(MXU_DIM // HEAD_DIM), np.ones((HEAD_DIM, HEAD_DIM))), BF16)
    return pl.pallas_call(
        functools.partial(_norm_proj_kernel, n_norm_cols=n_norm_cols),
        out_shape=jax.ShapeDtypeStruct((t, n), BF16),
        grid=(t // tm,),
        in_specs=[
            pl.BlockSpec((tm, d), lambda i: (i, 0)),
            _full_spec((1, d)),
            _full_spec((d, n)),
            _full_spec((1, n_norm_cols)),
            _full_spec((MXU_DIM, MXU_DIM)),
        ],
        out_specs=pl.BlockSpec((tm, n), lambda i: (i, 0)),
        compiler_params=_cparams("parallel"),
        name="norm_proj",
    )(h2, g.reshape(1, d), w.astype(BF16), head_gain.reshape(1, n_norm_cols).astype(F32), blockdiag)


def _proj_residual_kernel(a_ref, w_ref, h_ref, o_ref):
    o_ref[...] = h_ref[...] + _dot(a_ref[...], w_ref[...])


def _proj_residual(a2, w, h2, tm=512):
    t, d = h2.shape
    k = a2.shape[1]
    return pl.pallas_call(
        _proj_residual_kernel,
        out_shape=jax.ShapeDtypeStruct((t, d), F32),
        grid=(t // tm,),
        in_specs=[
            pl.BlockSpec((tm, k), lambda i: (i, 0)),
            _full_spec((k, d)),
            pl.BlockSpec((tm, d), lambda i: (i, 0)),
        ],
        out_specs=pl.BlockSpec((tm, d), lambda i: (i, 0)),
        compiler_params=_cparams("parallel"),
        name="proj_residual",
    )(a2, w.astype(BF16), h2)


def _alibi_slopes(n):
    return np.array([2.0 ** (-8.0 * (h + 1) / n) for h in range(n)], dtype=np.float32)


def _attn_b_tables():
    span = 3 * B_BLOCK
    rel = np.arange(span)[None, :] - B_BLOCK - np.arange(B_BLOCK)[:, None]
    in_window = np.abs(rel) <= B_BLOCK
    alibi = (-_alibi_slopes(B_HEADS)[:, None, None] * np.abs(rel)[None]).astype(np.float32)
    kblk = np.arange(span) // B_BLOCK
    tabs = []
    for kind in range(3):
        valid = in_window & ~((kind == 0) & (kblk == 0))[None, :] & ~((kind == 2) & (kblk == 2))[None, :]
        tabs.append(np.where(valid[None], alibi, np.float32(NEG)))
    return np.stack(tabs).astype(np.float32)


def _attn_b_kernel(sink_ref, q_ref, kp_ref, kc_ref, kn_ref, vp_ref, vc_ref, vn_ref, tab_ref, o_ref):
    grp = B_HEADS // B_KV_HEADS
    nq = q_ref.shape[0]
    lane = lax.broadcasted_iota(jnp.int32, (1, LANES), 1)
    low = lane < HEAD_DIM
    scores = []
    for kh in range(B_KV_HEADS):
        tile = slice(kh * LANES, (kh + 1) * LANES)
        kt = jnp.concatenate([kp_ref[:, tile], kc_ref[:, tile], kn_ref[:, tile]], axis=0)
        rows = []
        for qt in range(kh * grp // 2, (kh + 1) * grp // 2):
            q2 = q_ref[:, qt * LANES:(qt + 1) * LANES]
            zero = jnp.zeros_like(q2)
            rows += [jnp.where(low, q2, zero), jnp.where(low, zero, q2)]
        scores.append(_dot_nt(jnp.concatenate(rows, axis=0), kt) + tab_ref[kh])
    sh = jnp.concatenate(scores, axis=0)
    heads = [slice(h * nq, (h + 1) * nq) for h in range(B_HEADS)]
    rowmax = jnp.broadcast_to(jnp.max(sh, axis=-1, keepdims=True), (sh.shape[0], LANES))
    m = jnp.concatenate([jnp.maximum(rowmax[hs], sink_ref[h]) for h, hs in enumerate(heads)], axis=0)
    sink_term = jnp.concatenate([jnp.exp(sink_ref[h] - m[hs]) for h, hs in enumerate(heads)], axis=0)
    pe = jnp.exp(sh - jnp.concatenate([m] * (sh.shape[1] // LANES), axis=1)).astype(BF16)
    ones = jnp.ones((sh.shape[1], LANES), BF16)
    for kh in range(B_KV_HEADS):
        tile = slice(kh * LANES, (kh + 1) * LANES)
        grows = slice(kh * grp * nq, (kh + 1) * grp * nq)
        vt = jnp.concatenate([vp_ref[:, tile], vc_ref[:, tile], vn_ref[:, tile]], axis=0)
        od = _dot(pe[grows], jnp.concatenate([vt, ones], axis=1))
        o = od[:, :LANES] / (od[:, LANES:] + sink_term[grows])
        for j in range(grp // 2):
            qt = kh * grp // 2 + j
            even, odd = o[(2 * j) * nq:(2 * j + 1) * nq], o[(2 * j + 1) * nq:(2 * j + 2) * nq]
            o_ref[:, qt * LANES:(qt + 1) * LANES] = jnp.where(low, even, odd).astype(o_ref.dtype)


def _dup_heads(w, n_heads):
    d = w.shape[0]
    return jnp.repeat(w.reshape(d, n_heads, 1, HEAD_DIM), 2, axis=2).reshape(d, n_heads * LANES)


def _attn_b(qkv3, sink):
    b, s, _ = qkv3.shape
    nb = s // B_BLOCK
    grp = B_HEADS // B_KV_HEADS
    qw = B_HEADS * HEAD_DIM
    kvw = B_KV_HEADS * LANES
    kcol = qw // kvw
    vcol = kcol + 1
    tabs = jnp.asarray(_attn_b_tables().reshape(3, B_KV_HEADS, grp * B_BLOCK, 3 * B_BLOCK))
    prev = lambda n: jnp.maximum(n - 1, 0)
    nxt = lambda n: jnp.minimum(n + 1, nb - 1)
    kind = lambda n: jnp.where(n == 0, 0, jnp.where(n == nb - 1, 2, 1))
    kv = lambda col, f: pl.BlockSpec((None, B_BLOCK, kvw), lambda bi, n: (bi, f(n), col))
    same = lambda n: n
    return pl.pallas_call(
        _attn_b_kernel,
        out_shape=jax.ShapeDtypeStruct((b, s, qw), BF16),
        grid=(b, nb),
        in_specs=[
            pl.BlockSpec(memory_space=pltpu.SMEM),
            pl.BlockSpec((None, B_BLOCK, qw), lambda bi, n: (bi, n, 0)),
            kv(kcol, prev), kv(kcol, same), kv(kcol, nxt),
            kv(vcol, prev), kv(vcol, same), kv(vcol, nxt),
            pl.BlockSpec((None, B_KV_HEADS, grp * B_BLOCK, 3 * B_BLOCK), lambda bi, n: (kind(n), 0, 0, 0)),
        ],
        out_specs=pl.BlockSpec((None, B_BLOCK, qw), lambda bi, n: (bi, n, 0)),
        compiler_params=_cparams("parallel", "arbitrary"),
        name="attn_b",
    )(sink.astype(F32), qkv3, qkv3, qkv3, qkv3, qkv3, qkv3, qkv3, tabs)


def _attn_c_table(rpb, rows):
    w = GRID_W
    kr_n = min(C_WIN_ROWS, rows)
    c = np.arange(w)
    cs = np.clip(c - C_WIN_COLS // 2, 0, w - C_WIN_COLS)
    kcol = np.arange(w)
    colmask = (kcol[None, :] >= cs[:, None]) & (kcol[None, :] < cs[:, None] + C_WIN_COLS)
    coloff = np.clip(kcol[None, :] - c[:, None] + C_WIN_COLS - 1, 0, 2 * C_WIN_COLS - 2)
    n_off = 2 * C_WIN_ROWS - kr_n
    row_idx = np.arange(n_off)[:, None] + np.arange(kr_n)[None, :]
    bias = rpb.astype(F32)[:, row_idx][:, :, :, coloff]
    bias = jnp.where(jnp.asarray(colmask)[None, None, None], bias, NEG)
    return bias.transpose(1, 0, 3, 2, 4).reshape(n_off, rpb.shape[0] // 2, 2 * w, kr_n * w)


def _attn_c_kernel(q_ref, k_ref, v_ref, tab_ref, o_ref):
    nk = k_ref.shape[1] * k_ref.shape[2]
    nq = q_ref.shape[0]
    lane = lax.broadcasted_iota(jnp.int32, (1, LANES), 1)
    low = lane < HEAD_DIM
    scores = []
    for t in range(C_HEADS // 2):
        cols = slice(t * LANES, (t + 1) * LANES)
        q2 = q_ref[:, cols]
        zero = jnp.zeros_like(q2)
        qq = jnp.concatenate([jnp.where(low, q2, zero), jnp.where(low, zero, q2)], axis=0)
        kt = k_ref[0, :, :, cols].reshape(nk, LANES)
        scores.append(_dot_nt(qq, kt) + tab_ref[t])
    sh = jnp.concatenate(scores, axis=0)
    m = jnp.broadcast_to(jnp.max(sh, axis=-1, keepdims=True), (sh.shape[0], LANES))
    pe = jnp.exp(sh - jnp.concatenate([m] * (nk // LANES), axis=1)).astype(BF16)
    ones = jnp.ones((nk, LANES), BF16)
    for t in range(C_HEADS // 2):
        cols = slice(t * LANES, (t + 1) * LANES)
        rows = slice(2 * t * nq, 2 * (t + 1) * nq)
        vt = v_ref[0, :, :, cols].reshape(nk, LANES)
        od = _dot(pe[rows], jnp.concatenate([vt, ones], axis=1))
        o = od[:, :LANES] / od[:, LANES:]
        o_ref[:, cols] = jnp.where(low, o[:nq], o[nq:]).astype(o_ref.dtype)


def _attn_c(qkv4, rpb):
    b, rows, w, _ = qkv4.shape
    hw = C_HEADS * HEAD_DIM
    kr_n = min(C_WIN_ROWS, rows)
    table = _attn_c_table(rpb, rows)
    row_start = lambda r: jnp.clip(r - kr_n // 2, 0, rows - kr_n)
    el = pl.Element
    return pl.pallas_call(
        _attn_c_kernel,
        out_shape=jax.ShapeDtypeStruct((b, rows, w, hw), BF16),
        grid=(rows, b),
        in_specs=[
            pl.BlockSpec((None, None, w, hw), lambda r, bi: (bi, r, 0, 0)),
            pl.BlockSpec((el(1), el(kr_n), el(w), el(hw)), lambda r, bi: (bi, row_start(r), 0, hw)),
            pl.BlockSpec((el(1), el(kr_n), el(w), el(hw)), lambda r, bi: (bi, row_start(r), 0, 2 * hw)),
            pl.BlockSpec((None, C_HEADS // 2, 2 * w, kr_n * w),
                         lambda r, bi: (row_start(r) - r + C_WIN_ROWS - 1, 0, 0, 0)),
        ],
        out_specs=pl.BlockSpec((None, None, w, hw), lambda r, bi: (bi, r, 0, 0)),
        compiler_params=_cparams("arbitrary", "arbitrary"),
        name="attn_c",
    )(qkv4, qkv4, qkv4, table)


def _lane_cumsum(x, tri):
    e, s = x.shape
    off = jnp.zeros((e, 1), F32)
    outs = []
    for j in range(s // LANES):
        blk = x[:, j * LANES:(j + 1) * LANES]
        outs.append(_dot(blk.astype(BF16), tri) + off)
        off = off + jnp.sum(blk, axis=1, keepdims=True)
    return jnp.concatenate(outs, axis=1)


def _route_kernel(h_ref, g_ref, rwt_ref, tri_ref, xn_ref, slotr_ref, slotc_ref, gatec_ref, cnt_ref,
                  *, cap, chunk, tile):
    s = h_ref.shape[0]
    n_e = rwt_ref.shape[0] // 2
    logits = []
    for c in range(s // chunk):
        rows = slice(c * chunk, (c + 1) * chunk)
        xn = _rms(h_ref[rows, :], g_ref[...])
        xn_hi = xn.astype(BF16)
        xn_ref[rows, :] = xn_hi
        xn_lo = (xn - xn_hi.astype(F32)).astype(BF16)
        both = _dot_nt(rwt_ref[...], xn_hi)
        logits.append(both[:n_e] + both[n_e:] + _dot_nt(rwt_ref[:n_e, :], xn_lo))
    lg = jnp.concatenate(logits, axis=1)
    ex = jnp.exp(lg - jnp.max(lg, axis=0, keepdims=True))
    aff = ex / jnp.sum(ex, axis=0, keepdims=True)
    bits = pltpu.bitcast(aff, jnp.int32)
    thr = jnp.zeros((n_e, 1), jnp.int32)
    for shift in range(27, -1, -3):
        digit = jnp.zeros((n_e, 1), jnp.int32)
        for j in range(1, 8):
            cnt = jnp.sum(jnp.where(bits >= (thr | (j << shift)), 1.0, 0.0), axis=1, keepdims=True)
            digit = digit + jnp.where(cnt >= cap, 1, 0)
        thr = thr | (digit * (1 << shift))
    gt = bits > thr
    eq = bits == thr
    need = cap - jnp.sum(jnp.where(gt, 1.0, 0.0), axis=1, keepdims=True)
    eqf = jnp.where(eq, 1.0, 0.0)
    eq_rank = _lane_cumsum(eqf, tri_ref[...]) - eqf
    sel = gt | (eq & (eq_rank < need))
    self_ = jnp.where(sel, 1.0, 0.0)
    slot = jnp.where(sel, _lane_cumsum(self_, tri_ref[...]) - self_, -1.0)
    lane = lax.broadcasted_iota(jnp.int32, (n_e, LANES), 1)
    counts = jnp.zeros((n_e, LANES), F32)
    for i in range(s // tile):
        counts = jnp.where(lane == i, jnp.sum(self_[:, i * tile:(i + 1) * tile], axis=1, keepdims=True), counts)
    cnt_ref[...] = counts.astype(jnp.int32)
    pad_rows = LANES - n_e
    slot_p = jnp.concatenate([slot, jnp.full((pad_rows, s), -1.0, F32)], axis=0)
    gate_p = jnp.concatenate([jnp.where(sel, aff, 0.0), jnp.zeros((pad_rows, s), F32)], axis=0)
    slotr_ref[...] = slot
    for j in range(s // LANES):
        cols = slice(j * LANES, (j + 1) * LANES)
        slotc_ref[cols, :] = slot_p[:, cols].T.astype(BF16)
        gatec_ref[cols, :] = gate_p[:, cols].T.astype(BF16)


def _route(h3, g, router_w, cap):
    b, s, d = h3.shape
    n_e = router_w.shape[1]
    tri = jnp.asarray(np.triu(np.ones((LANES, LANES))), BF16)
    rw_t = router_w.T.astype(F32)
    rw_hi = rw_t.astype(BF16)
    rw_lo = (rw_t - rw_hi.astype(F32)).astype(BF16)
    per_seq = lambda *shape: pl.BlockSpec((None,) + shape, lambda i: (i,) + (0,) * len(shape))
    return pl.pallas_call(
        functools.partial(_route_kernel, cap=cap, chunk=256, tile=TOKEN_TILE),
        out_shape=(
            jax.ShapeDtypeStruct((b, s, d), BF16),
            jax.ShapeDtypeStruct((b, n_e, s), F32),
            jax.ShapeDtypeStruct((b, s, LANES), BF16),
            jax.ShapeDtypeStruct((b, s, LANES), BF16),
            jax.ShapeDtypeStruct((b, n_e, LANES), jnp.int32),
        ),
        grid=(b,),
        in_specs=[per_seq(s, d), _full_spec((1, d)), _full_spec((2 * n_e, d)), _full_spec((LANES, LANES))],
        out_specs=(per_seq(s, d), per_seq(n_e, s), per_seq(s, LANES), per_seq(s, LANES), per_seq(n_e, LANES)),
        compiler_params=_cparams("parallel"),
        name="route",
    )(h3, g.reshape(1, d), jnp.concatenate([rw_hi, rw_lo], axis=0), tri)


def _chunk_tables(cnt, cap):
    n_e = cnt.shape[1]
    kmax = n_e * (TOKEN_TILE // CHUNK + 1)
    kmax = -(-kmax // CHUNKS_PER_GROUP) * CHUNKS_PER_GROUP
    c0 = jnp.cumsum(cnt, axis=2) - cnt
    a0 = c0 // CHUNK
    a1 = jnp.where(cnt > 0, (c0 + cnt + CHUNK - 1) // CHUNK, a0)
    nch = (a1 - a0).transpose(0, 2, 1)
    a0 = a0.transpose(0, 2, 1)
    pos_end = jnp.cumsum(nch, axis=2)
    pos = pos_end - nch
    total = pos_end[..., -1]
    k = jnp.arange(kmax, dtype=jnp.int32)
    e_of_k = jnp.sum((k[None, None, :, None] >= pos_end[:, :, None, :]).astype(jnp.int32), axis=-1)
    e_of_k = jnp.minimum(e_of_k, n_e - 1)
    is_e = e_of_k[..., None] == jnp.arange(n_e, dtype=jnp.int32)
    q = k + jnp.sum(jnp.where(is_e, (a0 - pos)[:, :, None, :], 0), axis=-1)
    valid = k[None, None, :] < total[..., None]
    e_tab = jnp.where(valid, e_of_k, 0)
    slot_tab = jnp.where(valid, q * CHUNK, -(1 << 20))
    dst_tab = jnp.where(valid, e_of_k * cap + q * CHUNK, 0)
    groups = (total + CHUNKS_PER_GROUP - 1) // CHUNKS_PER_GROUP
    return jnp.concatenate([e_tab, slot_tab, dst_tab, groups[..., None]], axis=-1).astype(jnp.int32), kmax


def _gather_kernel(tab_ref, xn_ref, slotr_ref, xs_ref, rows_ref, *, kmax):
    @pl.when(pl.program_id(1) == 0)
    def _():
        xs_ref[...] = jnp.zeros_like(xs_ref)

    sub = lax.broadcasted_iota(jnp.int32, (CHUNK, slotr_ref.shape[1]), 0).astype(F32)

    def group(g, carry):
        base = g * CHUNKS_PER_GROUP
        onehot = []
        for c in range(CHUNKS_PER_GROUP):
            slots = slotr_ref[pl.ds(tab_ref[0, base + c], 1), :]
            want = sub + tab_ref[0, kmax + base + c].astype(F32)
            onehot.append(jnp.where(slots == want, 1.0, 0.0).astype(BF16))
        rows_ref[...] = _dot(jnp.concatenate(onehot, axis=0), xn_ref[...]).astype(BF16)
        for c0 in range(0, CHUNKS_PER_GROUP, 4):
            cs = range(c0, c0 + 4)
            dsts = [pl.ds(pl.multiple_of(tab_ref[0, 2 * kmax + base + c], CHUNK), CHUNK) for c in cs]
            sums = [xs_ref[dst, :] + rows_ref[c * CHUNK:(c + 1) * CHUNK, :] for c, dst in zip(cs, dsts)]
            for dst, total in reversed(list(zip(dsts, sums))):
                xs_ref[dst, :] = total
        return carry

    lax.fori_loop(0, tab_ref[0, 3 * kmax], group, 0)


def _expert_kernel(xs_ref, wg_ref, wu_ref, wd_ref, y_ref):
    nb, cap, d = xs_ref.shape
    xs = xs_ref[...].reshape(nb * cap, d)
    hg = _dot(xs, wg_ref[...])
    hu = _dot(xs, wu_ref[...])
    hdn = (hg * jax.nn.sigmoid(hg) * hu).astype(BF16)
    y_ref[...] = _dot(hdn, wd_ref[...]).astype(BF16).reshape(nb, cap, d)


def _scatter_kernel(tab_ref, y_ref, slotc_ref, gatec_ref, o_ref, yg_ref, *, kmax):
    o_ref[...] = jnp.zeros_like(o_ref)
    lane = lax.broadcasted_iota(jnp.int32, (1, GROUP), 1)
    sub = lax.broadcasted_iota(jnp.int32, (LANES, GROUP), 0)

    def group(g, carry):
        base = g * CHUNKS_PER_GROUP
        expert = jnp.zeros((1, GROUP), jnp.int32)
        want = jnp.zeros((1, GROUP), jnp.int32)
        for c in range(CHUNKS_PER_GROUP):
            in_chunk = (lane >= c * CHUNK) & (lane < (c + 1) * CHUNK)
            expert = jnp.where(in_chunk, tab_ref[0, base + c], expert)
            want = jnp.where(in_chunk, tab_ref[0, kmax + base + c] + lane - c * CHUNK, want)
            src = pl.ds(pl.multiple_of(tab_ref[0, 2 * kmax + base + c], CHUNK), CHUNK)
            yg_ref[c * CHUNK:(c + 1) * CHUNK, :] = y_ref[src, :]
        pick = jnp.where(sub == expert, 1.0, 0.0).astype(BF16)
        both = _dot(jnp.concatenate([slotc_ref[...], gatec_ref[...]], axis=0), pick)
        tile = slotc_ref.shape[0]
        weights = jnp.where(both[:tile] == want.astype(F32), both[tile:], 0.0).astype(BF16)
        o_ref[...] += _dot(weights, yg_ref[...])
        return carry

    lax.fori_loop(0, tab_ref[0, 3 * kmax], group, 0)


def _expert_ffn(xn3, slot_row, slot_col, gate_col, cnt, w_gate, w_up, w_down, cap, seqs_per_step=4):
    b, s, d = xn3.shape
    n_e, _, ff = w_gate.shape
    nt = s // TOKEN_TILE
    tables, kmax = _chunk_tables(cnt[:, :, :nt], cap)
    tables = tables.reshape(b * nt, 1, 3 * kmax + 1)
    tab_spec = pl.BlockSpec((None, 1, 3 * kmax + 1), lambda i, t: (i * nt + t, 0, 0), memory_space=pltpu.SMEM)
    xs = pl.pallas_call(
        functools.partial(_gather_kernel, kmax=kmax),
        out_shape=jax.ShapeDtypeStruct((b, n_e * cap, d), BF16),
        grid=(b, nt),
        in_specs=[
            tab_spec,
            pl.BlockSpec((None, TOKEN_TILE, d), lambda i, t: (i, t, 0)),
            pl.BlockSpec((None, n_e, TOKEN_TILE), lambda i, t: (i, 0, t)),
        ],
        out_specs=pl.BlockSpec((None, n_e * cap, d), lambda i, t: (i, 0, 0)),
        scratch_shapes=[pltpu.VMEM((GROUP, d), BF16)],
        compiler_params=_cparams("parallel", "arbitrary"),
        name="expert_gather",
    )(tables, xn3, slot_row)
    nb = seqs_per_step
    y = pl.pallas_call(
        _expert_kernel,
        out_shape=jax.ShapeDtypeStruct((b, n_e, cap, d), BF16),
        grid=(n_e, b // nb),
        in_specs=[
            pl.BlockSpec((nb, None, cap, d), lambda e, i: (i, e, 0, 0)),
            pl.BlockSpec((None, d, ff), lambda e, i: (e, 0, 0)),
            pl.BlockSpec((None, d, ff), lambda e, i: (e, 0, 0)),
            pl.BlockSpec((None, ff, d), lambda e, i: (e, 0, 0)),
        ],
        out_specs=pl.BlockSpec((nb, None, cap, d), lambda e, i: (i, e, 0, 0)),
        compiler_params=_cparams("parallel", "parallel"),
        name="expert_mlp",
    )(xs.reshape(b, n_e, cap, d), w_gate.astype(BF16), w_up.astype(BF16), w_down.astype(BF16))
    return pl.pallas_call(
        functools.partial(_scatter_kernel, kmax=kmax),
        out_shape=jax.ShapeDtypeStruct((b, s, d), F32),
        grid=(b, nt),
        in_specs=[
            tab_spec,
            pl.BlockSpec((None, n_e * cap, d), lambda i, t: (i, 0, 0)),
            pl.BlockSpec((None, TOKEN_TILE, LANES), lambda i, t: (i, t, 0)),
            pl.BlockSpec((None, TOKEN_TILE, LANES), lambda i, t: (i, t, 0)),
        ],
        out_specs=pl.BlockSpec((None, TOKEN_TILE, d), lambda i, t: (i, t, 0)),
        scratch_shapes=[pltpu.VMEM((GROUP, d), BF16)],
        compiler_params=_cparams("parallel", "arbitrary"),
        name="expert_scatter",
    )(tables, y.reshape(b, n_e * cap, d), slot_col, gate_col)


def _ple_kernel(h_ref, d_ref, p_ref, g_ref, wg_ref, wp_ref, o_ref):
    h = h_ref[...] + d_ref[...]
    xn = _rms(h, g_ref[...]).astype(BF16)
    gate = jax.nn.sigmoid(_dot(xn, wg_ref[...]))
    o_ref[...] = h + gate * _dot(p_ref[...].astype(BF16), wp_ref[...])


def _ple(h2, delta2, p3, layer, g, w_gate, w_proj, tm=512):
    t, d = h2.shape
    pd = p3.shape[2]
    row = lambda width: pl.BlockSpec((tm, width), lambda i: (i, 0))
    return pl.pallas_call(
        _ple_kernel,
        out_shape=jax.ShapeDtypeStruct((t, d), F32),
        grid=(t // tm,),
        in_specs=[row(d), row(d), pl.BlockSpec((None, tm, pd), lambda i: (layer, i, 0)),
                  _full_spec((1, d)), _full_spec((d, d)), _full_spec((pd, d))],
        out_specs=row(d),
        compiler_params=_cparams("parallel"),
        name="ple",
    )(h2, delta2, p3, g.reshape(1, d), w_gate.astype(BF16), w_proj.astype(BF16))


def kernel(x, p, norm_mix_g, norm_ffn_g, w_out, router_w, exp_w_gate, exp_w_up, exp_w_down, ple_norm_g, ple_gate_w, ple_proj_w, a_w_in, a_vnorm_g, a_w_s, a_b_s, b_w_in, b_qnorm_g, b_knorm_g, b_sink, c_w_in, c_qnorm_g, c_knorm_g, c_rpb):
    b, s, d = x.shape
    depth = norm_mix_g.shape[0]
    t = b * s
    cap = max(1, EC_CAPACITY_FACTOR * s // N_EXPERTS)
    scale = HEAD_DIM ** -0.5
    h = x.reshape(t, d)
    for i in range(depth):
        kind = i % N_MIXERS
        j = i // N_MIXERS
        if kind == 0:
            h = _mixer_a(h, norm_mix_g[i], a_w_in[j], a_vnorm_g[j], a_w_s[j], a_b_s[j], w_out[i])
        elif kind == 1:
            gain = jnp.concatenate([jnp.tile(b_qnorm_g[j] * scale, B_HEADS), jnp.tile(b_knorm_g[j], 2 * B_KV_HEADS)])
            qw, kw = B_HEADS * HEAD_DIM, B_KV_HEADS * HEAD_DIM
            w_b = jnp.concatenate([b_w_in[j][:, :qw], _dup_heads(b_w_in[j][:, qw:qw + kw], B_KV_HEADS),
                                   _dup_heads(b_w_in[j][:, qw + kw:], B_KV_HEADS)], axis=1)
            qkv = _norm_proj(h, norm_mix_g[i], w_b, gain, qw + 2 * kw)
            mix = _attn_b(qkv.reshape(b, s, -1), b_sink[j])
            h = _proj_residual(mix.reshape(t, -1), w_out[i], h)
        else:
            gain = jnp.concatenate([jnp.tile(c_qnorm_g[j] * scale, C_HEADS), jnp.tile(c_knorm_g[j], C_HEADS)])
            qkv = _norm_proj(h, norm_mix_g[i], c_w_in[j], gain, 2 * C_HEADS * HEAD_DIM)
            mix = _attn_c(qkv.reshape(b, s // GRID_W, GRID_W, -1), c_rpb[j])
            h = _proj_residual(mix.reshape(t, -1), w_out[i], h)
        xn, slot_row, slot_col, gate_col, cnt = _route(h.reshape(b, s, d), norm_ffn_g[i], router_w[i], cap)
        delta = _expert_ffn(xn, slot_row, slot_col, gate_col, cnt, exp_w_gate[i], exp_w_up[i], exp_w_down[i], cap)
        h = _ple(h, delta.reshape(t, d), p.reshape(depth, t, -1), i, ple_norm_g[i], ple_gate_w[i], ple_proj_w[i])
    return h.reshape(b, s, d)
```

```python
import functools

import numpy as np
import jax
import jax.numpy as jnp
from jax import lax
from jax.experimental import pallas as pl
from jax.experimental.pallas import tpu as pltpu

F32 = jnp.float32
BF16 = jnp.bfloat16

RMS_EPS = 1e-6
LN_EPS = 1e-5
NEG = -1e30
HEAD_DIM = 64
GRID_W = 64
A_GROUPS = 8
A_CHUNK = 128
B_HEADS = 16
B_KV_HEADS = 4
B_BLOCK = 128
C_HEADS = 16
C_WIN_ROWS = 8
C_WIN_COLS = 16
N_EXPERTS = 16
EC_CAPACITY_FACTOR = 2
N_MIXERS = 3

LANES = 128
MXU_DIM = 256
VMEM_LIMIT_BYTES = 60 * 1024 * 1024

TOKEN_TILE = 256
CHUNK = 16
GROUP = 2 * MXU_DIM
CHUNKS_PER_GROUP = GROUP // CHUNK


def _cparams(*sem):
    return pltpu.CompilerParams(dimension_semantics=sem, vmem_limit_bytes=VMEM_LIMIT_BYTES)


def _rms(xf, g):
    return xf * lax.rsqrt(jnp.mean(xf * xf, axis=-1, keepdims=True) + RMS_EPS) * g


def _dot(a, b):
    return jnp.dot(a, b, preferred_element_type=F32)


def _dot_nt(a, b):
    return lax.dot_general(a, b, (((1,), (1,)), ((), ())), preferred_element_type=F32)


def _full_spec(shape):
    nd = len(shape)
    return pl.BlockSpec(shape, lambda *_: (0,) * nd)


def _mixer_a_kernel(h_ref, g_ref, win_ref, vg_ref, ws_ref, bias_ref, wout_ref, o_ref, mix_ref):
    tm = h_ref.shape[0]
    width = vg_ref.shape[1]
    gw = width // A_GROUPS
    h = h_ref[...]
    xn = _rms(h, g_ref[...]).astype(BF16)
    z = _dot(xn, win_ref[...])
    z = 0.5 * z * (1.0 + jnp.tanh(np.sqrt(2.0 / np.pi).astype(np.float32)
                                  * (z + 0.044715 * (z * z * z))))
    u = z[:, :width]
    v = z[:, width:]
    mu = jnp.mean(v, axis=-1, keepdims=True)
    vc = v - mu
    var = jnp.mean(vc * vc, axis=-1, keepdims=True)
    vn = (vc * lax.rsqrt(var + LN_EPS) * vg_ref[...]).astype(BF16)
    for c in range(tm // A_CHUNK):
        rows = slice(c * A_CHUNK, (c + 1) * A_CHUNK)
        for g in range(A_GROUPS):
            cols = slice(g * gw, (g + 1) * gw)
            s = _dot(ws_ref[g], vn[rows, cols]) + bias_ref[:, cols]
            mix_ref[rows, cols] = (u[rows, cols] * s).astype(BF16)
    o_ref[...] = h + _dot(mix_ref[...], wout_ref[...])


def _mixer_a(h2, g, w_in, vnorm_g, w_s, b_s, w_out, tm=512):
    t, d = h2.shape
    width = vnorm_g.shape[0]
    gw = width // A_GROUPS
    bias = jnp.repeat(b_s.T.astype(F32), gw, axis=1)
    return pl.pallas_call(
        _mixer_a_kernel,
        out_shape=jax.ShapeDtypeStruct((t, d), F32),
        grid=(t // tm,),
        in_specs=[
            pl.BlockSpec((tm, d), lambda i: (i, 0)),
            _full_spec((1, d)),
            _full_spec((d, 2 * width)),
            _full_spec((1, width)),
            _full_spec((A_GROUPS, A_CHUNK, A_CHUNK)),
            _full_spec((A_CHUNK, width)),
            _full_spec((width, d)),
        ],
        out_specs=pl.BlockSpec((tm, d), lambda i: (i, 0)),
        scratch_shapes=[pltpu.VMEM((tm, width), BF16)],
        compiler_params=_cparams("parallel"),
        name="mixer_a",
    )(h2, g.reshape(1, d), w_in.astype(BF16), vnorm_g.reshape(1, width), w_s.astype(BF16), bias,
      w_out.astype(BF16))


def _norm_proj_kernel(h_ref, g_ref, w_ref, hg_ref, bd_ref, o_ref, *, n_norm_cols):
    xn = _rms(h_ref[...], g_ref[...]).astype(BF16)
    acc = _dot(xn, w_ref[...])
    for j in range(n_norm_cols // MXU_DIM):
        cols = slice(j * MXU_DIM, (j + 1) * MXU_DIM)
        blk = acc[:, cols]
        ss = _dot((blk * blk).astype(BF16), bd_ref[...])
        o_ref[:, cols] = (blk * lax.rsqrt(ss * (1.0 / HEAD_DIM) + RMS_EPS) * hg_ref[:, cols]).astype(BF16)
    o_ref[:, n_norm_cols:] = acc[:, n_norm_cols:].astype(BF16)


def _norm_proj(h2, g, w, head_gain, n_norm_cols, tm=512):
    t, d = h2.shape
    n = w.shape[1]
    blockdiag = jnp.asarray(np.kron(np.eye(MXU_DIM // HEAD_DIM), np.ones((HEAD_DIM, HEAD_DIM))), BF16)
    return pl.pallas_call(
        functools.partial(_norm_proj_kernel, n_norm_cols=n_norm_cols),
        out_shape=jax.ShapeDtypeStruct((t, n), BF16),
        grid=(t // tm,),
        in_specs=[
            pl.BlockSpec((tm, d), lambda i: (i, 0)),
            _full_spec((1, d)),
            _full_spec((d, n)),
            _full_spec((1, n_norm_cols)),
            _full_spec((MXU_DIM, MXU_DIM)),
        ],
        out_specs=pl.BlockSpec((tm, n), lambda i: (i, 0)),
        compiler_params=_cparams("parallel"),
        name="norm_proj",
    )(h2, g.reshape(1, d), w.astype(BF16), head_gain.reshape(1, n_norm_cols).astype(F32), blockdiag)


def _proj_residual_kernel(a_ref, w_ref, h_ref, o_ref):
    o_ref[...] = h_ref[...] + _dot(a_ref[...], w_ref[...])


def _proj_residual(a2, w, h2, tm=512):
    t, d = h2.shape
    k = a2.shape[1]
    return pl.pallas_call(
        _proj_residual_kernel,
        out_shape=jax.ShapeDtypeStruct((t, d), F32),
        grid=(t // tm,),
        in_specs=[
            pl.BlockSpec((tm, k), lambda i: (i, 0)),
            _full_spec((k, d)),
            pl.BlockSpec((tm, d), lambda i: (i, 0)),
        ],
        out_specs=pl.BlockSpec((tm, d), lambda i: (i, 0)),
        compiler_params=_cparams("parallel"),
        name="proj_residual",
    )(a2, w.astype(BF16), h2)


def _alibi_slopes(n):
    return np.array([2.0 ** (-8.0 * (h + 1) / n) for h in range(n)], dtype=np.float32)


def _attn_b_tables():
    span = 3 * B_BLOCK
    rel = np.arange(span)[None, :] - B_BLOCK - np.arange(B_BLOCK)[:, None]
    in_window = np.abs(rel) <= B_BLOCK
    alibi = (-_alibi_slopes(B_HEADS)[:, None, None] * np.abs(rel)[None]).astype(np.float32)
    kblk = np.arange(span) // B_BLOCK
    tabs = []
    for kind in range(3):
        valid = in_window & ~((kind == 0) & (kblk == 0))[None, :] & ~((kind == 2) & (kblk == 2))[None, :]
        tabs.append(np.where(valid[None], alibi, np.float32(NEG)))
    return np.stack(tabs).astype(np.float32)


def _attn_b_kernel(sink_ref, q_ref, kp_ref, kc_ref, kn_ref, vp_ref, vc_ref, vn_ref, tab_ref, o_ref):
    grp = B_HEADS // B_KV_HEADS
    nq = q_ref.shape[0]
    lane = lax.broadcasted_iota(jnp.int32, (1, LANES), 1)
    low = lane < HEAD_DIM
    scores = []
    for kh in range(B_KV_HEADS):
        tile = slice(kh * LANES, (kh + 1) * LANES)
        kt = jnp.concatenate([kp_ref[:, tile], kc_ref[:, tile], kn_ref[:, tile]], axis=0)
        rows = []
        for qt in range(kh * grp // 2, (kh + 1) * grp // 2):
            q2 = q_ref[:, qt * LANES:(qt + 1) * LANES]
            zero = jnp.zeros_like(q2)
            rows += [jnp.where(low, q2, zero), jnp.where(low, zero, q2)]
        scores.append(_dot_nt(jnp.concatenate(rows, axis=0), kt) + tab_ref[kh])
    sh = jnp.concatenate(scores, axis=0)
    heads = [slice(h * nq, (h + 1) * nq) for h in range(B_HEADS)]
    rowmax = jnp.broadcast_to(jnp.max(sh, axis=-1, keepdims=True), (sh.shape[0], LANES))
    m = jnp.concatenate([jnp.maximum(rowmax[hs], sink_ref[h]) for h, hs in enumerate(heads)], axis=0)
    sink_term = jnp.concatenate([jnp.exp(sink_ref[h] - m[hs]) for h, hs in enumerate(heads)], axis=0)
    pe = jnp.exp(sh - jnp.concatenate([m] * (sh.shape[1] // LANES), axis=1)).astype(BF16)
    ones = jnp.ones((sh.shape[1], LANES), BF16)
    for kh in range(B_KV_HEADS):
        tile = slice(kh * LANES, (kh + 1) * LANES)
        grows = slice(kh * grp * nq, (kh + 1) * grp * nq)
        vt = jnp.concatenate([vp_ref[:, tile], vc_ref[:, tile], vn_ref[:, tile]], axis=0)
        od = _dot(pe[grows], jnp.concatenate([vt, ones], axis=1))
        o = od[:, :LANES] / (od[:, LANES:] + sink_term[grows])
        for j in range(grp // 2):
            qt = kh * grp // 2 + j
            even, odd = o[(2 * j) * nq:(2 * j + 1) * nq], o[(2 * j + 1) * nq:(2 * j + 2) * nq]
            o_ref[:, qt * LANES:(qt + 1) * LANES] = jnp.where(low, even, odd).astype(o_ref.dtype)


def _dup_heads(w, n_heads):
    d = w.shape[0]
    return jnp.repeat(w.reshape(d, n_heads, 1, HEAD_DIM), 2, axis=2).reshape(d, n_heads * LANES)


def _attn_b(qkv3, sink):
    b, s, _ = qkv3.shape
    nb = s // B_BLOCK
    grp = B_HEADS // B_KV_HEADS
    qw = B_HEADS * HEAD_DIM
    kvw = B_KV_HEADS * LANES
    kcol = qw // kvw
    vcol = kcol + 1
    tabs = jnp.asarray(_attn_b_tables().reshape(3, B_KV_HEADS, grp * B_BLOCK, 3 * B_BLOCK))
    prev = lambda n: jnp.maximum(n - 1, 0)
    nxt = lambda n: jnp.minimum(n + 1, nb - 1)
    kind = lambda n: jnp.where(n == 0, 0, jnp.where(n == nb - 1, 2, 1))
    kv = lambda col, f: pl.BlockSpec((None, B_BLOCK, kvw), lambda bi, n: (bi, f(n), col))
    same = lambda n: n
    return pl.pallas_call(
        _attn_b_kernel,
        out_shape=jax.ShapeDtypeStruct((b, s, qw), BF16),
        grid=(b, nb),
        in_specs=[
            pl.BlockSpec(memory_space=pltpu.SMEM),
            pl.BlockSpec((None, B_BLOCK, qw), lambda bi, n: (bi, n, 0)),
            kv(kcol, prev), kv(kcol, same), kv(kcol, nxt),
            kv(vcol, prev), kv(vcol, same), kv(vcol, nxt),
            pl.BlockSpec((None, B_KV_HEADS, grp * B_BLOCK, 3 * B_BLOCK), lambda bi, n: (kind(n), 0, 0, 0)),
        ],
        out_specs=pl.BlockSpec((None, B_BLOCK, qw), lambda bi, n: (bi, n, 0)),
        compiler_params=_cparams("parallel", "arbitrary"),
        name="attn_b",
    )(sink.astype(F32), qkv3, qkv3, qkv3, qkv3, qkv3, qkv3, qkv3, tabs)


def _attn_c_table(rpb, rows):
    w = GRID_W
    kr_n = min(C_WIN_ROWS, rows)
    c = np.arange(w)
    cs = np.clip(c - C_WIN_COLS // 2, 0, w - C_WIN_COLS)
    kcol = np.arange(w)
    colmask = (kcol[None, :] >= cs[:, None]) & (kcol[None, :] < cs[:, None] + C_WIN_COLS)
    coloff = np.clip(kcol[None, :] - c[:, None] + C_WIN_COLS - 1, 0, 2 * C_WIN_COLS - 2)
    n_off = 2 * C_WIN_ROWS - kr_n
    row_idx = np.arange(n_off)[:, None] + np.arange(kr_n)[None, :]
    bias = rpb.astype(F32)[:, row_idx][:, :, :, coloff]
    bias = jnp.where(jnp.asarray(colmask)[None, None, None], bias, NEG)
    return bias.transpose(1, 0, 3, 2, 4).reshape(n_off, rpb.shape[0] // 2, 2 * w, kr_n * w)


def _attn_c_kernel(q_ref, k_ref, v_ref, tab_ref, o_ref):
    nk = k_ref.shape[1] * k_ref.shape[2]
    nq = q_ref.shape[1]
    lane = lax.broadcasted_iota(jnp.int32, (1, LANES), 1)
    low = lane < HEAD_DIM
    ones = jnp.ones((nk, LANES), BF16)

    def one_sequence(i, carry):
        scores = []
        for t in range(C_HEADS // 2):
            cols = slice(t * LANES, (t + 1) * LANES)
            q2 = q_ref[i, :, cols]
            zero = jnp.zeros_like(q2)
            qq = jnp.concatenate([jnp.where(low, q2, zero), jnp.where(low, zero, q2)], axis=0)
            kt = k_ref[i, :, :, cols].reshape(nk, LANES)
            scores.append(_dot_nt(qq, kt) + tab_ref[t])
        sh = jnp.concatenate(scores, axis=0)
        m = jnp.broadcast_to(jnp.max(sh, axis=-1, keepdims=True), (sh.shape[0], LANES))
        pe = jnp.exp(sh - jnp.concatenate([m] * (nk // LANES), axis=1)).astype(BF16)
        for t in range(C_HEADS // 2):
            cols = slice(t * LANES, (t + 1) * LANES)
            rows = slice(2 * t * nq, 2 * (t + 1) * nq)
            vt = v_ref[i, :, :, cols].reshape(nk, LANES)
            od = _dot(pe[rows], jnp.concatenate([vt, ones], axis=1))
            o = od[:, :LANES] / od[:, LANES:]
            o_ref[i, :, cols] = jnp.where(low, o[:nq], o[nq:]).astype(o_ref.dtype)
        return carry

    lax.fori_loop(0, q_ref.shape[0], one_sequence, 0)


def _attn_c(qkv4, rpb, seqs_per_step=4):
    b, rows, w, _ = qkv4.shape
    hw = C_HEADS * HEAD_DIM
    kr_n = min(C_WIN_ROWS, rows)
    nb = seqs_per_step
    table = _attn_c_table(rpb, rows)
    row_start = lambda r: jnp.clip(r - kr_n // 2, 0, rows - kr_n)
    el = pl.Element
    return pl.pallas_call(
        _attn_c_kernel,
        out_shape=jax.ShapeDtypeStruct((b, rows, w, hw), BF16),
        grid=(rows, b // nb),
        in_specs=[
            pl.BlockSpec((nb, None, w, hw), lambda r, bi: (bi, r, 0, 0)),
            pl.BlockSpec((el(nb), el(kr_n), el(w), el(hw)), lambda r, bi: (bi * nb, row_start(r), 0, hw)),
            pl.BlockSpec((el(nb), el(kr_n), el(w), el(hw)), lambda r, bi: (bi * nb, row_start(r), 0, 2 * hw)),
            pl.BlockSpec((None, C_HEADS // 2, 2 * w, kr_n * w),
                         lambda r, bi: (row_start(r) - r + C_WIN_ROWS - 1, 0, 0, 0)),
        ],
        out_specs=pl.BlockSpec((nb, None, w, hw), lambda r, bi: (bi, r, 0, 0)),
        compiler_params=_cparams("arbitrary", "arbitrary"),
        name="attn_c",
    )(qkv4, qkv4, qkv4, table)


def _lane_cumsum(x, tri):
    e, s = x.shape
    off = jnp.zeros((e, 1), F32)
    outs = []
    for j in range(s // LANES):
        blk = x[:, j * LANES:(j + 1) * LANES]
        outs.append(_dot(blk.astype(BF16), tri) + off)
        off = off + jnp.sum(blk, axis=1, keepdims=True)
    return jnp.concatenate(outs, axis=1)


def _route_kernel(h_ref, g_ref, rwt_ref, tri_ref, xn_ref, slotr_ref, slotc_ref, gatec_ref, cnt_ref,
                  *, cap, chunk, tile):
    s = h_ref.shape[0]
    n_e = rwt_ref.shape[0] // 2
    logits = []
    for c in range(s // chunk):
        rows = slice(c * chunk, (c + 1) * chunk)
        xn = _rms(h_ref[rows, :], g_ref[...])
        xn_hi = xn.astype(BF16)
        xn_ref[rows, :] = xn_hi
        xn_lo = (xn - xn_hi.astype(F32)).astype(BF16)
        both = _dot_nt(rwt_ref[...], xn_hi)
        logits.append(both[:n_e] + both[n_e:] + _dot_nt(rwt_ref[:n_e, :], xn_lo))
    lg = jnp.concatenate(logits, axis=1)
    ex = jnp.exp(lg - jnp.max(lg, axis=0, keepdims=True))
    aff = ex / jnp.sum(ex, axis=0, keepdims=True)
    bits = pltpu.bitcast(aff, jnp.int32)
    thr = jnp.zeros((n_e, 1), jnp.int32)
    for shift in range(27, -1, -3):
        digit = jnp.zeros((n_e, 1), jnp.int32)
        for j in range(1, 8):
            cnt = jnp.sum(jnp.where(bits >= (thr | (j << shift)), 1.0, 0.0), axis=1, keepdims=True)
            digit = digit + jnp.where(cnt >= cap, 1, 0)
        thr = thr | (digit * (1 << shift))
    gt = bits > thr
    eq = bits == thr
    need = cap - jnp.sum(jnp.where(gt, 1.0, 0.0), axis=1, keepdims=True)
    eqf = jnp.where(eq, 1.0, 0.0)
    eq_rank = _lane_cumsum(eqf, tri_ref[...]) - eqf
    sel = gt | (eq & (eq_rank < need))
    self_ = jnp.where(sel, 1.0, 0.0)
    slot = jnp.where(sel, _lane_cumsum(self_, tri_ref[...]) - self_, -1.0)
    lane = lax.broadcasted_iota(jnp.int32, (n_e, LANES), 1)
    counts = jnp.zeros((n_e, LANES), F32)
    for i in range(s // tile):
        counts = jnp.where(lane == i, jnp.sum(self_[:, i * tile:(i + 1) * tile], axis=1, keepdims=True), counts)
    cnt_ref[...] = counts.astype(jnp.int32)
    pad_rows = LANES - n_e
    slot_p = jnp.concatenate([slot, jnp.full((pad_rows, s), -1.0, F32)], axis=0)
    gate_p = jnp.concatenate([jnp.where(sel, aff, 0.0), jnp.zeros((pad_rows, s), F32)], axis=0)
    slotr_ref[...] = slot
    for j in range(s // LANES):
        cols = slice(j * LANES, (j + 1) * LANES)
        slotc_ref[cols, :] = slot_p[:, cols].T.astype(BF16)
        gatec_ref[cols, :] = gate_p[:, cols].T.astype(BF16)


def _route(h3, g, router_w, cap):
    b, s, d = h3.shape
    n_e = router_w.shape[1]
    tri = jnp.asarray(np.triu(np.ones((LANES, LANES))), BF16)
    rw_t = router_w.T.astype(F32)
    rw_hi = rw_t.astype(BF16)
    rw_lo = (rw_t - rw_hi.astype(F32)).astype(BF16)
    per_seq = lambda *shape: pl.BlockSpec((None,) + shape, lambda i: (i,) + (0,) * len(shape))
    return pl.pallas_call(
        functools.partial(_route_kernel, cap=cap, chunk=256, tile=TOKEN_TILE),
        out_shape=(
            jax.ShapeDtypeStruct((b, s, d), BF16),
            jax.ShapeDtypeStruct((b, n_e, s), F32),
            jax.ShapeDtypeStruct((b, s, LANES), BF16),
            jax.ShapeDtypeStruct((b, s, LANES), BF16),
            jax.ShapeDtypeStruct((b, n_e, LANES), jnp.int32),
        ),
        grid=(b,),
        in_specs=[per_seq(s, d), _full_spec((1, d)), _full_spec((2 * n_e, d)), _full_spec((LANES, LANES))],
        out_specs=(per_seq(s, d), per_seq(n_e, s), per_seq(s, LANES), per_seq(s, LANES), per_seq(n_e, LANES)),
        compiler_params=_cparams("parallel"),
        name="route",
    )(h3, g.reshape(1, d), jnp.concatenate([rw_hi, rw_lo], axis=0), tri)


def _chunk_tables(cnt, cap):
    n_e = cnt.shape[1]
    kmax = n_e * (TOKEN_TILE // CHUNK + 1)
    kmax = -(-kmax // CHUNKS_PER_GROUP) * CHUNKS_PER_GROUP
    c0 = jnp.cumsum(cnt, axis=2) - cnt
    a0 = c0 // CHUNK
    a1 = jnp.where(cnt > 0, (c0 + cnt + CHUNK - 1) // CHUNK, a0)
    nch = (a1 - a0).transpose(0, 2, 1)
    a0 = a0.transpose(0, 2, 1)
    pos_end = jnp.cumsum(nch, axis=2)
    pos = pos_end - nch
    total = pos_end[..., -1]
    k = jnp.arange(kmax, dtype=jnp.int32)
    e_of_k = jnp.sum((k[None, None, :, None] >= pos_end[:, :, None, :]).astype(jnp.int32), axis=-1)
    e_of_k = jnp.minimum(e_of_k, n_e - 1)
    is_e = e_of_k[..., None] == jnp.arange(n_e, dtype=jnp.int32)
    q = k + jnp.sum(jnp.where(is_e, (a0 - pos)[:, :, None, :], 0), axis=-1)
    valid = k[None, None, :] < total[..., None]
    e_tab = jnp.where(valid, e_of_k, 0)
    slot_tab = jnp.where(valid, q * CHUNK, -(1 << 20))
    dst_tab = jnp.where(valid, e_of_k * cap + q * CHUNK, 0)
    groups = (total + CHUNKS_PER_GROUP - 1) // CHUNKS_PER_GROUP
    return jnp.concatenate([e_tab, slot_tab, dst_tab, groups[..., None]], axis=-1).astype(jnp.int32), kmax


def _gather_kernel(tab_ref, xn_ref, slotr_ref, xs_ref, rows_ref, *, kmax):
    @pl.when(pl.program_id(1) == 0)
    def _():
        xs_ref[...] = jnp.zeros_like(xs_ref)

    sub = lax.broadcasted_iota(jnp.int32, (CHUNK, slotr_ref.shape[1]), 0).astype(F32)

    def group(g, carry):
        base = g * CHUNKS_PER_GROUP
        onehot = []
        for c in range(CHUNKS_PER_GROUP):
            slots = slotr_ref[pl.ds(tab_ref[0, base + c], 1), :]
            want = sub + tab_ref[0, kmax + base + c].astype(F32)
            onehot.append(jnp.where(slots == want, 1.0, 0.0).astype(BF16))
        rows_ref[...] = _dot(jnp.concatenate(onehot, axis=0), xn_ref[...]).astype(BF16)
        for c0 in range(0, CHUNKS_PER_GROUP, 4):
            cs = range(c0, c0 + 4)
            dsts = [pl.ds(pl.multiple_of(tab_ref[0, 2 * kmax + base + c], CHUNK), CHUNK) for c in cs]
            sums = [xs_ref[dst, :] + rows_ref[c * CHUNK:(c + 1) * CHUNK, :] for c, dst in zip(cs, dsts)]
            for dst, total in reversed(list(zip(dsts, sums))):
                xs_ref[dst, :] = total
        return carry

    lax.fori_loop(0, tab_ref[0, 3 * kmax], group, 0)


def _expert_kernel(xs_ref, wg_ref, wu_ref, wd_ref, y_ref):
    nb, cap, d = xs_ref.shape
    xs = xs_ref[...].reshape(nb * cap, d)
    hg = _dot(xs, wg_ref[...])
    hu = _dot(xs, wu_ref[...])
    hdn = (hg * jax.nn.sigmoid(hg) * hu).astype(BF16)
    y_ref[...] = _dot(hdn, wd_ref[...]).astype(BF16).reshape(nb, cap, d)


def _scatter_ple_kernel(tab_ref, y_ref, slotc_ref, gatec_ref, h_ref, p_ref, g_ref, wg_ref, wp_ref, o_ref,
                        yg_ref, acc_ref, *, kmax):
    acc_ref[...] = jnp.zeros_like(acc_ref)
    lane = lax.broadcasted_iota(jnp.int32, (1, GROUP), 1)
    sub = lax.broadcasted_iota(jnp.int32, (LANES, GROUP), 0)

    def group(g, carry):
        base = g * CHUNKS_PER_GROUP
        expert = jnp.zeros((1, GROUP), jnp.int32)
        want = jnp.zeros((1, GROUP), jnp.int32)
        for c in range(CHUNKS_PER_GROUP):
            in_chunk = (lane >= c * CHUNK) & (lane < (c + 1) * CHUNK)
            expert = jnp.where(in_chunk, tab_ref[0, base + c], expert)
            want = jnp.where(in_chunk, tab_ref[0, kmax + base + c] + lane - c * CHUNK, want)
            src = pl.ds(pl.multiple_of(tab_ref[0, 2 * kmax + base + c], CHUNK), CHUNK)
            yg_ref[c * CHUNK:(c + 1) * CHUNK, :] = y_ref[src, :]
        pick = jnp.where(sub == expert, 1.0, 0.0).astype(BF16)
        both = _dot(jnp.concatenate([slotc_ref[...], gatec_ref[...]], axis=0), pick)
        tile = slotc_ref.shape[0]
        weights = jnp.where(both[:tile] == want.astype(F32), both[tile:], 0.0).astype(BF16)
        acc_ref[...] += _dot(weights, yg_ref[...])
        return carry

    lax.fori_loop(0, tab_ref[0, 3 * kmax], group, 0)
    h = h_ref[...] + acc_ref[...]
    xn = _rms(h, g_ref[...]).astype(BF16)
    gate = jax.nn.sigmoid(_dot(xn, wg_ref[...]))
    o_ref[...] = h + gate * _dot(p_ref[...].astype(BF16), wp_ref[...])


def _expert_ffn_ple(h3, xn3, slot_row, slot_col, gate_col, cnt, w_gate, w_up, w_down, cap, p3, layer, ple_g, ple_wg,
                    ple_wp, seqs_per_step=4):
    b, s, d = xn3.shape
    pd = p3.shape[2]
    n_e, _, ff = w_gate.shape
    nt = s // TOKEN_TILE
    tables, kmax = _chunk_tables(cnt[:, :, :nt], cap)
    tables = tables.reshape(b * nt, 1, 3 * kmax + 1)
    tab_spec = pl.BlockSpec((None, 1, 3 * kmax + 1), lambda i, t: (i * nt + t, 0, 0), memory_space=pltpu.SMEM)
    xs = pl.pallas_call(
        functools.partial(_gather_kernel, kmax=kmax),
        out_shape=jax.ShapeDtypeStruct((b, n_e * cap, d), BF16),
        grid=(b, nt),
        in_specs=[
            tab_spec,
            pl.BlockSpec((None, TOKEN_TILE, d), lambda i, t: (i, t, 0)),
            pl.BlockSpec((None, n_e, TOKEN_TILE), lambda i, t: (i, 0, t)),
        ],
        out_specs=pl.BlockSpec((None, n_e * cap, d), lambda i, t: (i, 0, 0)),
        scratch_shapes=[pltpu.VMEM((GROUP, d), BF16)],
        compiler_params=_cparams("parallel", "arbitrary"),
        name="expert_gather",
    )(tables, xn3, slot_row)
    nb = seqs_per_step
    y = pl.pallas_call(
        _expert_kernel,
        out_shape=jax.ShapeDtypeStruct((b, n_e, cap, d), BF16),
        grid=(n_e, b // nb),
        in_specs=[
            pl.BlockSpec((nb, None, cap, d), lambda e, i: (i, e, 0, 0)),
            pl.BlockSpec((None, d, ff), lambda e, i: (e, 0, 0)),
            pl.BlockSpec((None, d, ff), lambda e, i: (e, 0, 0)),
            pl.BlockSpec((None, ff, d), lambda e, i: (e, 0, 0)),
        ],
        out_specs=pl.BlockSpec((nb, None, cap, d), lambda e, i: (i, e, 0, 0)),
        compiler_params=_cparams("parallel", "parallel"),
        name="expert_mlp",
    )(xs.reshape(b, n_e, cap, d), w_gate.astype(BF16), w_up.astype(BF16), w_down.astype(BF16))
    tile_rows = lambda width: pl.BlockSpec((None, TOKEN_TILE, width), lambda i, t: (i, t, 0))
    return pl.pallas_call(
        functools.partial(_scatter_ple_kernel, kmax=kmax),
        out_shape=jax.ShapeDtypeStruct((b, s, d), F32),
        grid=(b, nt),
        in_specs=[
            tab_spec,
            pl.BlockSpec((None, n_e * cap, d), lambda i, t: (i, 0, 0)),
            tile_rows(LANES), tile_rows(LANES), tile_rows(d),
            pl.BlockSpec((None, TOKEN_TILE, pd), lambda i, t: (layer, i * nt + t, 0)),
            _full_spec((1, d)), _full_spec((d, d)), _full_spec((pd, d)),
        ],
        out_specs=tile_rows(d),
        scratch_shapes=[pltpu.VMEM((GROUP, d), BF16), pltpu.VMEM((TOKEN_TILE, d), F32)],
        compiler_params=_cparams("parallel", "arbitrary"),
        name="expert_scatter_ple",
    )(tables, y.reshape(b, n_e * cap, d), slot_col, gate_col, h3, p3, ple_g.reshape(1, d), ple_wg.astype(BF16),
      ple_wp.astype(BF16))


def kernel(x, p, norm_mix_g, norm_ffn_g, w_out, router_w, exp_w_gate, exp_w_up, exp_w_down, ple_norm_g, ple_gate_w, ple_proj_w, a_w_in, a_vnorm_g, a_w_s, a_b_s, b_w_in, b_qnorm_g, b_knorm_g, b_sink, c_w_in, c_qnorm_g, c_knorm_g, c_rpb):
    b, s, d = x.shape
    depth = norm_mix_g.shape[0]
    t = b * s
    cap = max(1, EC_CAPACITY_FACTOR * s // N_EXPERTS)
    scale = HEAD_DIM ** -0.5
    h = x.reshape(t, d)
    for i in range(depth):
        kind = i % N_MIXERS
        j = i // N_MIXERS
        if kind == 0:
            h = _mixer_a(h, norm_mix_g[i], a_w_in[j], a_vnorm_g[j], a_w_s[j], a_b_s[j], w_out[i])
        elif kind == 1:
            gain = jnp.concatenate([jnp.tile(b_qnorm_g[j] * scale, B_HEADS), jnp.tile(b_knorm_g[j], 2 * B_KV_HEADS)])
            qw, kw = B_HEADS * HEAD_DIM, B_KV_HEADS * HEAD_DIM
            w_b = jnp.concatenate([b_w_in[j][:, :qw], _dup_heads(b_w_in[j][:, qw:qw + kw], B_KV_HEADS),
                                   _dup_heads(b_w_in[j][:, qw + kw:], B_KV_HEADS)], axis=1)
            qkv = _norm_proj(h, norm_mix_g[i], w_b, gain, qw + 2 * kw)
            mix = _attn_b(qkv.reshape(b, s, -1), b_sink[j])
            h = _proj_residual(mix.reshape(t, -1), w_out[i], h)
        else:
            gain = jnp.concatenate([jnp.tile(c_qnorm_g[j] * scale, C_HEADS), jnp.tile(c_knorm_g[j], C_HEADS)])
            qkv = _norm_proj(h, norm_mix_g[i], c_w_in[j], gain, 2 * C_HEADS * HEAD_DIM)
            mix = _attn_c(qkv.reshape(b, s // GRID_W, GRID_W, -1), c_rpb[j])
            h = _proj_residual(mix.reshape(t, -1), w_out[i], h)
        xn, slot_row, slot_col, gate_col, cnt = _route(h.reshape(b, s, d), norm_ffn_g[i], router_w[i], cap)
        h = _expert_ffn_ple(h.reshape(b, s, d), xn, slot_row, slot_col, gate_col, cnt, exp_w_gate[i], exp_w_up[i],
                            exp_w_down[i], cap, p.reshape(depth, t, -1), i, ple_norm_g[i], ple_gate_w[i],
                            ple_proj_w[i]).reshape(t, d)
    return h.reshape(b, s, d)
```

```python
import functools

import numpy as np
import jax
import jax.numpy as jnp
from jax import lax
from jax.experimental import pallas as pl
from jax.experimental.pallas import tpu as pltpu

F32 = jnp.float32
BF16 = jnp.bfloat16

RMS_EPS = 1e-6
LN_EPS = 1e-5
NEG = -1e30
HEAD_DIM = 64
GRID_W = 64
A_GROUPS = 8
A_CHUNK = 128
B_HEADS = 16
B_KV_HEADS = 4
B_BLOCK = 128
C_HEADS = 16
C_WIN_ROWS = 8
C_WIN_COLS = 16
N_EXPERTS = 16
EC_CAPACITY_FACTOR = 2
N_MIXERS = 3

LANES = 128
MXU_DIM = 256
VMEM_LIMIT_BYTES = 60 * 1024 * 1024

TOKEN_TILE = 256
CHUNK = 16
GROUP = 2 * MXU_DIM
CHUNKS_PER_GROUP = GROUP // CHUNK


def _cparams(*sem):
    return pltpu.CompilerParams(dimension_semantics=sem, vmem_limit_bytes=VMEM_LIMIT_BYTES)


def _rms(xf, g):
    return xf * lax.rsqrt(jnp.mean(xf * xf, axis=-1, keepdims=True) + RMS_EPS) * g


def _dot(a, b):
    return jnp.dot(a, b, preferred_element_type=F32)


def _dot_nt(a, b):
    return lax.dot_general(a, b, (((1,), (1,)), ((), ())), preferred_element_type=F32)


def _full_spec(shape):
    nd = len(shape)
    return pl.BlockSpec(shape, lambda *_: (0,) * nd)


def _mixer_a_kernel(h_ref, g_ref, win_ref, vg_ref, ws_ref, bias_ref, wout_ref, o_ref, mix_ref):
    tm = h_ref.shape[0]
    width = vg_ref.shape[1]
    gw = width // A_GROUPS
    h = h_ref[...]
    xn = _rms(h, g_ref[...]).astype(BF16)
    z = _dot(xn, win_ref[...])
    z = 0.5 * z * (1.0 + jnp.tanh(np.sqrt(2.0 / np.pi).astype(np.float32)
                                  * (z + 0.044715 * (z * z * z))))
    u = z[:, :width]
    v = z[:, width:]
    mu = jnp.mean(v, axis=-1, keepdims=True)
    vc = v - mu
    var = jnp.mean(vc * vc, axis=-1, keepdims=True)
    vn = (vc * lax.rsqrt(var + LN_EPS) * vg_ref[...]).astype(BF16)
    for c in range(tm // A_CHUNK):
        rows = slice(c * A_CHUNK, (c + 1) * A_CHUNK)
        for g in range(A_GROUPS):
            cols = slice(g * gw, (g + 1) * gw)
            s = _dot(ws_ref[g], vn[rows, cols]) + bias_ref[:, cols]
            mix_ref[rows, cols] = (u[rows, cols] * s).astype(BF16)
    o_ref[...] = h + _dot(mix_ref[...], wout_ref[...])


def _mixer_a(h2, g, w_in, vnorm_g, w_s, b_s, w_out, tm=512):
    t, d = h2.shape
    width = vnorm_g.shape[0]
    gw = width // A_GROUPS
    bias = jnp.repeat(b_s.T.astype(F32), gw, axis=1)
    return pl.pallas_call(
        _mixer_a_kernel,
        out_shape=jax.ShapeDtypeStruct((t, d), F32),
        grid=(t // tm,),
        in_specs=[
            pl.BlockSpec((tm, d), lambda i: (i, 0)),
            _full_spec((1, d)),
            _full_spec((d, 2 * width)),
            _full_spec((1, width)),
            _full_spec((A_GROUPS, A_CHUNK, A_CHUNK)),
            _full_spec((A_CHUNK, width)),
            _full_spec((width, d)),
        ],
        out_specs=pl.BlockSpec((tm, d), lambda i: (i, 0)),
        scratch_shapes=[pltpu.VMEM((tm, width), BF16)],
        compiler_params=_cparams("parallel"),
        name="mixer_a",
    )(h2, g.reshape(1, d), w_in.astype(BF16), vnorm_g.reshape(1, width), w_s.astype(BF16), bias,
      w_out.astype(BF16))


def _norm_proj_kernel(h_ref, g_ref, w_ref, hg_ref, bd_ref, o_ref, *, n_norm_cols):
    xn = _rms(h_ref[...], g_ref[...]).astype(BF16)
    acc = _dot(xn, w_ref[...])
    for j in range(n_norm_cols // MXU_DIM):
        cols = slice(j * MXU_DIM, (j + 1) * MXU_DIM)
        blk = acc[:, cols]
        ss = _dot((blk * blk).astype(BF16), bd_ref[...])
        o_ref[:, cols] = (blk * lax.rsqrt(ss * (1.0 / HEAD_DIM) + RMS_EPS) * hg_ref[:, cols]).astype(BF16)
    o_ref[:, n_norm_cols:] = acc[:, n_norm_cols:].astype(BF16)


def _norm_proj(h2, g, w, head_gain, n_norm_cols, tm=512):
    t, d = h2.shape
    n = w.shape[1]
    blockdiag = jnp.asarray(np.kron(np.eye(MXU_DIM // HEAD_DIM), np.ones((HEAD_DIM, HEAD_DIM))), BF16)
    return pl.pallas_call(
        functools.partial(_norm_proj_kernel, n_norm_cols=n_norm_cols),
        out_shape=jax.ShapeDtypeStruct((t, n), BF16),
        grid=(t // tm,),
        in_specs=[
            pl.BlockSpec((tm, d), lambda i: (i, 0)),
            _full_spec((1, d)),
            _full_spec((d, n)),
            _full_spec((1, n_norm_cols)),
            _full_spec((MXU_DIM, MXU_DIM)),
        ],
        out_specs=pl.BlockSpec((tm, n), lambda i: (i, 0)),
        compiler_params=_cparams("parallel"),
        name="norm_proj",
    )(h2, g.reshape(1, d), w.astype(BF16), head_gain.reshape(1, n_norm_cols).astype(F32), blockdiag)


def _proj_residual_kernel(a_ref, w_ref, h_ref, o_ref):
    o_ref[...] = h_ref[...] + _dot(a_ref[...], w_ref[...])


def _proj_residual(a2, w, h2, tm=512):
    t, d = h2.shape
    k = a2.shape[1]
    return pl.pallas_call(
        _proj_residual_kernel,
        out_shape=jax.ShapeDtypeStruct((t, d), F32),
        grid=(t // tm,),
        in_specs=[
            pl.BlockSpec((tm, k), lambda i: (i, 0)),
            _full_spec((k, d)),
            pl.BlockSpec((tm, d), lambda i: (i, 0)),
        ],
        out_specs=pl.BlockSpec((tm, d), lambda i: (i, 0)),
        compiler_params=_cparams("parallel"),
        name="proj_residual",
    )(a2, w.astype(BF16), h2)


def _alibi_slopes(n):
    return np.array([2.0 ** (-8.0 * (h + 1) / n) for h in range(n)], dtype=np.float32)


def _attn_b_tables():
    span = 3 * B_BLOCK
    rel = np.arange(span)[None, :] - B_BLOCK - np.arange(B_BLOCK)[:, None]
    in_window = np.abs(rel) <= B_BLOCK
    alibi = (-_alibi_slopes(B_HEADS)[:, None, None] * np.abs(rel)[None]).astype(np.float32)
    kblk = np.arange(span) // B_BLOCK
    tabs = []
    for kind in range(3):
        valid = in_window & ~((kind == 0) & (kblk == 0))[None, :] & ~((kind == 2) & (kblk == 2))[None, :]
        tabs.append(np.where(valid[None], alibi, np.float32(NEG)))
    return np.stack(tabs).astype(np.float32)


def _attn_b_kernel(sink_ref, q_ref, kp_ref, kc_ref, kn_ref, vp_ref, vc_ref, vn_ref, tab_ref, o_ref):
    grp = B_HEADS // B_KV_HEADS
    nq = q_ref.shape[0]
    lane = lax.broadcasted_iota(jnp.int32, (1, LANES), 1)
    low = lane < HEAD_DIM
    scores = []
    for kh in range(B_KV_HEADS):
        tile = slice(kh * LANES, (kh + 1) * LANES)
        kt = jnp.concatenate([kp_ref[:, tile], kc_ref[:, tile], kn_ref[:, tile]], axis=0)
        rows = []
        for qt in range(kh * grp // 2, (kh + 1) * grp // 2):
            q2 = q_ref[:, qt * LANES:(qt + 1) * LANES]
            zero = jnp.zeros_like(q2)
            rows += [jnp.where(low, q2, zero), jnp.where(low, zero, q2)]
        scores.append(_dot_nt(jnp.concatenate(rows, axis=0), kt) + tab_ref[kh])
    sh = jnp.concatenate(scores, axis=0)
    heads = [slice(h * nq, (h + 1) * nq) for h in range(B_HEADS)]
    rowmax = jnp.broadcast_to(jnp.max(sh, axis=-1, keepdims=True), (sh.shape[0], LANES))
    m = jnp.concatenate([jnp.maximum(rowmax[hs], sink_ref[h]) for h, hs in enumerate(heads)], axis=0)
    sink_term = jnp.concatenate([jnp.exp(sink_ref[h] - m[hs]) for h, hs in enumerate(heads)], axis=0)
    pe = jnp.exp(sh - jnp.concatenate([m] * (sh.shape[1] // LANES), axis=1)).astype(BF16)
    ones = jnp.ones((sh.shape[1], LANES), BF16)
    for kh in range(B_KV_HEADS):
        tile = slice(kh * LANES, (kh + 1) * LANES)
        grows = slice(kh * grp * nq, (kh + 1) * grp * nq)
        vt = jnp.concatenate([vp_ref[:, tile], vc_ref[:, tile], vn_ref[:, tile]], axis=0)
        od = _dot(pe[grows], jnp.concatenate([vt, ones], axis=1))
        o = od[:, :LANES] / (od[:, LANES:] + sink_term[grows])
        for j in range(grp // 2):
            qt = kh * grp // 2 + j
            even, odd = o[(2 * j) * nq:(2 * j + 1) * nq], o[(2 * j + 1) * nq:(2 * j + 2) * nq]
            o_ref[:, qt * LANES:(qt + 1) * LANES] = jnp.where(low, even, odd).astype(o_ref.dtype)


def _dup_heads(w, n_heads):
    d = w.shape[0]
    return jnp.repeat(w.reshape(d, n_heads, 1, HEAD_DIM), 2, axis=2).reshape(d, n_heads * LANES)


def _attn_b(qkv3, sink):
    b, s, _ = qkv3.shape
    nb = s // B_BLOCK
    grp = B_HEADS // B_KV_HEADS
    qw = B_HEADS * HEAD_DIM
    kvw = B_KV_HEADS * LANES
    kcol = qw // kvw
    vcol = kcol + 1
    tabs = jnp.asarray(_attn_b_tables().reshape(3, B_KV_HEADS, grp * B_BLOCK, 3 * B_BLOCK))
    prev = lambda n: jnp.maximum(n - 1, 0)
    nxt = lambda n: jnp.minimum(n + 1, nb - 1)
    kind = lambda n: jnp.where(n == 0, 0, jnp.where(n == nb - 1, 2, 1))
    kv = lambda col, f: pl.BlockSpec((None, B_BLOCK, kvw), lambda bi, n: (bi, f(n), col))
    same = lambda n: n
    return pl.pallas_call(
        _attn_b_kernel,
        out_shape=jax.ShapeDtypeStruct((b, s, qw), BF16),
        grid=(b, nb),
        in_specs=[
            pl.BlockSpec(memory_space=pltpu.SMEM),
            pl.BlockSpec((None, B_BLOCK, qw), lambda bi, n: (bi, n, 0)),
            kv(kcol, prev), kv(kcol, same), kv(kcol, nxt),
            kv(vcol, prev), kv(vcol, same), kv(vcol, nxt),
            pl.BlockSpec((None, B_KV_HEADS, grp * B_BLOCK, 3 * B_BLOCK), lambda bi, n: (kind(n), 0, 0, 0)),
        ],
        out_specs=pl.BlockSpec((None, B_BLOCK, qw), lambda bi, n: (bi, n, 0)),
        compiler_params=_cparams("parallel", "arbitrary"),
        name="attn_b",
    )(sink.astype(F32), qkv3, qkv3, qkv3, qkv3, qkv3, qkv3, qkv3, tabs)


def _attn_c_table(rpb, rows):
    w = GRID_W
    kr_n = min(C_WIN_ROWS, rows)
    c = np.arange(w)
    cs = np.clip(c - C_WIN_COLS // 2, 0, w - C_WIN_COLS)
    kcol = np.arange(w)
    colmask = (kcol[None, :] >= cs[:, None]) & (kcol[None, :] < cs[:, None] + C_WIN_COLS)
    coloff = np.clip(kcol[None, :] - c[:, None] + C_WIN_COLS - 1, 0, 2 * C_WIN_COLS - 2)
    n_off = 2 * C_WIN_ROWS - kr_n
    n_h, n_ri, n_co = rpb.shape
    idx = (np.arange(n_ri)[None, :, None] * n_co + coloff[:, None, :]).astype(np.int32)
    bias = jnp.take(rpb.astype(F32).reshape(n_h, n_ri * n_co), jnp.asarray(idx.reshape(-1)), axis=1)
    bias = jnp.where(jnp.asarray(colmask)[None, :, None, :], bias.reshape(n_h, w, n_ri, w), NEG)
    bias = bias.reshape(n_h, w, n_ri * w)
    slabs = jnp.stack([bias[:, :, n * w:(n + kr_n) * w] for n in range(n_off)])
    return slabs.reshape(n_off, n_h // 2, 2 * w, kr_n * w)


def _attn_c_kernel(q_ref, k_ref, v_ref, tab_ref, o_ref):
    nk = k_ref.shape[1] * k_ref.shape[2]
    nq = q_ref.shape[1]
    lane = lax.broadcasted_iota(jnp.int32, (1, LANES), 1)
    low = lane < HEAD_DIM
    ones = jnp.ones((nk, LANES), BF16)

    def one_sequence(i, carry):
        scores = []
        for t in range(C_HEADS // 2):
            cols = slice(t * LANES, (t + 1) * LANES)
            q2 = q_ref[i, :, cols]
            zero = jnp.zeros_like(q2)
            qq = jnp.concatenate([jnp.where(low, q2, zero), jnp.where(low, zero, q2)], axis=0)
            kt = k_ref[i, :, :, cols].reshape(nk, LANES)
            scores.append(_dot_nt(qq, kt) + tab_ref[t])
        sh = jnp.concatenate(scores, axis=0)
        m = jnp.broadcast_to(jnp.max(sh, axis=-1, keepdims=True), (sh.shape[0], LANES))
        pe = jnp.exp(sh - jnp.concatenate([m] * (nk // LANES), axis=1)).astype(BF16)
        for t in range(C_HEADS // 2):
            cols = slice(t * LANES, (t + 1) * LANES)
            rows = slice(2 * t * nq, 2 * (t + 1) * nq)
            vt = v_ref[i, :, :, cols].reshape(nk, LANES)
            od = _dot(pe[rows], jnp.concatenate([vt, ones], axis=1))
            o = od[:, :LANES] / od[:, LANES:]
            o_ref[i, :, cols] = jnp.where(low, o[:nq], o[nq:]).astype(o_ref.dtype)
        return carry

    lax.fori_loop(0, q_ref.shape[0], one_sequence, 0)


def _attn_c(qkv4, rpb, seqs_per_step=4):
    b, rows, w, _ = qkv4.shape
    hw = C_HEADS * HEAD_DIM
    kr_n = min(C_WIN_ROWS, rows)
    nb = seqs_per_step
    table = _attn_c_table(rpb, rows)
    row_start = lambda r: jnp.clip(r - kr_n // 2, 0, rows - kr_n)
    el = pl.Element
    return pl.pallas_call(
        _attn_c_kernel,
        out_shape=jax.ShapeDtypeStruct((b, rows, w, hw), BF16),
        grid=(rows, b // nb),
        in_specs=[
            pl.BlockSpec((nb, None, w, hw), lambda r, bi: (bi, r, 0, 0)),
            pl.BlockSpec((el(nb), el(kr_n), el(w), el(hw)), lambda r, bi: (bi * nb, row_start(r), 0, hw)),
            pl.BlockSpec((el(nb), el(kr_n), el(w), el(hw)), lambda r, bi: (bi * nb, row_start(r), 0, 2 * hw)),
            pl.BlockSpec((None, C_HEADS // 2, 2 * w, kr_n * w),
                         lambda r, bi: (row_start(r) - r + C_WIN_ROWS - 1, 0, 0, 0)),
        ],
        out_specs=pl.BlockSpec((nb, None, w, hw), lambda r, bi: (bi, r, 0, 0)),
        compiler_params=_cparams("arbitrary", "arbitrary"),
        name="attn_c",
    )(qkv4, qkv4, qkv4, table)


def _lane_cumsum(x, tri):
    e, s = x.shape
    off = jnp.zeros((e, 1), F32)
    outs = []
    for j in range(s // LANES):
        blk = x[:, j * LANES:(j + 1) * LANES]
        outs.append(_dot(blk.astype(BF16), tri) + off)
        off = off + jnp.sum(blk, axis=1, keepdims=True)
    return jnp.concatenate(outs, axis=1)


def _route_kernel(h_ref, g_ref, rwt_ref, tri_ref, xn_ref, slotr_ref, slotc_ref, gatec_ref, cnt_ref,
                  *, cap, chunk, tile):
    s = h_ref.shape[0]
    n_e = rwt_ref.shape[0] // 2
    logits = []
    for c in range(s // chunk):
        rows = slice(c * chunk, (c + 1) * chunk)
        xn = _rms(h_ref[rows, :], g_ref[...])
        xn_hi = xn.astype(BF16)
        xn_ref[rows, :] = xn_hi
        xn_lo = (xn - xn_hi.astype(F32)).astype(BF16)
        both = _dot_nt(rwt_ref[...], xn_hi)
        logits.append(both[:n_e] + both[n_e:] + _dot_nt(rwt_ref[:n_e, :], xn_lo))
    lg = jnp.concatenate(logits, axis=1)
    ex = jnp.exp(lg - jnp.max(lg, axis=0, keepdims=True))
    aff = ex / jnp.sum(ex, axis=0, keepdims=True)
    bits = pltpu.bitcast(aff, jnp.int32)
    thr = jnp.zeros((n_e, 1), jnp.int32)
    for shift in range(27, -1, -3):
        digit = jnp.zeros((n_e, 1), jnp.int32)
        for j in range(1, 8):
            cnt = jnp.sum(jnp.where(bits >= (thr | (j << shift)), 1.0, 0.0), axis=1, keepdims=True)
            digit = digit + jnp.where(cnt >= cap, 1, 0)
        thr = thr | (digit * (1 << shift))
    gt = bits > thr
    eq = bits == thr
    need = cap - jnp.sum(jnp.where(gt, 1.0, 0.0), axis=1, keepdims=True)
    eqf = jnp.where(eq, 1.0, 0.0)
    eq_rank = _lane_cumsum(eqf, tri_ref[...]) - eqf
    sel = gt | (eq & (eq_rank < need))
    self_ = jnp.where(sel, 1.0, 0.0)
    slot = jnp.where(sel, _lane_cumsum(self_, tri_ref[...]) - self_, -1.0)
    lane = lax.broadcasted_iota(jnp.int32, (n_e, LANES), 1)
    counts = jnp.zeros((n_e, LANES), F32)
    for i in range(s // tile):
        counts = jnp.where(lane == i, jnp.sum(self_[:, i * tile:(i + 1) * tile], axis=1, keepdims=True), counts)
    cnt_ref[...] = counts.astype(jnp.int32)
    pad_rows = LANES - n_e
    slot_p = jnp.concatenate([slot, jnp.full((pad_rows, s), -1.0, F32)], axis=0)
    gate_p = jnp.concatenate([jnp.where(sel, aff, 0.0), jnp.zeros((pad_rows, s), F32)], axis=0)
    slotr_ref[...] = slot
    for j in range(s // LANES):
        cols = slice(j * LANES, (j + 1) * LANES)
        slotc_ref[cols, :] = slot_p[:, cols].T.astype(BF16)
        gatec_ref[cols, :] = gate_p[:, cols].T.astype(BF16)


def _route(h3, g, router_w, cap):
    b, s, d = h3.shape
    n_e = router_w.shape[1]
    tri = jnp.asarray(np.triu(np.ones((LANES, LANES))), BF16)
    rw_t = router_w.T.astype(F32)
    rw_hi = rw_t.astype(BF16)
    rw_lo = (rw_t - rw_hi.astype(F32)).astype(BF16)
    per_seq = lambda *shape: pl.BlockSpec((None,) + shape, lambda i: (i,) + (0,) * len(shape))
    return pl.pallas_call(
        functools.partial(_route_kernel, cap=cap, chunk=256, tile=TOKEN_TILE),
        out_shape=(
            jax.ShapeDtypeStruct((b, s, d), BF16),
            jax.ShapeDtypeStruct((b, n_e, s), F32),
            jax.ShapeDtypeStruct((b, s, LANES), BF16),
            jax.ShapeDtypeStruct((b, s, LANES), BF16),
            jax.ShapeDtypeStruct((b, n_e, LANES), jnp.int32),
        ),
        grid=(b,),
        in_specs=[per_seq(s, d), _full_spec((1, d)), _full_spec((2 * n_e, d)), _full_spec((LANES, LANES))],
        out_specs=(per_seq(s, d), per_seq(n_e, s), per_seq(s, LANES), per_seq(s, LANES), per_seq(n_e, LANES)),
        compiler_params=_cparams("parallel"),
        name="route",
    )(h3, g.reshape(1, d), jnp.concatenate([rw_hi, rw_lo], axis=0), tri)


def _chunk_tables(cnt, cap):
    n_e = cnt.shape[1]
    kmax = n_e * (TOKEN_TILE // CHUNK + 1)
    kmax = -(-kmax // CHUNKS_PER_GROUP) * CHUNKS_PER_GROUP
    c0 = jnp.cumsum(cnt, axis=2) - cnt
    a0 = c0 // CHUNK
    a1 = jnp.where(cnt > 0, (c0 + cnt + CHUNK - 1) // CHUNK, a0)
    nch = (a1 - a0).transpose(0, 2, 1)
    a0 = a0.transpose(0, 2, 1)
    pos_end = jnp.cumsum(nch, axis=2)
    pos = pos_end - nch
    total = pos_end[..., -1]
    k = jnp.arange(kmax, dtype=jnp.int32)
    e_of_k = jnp.sum((k[None, None, :, None] >= pos_end[:, :, None, :]).astype(jnp.int32), axis=-1)
    e_of_k = jnp.minimum(e_of_k, n_e - 1)
    is_e = e_of_k[..., None] == jnp.arange(n_e, dtype=jnp.int32)
    q = k + jnp.sum(jnp.where(is_e, (a0 - pos)[:, :, None, :], 0), axis=-1)
    valid = k[None, None, :] < total[..., None]
    e_tab = jnp.where(valid, e_of_k, 0)
    slot_tab = jnp.where(valid, q * CHUNK, -(1 << 20))
    dst_tab = jnp.where(valid, e_of_k * cap + q * CHUNK, 0)
    groups = (total + CHUNKS_PER_GROUP - 1) // CHUNKS_PER_GROUP
    return jnp.concatenate([e_tab, slot_tab, dst_tab, groups[..., None]], axis=-1).astype(jnp.int32), kmax


def _gather_kernel(tab_ref, xn_ref, slotr_ref, xs_ref, rows_ref, *, kmax):
    @pl.when(pl.program_id(1) == 0)
    def _():
        xs_ref[...] = jnp.zeros_like(xs_ref)

    sub = lax.broadcasted_iota(jnp.int32, (CHUNK, slotr_ref.shape[1]), 0).astype(F32)

    def group(g, carry):
        base = g * CHUNKS_PER_GROUP
        onehot = []
        for c in range(CHUNKS_PER_GROUP):
            slots = slotr_ref[pl.ds(tab_ref[0, base + c], 1), :]
            want = sub + tab_ref[0, kmax + base + c].astype(F32)
            onehot.append(jnp.where(slots == want, 1.0, 0.0).astype(BF16))
        rows_ref[...] = _dot(jnp.concatenate(onehot, axis=0), xn_ref[...]).astype(BF16)
        for c0 in range(0, CHUNKS_PER_GROUP, 4):
            cs = range(c0, c0 + 4)
            dsts = [pl.ds(pl.multiple_of(tab_ref[0, 2 * kmax + base + c], CHUNK), CHUNK) for c in cs]
            sums = [xs_ref[dst, :] + rows_ref[c * CHUNK:(c + 1) * CHUNK, :] for c, dst in zip(cs, dsts)]
            for dst, total in reversed(list(zip(dsts, sums))):
                xs_ref[dst, :] = total
        return carry

    lax.fori_loop(0, tab_ref[0, 3 * kmax], group, 0)


def _expert_kernel(xs_ref, wg_ref, wu_ref, wd_ref, y_ref, wg_bf, wu_bf, wd_bf):
    @pl.when(pl.program_id(1) == 0)
    def _():
        wg_bf[...] = wg_ref[...].astype(BF16)
        wu_bf[...] = wu_ref[...].astype(BF16)
        wd_bf[...] = wd_ref[...].astype(BF16)

    nb, cap, d = xs_ref.shape
    xs = xs_ref[...].reshape(nb * cap, d)
    hg = _dot(xs, wg_bf[...])
    hu = _dot(xs, wu_bf[...])
    hdn = (hg * jax.nn.sigmoid(hg) * hu).astype(BF16)
    y_ref[...] = _dot(hdn, wd_bf[...]).astype(BF16).reshape(nb, cap, d)


def _scatter_ple_kernel(tab_ref, y_ref, slotc_ref, gatec_ref, h_ref, p_ref, g_ref, wg_ref, wp_ref, o_ref,
                        yg_ref, acc_ref, *, kmax):
    acc_ref[...] = jnp.zeros_like(acc_ref)
    lane = lax.broadcasted_iota(jnp.int32, (1, GROUP), 1)
    sub = lax.broadcasted_iota(jnp.int32, (LANES, GROUP), 0)

    def group(g, carry):
        base = g * CHUNKS_PER_GROUP
        expert = jnp.zeros((1, GROUP), jnp.int32)
        want = jnp.zeros((1, GROUP), jnp.int32)
        for c in range(CHUNKS_PER_GROUP):
            in_chunk = (lane >= c * CHUNK) & (lane < (c + 1) * CHUNK)
            expert = jnp.where(in_chunk, tab_ref[0, base + c], expert)
            want = jnp.where(in_chunk, tab_ref[0, kmax + base + c] + lane - c * CHUNK, want)
            src = pl.ds(pl.multiple_of(tab_ref[0, 2 * kmax + base + c], CHUNK), CHUNK)
            yg_ref[c * CHUNK:(c + 1) * CHUNK, :] = y_ref[src, :]
        pick = jnp.where(sub == expert, 1.0, 0.0).astype(BF16)
        both = _dot(jnp.concatenate([slotc_ref[...], gatec_ref[...]], axis=0), pick)
        tile = slotc_ref.shape[0]
        weights = jnp.where(both[:tile] == want.astype(F32), both[tile:], 0.0).astype(BF16)
        acc_ref[...] += _dot(weights, yg_ref[...])
        return carry

    lax.fori_loop(0, tab_ref[0, 3 * kmax], group, 0)
    h = h_ref[...] + acc_ref[...]
    xn = _rms(h, g_ref[...]).astype(BF16)
    gate = jax.nn.sigmoid(_dot(xn, wg_ref[...]))
    o_ref[...] = h + gate * _dot(p_ref[...].astype(BF16), wp_ref[...])


def _expert_ffn_ple(h3, xn3, slot_row, slot_col, gate_col, cnt, w_gate, w_up, w_down, cap, p3, layer, ple_g, ple_wg,
                    ple_wp, seqs_per_step=4):
    b, s, d = xn3.shape
    pd = p3.shape[2]
    _, n_e, _, ff = w_gate.shape
    nt = s // TOKEN_TILE
    tables, kmax = _chunk_tables(cnt[:, :, :nt], cap)
    tables = tables.reshape(b * nt, 1, 3 * kmax + 1)
    tab_spec = pl.BlockSpec((None, 1, 3 * kmax + 1), lambda i, t: (i * nt + t, 0, 0), memory_space=pltpu.SMEM)
    xs = pl.pallas_call(
        functools.partial(_gather_kernel, kmax=kmax),
        out_shape=jax.ShapeDtypeStruct((b, n_e * cap, d), BF16),
        grid=(b, nt),
        in_specs=[
            tab_spec,
            pl.BlockSpec((None, TOKEN_TILE, d), lambda i, t: (i, t, 0)),
            pl.BlockSpec((None, n_e, TOKEN_TILE), lambda i, t: (i, 0, t)),
        ],
        out_specs=pl.BlockSpec((None, n_e * cap, d), lambda i, t: (i, 0, 0)),
        scratch_shapes=[pltpu.VMEM((GROUP, d), BF16)],
        compiler_params=_cparams("parallel", "arbitrary"),
        name="expert_gather",
    )(tables, xn3, slot_row)
    nb = seqs_per_step
    y = pl.pallas_call(
        _expert_kernel,
        out_shape=jax.ShapeDtypeStruct((b, n_e, cap, d), BF16),
        grid=(n_e, b // nb),
        in_specs=[
            pl.BlockSpec((nb, None, cap, d), lambda e, i: (i, e, 0, 0)),
            pl.BlockSpec((None, None, d, ff), lambda e, i: (layer, e, 0, 0)),
            pl.BlockSpec((None, None, d, ff), lambda e, i: (layer, e, 0, 0)),
            pl.BlockSpec((None, None, ff, d), lambda e, i: (layer, e, 0, 0)),
        ],
        out_specs=pl.BlockSpec((nb, None, cap, d), lambda e, i: (i, e, 0, 0)),
        scratch_shapes=[pltpu.VMEM((d, ff), BF16), pltpu.VMEM((d, ff), BF16), pltpu.VMEM((ff, d), BF16)],
        compiler_params=_cparams("parallel", "arbitrary"),
        name="expert_mlp",
    )(xs.reshape(b, n_e, cap, d), w_gate, w_up, w_down)
    tile_rows = lambda width: pl.BlockSpec((None, TOKEN_TILE, width), lambda i, t: (i, t, 0))
    return pl.pallas_call(
        functools.partial(_scatter_ple_kernel, kmax=kmax),
        out_shape=jax.ShapeDtypeStruct((b, s, d), F32),
        grid=(b, nt),
        in_specs=[
            tab_spec,
            pl.BlockSpec((None, n_e * cap, d), lambda i, t: (i, 0, 0)),
            tile_rows(LANES), tile_rows(LANES), tile_rows(d),
            pl.BlockSpec((None, TOKEN_TILE, pd), lambda i, t: (layer, i * nt + t, 0)),
            _full_spec((1, d)), _full_spec((d, d)), _full_spec((pd, d)),
        ],
        out_specs=tile_rows(d),
        scratch_shapes=[pltpu.VMEM((GROUP, d), BF16), pltpu.VMEM((TOKEN_TILE, d), F32)],
        compiler_params=_cparams("parallel", "arbitrary"),
        name="expert_scatter_ple",
    )(tables, y.reshape(b, n_e * cap, d), slot_col, gate_col, h3, p3, ple_g.reshape(1, d), ple_wg.astype(BF16),
      ple_wp.astype(BF16))


def kernel(x, p, norm_mix_g, norm_ffn_g, w_out, router_w, exp_w_gate, exp_w_up, exp_w_down, ple_norm_g, ple_gate_w, ple_proj_w, a_w_in, a_vnorm_g, a_w_s, a_b_s, b_w_in, b_qnorm_g, b_knorm_g, b_sink, c_w_in, c_qnorm_g, c_knorm_g, c_rpb):
    b, s, d = x.shape
    depth = norm_mix_g.shape[0]
    t = b * s
    cap = max(1, EC_CAPACITY_FACTOR * s // N_EXPERTS)
    scale = HEAD_DIM ** -0.5
    h = x.reshape(t, d)
    for i in range(depth):
        kind = i % N_MIXERS
        j = i // N_MIXERS
        if kind == 0:
            h = _mixer_a(h, norm_mix_g[i], a_w_in[j], a_vnorm_g[j], a_w_s[j], a_b_s[j], w_out[i])
        elif kind == 1:
            gain = jnp.concatenate([jnp.tile(b_qnorm_g[j] * scale, B_HEADS), jnp.tile(b_knorm_g[j], 2 * B_KV_HEADS)])
            qw, kw = B_HEADS * HEAD_DIM, B_KV_HEADS * HEAD_DIM
            w_b = jnp.concatenate([b_w_in[j][:, :qw], _dup_heads(b_w_in[j][:, qw:qw + kw], B_KV_HEADS),
                                   _dup_heads(b_w_in[j][:, qw + kw:], B_KV_HEADS)], axis=1)
            qkv = _norm_proj(h, norm_mix_g[i], w_b, gain, qw + 2 * kw)
            mix = _attn_b(qkv.reshape(b, s, -1), b_sink[j])
            h = _proj_residual(mix.reshape(t, -1), w_out[i], h)
        else:
            gain = jnp.concatenate([jnp.tile(c_qnorm_g[j] * scale, C_HEADS), jnp.tile(c_knorm_g[j], C_HEADS)])
            qkv = _norm_proj(h, norm_mix_g[i], c_w_in[j], gain, 2 * C_HEADS * HEAD_DIM)
            mix = _attn_c(qkv.reshape(b, s // GRID_W, GRID_W, -1), c_rpb[j])
            h = _proj_residual(mix.reshape(t, -1), w_out[i], h)
        xn, slot_row, slot_col, gate_col, cnt = _route(h.reshape(b, s, d), norm_ffn_g[i], router_w[i], cap)
        h = _expert_ffn_ple(h.reshape(b, s, d), xn, slot_row, slot_col, gate_col, cnt, exp_w_gate, exp_w_up,
                            exp_w_down, cap, p.reshape(depth, t, -1), i, ple_norm_g[i], ple_gate_w[i],
                            ple_proj_w[i]).reshape(t, d)
    return h.reshape(b, s, d)
```

```python
import functools

import numpy as np
import jax
import jax.numpy as jnp
from jax import lax
from jax.experimental import pallas as pl
from jax.experimental.pallas import tpu as pltpu

F32 = jnp.float32
BF16 = jnp.bfloat16

RMS_EPS = 1e-6
LN_EPS = 1e-5
NEG = -1e30
HEAD_DIM = 64
GRID_W = 64
A_GROUPS = 8
A_CHUNK = 128
B_HEADS = 16
B_KV_HEADS = 4
B_BLOCK = 128
C_HEADS = 16
C_WIN_ROWS = 8
C_WIN_COLS = 16
N_EXPERTS = 16
EC_CAPACITY_FACTOR = 2
N_MIXERS = 3

LANES = 128
MXU_DIM = 256
VMEM_LIMIT_BYTES = 60 * 1024 * 1024

TOKEN_TILE = 256
TILES_PER_STEP = 2
CHUNK = 16
GROUP = 2 * MXU_DIM
CHUNKS_PER_GROUP = GROUP // CHUNK


def _cparams(*sem):
    return pltpu.CompilerParams(dimension_semantics=sem, vmem_limit_bytes=VMEM_LIMIT_BYTES)


def _rms(xf, g):
    return xf * lax.rsqrt(jnp.mean(xf * xf, axis=-1, keepdims=True) + RMS_EPS) * g


def _dot(a, b):
    return jnp.dot(a, b, preferred_element_type=F32)


def _dot_nt(a, b):
    return lax.dot_general(a, b, (((1,), (1,)), ((), ())), preferred_element_type=F32)


def _full_spec(shape):
    nd = len(shape)
    return pl.BlockSpec(shape, lambda *_: (0,) * nd)


def _mixer_a_kernel(h_ref, g_ref, win_ref, vg_ref, ws_ref, bias_ref, wout_ref, o_ref, mix_ref):
    tm = h_ref.shape[0]
    width = vg_ref.shape[1]
    gw = width // A_GROUPS
    h = h_ref[...]
    xn = _rms(h, g_ref[...]).astype(BF16)
    z = _dot(xn, win_ref[...])
    z = 0.5 * z * (1.0 + jnp.tanh(np.sqrt(2.0 / np.pi).astype(np.float32)
                                  * (z + 0.044715 * (z * z * z))))
    u = z[:, :width]
    v = z[:, width:]
    mu = jnp.mean(v, axis=-1, keepdims=True)
    vc = v - mu
    var = jnp.mean(vc * vc, axis=-1, keepdims=True)
    vn = (vc * lax.rsqrt(var + LN_EPS) * vg_ref[...]).astype(BF16)
    for c in range(tm // A_CHUNK):
        rows = slice(c * A_CHUNK, (c + 1) * A_CHUNK)
        for g in range(A_GROUPS):
            cols = slice(g * gw, (g + 1) * gw)
            s = _dot(ws_ref[g], vn[rows, cols]) + bias_ref[:, cols]
            mix_ref[rows, cols] = (u[rows, cols] * s).astype(BF16)
    o_ref[...] = h + _dot(mix_ref[...], wout_ref[...])


def _mixer_a(h2, g, w_in, vnorm_g, w_s, b_s, w_out, tm=512):
    t, d = h2.shape
    width = vnorm_g.shape[0]
    gw = width // A_GROUPS
    bias = jnp.repeat(b_s.T.astype(F32), gw, axis=1)
    return pl.pallas_call(
        _mixer_a_kernel,
        out_shape=jax.ShapeDtypeStruct((t, d), F32),
        grid=(t // tm,),
        in_specs=[
            pl.BlockSpec((tm, d), lambda i: (i, 0)),
            _full_spec((1, d)),
            _full_spec((d, 2 * width)),
            _full_spec((1, width)),
            _full_spec((A_GROUPS, A_CHUNK, A_CHUNK)),
            _full_spec((A_CHUNK, width)),
            _full_spec((width, d)),
        ],
        out_specs=pl.BlockSpec((tm, d), lambda i: (i, 0)),
        scratch_shapes=[pltpu.VMEM((tm, width), BF16)],
        compiler_params=_cparams("parallel"),
        name="mixer_a",
    )(h2, g.reshape(1, d), w_in.astype(BF16), vnorm_g.reshape(1, width), w_s.astype(BF16), bias,
      w_out.astype(BF16))


def _norm_proj_kernel(h_ref, g_ref, w_ref, hg_ref, bd_ref, o_ref, *, n_norm_cols):
    xn = _rms(h_ref[...], g_ref[...]).astype(BF16)
    acc = _dot(xn, w_ref[...])
    for j in range(n_norm_cols // MXU_DIM):
        cols = slice(j * MXU_DIM, (j + 1) * MXU_DIM)
        blk = acc[:, cols]
        ss = _dot((blk * blk).astype(BF16), bd_ref[...])
        o_ref[:, cols] = (blk * lax.rsqrt(ss * (1.0 / HEAD_DIM) + RMS_EPS) * hg_ref[:, cols]).astype(BF16)
    o_ref[:, n_norm_cols:] = acc[:, n_norm_cols:].astype(BF16)


def _norm_proj(h2, g, w, head_gain, n_norm_cols, tm=512):
    t, d = h2.shape
    n = w.shape[1]
    blockdiag = jnp.asarray(np.kron(np.eye(MXU_DIM // HEAD_DIM), np.ones((HEAD_DIM, HEAD_DIM))), BF16)
    return pl.pallas_call(
        functools.partial(_norm_proj_kernel, n_norm_cols=n_norm_cols),
        out_shape=jax.ShapeDtypeStruct((t, n), BF16),
        grid=(t // tm,),
        in_specs=[
            pl.BlockSpec((tm, d), lambda i: (i, 0)),
            _full_spec((1, d)),
            _full_spec((d, n)),
            _full_spec((1, n_norm_cols)),
            _full_spec((MXU_DIM, MXU_DIM)),
        ],
        out_specs=pl.BlockSpec((tm, n), lambda i: (i, 0)),
        compiler_params=_cparams("parallel"),
        name="norm_proj",
    )(h2, g.reshape(1, d), w.astype(BF16), head_gain.reshape(1, n_norm_cols).astype(F32), blockdiag)


def _proj_residual_kernel(a_ref, w_ref, h_ref, o_ref):
    o_ref[...] = h_ref[...] + _dot(a_ref[...], w_ref[...])


def _proj_residual(a2, w, h2, tm=512):
    t, d = h2.shape
    k = a2.shape[1]
    return pl.pallas_call(
        _proj_residual_kernel,
        out_shape=jax.ShapeDtypeStruct((t, d), F32),
        grid=(t // tm,),
        in_specs=[
            pl.BlockSpec((tm, k), lambda i: (i, 0)),
            _full_spec((k, d)),
            pl.BlockSpec((tm, d), lambda i: (i, 0)),
        ],
        out_specs=pl.BlockSpec((tm, d), lambda i: (i, 0)),
        compiler_params=_cparams("parallel"),
        name="proj_residual",
    )(a2, w.astype(BF16), h2)


def _alibi_slopes(n):
    return np.array([2.0 ** (-8.0 * (h + 1) / n) for h in range(n)], dtype=np.float32)


def _attn_b_tables():
    span = 3 * B_BLOCK
    rel = np.arange(span)[None, :] - B_BLOCK - np.arange(B_BLOCK)[:, None]
    in_window = np.abs(rel) <= B_BLOCK
    alibi = (-_alibi_slopes(B_HEADS)[:, None, None] * np.abs(rel)[None]).astype(np.float32)
    kblk = np.arange(span) // B_BLOCK
    tabs = []
    for kind in range(3):
        valid = in_window & ~((kind == 0) & (kblk == 0))[None, :] & ~((kind == 2) & (kblk == 2))[None, :]
        tabs.append(np.where(valid[None], alibi, np.float32(NEG)))
    return np.stack(tabs).astype(np.float32)


def _attn_b_kernel(sink_ref, q_ref, kp_ref, kc_ref, kn_ref, vp_ref, vc_ref, vn_ref, tab_ref, o_ref):
    grp = B_HEADS // B_KV_HEADS
    nq = q_ref.shape[0]
    lane = lax.broadcasted_iota(jnp.int32, (1, LANES), 1)
    low = lane < HEAD_DIM
    scores = []
    for kh in range(B_KV_HEADS):
        tile = slice(kh * LANES, (kh + 1) * LANES)
        kt = jnp.concatenate([kp_ref[:, tile], kc_ref[:, tile], kn_ref[:, tile]], axis=0)
        rows = []
        for qt in range(kh * grp // 2, (kh + 1) * grp // 2):
            q2 = q_ref[:, qt * LANES:(qt + 1) * LANES]
            zero = jnp.zeros_like(q2)
            rows += [jnp.where(low, q2, zero), jnp.where(low, zero, q2)]
        scores.append(_dot_nt(jnp.concatenate(rows, axis=0), kt) + tab_ref[kh])
    sh = jnp.concatenate(scores, axis=0)
    heads = [slice(h * nq, (h + 1) * nq) for h in range(B_HEADS)]
    rowmax = jnp.broadcast_to(jnp.max(sh, axis=-1, keepdims=True), (sh.shape[0], LANES))
    m = jnp.concatenate([jnp.maximum(rowmax[hs], sink_ref[h]) for h, hs in enumerate(heads)], axis=0)
    sink_term = jnp.concatenate([jnp.exp(sink_ref[h] - m[hs]) for h, hs in enumerate(heads)], axis=0)
    pe = jnp.exp(sh - jnp.concatenate([m] * (sh.shape[1] // LANES), axis=1)).astype(BF16)
    ones = jnp.ones((sh.shape[1], LANES), BF16)
    for kh in range(B_KV_HEADS):
        tile = slice(kh * LANES, (kh + 1) * LANES)
        grows = slice(kh * grp * nq, (kh + 1) * grp * nq)
        vt = jnp.concatenate([vp_ref[:, tile], vc_ref[:, tile], vn_ref[:, tile]], axis=0)
        od = _dot(pe[grows], jnp.concatenate([vt, ones], axis=1))
        o = od[:, :LANES] / (od[:, LANES:] + sink_term[grows])
        for j in range(grp // 2):
            qt = kh * grp // 2 + j
            even, odd = o[(2 * j) * nq:(2 * j + 1) * nq], o[(2 * j + 1) * nq:(2 * j + 2) * nq]
            o_ref[:, qt * LANES:(qt + 1) * LANES] = jnp.where(low, even, odd).astype(o_ref.dtype)


def _dup_heads(w, n_heads):
    d = w.shape[0]
    return jnp.repeat(w.reshape(d, n_heads, 1, HEAD_DIM), 2, axis=2).reshape(d, n_heads * LANES)


def _attn_b(qkv3, sink):
    b, s, _ = qkv3.shape
    nb = s // B_BLOCK
    grp = B_HEADS // B_KV_HEADS
    qw = B_HEADS * HEAD_DIM
    kvw = B_KV_HEADS * LANES
    kcol = qw // kvw
    vcol = kcol + 1
    tabs = jnp.asarray(_attn_b_tables().reshape(3, B_KV_HEADS, grp * B_BLOCK, 3 * B_BLOCK))
    prev = lambda n: jnp.maximum(n - 1, 0)
    nxt = lambda n: jnp.minimum(n + 1, nb - 1)
    kind = lambda n: jnp.where(n == 0, 0, jnp.where(n == nb - 1, 2, 1))
    kv = lambda col, f: pl.BlockSpec((None, B_BLOCK, kvw), lambda bi, n: (bi, f(n), col))
    same = lambda n: n
    return pl.pallas_call(
        _attn_b_kernel,
        out_shape=jax.ShapeDtypeStruct((b, s, qw), BF16),
        grid=(b, nb),
        in_specs=[
            pl.BlockSpec(memory_space=pltpu.SMEM),
            pl.BlockSpec((None, B_BLOCK, qw), lambda bi, n: (bi, n, 0)),
            kv(kcol, prev), kv(kcol, same), kv(kcol, nxt),
            kv(vcol, prev), kv(vcol, same), kv(vcol, nxt),
            pl.BlockSpec((None, B_KV_HEADS, grp * B_BLOCK, 3 * B_BLOCK), lambda bi, n: (kind(n), 0, 0, 0)),
        ],
        out_specs=pl.BlockSpec((None, B_BLOCK, qw), lambda bi, n: (bi, n, 0)),
        compiler_params=_cparams("parallel", "arbitrary"),
        name="attn_b",
    )(sink.astype(F32), qkv3, qkv3, qkv3, qkv3, qkv3, qkv3, qkv3, tabs)


def _attn_c_table(rpb, rows):
    w = GRID_W
    kr_n = min(C_WIN_ROWS, rows)
    c = np.arange(w)
    cs = np.clip(c - C_WIN_COLS // 2, 0, w - C_WIN_COLS)
    kcol = np.arange(w)
    colmask = (kcol[None, :] >= cs[:, None]) & (kcol[None, :] < cs[:, None] + C_WIN_COLS)
    n_off = 2 * C_WIN_ROWS - kr_n
    n_h, n_ri, _ = rpb.shape
    pad = w - C_WIN_COLS
    padded = jnp.pad(rpb.astype(F32), ((0, 0), (0, 0), (pad, pad)))
    bias = jnp.stack([padded[:, :, w - 1 - qc:2 * w - 1 - qc] for qc in range(w)], axis=1)
    bias = jnp.where(jnp.asarray(colmask)[None, :, None, :], bias, NEG)
    bias = bias.reshape(n_h, w, n_ri * w)
    slabs = jnp.stack([bias[:, :, n * w:(n + kr_n) * w] for n in range(n_off)])
    return slabs.reshape(n_off, n_h // 2, 2 * w, kr_n * w)


def _attn_c_kernel(q_ref, k_ref, v_ref, tab_ref, o_ref):
    nk = k_ref.shape[1] * k_ref.shape[2]
    nq = q_ref.shape[1]
    lane = lax.broadcasted_iota(jnp.int32, (1, LANES), 1)
    low = lane < HEAD_DIM
    ones = jnp.ones((nk, LANES), BF16)

    def one_sequence(i, carry):
        scores = []
        for t in range(C_HEADS // 2):
            cols = slice(t * LANES, (t + 1) * LANES)
            q2 = q_ref[i, :, cols]
            zero = jnp.zeros_like(q2)
            qq = jnp.concatenate([jnp.where(low, q2, zero), jnp.where(low, zero, q2)], axis=0)
            kt = k_ref[i, :, :, cols].reshape(nk, LANES)
            scores.append(_dot_nt(qq, kt) + tab_ref[t])
        sh = jnp.concatenate(scores, axis=0)
        m = jnp.broadcast_to(jnp.max(sh, axis=-1, keepdims=True), (sh.shape[0], LANES))
        pe = jnp.exp(sh - jnp.concatenate([m] * (nk // LANES), axis=1)).astype(BF16)
        for t in range(C_HEADS // 2):
            cols = slice(t * LANES, (t + 1) * LANES)
            rows = slice(2 * t * nq, 2 * (t + 1) * nq)
            vt = v_ref[i, :, :, cols].reshape(nk, LANES)
            od = _dot(pe[rows], jnp.concatenate([vt, ones], axis=1))
            o = od[:, :LANES] / od[:, LANES:]
            o_ref[i, :, cols] = jnp.where(low, o[:nq], o[nq:]).astype(o_ref.dtype)
        return carry

    lax.fori_loop(0, q_ref.shape[0], one_sequence, 0)


def _attn_c(qkv4, rpb, seqs_per_step=4):
    b, rows, w, _ = qkv4.shape
    hw = C_HEADS * HEAD_DIM
    kr_n = min(C_WIN_ROWS, rows)
    nb = seqs_per_step
    table = _attn_c_table(rpb, rows)
    row_start = lambda r: jnp.clip(r - kr_n // 2, 0, rows - kr_n)
    el = pl.Element
    return pl.pallas_call(
        _attn_c_kernel,
        out_shape=jax.ShapeDtypeStruct((b, rows, w, hw), BF16),
        grid=(rows, b // nb),
        in_specs=[
            pl.BlockSpec((nb, None, w, hw), lambda r, bi: (bi, r, 0, 0)),
            pl.BlockSpec((el(nb), el(kr_n), el(w), el(hw)), lambda r, bi: (bi * nb, row_start(r), 0, hw)),
            pl.BlockSpec((el(nb), el(kr_n), el(w), el(hw)), lambda r, bi: (bi * nb, row_start(r), 0, 2 * hw)),
            pl.BlockSpec((None, C_HEADS // 2, 2 * w, kr_n * w),
                         lambda r, bi: (row_start(r) - r + C_WIN_ROWS - 1, 0, 0, 0)),
        ],
        out_specs=pl.BlockSpec((nb, None, w, hw), lambda r, bi: (bi, r, 0, 0)),
        compiler_params=_cparams("arbitrary", "arbitrary"),
        name="attn_c",
    )(qkv4, qkv4, qkv4, table)


def _lane_cumsum(x, tri):
    e, s = x.shape
    off = jnp.zeros((e, 1), F32)
    outs = []
    for j in range(s // LANES):
        blk = x[:, j * LANES:(j + 1) * LANES]
        outs.append(_dot(blk.astype(BF16), tri) + off)
        off = off + jnp.sum(blk, axis=1, keepdims=True)
    return jnp.concatenate(outs, axis=1)


def _route_kernel(h_ref, g_ref, rwt_ref, tri_ref, xn_ref, slotr_ref, slotc_ref, gatec_ref, cnt_ref,
                  *, cap, chunk, tile):
    s = h_ref.shape[0]
    n_e = rwt_ref.shape[0] // 2
    logits = []
    for c in range(s // chunk):
        rows = slice(c * chunk, (c + 1) * chunk)
        xn = _rms(h_ref[rows, :], g_ref[...])
        xn_hi = xn.astype(BF16)
        xn_ref[rows, :] = xn_hi
        xn_lo = (xn - xn_hi.astype(F32)).astype(BF16)
        both = _dot_nt(rwt_ref[...], xn_hi)
        logits.append(both[:n_e] + both[n_e:] + _dot_nt(rwt_ref[:n_e, :], xn_lo))
    lg = jnp.concatenate(logits, axis=1)
    ex = jnp.exp(lg - jnp.max(lg, axis=0, keepdims=True))
    aff = ex / jnp.sum(ex, axis=0, keepdims=True)
    bits = pltpu.bitcast(aff, jnp.int32)
    thr = jnp.zeros((n_e, 1), jnp.int32)
    for shift in range(27, -1, -3):
        digit = jnp.zeros((n_e, 1), jnp.int32)
        for j in range(1, 8):
            cnt = jnp.sum(jnp.where(bits >= (thr | (j << shift)), 1.0, 0.0), axis=1, keepdims=True)
            digit = digit + jnp.where(cnt >= cap, 1, 0)
        thr = thr | (digit * (1 << shift))
    gt = bits > thr
    eq = bits == thr
    need = cap - jnp.sum(jnp.where(gt, 1.0, 0.0), axis=1, keepdims=True)
    eqf = jnp.where(eq, 1.0, 0.0)
    eq_rank = _lane_cumsum(eqf, tri_ref[...]) - eqf
    sel = gt | (eq & (eq_rank < need))
    self_ = jnp.where(sel, 1.0, 0.0)
    slot = jnp.where(sel, _lane_cumsum(self_, tri_ref[...]) - self_, -1.0)
    lane = lax.broadcasted_iota(jnp.int32, (n_e, LANES), 1)
    counts = jnp.zeros((n_e, LANES), F32)
    for i in range(s // tile):
        counts = jnp.where(lane == i, jnp.sum(self_[:, i * tile:(i + 1) * tile], axis=1, keepdims=True), counts)
    cnt_ref[...] = counts.astype(jnp.int32)
    pad_rows = LANES - n_e
    slot_p = jnp.concatenate([slot, jnp.full((pad_rows, s), -1.0, F32)], axis=0)
    gate_p = jnp.concatenate([jnp.where(sel, aff, 0.0), jnp.zeros((pad_rows, s), F32)], axis=0)
    slotr_ref[...] = slot
    for j in range(s // LANES):
        cols = slice(j * LANES, (j + 1) * LANES)
        slotc_ref[cols, :] = slot_p[:, cols].T.astype(BF16)
        gatec_ref[cols, :] = gate_p[:, cols].T.astype(BF16)


def _route(h3, g, router_w, cap):
    b, s, d = h3.shape
    n_e = router_w.shape[1]
    tri = jnp.asarray(np.triu(np.ones((LANES, LANES))), BF16)
    rw_t = router_w.T.astype(F32)
    rw_hi = rw_t.astype(BF16)
    rw_lo = (rw_t - rw_hi.astype(F32)).astype(BF16)
    per_seq = lambda *shape: pl.BlockSpec((None,) + shape, lambda i: (i,) + (0,) * len(shape))
    return pl.pallas_call(
        functools.partial(_route_kernel, cap=cap, chunk=256, tile=TOKEN_TILE),
        out_shape=(
            jax.ShapeDtypeStruct((b, s, d), BF16),
            jax.ShapeDtypeStruct((b, n_e, s), F32),
            jax.ShapeDtypeStruct((b, s, LANES), BF16),
            jax.ShapeDtypeStruct((b, s, LANES), BF16),
            jax.ShapeDtypeStruct((b, n_e, LANES), jnp.int32),
        ),
        grid=(b,),
        in_specs=[per_seq(s, d), _full_spec((1, d)), _full_spec((2 * n_e, d)), _full_spec((LANES, LANES))],
        out_specs=(per_seq(s, d), per_seq(n_e, s), per_seq(s, LANES), per_seq(s, LANES), per_seq(n_e, LANES)),
        compiler_params=_cparams("parallel"),
        name="route",
    )(h3, g.reshape(1, d), jnp.concatenate([rw_hi, rw_lo], axis=0), tri)


def _chunk_tables(cnt, cap):
    n_e = cnt.shape[1]
    kmax = n_e * (TOKEN_TILE // CHUNK + 1)
    kmax = -(-kmax // CHUNKS_PER_GROUP) * CHUNKS_PER_GROUP
    c0 = jnp.cumsum(cnt, axis=2) - cnt
    a0 = c0 // CHUNK
    a1 = jnp.where(cnt > 0, (c0 + cnt + CHUNK - 1) // CHUNK, a0)
    nch = (a1 - a0).transpose(0, 2, 1)
    a0 = a0.transpose(0, 2, 1)
    pos_end = jnp.cumsum(nch, axis=2)
    pos = pos_end - nch
    total = pos_end[..., -1]
    k = jnp.arange(kmax, dtype=jnp.int32)
    e_of_k = jnp.sum((k[None, None, :, None] >= pos_end[:, :, None, :]).astype(jnp.int32), axis=-1)
    e_of_k = jnp.minimum(e_of_k, n_e - 1)
    is_e = e_of_k[..., None] == jnp.arange(n_e, dtype=jnp.int32)
    q = k + jnp.sum(jnp.where(is_e, (a0 - pos)[:, :, None, :], 0), axis=-1)
    valid = k[None, None, :] < total[..., None]
    e_tab = jnp.where(valid, e_of_k, 0)
    slot_tab = jnp.where(valid, q * CHUNK, -(1 << 20))
    dst_tab = jnp.where(valid, e_of_k * cap + q * CHUNK, 0)
    groups = (total + CHUNKS_PER_GROUP - 1) // CHUNKS_PER_GROUP
    return jnp.concatenate([e_tab, slot_tab, dst_tab, groups[..., None]], axis=-1).astype(jnp.int32), kmax


def _gather_kernel(tab_ref, xn_ref, slotr_ref, xs_ref, rows_ref, *, kmax):
    @pl.when(pl.program_id(1) == 0)
    def _():
        xs_ref[...] = jnp.zeros_like(xs_ref)

    sub = lax.broadcasted_iota(jnp.int32, (CHUNK, TOKEN_TILE), 0).astype(F32)
    for tile in range(TILES_PER_STEP):
        tokens = slice(tile * TOKEN_TILE, (tile + 1) * TOKEN_TILE)

        def group(g, carry, tile=tile, tokens=tokens):
            base = g * CHUNKS_PER_GROUP
            onehot = []
            for c in range(CHUNKS_PER_GROUP):
                slots = slotr_ref[pl.ds(tab_ref[tile, base + c], 1), tokens]
                want = sub + tab_ref[tile, kmax + base + c].astype(F32)
                onehot.append(jnp.where(slots == want, 1.0, 0.0).astype(BF16))
            rows_ref[...] = _dot(jnp.concatenate(onehot, axis=0), xn_ref[tokens, :]).astype(BF16)
            for c0 in range(0, CHUNKS_PER_GROUP, 4):
                cs = range(c0, c0 + 4)
                dsts = [pl.ds(pl.multiple_of(tab_ref[tile, 2 * kmax + base + c], CHUNK), CHUNK) for c in cs]
                sums = [xs_ref[dst, :] + rows_ref[c * CHUNK:(c + 1) * CHUNK, :] for c, dst in zip(cs, dsts)]
                for dst, total in reversed(list(zip(dsts, sums))):
                    xs_ref[dst, :] = total
            return carry

        lax.fori_loop(0, tab_ref[tile, 3 * kmax], group, 0)


def _expert_kernel(xs_ref, wg_ref, wu_ref, wd_ref, y_ref, wg_bf, wu_bf, wd_bf):
    @pl.when(pl.program_id(1) == 0)
    def _():
        wg_bf[...] = wg_ref[...].astype(BF16)
        wu_bf[...] = wu_ref[...].astype(BF16)
        wd_bf[...] = wd_ref[...].astype(BF16)

    nb, cap, d = xs_ref.shape
    xs = xs_ref[...].reshape(nb * cap, d)
    hg = _dot(xs, wg_bf[...])
    hu = _dot(xs, wu_bf[...])
    hdn = (hg * jax.nn.sigmoid(hg) * hu).astype(BF16)
    y_ref[...] = _dot(hdn, wd_bf[...]).astype(BF16).reshape(nb, cap, d)


def _scatter_ple_kernel(tab_ref, y_ref, slotc_ref, gatec_ref, h_ref, p_ref, g_ref, wg_ref, wp_ref, o_ref,
                        yg_ref, acc_ref, *, kmax):
    acc_ref[...] = jnp.zeros_like(acc_ref)
    lane = lax.broadcasted_iota(jnp.int32, (1, GROUP), 1)
    sub = lax.broadcasted_iota(jnp.int32, (LANES, GROUP), 0)
    for tile in range(TILES_PER_STEP):
        tokens = slice(tile * TOKEN_TILE, (tile + 1) * TOKEN_TILE)

        def group(g, carry, tile=tile, tokens=tokens):
            base = g * CHUNKS_PER_GROUP
            expert = jnp.zeros((1, GROUP), jnp.int32)
            want = jnp.zeros((1, GROUP), jnp.int32)
            for c in range(CHUNKS_PER_GROUP):
                in_chunk = (lane >= c * CHUNK) & (lane < (c + 1) * CHUNK)
                expert = jnp.where(in_chunk, tab_ref[tile, base + c], expert)
                want = jnp.where(in_chunk, tab_ref[tile, kmax + base + c] + lane - c * CHUNK, want)
                src = pl.ds(pl.multiple_of(tab_ref[tile, 2 * kmax + base + c], CHUNK), CHUNK)
                yg_ref[c * CHUNK:(c + 1) * CHUNK, :] = y_ref[src, :]
            pick = jnp.where(sub == expert, 1.0, 0.0).astype(BF16)
            both = _dot(jnp.concatenate([slotc_ref[tokens, :], gatec_ref[tokens, :]], axis=0), pick)
            weights = jnp.where(both[:TOKEN_TILE] == want.astype(F32), both[TOKEN_TILE:], 0.0).astype(BF16)
            acc_ref[tokens, :] += _dot(weights, yg_ref[...])
            return carry

        lax.fori_loop(0, tab_ref[tile, 3 * kmax], group, 0)
    h = h_ref[...] + acc_ref[...]
    xn = _rms(h, g_ref[...]).astype(BF16)
    gate = jax.nn.sigmoid(_dot(xn, wg_ref[...]))
    o_ref[...] = h + gate * _dot(p_ref[...].astype(BF16), wp_ref[...])


def _expert_ffn_ple(h3, xn3, slot_row, slot_col, gate_col, cnt, w_gate, w_up, w_down, cap, p3, layer, ple_g, ple_wg,
                    ple_wp, seqs_per_step=4):
    b, s, d = xn3.shape
    pd = p3.shape[2]
    _, n_e, _, ff = w_gate.shape
    step_rows = TILES_PER_STEP * TOKEN_TILE
    nt = s // step_rows
    tables, kmax = _chunk_tables(cnt[:, :, :s // TOKEN_TILE], cap)
    tables = tables.reshape(b * nt, TILES_PER_STEP, 3 * kmax + 1)
    tab_spec = pl.BlockSpec((None, TILES_PER_STEP, 3 * kmax + 1), lambda i, t: (i * nt + t, 0, 0),
                            memory_space=pltpu.SMEM)
    xs = pl.pallas_call(
        functools.partial(_gather_kernel, kmax=kmax),
        out_shape=jax.ShapeDtypeStruct((b, n_e * cap, d), BF16),
        grid=(b, nt),
        in_specs=[
            tab_spec,
            pl.BlockSpec((None, step_rows, d), lambda i, t: (i, t, 0)),
            pl.BlockSpec((None, n_e, step_rows), lambda i, t: (i, 0, t)),
        ],
        out_specs=pl.BlockSpec((None, n_e * cap, d), lambda i, t: (i, 0, 0)),
        scratch_shapes=[pltpu.VMEM((GROUP, d), BF16)],
        compiler_params=_cparams("parallel", "arbitrary"),
        name="expert_gather",
    )(tables, xn3, slot_row)
    nb = seqs_per_step
    y = pl.pallas_call(
        _expert_kernel,
        out_shape=jax.ShapeDtypeStruct((b, n_e, cap, d), BF16),
        grid=(n_e, b // nb),
        in_specs=[
            pl.BlockSpec((nb, None, cap, d), lambda e, i: (i, e, 0, 0)),
            pl.BlockSpec((None, None, d, ff), lambda e, i: (layer, e, 0, 0)),
            pl.BlockSpec((None, None, d, ff), lambda e, i: (layer, e, 0, 0)),
            pl.BlockSpec((None, None, ff, d), lambda e, i: (layer, e, 0, 0)),
        ],
        out_specs=pl.BlockSpec((nb, None, cap, d), lambda e, i: (i, e, 0, 0)),
        scratch_shapes=[pltpu.VMEM((d, ff), BF16), pltpu.VMEM((d, ff), BF16), pltpu.VMEM((ff, d), BF16)],
        compiler_params=_cparams("parallel", "arbitrary"),
        name="expert_mlp",
    )(xs.reshape(b, n_e, cap, d), w_gate, w_up, w_down)
    tile_rows = lambda width: pl.BlockSpec((None, step_rows, width), lambda i, t: (i, t, 0))
    return pl.pallas_call(
        functools.partial(_scatter_ple_kernel, kmax=kmax),
        out_shape=jax.ShapeDtypeStruct((b, s, d), F32),
        grid=(b, nt),
        in_specs=[
            tab_spec,
            pl.BlockSpec((None, n_e * cap, d), lambda i, t: (i, 0, 0)),
            tile_rows(LANES), tile_rows(LANES), tile_rows(d),
            pl.BlockSpec((None, step_rows, pd), lambda i, t: (layer, i * nt + t, 0)),
            _full_spec((1, d)), _full_spec((d, d)), _full_spec((pd, d)),
        ],
        out_specs=tile_rows(d),
        scratch_shapes=[pltpu.VMEM((GROUP, d), BF16), pltpu.VMEM((step_rows, d), F32)],
        compiler_params=_cparams("parallel", "arbitrary"),
        name="expert_scatter_ple",
    )(tables, y.reshape(b, n_e * cap, d), slot_col, gate_col, h3, p3, ple_g.reshape(1, d), ple_wg.astype(BF16),
      ple_wp.astype(BF16))


def kernel(x, p, norm_mix_g, norm_ffn_g, w_out, router_w, exp_w_gate, exp_w_up, exp_w_down, ple_norm_g, ple_gate_w, ple_proj_w, a_w_in, a_vnorm_g, a_w_s, a_b_s, b_w_in, b_qnorm_g, b_knorm_g, b_sink, c_w_in, c_qnorm_g, c_knorm_g, c_rpb):
    b, s, d = x.shape
    depth = norm_mix_g.shape[0]
    t = b * s
    cap = max(1, EC_CAPACITY_FACTOR * s // N_EXPERTS)
    scale = HEAD_DIM ** -0.5
    h = x.reshape(t, d)
    for i in range(depth):
        kind = i % N_MIXERS
        j = i // N_MIXERS
        if kind == 0:
            h = _mixer_a(h, norm_mix_g[i], a_w_in[j], a_vnorm_g[j], a_w_s[j], a_b_s[j], w_out[i])
        elif kind == 1:
            gain = jnp.concatenate([jnp.tile(b_qnorm_g[j] * scale, B_HEADS), jnp.tile(b_knorm_g[j], 2 * B_KV_HEADS)])
            qw, kw = B_HEADS * HEAD_DIM, B_KV_HEADS * HEAD_DIM
            w_b = jnp.concatenate([b_w_in[j][:, :qw], _dup_heads(b_w_in[j][:, qw:qw + kw], B_KV_HEADS),
                                   _dup_heads(b_w_in[j][:, qw + kw:], B_KV_HEADS)], axis=1)
            qkv = _norm_proj(h, norm_mix_g[i], w_b, gain, qw + 2 * kw)
            mix = _attn_b(qkv.reshape(b, s, -1), b_sink[j])
            h = _proj_residual(mix.reshape(t, -1), w_out[i], h)
        else:
            gain = jnp.concatenate([jnp.tile(c_qnorm_g[j] * scale, C_HEADS), jnp.tile(c_knorm_g[j], C_HEADS)])
            qkv = _norm_proj(h, norm_mix_g[i], c_w_in[j], gain, 2 * C_HEADS * HEAD_DIM)
            mix = _attn_c(qkv.reshape(b, s // GRID_W, GRID_W, -1), c_rpb[j])
            h = _proj_residual(mix.reshape(t, -1), w_out[i], h)
        xn, slot_row, slot_col, gate_col, cnt = _route(h.reshape(b, s, d), norm_ffn_g[i], router_w[i], cap)
        h = _expert_ffn_ple(h.reshape(b, s, d), xn, slot_row, slot_col, gate_col, cnt, exp_w_gate, exp_w_up,
                            exp_w_down, cap, p.reshape(depth, t, -1), i, ple_norm_g[i], ple_gate_w[i],
                            ple_proj_w[i]).reshape(t, d)
    return h.reshape(b, s, d)
```

```python
import functools

import numpy as np
import jax
import jax.numpy as jnp
from jax import lax
from jax.experimental import pallas as pl
from jax.experimental.pallas import tpu as pltpu

F32 = jnp.float32
BF16 = jnp.bfloat16

RMS_EPS = 1e-6
LN_EPS = 1e-5
NEG = -1e30
HEAD_DIM = 64
GRID_W = 64
A_GROUPS = 8
A_CHUNK = 128
B_HEADS = 16
B_KV_HEADS = 4
B_BLOCK = 128
C_HEADS = 16
C_WIN_ROWS = 8
C_WIN_COLS = 16
N_EXPERTS = 16
EC_CAPACITY_FACTOR = 2
N_MIXERS = 3

LANES = 128
MXU_DIM = 256
VMEM_LIMIT_BYTES = 60 * 1024 * 1024

TOKEN_TILE = 256
TILES_PER_STEP = 4
A_SUBTILES = 2
CHUNK = 16
GROUP = 2 * MXU_DIM
CHUNKS_PER_GROUP = GROUP // CHUNK


def _cparams(*sem):
    return pltpu.CompilerParams(dimension_semantics=sem, vmem_limit_bytes=VMEM_LIMIT_BYTES)


def _rms(xf, g):
    return xf * lax.rsqrt(jnp.mean(xf * xf, axis=-1, keepdims=True) + RMS_EPS) * g


def _dot(a, b):
    return jnp.dot(a, b, preferred_element_type=F32)


def _dot_nt(a, b):
    return lax.dot_general(a, b, (((1,), (1,)), ((), ())), preferred_element_type=F32)


def _full_spec(shape):
    nd = len(shape)
    return pl.BlockSpec(shape, lambda *_: (0,) * nd)


def _mixer_a_kernel(h_ref, g_ref, win_ref, vg_ref, ws_ref, bias_ref, wout_ref, o_ref, mix_ref):
    tm = h_ref.shape[0]
    width = vg_ref.shape[1]
    gw = width // A_GROUPS
    sub = tm // A_SUBTILES
    halves = [slice(i * sub, (i + 1) * sub) for i in range(A_SUBTILES)]
    zs = [_dot(_rms(h_ref[hs, :], g_ref[...]).astype(BF16), win_ref[...]) for hs in halves]
    us, vns = [], []
    for z in zs:
        z = 0.5 * z * (1.0 + jnp.tanh(np.sqrt(2.0 / np.pi).astype(np.float32) * (z + 0.044715 * (z * z * z))))
        v = z[:, width:]
        mu = jnp.mean(v, axis=-1, keepdims=True)
        vc = v - mu
        var = jnp.mean(vc * vc, axis=-1, keepdims=True)
        us.append(z[:, :width])
        vns.append((vc * lax.rsqrt(var + LN_EPS) * vg_ref[...]).astype(BF16))
    for hs, u, vn in zip(halves, us, vns):
        for c in range(sub // A_CHUNK):
            rows = slice(c * A_CHUNK, (c + 1) * A_CHUNK)
            out_rows = slice(hs.start + c * A_CHUNK, hs.start + (c + 1) * A_CHUNK)
            for g in range(A_GROUPS):
                cols = slice(g * gw, (g + 1) * gw)
                s = _dot(ws_ref[g], vn[rows, cols]) + bias_ref[:, cols]
                mix_ref[out_rows, cols] = (u[rows, cols] * s).astype(BF16)
    o_ref[...] = h_ref[...] + _dot(mix_ref[...], wout_ref[...])


def _mixer_a(h2, g, w_in, vnorm_g, w_s, b_s, w_out, tm=512):
    t, d = h2.shape
    width = vnorm_g.shape[0]
    gw = width // A_GROUPS
    bias = jnp.repeat(b_s.T.astype(F32), gw, axis=1)
    return pl.pallas_call(
        _mixer_a_kernel,
        out_shape=jax.ShapeDtypeStruct((t, d), F32),
        grid=(t // tm,),
        in_specs=[
            pl.BlockSpec((tm, d), lambda i: (i, 0)),
            _full_spec((1, d)),
            _full_spec((d, 2 * width)),
            _full_spec((1, width)),
            _full_spec((A_GROUPS, A_CHUNK, A_CHUNK)),
            _full_spec((A_CHUNK, width)),
            _full_spec((width, d)),
        ],
        out_specs=pl.BlockSpec((tm, d), lambda i: (i, 0)),
        scratch_shapes=[pltpu.VMEM((tm, width), BF16)],
        compiler_params=_cparams("parallel"),
        name="mixer_a",
    )(h2, g.reshape(1, d), w_in.astype(BF16), vnorm_g.reshape(1, width), w_s.astype(BF16), bias,
      w_out.astype(BF16))


def _norm_proj_kernel(h_ref, g_ref, w_ref, hg_ref, bd_ref, o_ref, *, n_norm_cols):
    xn = _rms(h_ref[...], g_ref[...]).astype(BF16)
    acc = _dot(xn, w_ref[...])
    for j in range(n_norm_cols // MXU_DIM):
        cols = slice(j * MXU_DIM, (j + 1) * MXU_DIM)
        blk = acc[:, cols]
        ss = _dot((blk * blk).astype(BF16), bd_ref[...])
        o_ref[:, cols] = (blk * lax.rsqrt(ss * (1.0 / HEAD_DIM) + RMS_EPS) * hg_ref[:, cols]).astype(BF16)
    o_ref[:, n_norm_cols:] = acc[:, n_norm_cols:].astype(BF16)


def _norm_proj(h2, g, w, head_gain, n_norm_cols, tm=512):
    t, d = h2.shape
    n = w.shape[1]
    blockdiag = jnp.asarray(np.kron(np.eye(MXU_DIM // HEAD_DIM), np.ones((HEAD_DIM, HEAD_DIM))), BF16)
    return pl.pallas_call(
        functools.partial(_norm_proj_kernel, n_norm_cols=n_norm_cols),
        out_shape=jax.ShapeDtypeStruct((t, n), BF16),
        grid=(t // tm,),
        in_specs=[
            pl.BlockSpec((tm, d), lambda i: (i, 0)),
            _full_spec((1, d)),
            _full_spec((d, n)),
            _full_spec((1, n_norm_cols)),
            _full_spec((MXU_DIM, MXU_DIM)),
        ],
        out_specs=pl.BlockSpec((tm, n), lambda i: (i, 0)),
        compiler_params=_cparams("parallel"),
        name="norm_proj",
    )(h2, g.reshape(1, d), w.astype(BF16), head_gain.reshape(1, n_norm_cols).astype(F32), blockdiag)


def _proj_residual_kernel(a_ref, w_ref, h_ref, o_ref):
    o_ref[...] = h_ref[...] + _dot(a_ref[...], w_ref[...])


def _proj_residual(a2, w, h2, tm=512):
    t, d = h2.shape
    k = a2.shape[1]
    return pl.pallas_call(
        _proj_residual_kernel,
        out_shape=jax.ShapeDtypeStruct((t, d), F32),
        grid=(t // tm,),
        in_specs=[
            pl.BlockSpec((tm, k), lambda i: (i, 0)),
            _full_spec((k, d)),
            pl.BlockSpec((tm, d), lambda i: (i, 0)),
        ],
        out_specs=pl.BlockSpec((tm, d), lambda i: (i, 0)),
        compiler_params=_cparams("parallel"),
        name="proj_residual",
    )(a2, w.astype(BF16), h2)


def _alibi_slopes(n):
    return np.array([2.0 ** (-8.0 * (h + 1) / n) for h in range(n)], dtype=np.float32)


def _attn_b_tables():
    span = 3 * B_BLOCK
    rel = np.arange(span)[None, :] - B_BLOCK - np.arange(B_BLOCK)[:, None]
    in_window = np.abs(rel) <= B_BLOCK
    alibi = (-_alibi_slopes(B_HEADS)[:, None, None] * np.abs(rel)[None]).astype(np.float32)
    kblk = np.arange(span) // B_BLOCK
    tabs = []
    for kind in range(3):
        valid = in_window & ~((kind == 0) & (kblk == 0))[None, :] & ~((kind == 2) & (kblk == 2))[None, :]
        tabs.append(np.where(valid[None], alibi, np.float32(NEG)))
    return np.stack(tabs).astype(np.float32)


def _attn_b_kernel(sink_ref, q_ref, kp_ref, kc_ref, kn_ref, vp_ref, vc_ref, vn_ref, tab_ref, o_ref):
    grp = B_HEADS // B_KV_HEADS
    nq = q_ref.shape[1]
    lane = lax.broadcasted_iota(jnp.int32, (1, LANES), 1)
    low = lane < HEAD_DIM
    span = kp_ref.shape[1] + kc_ref.shape[1] + kn_ref.shape[1]
    ones = jnp.ones((span, LANES), BF16)
    heads = [slice(h * nq, (h + 1) * nq) for h in range(B_HEADS)]

    def one_sequence(i, carry):
        scores = []
        for kh in range(B_KV_HEADS):
            tile = slice(kh * LANES, (kh + 1) * LANES)
            kt = jnp.concatenate([kp_ref[i, :, tile], kc_ref[i, :, tile], kn_ref[i, :, tile]], axis=0)
            rows = []
            for qt in range(kh * grp // 2, (kh + 1) * grp // 2):
                q2 = q_ref[i, :, qt * LANES:(qt + 1) * LANES]
                zero = jnp.zeros_like(q2)
                rows += [jnp.where(low, q2, zero), jnp.where(low, zero, q2)]
            scores.append(_dot_nt(jnp.concatenate(rows, axis=0), kt) + tab_ref[kh])
        sh = jnp.concatenate(scores, axis=0)
        rowmax = jnp.broadcast_to(jnp.max(sh, axis=-1, keepdims=True), (sh.shape[0], LANES))
        m = jnp.concatenate([jnp.maximum(rowmax[hs], sink_ref[h]) for h, hs in enumerate(heads)], axis=0)
        sink_term = jnp.concatenate([jnp.exp(sink_ref[h] - m[hs]) for h, hs in enumerate(heads)], axis=0)
        pe = jnp.exp(sh - jnp.concatenate([m] * (span // LANES), axis=1)).astype(BF16)
        for kh in range(B_KV_HEADS):
            tile = slice(kh * LANES, (kh + 1) * LANES)
            grows = slice(kh * grp * nq, (kh + 1) * grp * nq)
            vt = jnp.concatenate([vp_ref[i, :, tile], vc_ref[i, :, tile], vn_ref[i, :, tile]], axis=0)
            od = _dot(pe[grows], jnp.concatenate([vt, ones], axis=1))
            o = od[:, :LANES] / (od[:, LANES:] + sink_term[grows])
            for j in range(grp // 2):
                qt = kh * grp // 2 + j
                even, odd = o[(2 * j) * nq:(2 * j + 1) * nq], o[(2 * j + 1) * nq:(2 * j + 2) * nq]
                o_ref[i, :, qt * LANES:(qt + 1) * LANES] = jnp.where(low, even, odd).astype(o_ref.dtype)
        return carry

    lax.fori_loop(0, q_ref.shape[0], one_sequence, 0)


def _dup_heads(w, n_heads):
    d = w.shape[0]
    return jnp.repeat(w.reshape(d, n_heads, 1, HEAD_DIM), 2, axis=2).reshape(d, n_heads * LANES)


def _attn_b(qkv3, sink, seqs_per_step=4):
    b, s, _ = qkv3.shape
    nb = s // B_BLOCK
    ns = seqs_per_step
    grp = B_HEADS // B_KV_HEADS
    qw = B_HEADS * HEAD_DIM
    kvw = B_KV_HEADS * LANES
    kcol = qw // kvw
    vcol = kcol + 1
    tabs = jnp.asarray(_attn_b_tables().reshape(3, B_KV_HEADS, grp * B_BLOCK, 3 * B_BLOCK))
    prev = lambda n: jnp.maximum(n - 1, 0)
    nxt = lambda n: jnp.minimum(n + 1, nb - 1)
    kind = lambda n: jnp.where(n == 0, 0, jnp.where(n == nb - 1, 2, 1))
    kv = lambda col, f: pl.BlockSpec((ns, B_BLOCK, kvw), lambda bi, n: (bi, f(n), col))
    same = lambda n: n
    return pl.pallas_call(
        _attn_b_kernel,
        out_shape=jax.ShapeDtypeStruct((b, s, qw), BF16),
        grid=(b // ns, nb),
        in_specs=[
            pl.BlockSpec(memory_space=pltpu.SMEM),
            pl.BlockSpec((ns, B_BLOCK, qw), lambda bi, n: (bi, n, 0)),
            kv(kcol, prev), kv(kcol, same), kv(kcol, nxt),
            kv(vcol, prev), kv(vcol, same), kv(vcol, nxt),
            pl.BlockSpec((None, B_KV_HEADS, grp * B_BLOCK, 3 * B_BLOCK), lambda bi, n: (kind(n), 0, 0, 0)),
        ],
        out_specs=pl.BlockSpec((ns, B_BLOCK, qw), lambda bi, n: (bi, n, 0)),
        compiler_params=_cparams("parallel", "arbitrary"),
        name="attn_b",
    )(sink.astype(F32), qkv3, qkv3, qkv3, qkv3, qkv3, qkv3, qkv3, tabs)


def _attn_c_table(rpb, rows):
    w = GRID_W
    kr_n = min(C_WIN_ROWS, rows)
    c = np.arange(w)
    cs = np.clip(c - C_WIN_COLS // 2, 0, w - C_WIN_COLS)
    kcol = np.arange(w)
    colmask = (kcol[None, :] >= cs[:, None]) & (kcol[None, :] < cs[:, None] + C_WIN_COLS)
    n_off = 2 * C_WIN_ROWS - kr_n
    n_h, n_ri, _ = rpb.shape
    pad = w - C_WIN_COLS
    padded = jnp.pad(rpb.astype(F32), ((0, 0), (0, 0), (pad, pad)))
    bias = jnp.stack([padded[:, :, w - 1 - qc:2 * w - 1 - qc] for qc in range(w)], axis=1)
    bias = jnp.where(jnp.asarray(colmask)[None, :, None, :], bias, NEG)
    bias = bias.reshape(n_h, w, n_ri * w)
    slabs = jnp.stack([bias[:, :, n * w:(n + kr_n) * w] for n in range(n_off)])
    return slabs.reshape(n_off, n_h // 2, 2 * w, kr_n * w)


def _attn_c_kernel(q_ref, k_ref, v_ref, tab_ref, o_ref):
    nk = k_ref.shape[1] * k_ref.shape[2]
    nq = q_ref.shape[1]
    lane = lax.broadcasted_iota(jnp.int32, (1, LANES), 1)
    low = lane < HEAD_DIM
    ones = jnp.ones((nk, LANES), BF16)

    def one_sequence(i, carry):
        scores = []
        for t in range(C_HEADS // 2):
            cols = slice(t * LANES, (t + 1) * LANES)
            q2 = q_ref[i, :, cols]
            zero = jnp.zeros_like(q2)
            qq = jnp.concatenate([jnp.where(low, q2, zero), jnp.where(low, zero, q2)], axis=0)
            kt = k_ref[i, :, :, cols].reshape(nk, LANES)
            scores.append(_dot_nt(qq, kt) + tab_ref[t])
        sh = jnp.concatenate(scores, axis=0)
        m = jnp.broadcast_to(jnp.max(sh, axis=-1, keepdims=True), (sh.shape[0], LANES))
        pe = jnp.exp(sh - jnp.concatenate([m] * (nk // LANES), axis=1)).astype(BF16)
        for t in range(C_HEADS // 2):
            cols = slice(t * LANES, (t + 1) * LANES)
            rows = slice(2 * t * nq, 2 * (t + 1) * nq)
            vt = v_ref[i, :, :, cols].reshape(nk, LANES)
            od = _dot(pe[rows], jnp.concatenate([vt, ones], axis=1))
            o = od[:, :LANES] / od[:, LANES:]
            o_ref[i, :, cols] = jnp.where(low, o[:nq], o[nq:]).astype(o_ref.dtype)
        return carry

    lax.fori_loop(0, q_ref.shape[0], one_sequence, 0)


def _attn_c(qkv4, rpb, seqs_per_step=4):
    b, rows, w, _ = qkv4.shape
    hw = C_HEADS * HEAD_DIM
    kr_n = min(C_WIN_ROWS, rows)
    nb = seqs_per_step
    table = _attn_c_table(rpb, rows)
    row_start = lambda r: jnp.clip(r - kr_n // 2, 0, rows - kr_n)
    el = pl.Element
    return pl.pallas_call(
        _attn_c_kernel,
        out_shape=jax.ShapeDtypeStruct((b, rows, w, hw), BF16),
        grid=(rows, b // nb),
        in_specs=[
            pl.BlockSpec((nb, None, w, hw), lambda r, bi: (bi, r, 0, 0)),
            pl.BlockSpec((el(nb), el(kr_n), el(w), el(hw)), lambda r, bi: (bi * nb, row_start(r), 0, hw)),
            pl.BlockSpec((el(nb), el(kr_n), el(w), el(hw)), lambda r, bi: (bi * nb, row_start(r), 0, 2 * hw)),
            pl.BlockSpec((None, C_HEADS // 2, 2 * w, kr_n * w),
                         lambda r, bi: (row_start(r) - r + C_WIN_ROWS - 1, 0, 0, 0)),
        ],
        out_specs=pl.BlockSpec((nb, None, w, hw), lambda r, bi: (bi, r, 0, 0)),
        compiler_params=_cparams("arbitrary", "arbitrary"),
        name="attn_c",
    )(qkv4, qkv4, qkv4, table)


def _lane_cumsum(x, tri):
    e, s = x.shape
    off = jnp.zeros((e, 1), F32)
    outs = []
    for j in range(s // LANES):
        blk = x[:, j * LANES:(j + 1) * LANES]
        outs.append(_dot(blk.astype(BF16), tri) + off)
        off = off + jnp.sum(blk, axis=1, keepdims=True)
    return jnp.concatenate(outs, axis=1)


def _route_kernel(h_ref, g_ref, rwt_ref, tri_ref, xn_ref, slotr_ref, slotc_ref, gatec_ref, cnt_ref,
                  *, cap, chunk, tile):
    s = h_ref.shape[0]
    n_e = rwt_ref.shape[0] // 2
    logits = []
    for c in range(s // chunk):
        rows = slice(c * chunk, (c + 1) * chunk)
        xn = _rms(h_ref[rows, :], g_ref[...])
        xn_hi = xn.astype(BF16)
        xn_ref[rows, :] = xn_hi
        xn_lo = (xn - xn_hi.astype(F32)).astype(BF16)
        both = _dot_nt(rwt_ref[...], xn_hi)
        logits.append(both[:n_e] + both[n_e:] + _dot_nt(rwt_ref[:n_e, :], xn_lo))
    lg = jnp.concatenate(logits, axis=1)
    ex = jnp.exp(lg - jnp.max(lg, axis=0, keepdims=True))
    aff = ex / jnp.sum(ex, axis=0, keepdims=True)
    bits = pltpu.bitcast(aff, jnp.int32)
    thr = jnp.zeros((n_e, 1), jnp.int32)
    for shift in range(27, -1, -3):
        digit = jnp.zeros((n_e, 1), jnp.int32)
        for j in range(1, 8):
            cnt = jnp.sum(jnp.where(bits >= (thr | (j << shift)), 1.0, 0.0), axis=1, keepdims=True)
            digit = digit + jnp.where(cnt >= cap, 1, 0)
        thr = thr | (digit * (1 << shift))
    gt = bits > thr
    eq = bits == thr
    need = cap - jnp.sum(jnp.where(gt, 1.0, 0.0), axis=1, keepdims=True)
    eqf = jnp.where(eq, 1.0, 0.0)
    eq_rank = _lane_cumsum(eqf, tri_ref[...]) - eqf
    sel = gt | (eq & (eq_rank < need))
    self_ = jnp.where(sel, 1.0, 0.0)
    slot = jnp.where(sel, _lane_cumsum(self_, tri_ref[...]) - self_, -1.0)
    lane = lax.broadcasted_iota(jnp.int32, (n_e, LANES), 1)
    counts = jnp.zeros((n_e, LANES), F32)
    for i in range(s // tile):
        counts = jnp.where(lane == i, jnp.sum(self_[:, i * tile:(i + 1) * tile], axis=1, keepdims=True), counts)
    cnt_ref[...] = counts.astype(jnp.int32)
    pad_rows = LANES - n_e
    slot_p = jnp.concatenate([slot, jnp.full((pad_rows, s), -1.0, F32)], axis=0)
    gate_p = jnp.concatenate([jnp.where(sel, aff, 0.0), jnp.zeros((pad_rows, s), F32)], axis=0)
    slotr_ref[...] = slot
    for j in range(s // LANES):
        cols = slice(j * LANES, (j + 1) * LANES)
        slotc_ref[cols, :] = slot_p[:, cols].T.astype(BF16)
        gatec_ref[cols, :] = gate_p[:, cols].T.astype(BF16)


def _route(h3, g, router_w, cap):
    b, s, d = h3.shape
    n_e = router_w.shape[1]
    tri = jnp.asarray(np.triu(np.ones((LANES, LANES))), BF16)
    rw_t = router_w.T.astype(F32)
    rw_hi = rw_t.astype(BF16)
    rw_lo = (rw_t - rw_hi.astype(F32)).astype(BF16)
    per_seq = lambda *shape: pl.BlockSpec((None,) + shape, lambda i: (i,) + (0,) * len(shape))
    return pl.pallas_call(
        functools.partial(_route_kernel, cap=cap, chunk=256, tile=TOKEN_TILE),
        out_shape=(
            jax.ShapeDtypeStruct((b, s, d), BF16),
            jax.ShapeDtypeStruct((b, n_e, s), F32),
            jax.ShapeDtypeStruct((b, s, LANES), BF16),
            jax.ShapeDtypeStruct((b, s, LANES), BF16),
            jax.ShapeDtypeStruct((b, n_e, LANES), jnp.int32),
        ),
        grid=(b,),
        in_specs=[per_seq(s, d), _full_spec((1, d)), _full_spec((2 * n_e, d)), _full_spec((LANES, LANES))],
        out_specs=(per_seq(s, d), per_seq(n_e, s), per_seq(s, LANES), per_seq(s, LANES), per_seq(n_e, LANES)),
        compiler_params=_cparams("parallel"),
        name="route",
    )(h3, g.reshape(1, d), jnp.concatenate([rw_hi, rw_lo], axis=0), tri)


def _chunk_tables(cnt, cap):
    n_e = cnt.shape[1]
    kmax = n_e * (TOKEN_TILE // CHUNK + 1)
    kmax = -(-kmax // CHUNKS_PER_GROUP) * CHUNKS_PER_GROUP
    c0 = jnp.cumsum(cnt, axis=2) - cnt
    a0 = c0 // CHUNK
    a1 = jnp.where(cnt > 0, (c0 + cnt + CHUNK - 1) // CHUNK, a0)
    nch = (a1 - a0).transpose(0, 2, 1)
    a0 = a0.transpose(0, 2, 1)
    pos_end = jnp.cumsum(nch, axis=2)
    pos = pos_end - nch
    total = pos_end[..., -1]
    k = jnp.arange(kmax, dtype=jnp.int32)
    e_of_k = jnp.sum((k[None, None, :, None] >= pos_end[:, :, None, :]).astype(jnp.int32), axis=-1)
    e_of_k = jnp.minimum(e_of_k, n_e - 1)
    is_e = e_of_k[..., None] == jnp.arange(n_e, dtype=jnp.int32)
    q = k + jnp.sum(jnp.where(is_e, (a0 - pos)[:, :, None, :], 0), axis=-1)
    valid = k[None, None, :] < total[..., None]
    e_tab = jnp.where(valid, e_of_k, 0)
    slot_tab = jnp.where(valid, q * CHUNK, -(1 << 20))
    dst_tab = jnp.where(valid, e_of_k * cap + q * CHUNK, 0)
    groups = (total + CHUNKS_PER_GROUP - 1) // CHUNKS_PER_GROUP
    return jnp.concatenate([e_tab, slot_tab, dst_tab, groups[..., None]], axis=-1).astype(jnp.int32), kmax


def _gather_kernel(tab_ref, xn_ref, slotr_ref, xs_ref, rows_ref, *, kmax):
    @pl.when(pl.program_id(1) == 0)
    def _():
        xs_ref[...] = jnp.zeros_like(xs_ref)

    sub = lax.broadcasted_iota(jnp.int32, (CHUNK, TOKEN_TILE), 0).astype(F32)
    for tile in range(TILES_PER_STEP):
        tokens = slice(tile * TOKEN_TILE, (tile + 1) * TOKEN_TILE)

        def group(g, carry, tile=tile, tokens=tokens):
            base = g * CHUNKS_PER_GROUP
            onehot = []
            for c in range(CHUNKS_PER_GROUP):
                slots = slotr_ref[pl.ds(tab_ref[tile, base + c], 1), tokens]
                want = sub + tab_ref[tile, kmax + base + c].astype(F32)
                onehot.append(jnp.where(slots == want, 1.0, 0.0).astype(BF16))
            rows_ref[...] = _dot(jnp.concatenate(onehot, axis=0), xn_ref[tokens, :]).astype(BF16)
            for c0 in range(0, CHUNKS_PER_GROUP, 4):
                cs = range(c0, c0 + 4)
                dsts = [pl.ds(pl.multiple_of(tab_ref[tile, 2 * kmax + base + c], CHUNK), CHUNK) for c in cs]
                sums = [xs_ref[dst, :] + rows_ref[c * CHUNK:(c + 1) * CHUNK, :] for c, dst in zip(cs, dsts)]
                for dst, total in reversed(list(zip(dsts, sums))):
                    xs_ref[dst, :] = total
            return carry

        lax.fori_loop(0, tab_ref[tile, 3 * kmax], group, 0)


def _expert_kernel(xs_ref, wg_ref, wu_ref, wd_ref, y_ref, wg_bf, wu_bf, wd_bf):
    @pl.when(pl.program_id(1) == 0)
    def _():
        wg_bf[...] = wg_ref[...].astype(BF16)
        wu_bf[...] = wu_ref[...].astype(BF16)
        wd_bf[...] = wd_ref[...].astype(BF16)

    nb, cap, d = xs_ref.shape
    xs = xs_ref[...].reshape(nb * cap, d)
    hg = _dot(xs, wg_bf[...])
    hu = _dot(xs, wu_bf[...])
    hdn = (hg * jax.nn.sigmoid(hg) * hu).astype(BF16)
    y_ref[...] = _dot(hdn, wd_bf[...]).astype(BF16).reshape(nb, cap, d)


def _scatter_ple_kernel(tab_ref, y_ref, slotc_ref, gatec_ref, h_ref, p_ref, g_ref, wg_ref, wp_ref, o_ref,
                        yg_ref, acc_ref, *, kmax):
    acc_ref[...] = jnp.zeros_like(acc_ref)
    lane = lax.broadcasted_iota(jnp.int32, (1, GROUP), 1)
    sub = lax.broadcasted_iota(jnp.int32, (LANES, GROUP), 0)
    for tile in range(TILES_PER_STEP):
        tokens = slice(tile * TOKEN_TILE, (tile + 1) * TOKEN_TILE)

        def group(g, carry, tile=tile, tokens=tokens):
            base = g * CHUNKS_PER_GROUP
            expert = jnp.zeros((1, GROUP), jnp.int32)
            want = jnp.zeros((1, GROUP), jnp.int32)
            for c in range(CHUNKS_PER_GROUP):
                in_chunk = (lane >= c * CHUNK) & (lane < (c + 1) * CHUNK)
                expert = jnp.where(in_chunk, tab_ref[tile, base + c], expert)
                want = jnp.where(in_chunk, tab_ref[tile, kmax + base + c] + lane - c * CHUNK, want)
                src = pl.ds(pl.multiple_of(tab_ref[tile, 2 * kmax + base + c], CHUNK), CHUNK)
                yg_ref[c * CHUNK:(c + 1) * CHUNK, :] = y_ref[src, :]
            pick = jnp.where(sub == expert, 1.0, 0.0).astype(BF16)
            both = _dot(jnp.concatenate([slotc_ref[tokens, :], gatec_ref[tokens, :]], axis=0), pick)
            weights = jnp.where(both[:TOKEN_TILE] == want.astype(F32), both[TOKEN_TILE:], 0.0).astype(BF16)
            acc_ref[tokens, :] += _dot(weights, yg_ref[...])
            return carry

        lax.fori_loop(0, tab_ref[tile, 3 * kmax], group, 0)
    h = h_ref[...] + acc_ref[...]
    xn = _rms(h, g_ref[...]).astype(BF16)
    gate = jax.nn.sigmoid(_dot(xn, wg_ref[...]))
    o_ref[...] = h + gate * _dot(p_ref[...].astype(BF16), wp_ref[...])


def _expert_ffn_ple(h3, xn3, slot_row, slot_col, gate_col, cnt, w_gate, w_up, w_down, cap, p3, layer, ple_g, ple_wg,
                    ple_wp, seqs_per_step=4):
    b, s, d = xn3.shape
    pd = p3.shape[2]
    _, n_e, _, ff = w_gate.shape
    step_rows = TILES_PER_STEP * TOKEN_TILE
    nt = s // step_rows
    tables, kmax = _chunk_tables(cnt[:, :, :s // TOKEN_TILE], cap)
    tables = tables.reshape(b * nt, TILES_PER_STEP, 3 * kmax + 1)
    tab_spec = pl.BlockSpec((None, TILES_PER_STEP, 3 * kmax + 1), lambda i, t: (i * nt + t, 0, 0),
                            memory_space=pltpu.SMEM)
    xs = pl.pallas_call(
        functools.partial(_gather_kernel, kmax=kmax),
        out_shape=jax.ShapeDtypeStruct((b, n_e * cap, d), BF16),
        grid=(b, nt),
        in_specs=[
            tab_spec,
            pl.BlockSpec((None, step_rows, d), lambda i, t: (i, t, 0)),
            pl.BlockSpec((None, n_e, step_rows), lambda i, t: (i, 0, t)),
        ],
        out_specs=pl.BlockSpec((None, n_e * cap, d), lambda i, t: (i, 0, 0)),
        scratch_shapes=[pltpu.VMEM((GROUP, d), BF16)],
        compiler_params=_cparams("parallel", "arbitrary"),
        name="expert_gather",
    )(tables, xn3, slot_row)
    nb = seqs_per_step
    y = pl.pallas_call(
        _expert_kernel,
        out_shape=jax.ShapeDtypeStruct((b, n_e, cap, d), BF16),
        grid=(n_e, b // nb),
        in_specs=[
            pl.BlockSpec((nb, None, cap, d), lambda e, i: (i, e, 0, 0)),
            pl.BlockSpec((None, None, d, ff), lambda e, i: (layer, e, 0, 0)),
            pl.BlockSpec((None, None, d, ff), lambda e, i: (layer, e, 0, 0)),
            pl.BlockSpec((None, None, ff, d), lambda e, i: (layer, e, 0, 0)),
        ],
        out_specs=pl.BlockSpec((nb, None, cap, d), lambda e, i: (i, e, 0, 0)),
        scratch_shapes=[pltpu.VMEM((d, ff), BF16), pltpu.VMEM((d, ff), BF16), pltpu.VMEM((ff, d), BF16)],
        compiler_params=_cparams("parallel", "arbitrary"),
        name="expert_mlp",
    )(xs.reshape(b, n_e, cap, d), w_gate, w_up, w_down)
    tile_rows = lambda width: pl.BlockSpec((None, step_rows, width), lambda i, t: (i, t, 0))
    return pl.pallas_call(
        functools.partial(_scatter_ple_kernel, kmax=kmax),
        out_shape=jax.ShapeDtypeStruct((b, s, d), F32),
        grid=(b, nt),
        in_specs=[
            tab_spec,
            pl.BlockSpec((None, n_e * cap, d), lambda i, t: (i, 0, 0)),
            tile_rows(LANES), tile_rows(LANES), tile_rows(d),
            pl.BlockSpec((None, step_rows, pd), lambda i, t: (layer, i * nt + t, 0)),
            _full_spec((1, d)), _full_spec((d, d)), _full_spec((pd, d)),
        ],
        out_specs=tile_rows(d),
        scratch_shapes=[pltpu.VMEM((GROUP, d), BF16), pltpu.VMEM((step_rows, d), F32)],
        compiler_params=_cparams("parallel", "arbitrary"),
        name="expert_scatter_ple",
    )(tables, y.reshape(b, n_e * cap, d), slot_col, gate_col, h3, p3, ple_g.reshape(1, d), ple_wg.astype(BF16),
      ple_wp.astype(BF16))


def kernel(x, p, norm_mix_g, norm_ffn_g, w_out, router_w, exp_w_gate, exp_w_up, exp_w_down, ple_norm_g, ple_gate_w, ple_proj_w, a_w_in, a_vnorm_g, a_w_s, a_b_s, b_w_in, b_qnorm_g, b_knorm_g, b_sink, c_w_in, c_qnorm_g, c_knorm_g, c_rpb):
    b, s, d = x.shape
    depth = norm_mix_g.shape[0]
    t = b * s
    cap = max(1, EC_CAPACITY_FACTOR * s // N_EXPERTS)
    scale = HEAD_DIM ** -0.5
    h = x.reshape(t, d)
    for i in range(depth):
        kind = i % N_MIXERS
        j = i // N_MIXERS
        if kind == 0:
            h = _mixer_a(h, norm_mix_g[i], a_w_in[j], a_vnorm_g[j], a_w_s[j], a_b_s[j], w_out[i])
        elif kind == 1:
            gain = jnp.concatenate([jnp.tile(b_qnorm_g[j] * scale, B_HEADS), jnp.tile(b_knorm_g[j], 2 * B_KV_HEADS)])
            qw, kw = B_HEADS * HEAD_DIM, B_KV_HEADS * HEAD_DIM
            w_b = jnp.concatenate([b_w_in[j][:, :qw], _dup_heads(b_w_in[j][:, qw:qw + kw], B_KV_HEADS),
                                   _dup_heads(b_w_in[j][:, qw + kw:], B_KV_HEADS)], axis=1)
            qkv = _norm_proj(h, norm_mix_g[i], w_b, gain, qw + 2 * kw)
            mix = _attn_b(qkv.reshape(b, s, -1), b_sink[j])
            h = _proj_residual(mix.reshape(t, -1), w_out[i], h)
        else:
            gain = jnp.concatenate([jnp.tile(c_qnorm_g[j] * scale, C_HEADS), jnp.tile(c_knorm_g[j], C_HEADS)])
            qkv = _norm_proj(h, norm_mix_g[i], c_w_in[j], gain, 2 * C_HEADS * HEAD_DIM)
            mix = _attn_c(qkv.reshape(b, s // GRID_W, GRID_W, -1), c_rpb[j])
            h = _proj_residual(mix.reshape(t, -1), w_out[i], h)
        xn, slot_row, slot_col, gate_col, cnt = _route(h.reshape(b, s, d), norm_ffn_g[i], router_w[i], cap)
        h = _expert_ffn_ple(h.reshape(b, s, d), xn, slot_row, slot_col, gate_col, cnt, exp_w_gate, exp_w_up,
                            exp_w_down, cap, p.reshape(depth, t, -1), i, ple_norm_g[i], ple_gate_w[i],
                            ple_proj_w[i]).reshape(t, d)
    return h.reshape(b, s, d)
```

```python
import functools

import numpy as np
import jax
import jax.numpy as jnp
from jax import lax
from jax.experimental import pallas as pl
from jax.experimental.pallas import tpu as pltpu

F32 = jnp.float32
BF16 = jnp.bfloat16

RMS_EPS = 1e-6
LN_EPS = 1e-5
NEG = -1e30
HEAD_DIM = 64
GRID_W = 64
A_GROUPS = 8
A_CHUNK = 128
B_HEADS = 16
B_KV_HEADS = 4
B_BLOCK = 128
C_HEADS = 16
C_WIN_ROWS = 8
C_WIN_COLS = 16
N_EXPERTS = 16
EC_CAPACITY_FACTOR = 2
N_MIXERS = 3

LANES = 128
MXU_DIM = 256
VMEM_LIMIT_BYTES = 60 * 1024 * 1024

TOKEN_TILE = 256
TILES_PER_STEP = 4
A_SUBTILES = 2
CHUNK = 16
GROUP = 4 * MXU_DIM
CHUNKS_PER_GROUP = GROUP // CHUNK


def _cparams(*sem):
    return pltpu.CompilerParams(dimension_semantics=sem, vmem_limit_bytes=VMEM_LIMIT_BYTES)


def _rms(xf, g):
    return xf * lax.rsqrt(jnp.mean(xf * xf, axis=-1, keepdims=True) + RMS_EPS) * g


def _dot(a, b):
    return jnp.dot(a, b, preferred_element_type=F32)


def _dot_nt(a, b):
    return lax.dot_general(a, b, (((1,), (1,)), ((), ())), preferred_element_type=F32)


def _full_spec(shape):
    nd = len(shape)
    return pl.BlockSpec(shape, lambda *_: (0,) * nd)


def _mixer_a_kernel(h_ref, g_ref, win_ref, vg_ref, ws_ref, bias_ref, wout_ref, o_ref, mix_ref):
    tm = h_ref.shape[0]
    width = vg_ref.shape[1]
    gw = width // A_GROUPS
    sub = tm // A_SUBTILES
    halves = [slice(i * sub, (i + 1) * sub) for i in range(A_SUBTILES)]
    zs = [_dot(_rms(h_ref[hs, :], g_ref[...]).astype(BF16), win_ref[...]) for hs in halves]
    us, vns = [], []
    for z in zs:
        z = 0.5 * z * (1.0 + jnp.tanh(np.sqrt(2.0 / np.pi).astype(np.float32) * (z + 0.044715 * (z * z * z))))
        v = z[:, width:]
        mu = jnp.mean(v, axis=-1, keepdims=True)
        vc = v - mu
        var = jnp.mean(vc * vc, axis=-1, keepdims=True)
        us.append(z[:, :width])
        vns.append((vc * lax.rsqrt(var + LN_EPS) * vg_ref[...]).astype(BF16))
    for hs, u, vn in zip(halves, us, vns):
        for c in range(sub // A_CHUNK):
            rows = slice(c * A_CHUNK, (c + 1) * A_CHUNK)
            out_rows = slice(hs.start + c * A_CHUNK, hs.start + (c + 1) * A_CHUNK)
            for g in range(A_GROUPS):
                cols = slice(g * gw, (g + 1) * gw)
                s = _dot(ws_ref[g], vn[rows, cols]) + bias_ref[:, cols]
                mix_ref[out_rows, cols] = (u[rows, cols] * s).astype(BF16)
    o_ref[...] = h_ref[...] + _dot(mix_ref[...], wout_ref[...])


def _mixer_a(h2, g, w_in, vnorm_g, w_s, b_s, w_out, tm=512):
    t, d = h2.shape
    width = vnorm_g.shape[0]
    gw = width // A_GROUPS
    bias = jnp.repeat(b_s.T.astype(F32), gw, axis=1)
    return pl.pallas_call(
        _mixer_a_kernel,
        out_shape=jax.ShapeDtypeStruct((t, d), F32),
        grid=(t // tm,),
        in_specs=[
            pl.BlockSpec((tm, d), lambda i: (i, 0)),
            _full_spec((1, d)),
            _full_spec((d, 2 * width)),
            _full_spec((1, width)),
            _full_spec((A_GROUPS, A_CHUNK, A_CHUNK)),
            _full_spec((A_CHUNK, width)),
            _full_spec((width, d)),
        ],
        out_specs=pl.BlockSpec((tm, d), lambda i: (i, 0)),
        scratch_shapes=[pltpu.VMEM((tm, width), BF16)],
        compiler_params=_cparams("parallel"),
        name="mixer_a",
    )(h2, g.reshape(1, d), w_in.astype(BF16), vnorm_g.reshape(1, width), w_s.astype(BF16), bias,
      w_out.astype(BF16))


def _norm_proj_kernel(h_ref, g_ref, w_ref, hg_ref, bd_ref, o_ref, *, n_norm_cols):
    xn = _rms(h_ref[...], g_ref[...]).astype(BF16)
    acc = _dot(xn, w_ref[...])
    for j in range(n_norm_cols // MXU_DIM):
        cols = slice(j * MXU_DIM, (j + 1) * MXU_DIM)
        blk = acc[:, cols]
        ss = _dot((blk * blk).astype(BF16), bd_ref[...])
        o_ref[:, cols] = (blk * lax.rsqrt(ss * (1.0 / HEAD_DIM) + RMS_EPS) * hg_ref[:, cols]).astype(BF16)
    o_ref[:, n_norm_cols:] = acc[:, n_norm_cols:].astype(BF16)


def _norm_proj(h2, g, w, head_gain, n_norm_cols, tm=512):
    t, d = h2.shape
    n = w.shape[1]
    blockdiag = jnp.asarray(np.kron(np.eye(MXU_DIM // HEAD_DIM), np.ones((HEAD_DIM, HEAD_DIM))), BF16)
    return pl.pallas_call(
        functools.partial(_norm_proj_kernel, n_norm_cols=n_norm_cols),
        out_shape=jax.ShapeDtypeStruct((t, n), BF16),
        grid=(t // tm,),
        in_specs=[
            pl.BlockSpec((tm, d), lambda i: (i, 0)),
            _full_spec((1, d)),
            _full_spec((d, n)),
            _full_spec((1, n_norm_cols)),
            _full_spec((MXU_DIM, MXU_DIM)),
        ],
        out_specs=pl.BlockSpec((tm, n), lambda i: (i, 0)),
        compiler_params=_cparams("parallel"),
        name="norm_proj",
    )(h2, g.reshape(1, d), w.astype(BF16), head_gain.reshape(1, n_norm_cols).astype(F32), blockdiag)


def _alibi_slopes(n):
    return np.array([2.0 ** (-8.0 * (h + 1) / n) for h in range(n)], dtype=np.float32)


def _attn_b_tables():
    span = 3 * B_BLOCK
    rel = np.arange(span)[None, :] - B_BLOCK - np.arange(B_BLOCK)[:, None]
    in_window = np.abs(rel) <= B_BLOCK
    alibi = (-_alibi_slopes(B_HEADS)[:, None, None] * np.abs(rel)[None]).astype(np.float32)
    kblk = np.arange(span) // B_BLOCK
    tabs = []
    for kind in range(3):
        valid = in_window & ~((kind == 0) & (kblk == 0))[None, :] & ~((kind == 2) & (kblk == 2))[None, :]
        tabs.append(np.where(valid[None], alibi, np.float32(NEG)))
    return np.stack(tabs).astype(np.float32)


def _attn_b_kernel(sink_ref, q_ref, kp_ref, kc_ref, kn_ref, vp_ref, vc_ref, vn_ref, tab_ref, h_ref, wout_ref, o_ref,
                   mix_ref):
    grp = B_HEADS // B_KV_HEADS
    nq = q_ref.shape[1]
    lane = lax.broadcasted_iota(jnp.int32, (1, LANES), 1)
    low = lane < HEAD_DIM
    span = kp_ref.shape[1] + kc_ref.shape[1] + kn_ref.shape[1]
    ones = jnp.ones((span, LANES), BF16)
    heads = [slice(h * nq, (h + 1) * nq) for h in range(B_HEADS)]

    def one_sequence(i, carry):
        scores = []
        for kh in range(B_KV_HEADS):
            tile = slice(kh * LANES, (kh + 1) * LANES)
            kt = jnp.concatenate([kp_ref[i, :, tile], kc_ref[i, :, tile], kn_ref[i, :, tile]], axis=0)
            rows = []
            for qt in range(kh * grp // 2, (kh + 1) * grp // 2):
                q2 = q_ref[i, :, qt * LANES:(qt + 1) * LANES]
                zero = jnp.zeros_like(q2)
                rows += [jnp.where(low, q2, zero), jnp.where(low, zero, q2)]
            scores.append(_dot_nt(jnp.concatenate(rows, axis=0), kt) + tab_ref[kh])
        sh = jnp.concatenate(scores, axis=0)
        rowmax = jnp.broadcast_to(jnp.max(sh, axis=-1, keepdims=True), (sh.shape[0], LANES))
        m = jnp.concatenate([jnp.maximum(rowmax[hs], sink_ref[h]) for h, hs in enumerate(heads)], axis=0)
        sink_term = jnp.concatenate([jnp.exp(sink_ref[h] - m[hs]) for h, hs in enumerate(heads)], axis=0)
        pe = jnp.exp(sh - jnp.concatenate([m] * (span // LANES), axis=1)).astype(BF16)
        for kh in range(B_KV_HEADS):
            tile = slice(kh * LANES, (kh + 1) * LANES)
            grows = slice(kh * grp * nq, (kh + 1) * grp * nq)
            vt = jnp.concatenate([vp_ref[i, :, tile], vc_ref[i, :, tile], vn_ref[i, :, tile]], axis=0)
            od = _dot(pe[grows], jnp.concatenate([vt, ones], axis=1))
            o = od[:, :LANES] / (od[:, LANES:] + sink_term[grows])
            for j in range(grp // 2):
                qt = kh * grp // 2 + j
                even, odd = o[(2 * j) * nq:(2 * j + 1) * nq], o[(2 * j + 1) * nq:(2 * j + 2) * nq]
                mix_ref[:, qt * LANES:(qt + 1) * LANES] = jnp.where(low, even, odd).astype(BF16)
        o_ref[i] = h_ref[i] + _dot(mix_ref[...], wout_ref[...])
        return carry

    lax.fori_loop(0, q_ref.shape[0], one_sequence, 0)


def _dup_heads(w, n_heads):
    d = w.shape[0]
    return jnp.repeat(w.reshape(d, n_heads, 1, HEAD_DIM), 2, axis=2).reshape(d, n_heads * LANES)


def _attn_b(qkv3, sink, h3, w_out, seqs_per_step=4):
    b, s, d = h3.shape
    nb = s // B_BLOCK
    ns = seqs_per_step
    grp = B_HEADS // B_KV_HEADS
    qw = B_HEADS * HEAD_DIM
    kvw = B_KV_HEADS * LANES
    kcol = qw // kvw
    vcol = kcol + 1
    tabs = jnp.asarray(_attn_b_tables().reshape(3, B_KV_HEADS, grp * B_BLOCK, 3 * B_BLOCK))
    prev = lambda n: jnp.maximum(n - 1, 0)
    nxt = lambda n: jnp.minimum(n + 1, nb - 1)
    kind = lambda n: jnp.where(n == 0, 0, jnp.where(n == nb - 1, 2, 1))
    kv = lambda col, f: pl.BlockSpec((ns, B_BLOCK, kvw), lambda bi, n: (bi, f(n), col))
    same = lambda n: n
    return pl.pallas_call(
        _attn_b_kernel,
        out_shape=jax.ShapeDtypeStruct((b, s, d), F32),
        grid=(b // ns, nb),
        in_specs=[
            pl.BlockSpec(memory_space=pltpu.SMEM),
            pl.BlockSpec((ns, B_BLOCK, qw), lambda bi, n: (bi, n, 0)),
            kv(kcol, prev), kv(kcol, same), kv(kcol, nxt),
            kv(vcol, prev), kv(vcol, same), kv(vcol, nxt),
            pl.BlockSpec((None, B_KV_HEADS, grp * B_BLOCK, 3 * B_BLOCK), lambda bi, n: (kind(n), 0, 0, 0)),
            pl.BlockSpec((ns, B_BLOCK, d), lambda bi, n: (bi, n, 0)),
            _full_spec((qw, d)),
        ],
        out_specs=pl.BlockSpec((ns, B_BLOCK, d), lambda bi, n: (bi, n, 0)),
        scratch_shapes=[pltpu.VMEM((B_BLOCK, qw), BF16)],
        compiler_params=_cparams("parallel", "arbitrary"),
        name="attn_b",
    )(sink.astype(F32), qkv3, qkv3, qkv3, qkv3, qkv3, qkv3, qkv3, tabs, h3, w_out.astype(BF16))


def _attn_c_table(rpb, rows):
    w = GRID_W
    kr_n = min(C_WIN_ROWS, rows)
    c = np.arange(w)
    cs = np.clip(c - C_WIN_COLS // 2, 0, w - C_WIN_COLS)
    kcol = np.arange(w)
    colmask = (kcol[None, :] >= cs[:, None]) & (kcol[None, :] < cs[:, None] + C_WIN_COLS)
    n_off = 2 * C_WIN_ROWS - kr_n
    n_h, n_ri, _ = rpb.shape
    pad = w - C_WIN_COLS
    padded = jnp.pad(rpb.astype(F32), ((0, 0), (0, 0), (pad, pad)))
    bias = jnp.stack([padded[:, :, w - 1 - qc:2 * w - 1 - qc] for qc in range(w)], axis=1)
    bias = jnp.where(jnp.asarray(colmask)[None, :, None, :], bias, NEG)
    bias = bias.reshape(n_h, w, n_ri * w)
    slabs = jnp.stack([bias[:, :, n * w:(n + kr_n) * w] for n in range(n_off)])
    return slabs.reshape(n_off, n_h // 2, 2 * w, kr_n * w)


def _attn_c_kernel(q_ref, k_ref, v_ref, tab_ref, h_ref, wout_ref, o_ref, mix_ref):
    nk = k_ref.shape[1] * k_ref.shape[2]
    nq = q_ref.shape[1]
    lane = lax.broadcasted_iota(jnp.int32, (1, LANES), 1)
    low = lane < HEAD_DIM
    ones = jnp.ones((nk, LANES), BF16)

    def one_sequence(i, carry):
        scores = []
        for t in range(C_HEADS // 2):
            cols = slice(t * LANES, (t + 1) * LANES)
            q2 = q_ref[i, :, cols]
            zero = jnp.zeros_like(q2)
            qq = jnp.concatenate([jnp.where(low, q2, zero), jnp.where(low, zero, q2)], axis=0)
            kt = k_ref[i, :, :, cols].reshape(nk, LANES)
            scores.append(_dot_nt(qq, kt) + tab_ref[t])
        sh = jnp.concatenate(scores, axis=0)
        m = jnp.broadcast_to(jnp.max(sh, axis=-1, keepdims=True), (sh.shape[0], LANES))
        pe = jnp.exp(sh - jnp.concatenate([m] * (nk // LANES), axis=1)).astype(BF16)
        for t in range(C_HEADS // 2):
            cols = slice(t * LANES, (t + 1) * LANES)
            rows = slice(2 * t * nq, 2 * (t + 1) * nq)
            vt = v_ref[i, :, :, cols].reshape(nk, LANES)
            od = _dot(pe[rows], jnp.concatenate([vt, ones], axis=1))
            o = od[:, :LANES] / od[:, LANES:]
            mix_ref[pl.ds(pl.multiple_of(i * nq, nq), nq), cols] = jnp.where(low, o[:nq], o[nq:]).astype(BF16)
        return carry

    n_seq = q_ref.shape[0]
    lax.fori_loop(0, n_seq, one_sequence, 0)
    proj = _dot(mix_ref[...], wout_ref[...])
    o_ref[...] = h_ref[...] + proj.reshape(n_seq, nq, proj.shape[1])


def _attn_c(qkv4, rpb, h4, w_out, seqs_per_step=4):
    b, rows, w, d = h4.shape
    hw = C_HEADS * HEAD_DIM
    kr_n = min(C_WIN_ROWS, rows)
    nb = seqs_per_step
    table = _attn_c_table(rpb, rows)
    row_start = lambda r: jnp.clip(r - kr_n // 2, 0, rows - kr_n)
    el = pl.Element
    return pl.pallas_call(
        _attn_c_kernel,
        out_shape=jax.ShapeDtypeStruct((b, rows, w, d), F32),
        grid=(rows, b // nb),
        in_specs=[
            pl.BlockSpec((nb, None, w, hw), lambda r, bi: (bi, r, 0, 0)),
            pl.BlockSpec((el(nb), el(kr_n), el(w), el(hw)), lambda r, bi: (bi * nb, row_start(r), 0, hw)),
            pl.BlockSpec((el(nb), el(kr_n), el(w), el(hw)), lambda r, bi: (bi * nb, row_start(r), 0, 2 * hw)),
            pl.BlockSpec((None, C_HEADS // 2, 2 * w, kr_n * w),
                         lambda r, bi: (row_start(r) - r + C_WIN_ROWS - 1, 0, 0, 0)),
            pl.BlockSpec((nb, None, w, d), lambda r, bi: (bi, r, 0, 0)),
            _full_spec((hw, d)),
        ],
        out_specs=pl.BlockSpec((nb, None, w, d), lambda r, bi: (bi, r, 0, 0)),
        scratch_shapes=[pltpu.VMEM((nb * w, hw), BF16)],
        compiler_params=_cparams("arbitrary", "arbitrary"),
        name="attn_c",
    )(qkv4, qkv4, qkv4, table, h4, w_out.astype(BF16))


def _lane_cumsum(x, tri):
    e, s = x.shape
    off = jnp.zeros((e, 1), F32)
    outs = []
    for j in range(s // LANES):
        blk = x[:, j * LANES:(j + 1) * LANES]
        outs.append(_dot(blk.astype(BF16), tri) + off)
        off = off + jnp.sum(blk, axis=1, keepdims=True)
    return jnp.concatenate(outs, axis=1)


def _route_kernel(h_ref, g_ref, rwt_ref, tri_ref, xn_ref, slotr_ref, slotc_ref, gatec_ref, cnt_ref,
                  *, cap, chunk, tile):
    s = h_ref.shape[0]
    n_e = rwt_ref.shape[0] // 2
    logits = []
    for c in range(s // chunk):
        rows = slice(c * chunk, (c + 1) * chunk)
        xn = _rms(h_ref[rows, :], g_ref[...])
        xn_hi = xn.astype(BF16)
        xn_ref[rows, :] = xn_hi
        xn_lo = (xn - xn_hi.astype(F32)).astype(BF16)
        both = _dot_nt(rwt_ref[...], xn_hi)
        logits.append(both[:n_e] + both[n_e:] + _dot_nt(rwt_ref[:n_e, :], xn_lo))
    lg = jnp.concatenate(logits, axis=1)
    ex = jnp.exp(lg - jnp.max(lg, axis=0, keepdims=True))
    aff = ex / jnp.sum(ex, axis=0, keepdims=True)
    bits = pltpu.bitcast(aff, jnp.int32)
    thr = jnp.zeros((n_e, 1), jnp.int32)
    for shift in range(27, -1, -3):
        digit = jnp.zeros((n_e, 1), jnp.int32)
        for j in range(1, 8):
            cnt = jnp.sum(jnp.where(bits >= (thr | (j << shift)), 1.0, 0.0), axis=1, keepdims=True)
            digit = digit + jnp.where(cnt >= cap, 1, 0)
        thr = thr | (digit * (1 << shift))
    gt = bits > thr
    eq = bits == thr
    need = cap - jnp.sum(jnp.where(gt, 1.0, 0.0), axis=1, keepdims=True)
    eqf = jnp.where(eq, 1.0, 0.0)
    eq_rank = _lane_cumsum(eqf, tri_ref[...]) - eqf
    sel = gt | (eq & (eq_rank < need))
    self_ = jnp.where(sel, 1.0, 0.0)
    slot = jnp.where(sel, _lane_cumsum(self_, tri_ref[...]) - self_, -1.0)
    lane = lax.broadcasted_iota(jnp.int32, (n_e, LANES), 1)
    counts = jnp.zeros((n_e, LANES), F32)
    for i in range(s // tile):
        counts = jnp.where(lane == i, jnp.sum(self_[:, i * tile:(i + 1) * tile], axis=1, keepdims=True), counts)
    cnt_ref[...] = counts.astype(jnp.int32)
    pad_rows = LANES - n_e
    slot_p = jnp.concatenate([slot, jnp.full((pad_rows, s), -1.0, F32)], axis=0)
    gate_p = jnp.concatenate([jnp.where(sel, aff, 0.0), jnp.zeros((pad_rows, s), F32)], axis=0)
    slotr_ref[...] = slot
    for j in range(s // LANES):
        cols = slice(j * LANES, (j + 1) * LANES)
        slotc_ref[cols, :] = slot_p[:, cols].T.astype(BF16)
        gatec_ref[cols, :] = gate_p[:, cols].T.astype(BF16)


def _route(h3, g, router_w, cap):
    b, s, d = h3.shape
    n_e = router_w.shape[1]
    tri = jnp.asarray(np.triu(np.ones((LANES, LANES))), BF16)
    rw_t = router_w.T.astype(F32)
    rw_hi = rw_t.astype(BF16)
    rw_lo = (rw_t - rw_hi.astype(F32)).astype(BF16)
    per_seq = lambda *shape: pl.BlockSpec((None,) + shape, lambda i: (i,) + (0,) * len(shape))
    return pl.pallas_call(
        functools.partial(_route_kernel, cap=cap, chunk=256, tile=TOKEN_TILE),
        out_shape=(
            jax.ShapeDtypeStruct((b, s, d), BF16),
            jax.ShapeDtypeStruct((b, n_e, s), F32),
            jax.ShapeDtypeStruct((b, s, LANES), BF16),
            jax.ShapeDtypeStruct((b, s, LANES), BF16),
            jax.ShapeDtypeStruct((b, n_e, LANES), jnp.int32),
        ),
        grid=(b,),
        in_specs=[per_seq(s, d), _full_spec((1, d)), _full_spec((2 * n_e, d)), _full_spec((LANES, LANES))],
        out_specs=(per_seq(s, d), per_seq(n_e, s), per_seq(s, LANES), per_seq(s, LANES), per_seq(n_e, LANES)),
        compiler_params=_cparams("parallel"),
        name="route",
    )(h3, g.reshape(1, d), jnp.concatenate([rw_hi, rw_lo], axis=0), tri)


def _chunk_tables(cnt, cap):
    n_e = cnt.shape[1]
    kmax = n_e * (TOKEN_TILE // CHUNK + 1)
    kmax = -(-kmax // CHUNKS_PER_GROUP) * CHUNKS_PER_GROUP
    c0 = jnp.cumsum(cnt, axis=2) - cnt
    a0 = c0 // CHUNK
    a1 = jnp.where(cnt > 0, (c0 + cnt + CHUNK - 1) // CHUNK, a0)
    nch = (a1 - a0).transpose(0, 2, 1)
    a0 = a0.transpose(0, 2, 1)
    pos_end = jnp.cumsum(nch, axis=2)
    pos = pos_end - nch
    total = pos_end[..., -1]
    k = jnp.arange(kmax, dtype=jnp.int32)
    e_of_k = jnp.sum((k[None, None, :, None] >= pos_end[:, :, None, :]).astype(jnp.int32), axis=-1)
    e_of_k = jnp.minimum(e_of_k, n_e - 1)
    is_e = e_of_k[..., None] == jnp.arange(n_e, dtype=jnp.int32)
    q = k + jnp.sum(jnp.where(is_e, (a0 - pos)[:, :, None, :], 0), axis=-1)
    valid = k[None, None, :] < total[..., None]
    e_tab = jnp.where(valid, e_of_k, 0)
    slot_tab = jnp.where(valid, q * CHUNK, -(1 << 20))
    dst_tab = jnp.where(valid, e_of_k * cap + q * CHUNK, 0)
    groups = (total + CHUNKS_PER_GROUP - 1) // CHUNKS_PER_GROUP
    return jnp.concatenate([e_tab, slot_tab, dst_tab, groups[..., None]], axis=-1).astype(jnp.int32), kmax


def _gather_kernel(tab_ref, xn_ref, slotr_ref, xs_ref, rows_ref, *, kmax):
    @pl.when(pl.program_id(1) == 0)
    def _():
        xs_ref[...] = jnp.zeros_like(xs_ref)

    sub = lax.broadcasted_iota(jnp.int32, (CHUNK, TOKEN_TILE), 0).astype(F32)
    for tile in range(TILES_PER_STEP):
        tokens = slice(tile * TOKEN_TILE, (tile + 1) * TOKEN_TILE)

        def group(g, carry, tile=tile, tokens=tokens):
            base = g * CHUNKS_PER_GROUP
            onehot = []
            for c in range(CHUNKS_PER_GROUP):
                slots = slotr_ref[pl.ds(tab_ref[tile, base + c], 1), tokens]
                want = sub + tab_ref[tile, kmax + base + c].astype(F32)
                onehot.append(jnp.where(slots == want, 1.0, 0.0).astype(BF16))
            rows_ref[...] = _dot(jnp.concatenate(onehot, axis=0), xn_ref[tokens, :]).astype(BF16)
            for c0 in range(0, CHUNKS_PER_GROUP, 4):
                cs = range(c0, c0 + 4)
                dsts = [pl.ds(pl.multiple_of(tab_ref[tile, 2 * kmax + base + c], CHUNK), CHUNK) for c in cs]
                sums = [xs_ref[dst, :] + rows_ref[c * CHUNK:(c + 1) * CHUNK, :] for c, dst in zip(cs, dsts)]
                for dst, total in reversed(list(zip(dsts, sums))):
                    xs_ref[dst, :] = total
            return carry

        lax.fori_loop(0, tab_ref[tile, 3 * kmax], group, 0)


def _expert_kernel(xs_ref, wg_ref, wu_ref, wd_ref, y_ref, wg_bf, wu_bf, wd_bf):
    @pl.when(pl.program_id(1) == 0)
    def _():
        wg_bf[...] = wg_ref[...].astype(BF16)
        wu_bf[...] = wu_ref[...].astype(BF16)
        wd_bf[...] = wd_ref[...].astype(BF16)

    nb, cap, d = xs_ref.shape
    xs = xs_ref[...].reshape(nb * cap, d)
    hg = _dot(xs, wg_bf[...])
    hu = _dot(xs, wu_bf[...])
    hdn = (hg * jax.nn.sigmoid(hg) * hu).astype(BF16)
    y_ref[...] = _dot(hdn, wd_bf[...]).astype(BF16).reshape(nb, cap, d)


def _scatter_ple_kernel(tab_ref, y_ref, slotc_ref, gatec_ref, h_ref, p_ref, g_ref, wg_ref, wp_ref, o_ref,
                        yg_ref, acc_ref, *, kmax):
    acc_ref[...] = jnp.zeros_like(acc_ref)
    lane = lax.broadcasted_iota(jnp.int32, (1, GROUP), 1)
    sub = lax.broadcasted_iota(jnp.int32, (LANES, GROUP), 0)
    for tile in range(TILES_PER_STEP):
        tokens = slice(tile * TOKEN_TILE, (tile + 1) * TOKEN_TILE)

        def group(g, carry, tile=tile, tokens=tokens):
            base = g * CHUNKS_PER_GROUP
            expert = jnp.zeros((1, GROUP), jnp.int32)
            want = jnp.zeros((1, GROUP), jnp.int32)
            for c in range(CHUNKS_PER_GROUP):
                in_chunk = (lane >= c * CHUNK) & (lane < (c + 1) * CHUNK)
                expert = jnp.where(in_chunk, tab_ref[tile, base + c], expert)
                want = jnp.where(in_chunk, tab_ref[tile, kmax + base + c] + lane - c * CHUNK, want)
                src = pl.ds(pl.multiple_of(tab_ref[tile, 2 * kmax + base + c], CHUNK), CHUNK)
                yg_ref[c * CHUNK:(c + 1) * CHUNK, :] = y_ref[src, :]
            pick = jnp.where(sub == expert, 1.0, 0.0).astype(BF16)
            both = _dot(jnp.concatenate([slotc_ref[tokens, :], gatec_ref[tokens, :]], axis=0), pick)
            weights = jnp.where(both[:TOKEN_TILE] == want.astype(F32), both[TOKEN_TILE:], 0.0).astype(BF16)
            acc_ref[tokens, :] += _dot(weights, yg_ref[...])
            return carry

        lax.fori_loop(0, tab_ref[tile, 3 * kmax], group, 0)
    h = h_ref[...] + acc_ref[...]
    xn = _rms(h, g_ref[...]).astype(BF16)
    gate = jax.nn.sigmoid(_dot(xn, wg_ref[...]))
    o_ref[...] = h + gate * _dot(p_ref[...].astype(BF16), wp_ref[...])


def _expert_ffn_ple(h3, xn3, slot_row, slot_col, gate_col, cnt, w_gate, w_up, w_down, cap, p3, layer, ple_g, ple_wg,
                    ple_wp, seqs_per_step=4):
    b, s, d = xn3.shape
    pd = p3.shape[2]
    _, n_e, _, ff = w_gate.shape
    step_rows = TILES_PER_STEP * TOKEN_TILE
    nt = s // step_rows
    tables, kmax = _chunk_tables(cnt[:, :, :s // TOKEN_TILE], cap)
    tables = tables.reshape(b * nt, TILES_PER_STEP, 3 * kmax + 1)
    tab_spec = pl.BlockSpec((None, TILES_PER_STEP, 3 * kmax + 1), lambda i, t: (i * nt + t, 0, 0),
                            memory_space=pltpu.SMEM)
    xs = pl.pallas_call(
        functools.partial(_gather_kernel, kmax=kmax),
        out_shape=jax.ShapeDtypeStruct((b, n_e * cap, d), BF16),
        grid=(b, nt),
        in_specs=[
            tab_spec,
            pl.BlockSpec((None, step_rows, d), lambda i, t: (i, t, 0)),
            pl.BlockSpec((None, n_e, step_rows), lambda i, t: (i, 0, t)),
        ],
        out_specs=pl.BlockSpec((None, n_e * cap, d), lambda i, t: (i, 0, 0)),
        scratch_shapes=[pltpu.VMEM((GROUP, d), BF16)],
        compiler_params=_cparams("parallel", "arbitrary"),
        name="expert_gather",
    )(tables, xn3, slot_row)
    nb = seqs_per_step
    y = pl.pallas_call(
        _expert_kernel,
        out_shape=jax.ShapeDtypeStruct((b, n_e, cap, d), BF16),
        grid=(n_e, b // nb),
        in_specs=[
            pl.BlockSpec((nb, None, cap, d), lambda e, i: (i, e, 0, 0)),
            pl.BlockSpec((None, None, d, ff), lambda e, i: (layer, e, 0, 0)),
            pl.BlockSpec((None, None, d, ff), lambda e, i: (layer, e, 0, 0)),
            pl.BlockSpec((None, None, ff, d), lambda e, i: (layer, e, 0, 0)),
        ],
        out_specs=pl.BlockSpec((nb, None, cap, d), lambda e, i: (i, e, 0, 0)),
        scratch_shapes=[pltpu.VMEM((d, ff), BF16), pltpu.VMEM((d, ff), BF16), pltpu.VMEM((ff, d), BF16)],
        compiler_params=_cparams("parallel", "arbitrary"),
        name="expert_mlp",
    )(xs.reshape(b, n_e, cap, d), w_gate, w_up, w_down)
    tile_rows = lambda width: pl.BlockSpec((None, step_rows, width), lambda i, t: (i, t, 0))
    return pl.pallas_call(
        functools.partial(_scatter_ple_kernel, kmax=kmax),
        out_shape=jax.ShapeDtypeStruct((b, s, d), F32),
        grid=(b, nt),
        in_specs=[
            tab_spec,
            pl.BlockSpec((None, n_e * cap, d), lambda i, t: (i, 0, 0)),
            tile_rows(LANES), tile_rows(LANES), tile_rows(d),
            pl.BlockSpec((None, step_rows, pd), lambda i, t: (layer, i * nt + t, 0)),
            _full_spec((1, d)), _full_spec((d, d)), _full_spec((pd, d)),
        ],
        out_specs=tile_rows(d),
        scratch_shapes=[pltpu.VMEM((GROUP, d), BF16), pltpu.VMEM((step_rows, d), F32)],
        compiler_params=_cparams("parallel", "arbitrary"),
        name="expert_scatter_ple",
    )(tables, y.reshape(b, n_e * cap, d), slot_col, gate_col, h3, p3, ple_g.reshape(1, d), ple_wg.astype(BF16),
      ple_wp.astype(BF16))


def kernel(x, p, norm_mix_g, norm_ffn_g, w_out, router_w, exp_w_gate, exp_w_up, exp_w_down, ple_norm_g, ple_gate_w, ple_proj_w, a_w_in, a_vnorm_g, a_w_s, a_b_s, b_w_in, b_qnorm_g, b_knorm_g, b_sink, c_w_in, c_qnorm_g, c_knorm_g, c_rpb):
    b, s, d = x.shape
    depth = norm_mix_g.shape[0]
    t = b * s
    cap = max(1, EC_CAPACITY_FACTOR * s // N_EXPERTS)
    scale = HEAD_DIM ** -0.5
    h = x.reshape(t, d)
    for i in range(depth):
        kind = i % N_MIXERS
        j = i // N_MIXERS
        if kind == 0:
            h = _mixer_a(h, norm_mix_g[i], a_w_in[j], a_vnorm_g[j], a_w_s[j], a_b_s[j], w_out[i])
        elif kind == 1:
            gain = jnp.concatenate([jnp.tile(b_qnorm_g[j] * scale, B_HEADS), jnp.tile(b_knorm_g[j], 2 * B_KV_HEADS)])
            qw, kw = B_HEADS * HEAD_DIM, B_KV_HEADS * HEAD_DIM
            w_b = jnp.concatenate([b_w_in[j][:, :qw], _dup_heads(b_w_in[j][:, qw:qw + kw], B_KV_HEADS),
                                   _dup_heads(b_w_in[j][:, qw + kw:], B_KV_HEADS)], axis=1)
            qkv = _norm_proj(h, norm_mix_g[i], w_b, gain, qw + 2 * kw)
            h = _attn_b(qkv.reshape(b, s, -1), b_sink[j], h.reshape(b, s, d), w_out[i]).reshape(t, d)
        else:
            gain = jnp.concatenate([jnp.tile(c_qnorm_g[j] * scale, C_HEADS), jnp.tile(c_knorm_g[j], C_HEADS)])
            qkv = _norm_proj(h, norm_mix_g[i], c_w_in[j], gain, 2 * C_HEADS * HEAD_DIM)
            h = _attn_c(qkv.reshape(b, s // GRID_W, GRID_W, -1), c_rpb[j], h.reshape(b, s // GRID_W, GRID_W, d),
                        w_out[i]).reshape(t, d)
        xn, slot_row, slot_col, gate_col, cnt = _route(h.reshape(b, s, d), norm_ffn_g[i], router_w[i], cap)
        h = _expert_ffn_ple(h.reshape(b, s, d), xn, slot_row, slot_col, gate_col, cnt, exp_w_gate, exp_w_up,
                            exp_w_down, cap, p.reshape(depth, t, -1), i, ple_norm_g[i], ple_gate_w[i],
                            ple_proj_w[i]).reshape(t, d)
    return h.reshape(b, s, d)
```

```python
import functools

import numpy as np
import jax
import jax.numpy as jnp
from jax import lax
from jax.experimental import pallas as pl
from jax.experimental.pallas import tpu as pltpu

F32 = jnp.float32
BF16 = jnp.bfloat16

RMS_EPS = 1e-6
LN_EPS = 1e-5
NEG = -1e30
HEAD_DIM = 64
GRID_W = 64
A_GROUPS = 8
A_CHUNK = 128
B_HEADS = 16
B_KV_HEADS = 4
B_BLOCK = 128
C_HEADS = 16
C_WIN_ROWS = 8
C_WIN_COLS = 16
N_EXPERTS = 16
EC_CAPACITY_FACTOR = 2
N_MIXERS = 3

LANES = 128
MXU_DIM = 256
VMEM_LIMIT_BYTES = 60 * 1024 * 1024

TOKEN_TILE = 256
TILES_PER_STEP = 4
A_SUBTILES = 2
CHUNK = 16
GROUP = 4 * MXU_DIM
CHUNKS_PER_GROUP = GROUP // CHUNK


def _cparams(*sem):
    return pltpu.CompilerParams(dimension_semantics=sem, vmem_limit_bytes=VMEM_LIMIT_BYTES)


def _rms(xf, g):
    return xf * lax.rsqrt(jnp.mean(xf * xf, axis=-1, keepdims=True) + RMS_EPS) * g


def _dot(a, b):
    return jnp.dot(a, b, preferred_element_type=F32)


def _dot_nt(a, b):
    return lax.dot_general(a, b, (((1,), (1,)), ((), ())), preferred_element_type=F32)


def _full_spec(shape):
    nd = len(shape)
    return pl.BlockSpec(shape, lambda *_: (0,) * nd)


def _mixer_a_kernel(h_ref, g_ref, win_ref, vg_ref, ws_ref, bias_ref, wout_ref, o_ref, mix_ref):
    tm = h_ref.shape[0]
    width = vg_ref.shape[1]
    gw = width // A_GROUPS
    sub = tm // A_SUBTILES
    halves = [slice(i * sub, (i + 1) * sub) for i in range(A_SUBTILES)]
    zs = [_dot(_rms(h_ref[hs, :], g_ref[...]).astype(BF16), win_ref[...]) for hs in halves]
    us, vns = [], []
    for z in zs:
        z = 0.5 * z * (1.0 + jnp.tanh(np.sqrt(2.0 / np.pi).astype(np.float32) * (z + 0.044715 * (z * z * z))))
        v = z[:, width:]
        mu = jnp.mean(v, axis=-1, keepdims=True)
        vc = v - mu
        var = jnp.mean(vc * vc, axis=-1, keepdims=True)
        us.append(z[:, :width])
        vns.append((vc * lax.rsqrt(var + LN_EPS) * vg_ref[...]).astype(BF16))
    for hs, u, vn in zip(halves, us, vns):
        for c in range(sub // A_CHUNK):
            rows = slice(c * A_CHUNK, (c + 1) * A_CHUNK)
            out_rows = slice(hs.start + c * A_CHUNK, hs.start + (c + 1) * A_CHUNK)
            for g in range(A_GROUPS):
                cols = slice(g * gw, (g + 1) * gw)
                s = _dot(ws_ref[g], vn[rows, cols]) + bias_ref[:, cols]
                mix_ref[out_rows, cols] = (u[rows, cols] * s).astype(BF16)
    o_ref[...] = h_ref[...] + _dot(mix_ref[...], wout_ref[...])


def _mixer_a(h2, g, w_in, vnorm_g, w_s, b_s, w_out, tm=512):
    t, d = h2.shape
    width = vnorm_g.shape[0]
    gw = width // A_GROUPS
    bias = jnp.repeat(b_s.T.astype(F32), gw, axis=1)
    return pl.pallas_call(
        _mixer_a_kernel,
        out_shape=jax.ShapeDtypeStruct((t, d), F32),
        grid=(t // tm,),
        in_specs=[
            pl.BlockSpec((tm, d), lambda i: (i, 0)),
            _full_spec((1, d)),
            _full_spec((d, 2 * width)),
            _full_spec((1, width)),
            _full_spec((A_GROUPS, A_CHUNK, A_CHUNK)),
            _full_spec((A_CHUNK, width)),
            _full_spec((width, d)),
        ],
        out_specs=pl.BlockSpec((tm, d), lambda i: (i, 0)),
        scratch_shapes=[pltpu.VMEM((tm, width), BF16)],
        compiler_params=_cparams("parallel"),
        name="mixer_a",
    )(h2, g.reshape(1, d), w_in.astype(BF16), vnorm_g.reshape(1, width), w_s.astype(BF16), bias,
      w_out.astype(BF16))


def _norm_proj_kernel(h_ref, g_ref, w_ref, hg_ref, bd_ref, o_ref, *, n_norm_cols):
    xn = _rms(h_ref[...], g_ref[...]).astype(BF16)
    acc = _dot(xn, w_ref[...])
    for j in range(n_norm_cols // MXU_DIM):
        cols = slice(j * MXU_DIM, (j + 1) * MXU_DIM)
        blk = acc[:, cols]
        ss = _dot((blk * blk).astype(BF16), bd_ref[...])
        o_ref[:, cols] = (blk * lax.rsqrt(ss * (1.0 / HEAD_DIM) + RMS_EPS) * hg_ref[:, cols]).astype(BF16)
    o_ref[:, n_norm_cols:] = acc[:, n_norm_cols:].astype(BF16)


def _norm_proj(h2, g, w, head_gain, n_norm_cols, tm=512):
    t, d = h2.shape
    n = w.shape[1]
    blockdiag = jnp.asarray(np.kron(np.eye(MXU_DIM // HEAD_DIM), np.ones((HEAD_DIM, HEAD_DIM))), BF16)
    return pl.pallas_call(
        functools.partial(_norm_proj_kernel, n_norm_cols=n_norm_cols),
        out_shape=jax.ShapeDtypeStruct((t, n), BF16),
        grid=(t // tm,),
        in_specs=[
            pl.BlockSpec((tm, d), lambda i: (i, 0)),
            _full_spec((1, d)),
            _full_spec((d, n)),
            _full_spec((1, n_norm_cols)),
            _full_spec((MXU_DIM, MXU_DIM)),
        ],
        out_specs=pl.BlockSpec((tm, n), lambda i: (i, 0)),
        compiler_params=_cparams("parallel"),
        name="norm_proj",
    )(h2, g.reshape(1, d), w.astype(BF16), head_gain.reshape(1, n_norm_cols).astype(F32), blockdiag)


def _alibi_slopes(n):
    return np.array([2.0 ** (-8.0 * (h + 1) / n) for h in range(n)], dtype=np.float32)


def _attn_b_tables():
    span = 3 * B_BLOCK
    rel = np.arange(span)[None, :] - B_BLOCK - np.arange(B_BLOCK)[:, None]
    in_window = np.abs(rel) <= B_BLOCK
    alibi = (-_alibi_slopes(B_HEADS)[:, None, None] * np.abs(rel)[None]).astype(np.float32)
    kblk = np.arange(span) // B_BLOCK
    tabs = []
    for kind in range(3):
        valid = in_window & ~((kind == 0) & (kblk == 0))[None, :] & ~((kind == 2) & (kblk == 2))[None, :]
        tabs.append(np.where(valid[None], alibi, np.float32(NEG)))
    return np.stack(tabs).astype(np.float32)


def _attn_b_kernel(sink_ref, q_ref, kp_ref, kc_ref, kn_ref, vp_ref, vc_ref, vn_ref, tab_ref, h_ref, wout_ref, o_ref,
                   mix_ref):
    grp = B_HEADS // B_KV_HEADS
    nq = q_ref.shape[1]
    lane = lax.broadcasted_iota(jnp.int32, (1, LANES), 1)
    low = lane < HEAD_DIM
    span = kp_ref.shape[1] + kc_ref.shape[1] + kn_ref.shape[1]
    ones = jnp.ones((span, LANES), BF16)
    heads = [slice(h * nq, (h + 1) * nq) for h in range(B_HEADS)]

    def all_scores(i):
        scores = []
        for kh in range(B_KV_HEADS):
            tile = slice(kh * LANES, (kh + 1) * LANES)
            kt = jnp.concatenate([kp_ref[i, :, tile], kc_ref[i, :, tile], kn_ref[i, :, tile]], axis=0)
            rows = []
            for qt in range(kh * grp // 2, (kh + 1) * grp // 2):
                q2 = q_ref[i, :, qt * LANES:(qt + 1) * LANES]
                zero = jnp.zeros_like(q2)
                rows += [jnp.where(low, q2, zero), jnp.where(low, zero, q2)]
            scores.append(_dot_nt(jnp.concatenate(rows, axis=0), kt) + tab_ref[kh])
        return jnp.concatenate(scores, axis=0)

    def softmax_numerators(sh):
        rowmax = jnp.broadcast_to(jnp.max(sh, axis=-1, keepdims=True), (sh.shape[0], LANES))
        m = jnp.concatenate([jnp.maximum(rowmax[hs], sink_ref[h]) for h, hs in enumerate(heads)], axis=0)
        sink_term = jnp.concatenate([jnp.exp(sink_ref[h] - m[hs]) for h, hs in enumerate(heads)], axis=0)
        return jnp.exp(sh - jnp.concatenate([m] * (span // LANES), axis=1)).astype(BF16), sink_term

    def outputs(i, pe, sink_term):
        seq_rows = pl.ds(pl.multiple_of(i * nq, nq), nq)
        for kh in range(B_KV_HEADS):
            tile = slice(kh * LANES, (kh + 1) * LANES)
            grows = slice(kh * grp * nq, (kh + 1) * grp * nq)
            vt = jnp.concatenate([vp_ref[i, :, tile], vc_ref[i, :, tile], vn_ref[i, :, tile]], axis=0)
            od = _dot(pe[grows], jnp.concatenate([vt, ones], axis=1))
            o = od[:, :LANES] / (od[:, LANES:] + sink_term[grows])
            for j in range(grp // 2):
                qt = kh * grp // 2 + j
                even, odd = o[(2 * j) * nq:(2 * j + 1) * nq], o[(2 * j + 1) * nq:(2 * j + 2) * nq]
                mix_ref[seq_rows, qt * LANES:(qt + 1) * LANES] = jnp.where(low, even, odd).astype(BF16)

    def two_sequences(j, carry):
        first, second = 2 * j, 2 * j + 1
        sh_a = all_scores(first)
        sh_b = all_scores(second)
        pe_a, sink_a = softmax_numerators(sh_a)
        pe_b, sink_b = softmax_numerators(sh_b)
        outputs(first, pe_a, sink_a)
        outputs(second, pe_b, sink_b)
        return carry

    n_seq = q_ref.shape[0]
    lax.fori_loop(0, n_seq // 2, two_sequences, 0)
    proj = _dot(mix_ref[...], wout_ref[...])
    o_ref[...] = h_ref[...] + proj.reshape(n_seq, nq, proj.shape[1])


def _dup_heads(w, n_heads):
    d = w.shape[0]
    return jnp.repeat(w.reshape(d, n_heads, 1, HEAD_DIM), 2, axis=2).reshape(d, n_heads * LANES)


def _attn_b(qkv3, sink, h3, w_out, seqs_per_step=4):
    b, s, d = h3.shape
    nb = s // B_BLOCK
    ns = seqs_per_step
    grp = B_HEADS // B_KV_HEADS
    qw = B_HEADS * HEAD_DIM
    kvw = B_KV_HEADS * LANES
    kcol = qw // kvw
    vcol = kcol + 1
    tabs = jnp.asarray(_attn_b_tables().reshape(3, B_KV_HEADS, grp * B_BLOCK, 3 * B_BLOCK))
    prev = lambda n: jnp.maximum(n - 1, 0)
    nxt = lambda n: jnp.minimum(n + 1, nb - 1)
    kind = lambda n: jnp.where(n == 0, 0, jnp.where(n == nb - 1, 2, 1))
    kv = lambda col, f: pl.BlockSpec((ns, B_BLOCK, kvw), lambda bi, n: (bi, f(n), col))
    same = lambda n: n
    return pl.pallas_call(
        _attn_b_kernel,
        out_shape=jax.ShapeDtypeStruct((b, s, d), F32),
        grid=(b // ns, nb),
        in_specs=[
            pl.BlockSpec(memory_space=pltpu.SMEM),
            pl.BlockSpec((ns, B_BLOCK, qw), lambda bi, n: (bi, n, 0)),
            kv(kcol, prev), kv(kcol, same), kv(kcol, nxt),
            kv(vcol, prev), kv(vcol, same), kv(vcol, nxt),
            pl.BlockSpec((None, B_KV_HEADS, grp * B_BLOCK, 3 * B_BLOCK), lambda bi, n: (kind(n), 0, 0, 0)),
            pl.BlockSpec((ns, B_BLOCK, d), lambda bi, n: (bi, n, 0)),
            _full_spec((qw, d)),
        ],
        out_specs=pl.BlockSpec((ns, B_BLOCK, d), lambda bi, n: (bi, n, 0)),
        scratch_shapes=[pltpu.VMEM((ns * B_BLOCK, qw), BF16)],
        compiler_params=_cparams("parallel", "arbitrary"),
        name="attn_b",
    )(sink.astype(F32), qkv3, qkv3, qkv3, qkv3, qkv3, qkv3, qkv3, tabs, h3, w_out.astype(BF16))


def _attn_c_table(rpb, rows):
    w = GRID_W
    kr_n = min(C_WIN_ROWS, rows)
    c = np.arange(w)
    cs = np.clip(c - C_WIN_COLS // 2, 0, w - C_WIN_COLS)
    kcol = np.arange(w)
    colmask = (kcol[None, :] >= cs[:, None]) & (kcol[None, :] < cs[:, None] + C_WIN_COLS)
    n_off = 2 * C_WIN_ROWS - kr_n
    n_h, n_ri, _ = rpb.shape
    pad = w - C_WIN_COLS
    padded = jnp.pad(rpb.astype(F32), ((0, 0), (0, 0), (pad, pad)))
    bias = jnp.stack([padded[:, :, w - 1 - qc:2 * w - 1 - qc] for qc in range(w)], axis=1)
    bias = jnp.where(jnp.asarray(colmask)[None, :, None, :], bias, NEG)
    bias = bias.reshape(n_h, w, n_ri * w)
    slabs = jnp.stack([bias[:, :, n * w:(n + kr_n) * w] for n in range(n_off)])
    return slabs.reshape(n_off, n_h // 2, 2 * w, kr_n * w)


def _attn_c_kernel(q_ref, k_ref, v_ref, tab_ref, h_ref, wout_ref, o_ref, mix_ref):
    nk = k_ref.shape[1] * k_ref.shape[2]
    nq = q_ref.shape[1]
    lane = lax.broadcasted_iota(jnp.int32, (1, LANES), 1)
    low = lane < HEAD_DIM
    ones = jnp.ones((nk, LANES), BF16)

    def all_scores(i):
        scores = []
        for t in range(C_HEADS // 2):
            cols = slice(t * LANES, (t + 1) * LANES)
            q2 = q_ref[i, :, cols]
            zero = jnp.zeros_like(q2)
            qq = jnp.concatenate([jnp.where(low, q2, zero), jnp.where(low, zero, q2)], axis=0)
            kt = k_ref[i, :, :, cols].reshape(nk, LANES)
            scores.append(_dot_nt(qq, kt) + tab_ref[t])
        return jnp.concatenate(scores, axis=0)

    def softmax_numerators(sh):
        m = jnp.broadcast_to(jnp.max(sh, axis=-1, keepdims=True), (sh.shape[0], LANES))
        return jnp.exp(sh - jnp.concatenate([m] * (nk // LANES), axis=1)).astype(BF16)

    def outputs(i, pe):
        for t in range(C_HEADS // 2):
            cols = slice(t * LANES, (t + 1) * LANES)
            rows = slice(2 * t * nq, 2 * (t + 1) * nq)
            vt = v_ref[i, :, :, cols].reshape(nk, LANES)
            od = _dot(pe[rows], jnp.concatenate([vt, ones], axis=1))
            o = od[:, :LANES] / od[:, LANES:]
            mix_ref[pl.ds(pl.multiple_of(i * nq, nq), nq), cols] = jnp.where(low, o[:nq], o[nq:]).astype(BF16)

    def two_sequences(j, carry):
        first, second = 2 * j, 2 * j + 1
        sh_a = all_scores(first)
        sh_b = all_scores(second)
        pe_a = softmax_numerators(sh_a)
        pe_b = softmax_numerators(sh_b)
        outputs(first, pe_a)
        outputs(second, pe_b)
        return carry

    n_seq = q_ref.shape[0]
    lax.fori_loop(0, n_seq // 2, two_sequences, 0)
    proj = _dot(mix_ref[...], wout_ref[...])
    o_ref[...] = h_ref[...] + proj.reshape(n_seq, nq, proj.shape[1])


def _attn_c(qkv4, rpb, h4, w_out, seqs_per_step=4):
    b, rows, w, d = h4.shape
    hw = C_HEADS * HEAD_DIM
    kr_n = min(C_WIN_ROWS, rows)
    nb = seqs_per_step
    table = _attn_c_table(rpb, rows)
    row_start = lambda r: jnp.clip(r - kr_n // 2, 0, rows - kr_n)
    el = pl.Element
    return pl.pallas_call(
        _attn_c_kernel,
        out_shape=jax.ShapeDtypeStruct((b, rows, w, d), F32),
        grid=(rows, b // nb),
        in_specs=[
            pl.BlockSpec((nb, None, w, hw), lambda r, bi: (bi, r, 0, 0)),
            pl.BlockSpec((el(nb), el(kr_n), el(w), el(hw)), lambda r, bi: (bi * nb, row_start(r), 0, hw)),
            pl.BlockSpec((el(nb), el(kr_n), el(w), el(hw)), lambda r, bi: (bi * nb, row_start(r), 0, 2 * hw)),
            pl.BlockSpec((None, C_HEADS // 2, 2 * w, kr_n * w),
                         lambda r, bi: (row_start(r) - r + C_WIN_ROWS - 1, 0, 0, 0)),
            pl.BlockSpec((nb, None, w, d), lambda r, bi: (bi, r, 0, 0)),
            _full_spec((hw, d)),
        ],
        out_specs=pl.BlockSpec((nb, None, w, d), lambda r, bi: (bi, r, 0, 0)),
        scratch_shapes=[pltpu.VMEM((nb * w, hw), BF16)],
        compiler_params=_cparams("arbitrary", "arbitrary"),
        name="attn_c",
    )(qkv4, qkv4, qkv4, table, h4, w_out.astype(BF16))


def _lane_cumsum(x, tri):
    e, s = x.shape
    off = jnp.zeros((e, 1), F32)
    outs = []
    for j in range(s // LANES):
        blk = x[:, j * LANES:(j + 1) * LANES]
        outs.append(_dot(blk.astype(BF16), tri) + off)
        off = off + jnp.sum(blk, axis=1, keepdims=True)
    return jnp.concatenate(outs, axis=1)


def _route_kernel(h_ref, g_ref, rwt_ref, tri_ref, xn_ref, slotr_ref, slotc_ref, gatec_ref, cnt_ref,
                  *, cap, chunk, tile):
    s = h_ref.shape[0]
    n_e = rwt_ref.shape[0] // 2
    logits = []
    for c in range(s // chunk):
        rows = slice(c * chunk, (c + 1) * chunk)
        xn = _rms(h_ref[rows, :], g_ref[...])
        xn_hi = xn.astype(BF16)
        xn_ref[rows, :] = xn_hi
        xn_lo = (xn - xn_hi.astype(F32)).astype(BF16)
        both = _dot_nt(rwt_ref[...], xn_hi)
        logits.append(both[:n_e] + both[n_e:] + _dot_nt(rwt_ref[:n_e, :], xn_lo))
    lg = jnp.concatenate(logits, axis=1)
    ex = jnp.exp(lg - jnp.max(lg, axis=0, keepdims=True))
    aff = ex / jnp.sum(ex, axis=0, keepdims=True)
    bits = pltpu.bitcast(aff, jnp.int32)
    thr = jnp.zeros((n_e, 1), jnp.int32)
    for shift in range(27, -1, -3):
        digit = jnp.zeros((n_e, 1), jnp.int32)
        for j in range(1, 8):
            cnt = jnp.sum(jnp.where(bits >= (thr | (j << shift)), 1.0, 0.0), axis=1, keepdims=True)
            digit = digit + jnp.where(cnt >= cap, 1, 0)
        thr = thr | (digit * (1 << shift))
    gt = bits > thr
    eq = bits == thr
    need = cap - jnp.sum(jnp.where(gt, 1.0, 0.0), axis=1, keepdims=True)
    eqf = jnp.where(eq, 1.0, 0.0)
    eq_rank = _lane_cumsum(eqf, tri_ref[...]) - eqf
    sel = gt | (eq & (eq_rank < need))
    self_ = jnp.where(sel, 1.0, 0.0)
    slot = jnp.where(sel, _lane_cumsum(self_, tri_ref[...]) - self_, -1.0)
    lane = lax.broadcasted_iota(jnp.int32, (n_e, LANES), 1)
    counts = jnp.zeros((n_e, LANES), F32)
    for i in range(s // tile):
        counts = jnp.where(lane == i, jnp.sum(self_[:, i * tile:(i + 1) * tile], axis=1, keepdims=True), counts)
    cnt_ref[...] = counts.astype(jnp.int32)
    pad_rows = LANES - n_e
    slot_p = jnp.concatenate([slot, jnp.full((pad_rows, s), -1.0, F32)], axis=0)
    gate_p = jnp.concatenate([jnp.where(sel, aff, 0.0), jnp.zeros((pad_rows, s), F32)], axis=0)
    slotr_ref[...] = slot
    for j in range(s // LANES):
        cols = slice(j * LANES, (j + 1) * LANES)
        slotc_ref[cols, :] = slot_p[:, cols].T.astype(BF16)
        gatec_ref[cols, :] = gate_p[:, cols].T.astype(BF16)


def _route(h3, g, router_w, cap):
    b, s, d = h3.shape
    n_e = router_w.shape[1]
    tri = jnp.asarray(np.triu(np.ones((LANES, LANES))), BF16)
    rw_t = router_w.T.astype(F32)
    rw_hi = rw_t.astype(BF16)
    rw_lo = (rw_t - rw_hi.astype(F32)).astype(BF16)
    per_seq = lambda *shape: pl.BlockSpec((None,) + shape, lambda i: (i,) + (0,) * len(shape))
    return pl.pallas_call(
        functools.partial(_route_kernel, cap=cap, chunk=256, tile=TOKEN_TILE),
        out_shape=(
            jax.ShapeDtypeStruct((b, s, d), BF16),
            jax.ShapeDtypeStruct((b, n_e, s), F32),
            jax.ShapeDtypeStruct((b, s, LANES), BF16),
            jax.ShapeDtypeStruct((b, s, LANES), BF16),
            jax.ShapeDtypeStruct((b, n_e, LANES), jnp.int32),
        ),
        grid=(b,),
        in_specs=[per_seq(s, d), _full_spec((1, d)), _full_spec((2 * n_e, d)), _full_spec((LANES, LANES))],
        out_specs=(per_seq(s, d), per_seq(n_e, s), per_seq(s, LANES), per_seq(s, LANES), per_seq(n_e, LANES)),
        compiler_params=_cparams("parallel"),
        name="route",
    )(h3, g.reshape(1, d), jnp.concatenate([rw_hi, rw_lo], axis=0), tri)


def _chunk_tables(cnt, cap):
    n_e = cnt.shape[1]
    kmax = n_e * (TOKEN_TILE // CHUNK + 1)
    kmax = -(-kmax // CHUNKS_PER_GROUP) * CHUNKS_PER_GROUP
    c0 = jnp.cumsum(cnt, axis=2) - cnt
    a0 = c0 // CHUNK
    a1 = jnp.where(cnt > 0, (c0 + cnt + CHUNK - 1) // CHUNK, a0)
    nch = (a1 - a0).transpose(0, 2, 1)
    a0 = a0.transpose(0, 2, 1)
    pos_end = jnp.cumsum(nch, axis=2)
    pos = pos_end - nch
    total = pos_end[..., -1]
    k = jnp.arange(kmax, dtype=jnp.int32)
    e_of_k = jnp.sum((k[None, None, :, None] >= pos_end[:, :, None, :]).astype(jnp.int32), axis=-1)
    e_of_k = jnp.minimum(e_of_k, n_e - 1)
    is_e = e_of_k[..., None] == jnp.arange(n_e, dtype=jnp.int32)
    q = k + jnp.sum(jnp.where(is_e, (a0 - pos)[:, :, None, :], 0), axis=-1)
    valid = k[None, None, :] < total[..., None]
    e_tab = jnp.where(valid, e_of_k, 0)
    slot_tab = jnp.where(valid, q * CHUNK, -(1 << 20))
    dst_tab = jnp.where(valid, e_of_k * cap + q * CHUNK, 0)
    groups = (total + CHUNKS_PER_GROUP - 1) // CHUNKS_PER_GROUP
    return jnp.concatenate([e_tab, slot_tab, dst_tab, groups[..., None]], axis=-1).astype(jnp.int32), kmax


def _gather_kernel(tab_ref, xn_ref, slotr_ref, xs_ref, rows_ref, *, kmax):
    @pl.when(pl.program_id(1) == 0)
    def _():
        xs_ref[...] = jnp.zeros_like(xs_ref)

    sub = lax.broadcasted_iota(jnp.int32, (CHUNK, TOKEN_TILE), 0).astype(F32)
    for tile in range(TILES_PER_STEP):
        tokens = slice(tile * TOKEN_TILE, (tile + 1) * TOKEN_TILE)

        def group(g, carry, tile=tile, tokens=tokens):
            base = g * CHUNKS_PER_GROUP
            onehot = []
            for c in range(CHUNKS_PER_GROUP):
                slots = slotr_ref[pl.ds(tab_ref[tile, base + c], 1), tokens]
                want = sub + tab_ref[tile, kmax + base + c].astype(F32)
                onehot.append(jnp.where(slots == want, 1.0, 0.0).astype(BF16))
            rows_ref[...] = _dot(jnp.concatenate(onehot, axis=0), xn_ref[tokens, :]).astype(BF16)
            for c0 in range(0, CHUNKS_PER_GROUP, 4):
                cs = range(c0, c0 + 4)
                dsts = [pl.ds(pl.multiple_of(tab_ref[tile, 2 * kmax + base + c], CHUNK), CHUNK) for c in cs]
                sums = [xs_ref[dst, :] + rows_ref[c * CHUNK:(c + 1) * CHUNK, :] for c, dst in zip(cs, dsts)]
                for dst, total in reversed(list(zip(dsts, sums))):
                    xs_ref[dst, :] = total
            return carry

        lax.fori_loop(0, tab_ref[tile, 3 * kmax], group, 0)


def _expert_kernel(xs_ref, wg_ref, wu_ref, wd_ref, y_ref, wg_bf, wu_bf, wd_bf):
    @pl.when(pl.program_id(1) == 0)
    def _():
        wg_bf[...] = wg_ref[...].astype(BF16)
        wu_bf[...] = wu_ref[...].astype(BF16)
        wd_bf[...] = wd_ref[...].astype(BF16)

    nb, cap, d = xs_ref.shape
    xs = xs_ref[...].reshape(nb * cap, d)
    hg = _dot(xs, wg_bf[...])
    hu = _dot(xs, wu_bf[...])
    hdn = (hg * jax.nn.sigmoid(hg) * hu).astype(BF16)
    y_ref[...] = _dot(hdn, wd_bf[...]).astype(BF16).reshape(nb, cap, d)


def _scatter_ple_kernel(tab_ref, y_ref, slotc_ref, gatec_ref, h_ref, p_ref, g_ref, wg_ref, wp_ref, o_ref,
                        yg_ref, acc_ref, *, kmax):
    acc_ref[...] = jnp.zeros_like(acc_ref)
    lane = lax.broadcasted_iota(jnp.int32, (1, GROUP), 1)
    sub = lax.broadcasted_iota(jnp.int32, (LANES, GROUP), 0)
    for tile in range(TILES_PER_STEP):
        tokens = slice(tile * TOKEN_TILE, (tile + 1) * TOKEN_TILE)

        def group(g, carry, tile=tile, tokens=tokens):
            base = g * CHUNKS_PER_GROUP
            expert = jnp.zeros((1, GROUP), jnp.int32)
            want = jnp.zeros((1, GROUP), jnp.int32)
            for c in range(CHUNKS_PER_GROUP):
                in_chunk = (lane >= c * CHUNK) & (lane < (c + 1) * CHUNK)
                expert = jnp.where(in_chunk, tab_ref[tile, base + c], expert)
                want = jnp.where(in_chunk, tab_ref[tile, kmax + base + c] + lane - c * CHUNK, want)
                src = pl.ds(pl.multiple_of(tab_ref[tile, 2 * kmax + base + c], CHUNK), CHUNK)
                yg_ref[c * CHUNK:(c + 1) * CHUNK, :] = y_ref[src, :]
            pick = jnp.where(sub == expert, 1.0, 0.0).astype(BF16)
            both = _dot(jnp.concatenate([slotc_ref[tokens, :], gatec_ref[tokens, :]], axis=0), pick)
            weights = jnp.where(both[:TOKEN_TILE] == want.astype(F32), both[TOKEN_TILE:], 0.0).astype(BF16)
            acc_ref[tokens, :] += _dot(weights, yg_ref[...])
            return carry

        lax.fori_loop(0, tab_ref[tile, 3 * kmax], group, 0)
    h = h_ref[...] + acc_ref[...]
    xn = _rms(h, g_ref[...]).astype(BF16)
    gate = jax.nn.sigmoid(_dot(xn, wg_ref[...]))
    o_ref[...] = h + gate * _dot(p_ref[...].astype(BF16), wp_ref[...])


def _expert_ffn_ple(h3, xn3, slot_row, slot_col, gate_col, cnt, w_gate, w_up, w_down, cap, p3, layer, ple_g, ple_wg,
                    ple_wp, seqs_per_step=4):
    b, s, d = xn3.shape
    pd = p3.shape[2]
    _, n_e, _, ff = w_gate.shape
    step_rows = TILES_PER_STEP * TOKEN_TILE
    nt = s // step_rows
    tables, kmax = _chunk_tables(cnt[:, :, :s // TOKEN_TILE], cap)
    tables = tables.reshape(b * nt, TILES_PER_STEP, 3 * kmax + 1)
    tab_spec = pl.BlockSpec((None, TILES_PER_STEP, 3 * kmax + 1), lambda i, t: (i * nt + t, 0, 0),
                            memory_space=pltpu.SMEM)
    xs = pl.pallas_call(
        functools.partial(_gather_kernel, kmax=kmax),
        out_shape=jax.ShapeDtypeStruct((b, n_e * cap, d), BF16),
        grid=(b, nt),
        in_specs=[
            tab_spec,
            pl.BlockSpec((None, step_rows, d), lambda i, t: (i, t, 0)),
            pl.BlockSpec((None, n_e, step_rows), lambda i, t: (i, 0, t)),
        ],
        out_specs=pl.BlockSpec((None, n_e * cap, d), lambda i, t: (i, 0, 0)),
        scratch_shapes=[pltpu.VMEM((GROUP, d), BF16)],
        compiler_params=_cparams("parallel", "arbitrary"),
        name="expert_gather",
    )(tables, xn3, slot_row)
    nb = seqs_per_step
    y = pl.pallas_call(
        _expert_kernel,
        out_shape=jax.ShapeDtypeStruct((b, n_e, cap, d), BF16),
        grid=(n_e, b // nb),
        in_specs=[
            pl.BlockSpec((nb, None, cap, d), lambda e, i: (i, e, 0, 0)),
            pl.BlockSpec((None, None, d, ff), lambda e, i: (layer, e, 0, 0)),
            pl.BlockSpec((None, None, d, ff), lambda e, i: (layer, e, 0, 0)),
            pl.BlockSpec((None, None, ff, d), lambda e, i: (layer, e, 0, 0)),
        ],
        out_specs=pl.BlockSpec((nb, None, cap, d), lambda e, i: (i, e, 0, 0)),
        scratch_shapes=[pltpu.VMEM((d, ff), BF16), pltpu.VMEM((d, ff), BF16), pltpu.VMEM((ff, d), BF16)],
        compiler_params=_cparams("parallel", "arbitrary"),
        name="expert_mlp",
    )(xs.reshape(b, n_e, cap, d), w_gate, w_up, w_down)
    tile_rows = lambda width: pl.BlockSpec((None, step_rows, width), lambda i, t: (i, t, 0))
    return pl.pallas_call(
        functools.partial(_scatter_ple_kernel, kmax=kmax),
        out_shape=jax.ShapeDtypeStruct((b, s, d), F32),
        grid=(b, nt),
        in_specs=[
            tab_spec,
            pl.BlockSpec((None, n_e * cap, d), lambda i, t: (i, 0, 0)),
            tile_rows(LANES), tile_rows(LANES), tile_rows(d),
            pl.BlockSpec((None, step_rows, pd), lambda i, t: (layer, i * nt + t, 0)),
            _full_spec((1, d)), _full_spec((d, d)), _full_spec((pd, d)),
        ],
        out_specs=tile_rows(d),
        scratch_shapes=[pltpu.VMEM((GROUP, d), BF16), pltpu.VMEM((step_rows, d), F32)],
        compiler_params=_cparams("parallel", "arbitrary"),
        name="expert_scatter_ple",
    )(tables, y.reshape(b, n_e * cap, d), slot_col, gate_col, h3, p3, ple_g.reshape(1, d), ple_wg.astype(BF16),
      ple_wp.astype(BF16))


def kernel(x, p, norm_mix_g, norm_ffn_g, w_out, router_w, exp_w_gate, exp_w_up, exp_w_down, ple_norm_g, ple_gate_w, ple_proj_w, a_w_in, a_vnorm_g, a_w_s, a_b_s, b_w_in, b_qnorm_g, b_knorm_g, b_sink, c_w_in, c_qnorm_g, c_knorm_g, c_rpb):
    b, s, d = x.shape
    depth = norm_mix_g.shape[0]
    t = b * s
    cap = max(1, EC_CAPACITY_FACTOR * s // N_EXPERTS)
    scale = HEAD_DIM ** -0.5
    h = x.reshape(t, d)
    for i in range(depth):
        kind = i % N_MIXERS
        j = i // N_MIXERS
        if kind == 0:
            h = _mixer_a(h, norm_mix_g[i], a_w_in[j], a_vnorm_g[j], a_w_s[j], a_b_s[j], w_out[i])
        elif kind == 1:
            gain = jnp.concatenate([jnp.tile(b_qnorm_g[j] * scale, B_HEADS), jnp.tile(b_knorm_g[j], 2 * B_KV_HEADS)])
            qw, kw = B_HEADS * HEAD_DIM, B_KV_HEADS * HEAD_DIM
            w_b = jnp.concatenate([b_w_in[j][:, :qw], _dup_heads(b_w_in[j][:, qw:qw + kw], B_KV_HEADS),
                                   _dup_heads(b_w_in[j][:, qw + kw:], B_KV_HEADS)], axis=1)
            qkv = _norm_proj(h, norm_mix_g[i], w_b, gain, qw + 2 * kw)
            h = _attn_b(qkv.reshape(b, s, -1), b_sink[j], h.reshape(b, s, d), w_out[i]).reshape(t, d)
        else:
            gain = jnp.concatenate([jnp.tile(c_qnorm_g[j] * scale, C_HEADS), jnp.tile(c_knorm_g[j], C_HEADS)])
            qkv = _norm_proj(h, norm_mix_g[i], c_w_in[j], gain, 2 * C_HEADS * HEAD_DIM)
            h = _attn_c(qkv.reshape(b, s // GRID_W, GRID_W, -1), c_rpb[j], h.reshape(b, s // GRID_W, GRID_W, d),
                        w_out[i]).reshape(t, d)
        xn, slot_row, slot_col, gate_col, cnt = _route(h.reshape(b, s, d), norm_ffn_g[i], router_w[i], cap)
        h = _expert_ffn_ple(h.reshape(b, s, d), xn, slot_row, slot_col, gate_col, cnt, exp_w_gate, exp_w_up,
                            exp_w_down, cap, p.reshape(depth, t, -1), i, ple_norm_g[i], ple_gate_w[i],
                            ple_proj_w[i]).reshape(t, d)
    return h.reshape(b, s, d)
```

```python
import functools

import numpy as np
import jax
import jax.numpy as jnp
from jax import lax
from jax.experimental import pallas as pl
from jax.experimental.pallas import tpu as pltpu

F32 = jnp.float32
BF16 = jnp.bfloat16

RMS_EPS = 1e-6
LN_EPS = 1e-5
NEG = -1e30
HEAD_DIM = 64
GRID_W = 64
A_GROUPS = 8
A_CHUNK = 128
B_HEADS = 16
B_KV_HEADS = 4
B_BLOCK = 128
C_HEADS = 16
C_WIN_ROWS = 8
C_WIN_COLS = 16
N_EXPERTS = 16
EC_CAPACITY_FACTOR = 2
N_MIXERS = 3

LANES = 128
MXU_DIM = 256
VMEM_LIMIT_BYTES = 60 * 1024 * 1024

TOKEN_TILE = 256
TILES_PER_STEP = 4
A_SUBTILES = 2
CHUNK = 16
GROUP = 4 * MXU_DIM
CHUNKS_PER_GROUP = GROUP // CHUNK


def _cparams(*sem):
    return pltpu.CompilerParams(dimension_semantics=sem, vmem_limit_bytes=VMEM_LIMIT_BYTES)


def _rms(xf, g):
    return xf * lax.rsqrt(jnp.mean(xf * xf, axis=-1, keepdims=True) + RMS_EPS) * g


def _dot(a, b):
    return jnp.dot(a, b, preferred_element_type=F32)


def _dot_nt(a, b):
    return lax.dot_general(a, b, (((1,), (1,)), ((), ())), preferred_element_type=F32)


def _full_spec(shape):
    nd = len(shape)
    return pl.BlockSpec(shape, lambda *_: (0,) * nd)


def _mixer_a_kernel(h_ref, g_ref, win_ref, vg_ref, ws_ref, bias_ref, wout_ref, o_ref, mix_ref):
    tm = h_ref.shape[0]
    width = vg_ref.shape[1]
    gw = width // A_GROUPS
    sub = tm // A_SUBTILES
    halves = [slice(i * sub, (i + 1) * sub) for i in range(A_SUBTILES)]
    zs = [_dot(_rms(h_ref[hs, :], g_ref[...]).astype(BF16), win_ref[...]) for hs in halves]
    us, vns = [], []
    for z in zs:
        z = 0.5 * z * (1.0 + jnp.tanh(np.sqrt(2.0 / np.pi).astype(np.float32) * (z + 0.044715 * (z * z * z))))
        v = z[:, width:]
        mu = jnp.mean(v, axis=-1, keepdims=True)
        vc = v - mu
        var = jnp.mean(vc * vc, axis=-1, keepdims=True)
        us.append(z[:, :width])
        vns.append((vc * lax.rsqrt(var + LN_EPS) * vg_ref[...]).astype(BF16))
    for hs, u, vn in zip(halves, us, vns):
        for c in range(sub // A_CHUNK):
            rows = slice(c * A_CHUNK, (c + 1) * A_CHUNK)
            out_rows = slice(hs.start + c * A_CHUNK, hs.start + (c + 1) * A_CHUNK)
            for g in range(A_GROUPS):
                cols = slice(g * gw, (g + 1) * gw)
                s = _dot(ws_ref[g], vn[rows, cols]) + bias_ref[:, cols]
                mix_ref[out_rows, cols] = (u[rows, cols] * s).astype(BF16)
    o_ref[...] = h_ref[...] + _dot(mix_ref[...], wout_ref[...])


def _mixer_a(h2, g, w_in, vnorm_g, w_s, b_s, w_out, tm=512):
    t, d = h2.shape
    width = vnorm_g.shape[0]
    gw = width // A_GROUPS
    bias = jnp.repeat(b_s.T.astype(F32), gw, axis=1)
    return pl.pallas_call(
        _mixer_a_kernel,
        out_shape=jax.ShapeDtypeStruct((t, d), F32),
        grid=(t // tm,),
        in_specs=[
            pl.BlockSpec((tm, d), lambda i: (i, 0)),
            _full_spec((1, d)),
            _full_spec((d, 2 * width)),
            _full_spec((1, width)),
            _full_spec((A_GROUPS, A_CHUNK, A_CHUNK)),
            _full_spec((A_CHUNK, width)),
            _full_spec((width, d)),
        ],
        out_specs=pl.BlockSpec((tm, d), lambda i: (i, 0)),
        scratch_shapes=[pltpu.VMEM((tm, width), BF16)],
        compiler_params=_cparams("parallel"),
        name="mixer_a",
    )(h2, g.reshape(1, d), w_in.astype(BF16), vnorm_g.reshape(1, width), w_s.astype(BF16), bias,
      w_out.astype(BF16))


def _norm_proj_kernel(h_ref, g_ref, w_ref, hg_ref, bd_ref, o_ref, *, n_norm_cols):
    xn = _rms(h_ref[...], g_ref[...]).astype(BF16)
    acc = _dot(xn, w_ref[...])
    for j in range(n_norm_cols // MXU_DIM):
        cols = slice(j * MXU_DIM, (j + 1) * MXU_DIM)
        blk = acc[:, cols]
        ss = _dot((blk * blk).astype(BF16), bd_ref[...])
        o_ref[:, cols] = (blk * lax.rsqrt(ss * (1.0 / HEAD_DIM) + RMS_EPS) * hg_ref[:, cols]).astype(BF16)
    o_ref[:, n_norm_cols:] = acc[:, n_norm_cols:].astype(BF16)


def _norm_proj(h2, g, w, head_gain, n_norm_cols, tm=512):
    t, d = h2.shape
    n = w.shape[1]
    blockdiag = jnp.asarray(np.kron(np.eye(MXU_DIM // HEAD_DIM), np.ones((HEAD_DIM, HEAD_DIM))), BF16)
    return pl.pallas_call(
        functools.partial(_norm_proj_kernel, n_norm_cols=n_norm_cols),
        out_shape=jax.ShapeDtypeStruct((t, n), BF16),
        grid=(t // tm,),
        in_specs=[
            pl.BlockSpec((tm, d), lambda i: (i, 0)),
            _full_spec((1, d)),
            _full_spec((d, n)),
            _full_spec((1, n_norm_cols)),
            _full_spec((MXU_DIM, MXU_DIM)),
        ],
        out_specs=pl.BlockSpec((tm, n), lambda i: (i, 0)),
        compiler_params=_cparams("parallel"),
        name="norm_proj",
    )(h2, g.reshape(1, d), w.astype(BF16), head_gain.reshape(1, n_norm_cols).astype(F32), blockdiag)


def _alibi_slopes(n):
    return np.array([2.0 ** (-8.0 * (h + 1) / n) for h in range(n)], dtype=np.float32)


def _attn_b_tables():
    span = 3 * B_BLOCK
    rel = np.arange(span)[None, :] - B_BLOCK - np.arange(B_BLOCK)[:, None]
    in_window = np.abs(rel) <= B_BLOCK
    alibi = (-_alibi_slopes(B_HEADS)[:, None, None] * np.abs(rel)[None]).astype(np.float32)
    kblk = np.arange(span) // B_BLOCK
    tabs = []
    for kind in range(3):
        valid = in_window & ~((kind == 0) & (kblk == 0))[None, :] & ~((kind == 2) & (kblk == 2))[None, :]
        tabs.append(np.where(valid[None], alibi, np.float32(NEG)))
    return np.stack(tabs).astype(np.float32)


def _attn_b_kernel(sink_ref, q_ref, kp_ref, kc_ref, kn_ref, vp_ref, vc_ref, vn_ref, tab_ref, h_ref, wout_ref, o_ref,
                   mix_ref):
    grp = B_HEADS // B_KV_HEADS
    nq = q_ref.shape[1]
    lane = lax.broadcasted_iota(jnp.int32, (1, LANES), 1)
    low = lane < HEAD_DIM
    span = kp_ref.shape[1] + kc_ref.shape[1] + kn_ref.shape[1]
    ones = jnp.ones((span, LANES), BF16)
    heads = [slice(h * nq, (h + 1) * nq) for h in range(B_HEADS)]

    def all_scores(i):
        scores = []
        for kh in range(B_KV_HEADS):
            tile = slice(kh * LANES, (kh + 1) * LANES)
            kt = jnp.concatenate([kp_ref[i, :, tile], kc_ref[i, :, tile], kn_ref[i, :, tile]], axis=0)
            rows = []
            for qt in range(kh * grp // 2, (kh + 1) * grp // 2):
                q2 = q_ref[i, :, qt * LANES:(qt + 1) * LANES]
                zero = jnp.zeros_like(q2)
                rows += [jnp.where(low, q2, zero), jnp.where(low, zero, q2)]
            scores.append(_dot_nt(jnp.concatenate(rows, axis=0), kt) + tab_ref[kh])
        return jnp.concatenate(scores, axis=0)

    def softmax_numerators(sh):
        rowmax = jnp.broadcast_to(jnp.max(sh, axis=-1, keepdims=True), (sh.shape[0], LANES))
        m = jnp.concatenate([jnp.maximum(rowmax[hs], sink_ref[h]) for h, hs in enumerate(heads)], axis=0)
        sink_term = jnp.concatenate([jnp.exp(sink_ref[h] - m[hs]) for h, hs in enumerate(heads)], axis=0)
        return jnp.exp(sh - jnp.concatenate([m] * (span // LANES), axis=1)).astype(BF16), sink_term

    def outputs(i, pe, sink_term):
        seq_rows = pl.ds(pl.multiple_of(i * nq, nq), nq)
        for kh in range(B_KV_HEADS):
            tile = slice(kh * LANES, (kh + 1) * LANES)
            grows = slice(kh * grp * nq, (kh + 1) * grp * nq)
            vt = jnp.concatenate([vp_ref[i, :, tile], vc_ref[i, :, tile], vn_ref[i, :, tile]], axis=0)
            od = _dot(pe[grows], jnp.concatenate([vt, ones], axis=1))
            o = od[:, :LANES] / (od[:, LANES:] + sink_term[grows])
            for j in range(grp // 2):
                qt = kh * grp // 2 + j
                even, odd = o[(2 * j) * nq:(2 * j + 1) * nq], o[(2 * j + 1) * nq:(2 * j + 2) * nq]
                mix_ref[seq_rows, qt * LANES:(qt + 1) * LANES] = jnp.where(low, even, odd).astype(BF16)

    def two_sequences(j, carry):
        first, second = 2 * j, 2 * j + 1
        sh_a = all_scores(first)
        sh_b = all_scores(second)
        pe_a, sink_a = softmax_numerators(sh_a)
        pe_b, sink_b = softmax_numerators(sh_b)
        outputs(first, pe_a, sink_a)
        outputs(second, pe_b, sink_b)
        return carry

    n_seq = q_ref.shape[0]
    lax.fori_loop(0, n_seq // 2, two_sequences, 0)
    proj = _dot(mix_ref[...], wout_ref[...])
    o_ref[...] = h_ref[...] + proj.reshape(n_seq, nq, proj.shape[1])


def _dup_heads(w, n_heads):
    d = w.shape[0]
    return jnp.repeat(w.reshape(d, n_heads, 1, HEAD_DIM), 2, axis=2).reshape(d, n_heads * LANES)


def _attn_b(qkv3, sink, h3, w_out, seqs_per_step=4):
    b, s, d = h3.shape
    nb = s // B_BLOCK
    ns = seqs_per_step
    grp = B_HEADS // B_KV_HEADS
    qw = B_HEADS * HEAD_DIM
    kvw = B_KV_HEADS * LANES
    kcol = qw // kvw
    vcol = kcol + 1
    tabs = jnp.asarray(_attn_b_tables().reshape(3, B_KV_HEADS, grp * B_BLOCK, 3 * B_BLOCK))
    prev = lambda n: jnp.maximum(n - 1, 0)
    nxt = lambda n: jnp.minimum(n + 1, nb - 1)
    kind = lambda n: jnp.where(n == 0, 0, jnp.where(n == nb - 1, 2, 1))
    kv = lambda col, f: pl.BlockSpec((ns, B_BLOCK, kvw), lambda bi, n: (bi, f(n), col))
    same = lambda n: n
    return pl.pallas_call(
        _attn_b_kernel,
        out_shape=jax.ShapeDtypeStruct((b, s, d), F32),
        grid=(b // ns, nb),
        in_specs=[
            pl.BlockSpec(memory_space=pltpu.SMEM),
            pl.BlockSpec((ns, B_BLOCK, qw), lambda bi, n: (bi, n, 0)),
            kv(kcol, prev), kv(kcol, same), kv(kcol, nxt),
            kv(vcol, prev), kv(vcol, same), kv(vcol, nxt),
            pl.BlockSpec((None, B_KV_HEADS, grp * B_BLOCK, 3 * B_BLOCK), lambda bi, n: (kind(n), 0, 0, 0)),
            pl.BlockSpec((ns, B_BLOCK, d), lambda bi, n: (bi, n, 0)),
            _full_spec((qw, d)),
        ],
        out_specs=pl.BlockSpec((ns, B_BLOCK, d), lambda bi, n: (bi, n, 0)),
        scratch_shapes=[pltpu.VMEM((ns * B_BLOCK, qw), BF16)],
        compiler_params=_cparams("parallel", "arbitrary"),
        name="attn_b",
    )(sink.astype(F32), qkv3, qkv3, qkv3, qkv3, qkv3, qkv3, qkv3, tabs, h3, w_out.astype(BF16))


def _attn_c_table(rpb, rows):
    w = GRID_W
    kr_n = min(C_WIN_ROWS, rows)
    c = np.arange(w)
    cs = np.clip(c - C_WIN_COLS // 2, 0, w - C_WIN_COLS)
    kcol = np.arange(w)
    colmask = (kcol[None, :] >= cs[:, None]) & (kcol[None, :] < cs[:, None] + C_WIN_COLS)
    n_off = 2 * C_WIN_ROWS - kr_n
    n_h, n_ri, _ = rpb.shape
    pad = w - C_WIN_COLS
    padded = jnp.pad(rpb.astype(F32), ((0, 0), (0, 0), (pad, pad)))
    bias = jnp.stack([padded[:, :, w - 1 - qc:2 * w - 1 - qc] for qc in range(w)], axis=1)
    bias = jnp.where(jnp.asarray(colmask)[None, :, None, :], bias, NEG)
    bias = bias.reshape(n_h, w, n_ri * w)
    slabs = jnp.stack([bias[:, :, n * w:(n + kr_n) * w] for n in range(n_off)])
    return slabs.reshape(n_off, n_h // 2, 2 * w, kr_n * w)


def _attn_c_kernel(q_ref, k_ref, v_ref, tab_ref, h_ref, wout_ref, o_ref, mix_ref):
    nk = k_ref.shape[1] * k_ref.shape[2]
    nq = q_ref.shape[1]
    lane = lax.broadcasted_iota(jnp.int32, (1, LANES), 1)
    low = lane < HEAD_DIM
    ones = jnp.ones((nk, LANES), BF16)

    def all_scores(i):
        scores = []
        for t in range(C_HEADS // 2):
            cols = slice(t * LANES, (t + 1) * LANES)
            q2 = q_ref[i, :, cols]
            zero = jnp.zeros_like(q2)
            qq = jnp.concatenate([jnp.where(low, q2, zero), jnp.where(low, zero, q2)], axis=0)
            kt = k_ref[i, :, :, cols].reshape(nk, LANES)
            scores.append(_dot_nt(qq, kt) + tab_ref[t])
        return jnp.concatenate(scores, axis=0)

    def softmax_numerators(sh):
        m = jnp.broadcast_to(jnp.max(sh, axis=-1, keepdims=True), (sh.shape[0], LANES))
        return jnp.exp(sh - jnp.concatenate([m] * (nk // LANES), axis=1)).astype(BF16)

    def outputs(i, pe):
        for t in range(C_HEADS // 2):
            cols = slice(t * LANES, (t + 1) * LANES)
            rows = slice(2 * t * nq, 2 * (t + 1) * nq)
            vt = v_ref[i, :, :, cols].reshape(nk, LANES)
            od = _dot(pe[rows], jnp.concatenate([vt, ones], axis=1))
            o = od[:, :LANES] / od[:, LANES:]
            mix_ref[pl.ds(pl.multiple_of(i * nq, nq), nq), cols] = jnp.where(low, o[:nq], o[nq:]).astype(BF16)

    def two_sequences(j, carry):
        first, second = 2 * j, 2 * j + 1
        sh_a = all_scores(first)
        sh_b = all_scores(second)
        pe_a = softmax_numerators(sh_a)
        pe_b = softmax_numerators(sh_b)
        outputs(first, pe_a)
        outputs(second, pe_b)
        return carry

    n_seq = q_ref.shape[0]
    lax.fori_loop(0, n_seq // 2, two_sequences, 0)
    proj = _dot(mix_ref[...], wout_ref[...])
    o_ref[...] = h_ref[...] + proj.reshape(n_seq, nq, proj.shape[1])


def _attn_c(qkv4, rpb, h4, w_out, seqs_per_step=4):
    b, rows, w, d = h4.shape
    hw = C_HEADS * HEAD_DIM
    kr_n = min(C_WIN_ROWS, rows)
    nb = seqs_per_step
    table = _attn_c_table(rpb, rows)
    row_start = lambda r: jnp.clip(r - kr_n // 2, 0, rows - kr_n)
    el = pl.Element
    return pl.pallas_call(
        _attn_c_kernel,
        out_shape=jax.ShapeDtypeStruct((b, rows, w, d), F32),
        grid=(rows, b // nb),
        in_specs=[
            pl.BlockSpec((nb, None, w, hw), lambda r, bi: (bi, r, 0, 0)),
            pl.BlockSpec((el(nb), el(kr_n), el(w), el(hw)), lambda r, bi: (bi * nb, row_start(r), 0, hw)),
            pl.BlockSpec((el(nb), el(kr_n), el(w), el(hw)), lambda r, bi: (bi * nb, row_start(r), 0, 2 * hw)),
            pl.BlockSpec((None, C_HEADS // 2, 2 * w, kr_n * w),
                         lambda r, bi: (row_start(r) - r + C_WIN_ROWS - 1, 0, 0, 0)),
            pl.BlockSpec((nb, None, w, d), lambda r, bi: (bi, r, 0, 0)),
            _full_spec((hw, d)),
        ],
        out_specs=pl.BlockSpec((nb, None, w, d), lambda r, bi: (bi, r, 0, 0)),
        scratch_shapes=[pltpu.VMEM((nb * w, hw), BF16)],
        compiler_params=_cparams("arbitrary", "arbitrary"),
        name="attn_c",
    )(qkv4, qkv4, qkv4, table, h4, w_out.astype(BF16))


def _lane_cumsum(x, tri):
    e, s = x.shape
    off = jnp.zeros((e, 1), F32)
    outs = []
    for j in range(s // LANES):
        blk = x[:, j * LANES:(j + 1) * LANES]
        outs.append(_dot(blk.astype(BF16), tri) + off)
        off = off + jnp.sum(blk, axis=1, keepdims=True)
    return jnp.concatenate(outs, axis=1)


def _route_kernel(h_ref, g_ref, rwt_ref, tri_ref, xn_ref, slotr_ref, slotc_ref, gatec_ref, cnt_ref,
                  *, cap, chunk, tile):
    s = h_ref.shape[0]
    n_e = rwt_ref.shape[0] // 2
    logits = []
    for c in range(s // chunk):
        rows = slice(c * chunk, (c + 1) * chunk)
        xn = _rms(h_ref[rows, :], g_ref[...])
        xn_hi = xn.astype(BF16)
        xn_ref[rows, :] = xn_hi
        xn_lo = (xn - xn_hi.astype(F32)).astype(BF16)
        both = _dot_nt(rwt_ref[...], xn_hi)
        logits.append(both[:n_e] + both[n_e:] + _dot_nt(rwt_ref[:n_e, :], xn_lo))
    lg = jnp.concatenate(logits, axis=1)
    ex = jnp.exp(lg - jnp.max(lg, axis=0, keepdims=True))
    aff = ex / jnp.sum(ex, axis=0, keepdims=True)
    bits = pltpu.bitcast(aff, jnp.int32)
    thr = jnp.zeros((n_e, 1), jnp.int32)
    for shift in range(27, -1, -3):
        digit = jnp.zeros((n_e, 1), jnp.int32)
        for j in range(1, 8):
            cnt = jnp.sum(jnp.where(bits >= (thr | (j << shift)), 1.0, 0.0), axis=1, keepdims=True)
            digit = digit + jnp.where(cnt >= cap, 1, 0)
        thr = thr | (digit * (1 << shift))
    gt = bits > thr
    eq = bits == thr
    need = cap - jnp.sum(jnp.where(gt, 1.0, 0.0), axis=1, keepdims=True)
    eqf = jnp.where(eq, 1.0, 0.0)
    eq_rank = _lane_cumsum(eqf, tri_ref[...]) - eqf
    sel = gt | (eq & (eq_rank < need))
    self_ = jnp.where(sel, 1.0, 0.0)
    slot = jnp.where(sel, _lane_cumsum(self_, tri_ref[...]) - self_, -1.0)
    lane = lax.broadcasted_iota(jnp.int32, (n_e, LANES), 1)
    counts = jnp.zeros((n_e, LANES), F32)
    for i in range(s // tile):
        counts = jnp.where(lane == i, jnp.sum(self_[:, i * tile:(i + 1) * tile], axis=1, keepdims=True), counts)
    cnt_ref[...] = counts.astype(jnp.int32)
    pad_rows = LANES - n_e
    slot_p = jnp.concatenate([slot, jnp.full((pad_rows, s), -1.0, F32)], axis=0)
    gate_p = jnp.concatenate([jnp.where(sel, aff, 0.0), jnp.zeros((pad_rows, s), F32)], axis=0)
    slotr_ref[...] = slot
    for j in range(s // LANES):
        cols = slice(j * LANES, (j + 1) * LANES)
        slotc_ref[cols, :] = slot_p[:, cols].T.astype(BF16)
        gatec_ref[cols, :] = gate_p[:, cols].T.astype(BF16)


def _route(h3, g, router_w, cap):
    b, s, d = h3.shape
    n_e = router_w.shape[1]
    tri = jnp.asarray(np.triu(np.ones((LANES, LANES))), BF16)
    rw_t = router_w.T.astype(F32)
    rw_hi = rw_t.astype(BF16)
    rw_lo = (rw_t - rw_hi.astype(F32)).astype(BF16)
    per_seq = lambda *shape: pl.BlockSpec((None,) + shape, lambda i: (i,) + (0,) * len(shape))
    return pl.pallas_call(
        functools.partial(_route_kernel, cap=cap, chunk=256, tile=TOKEN_TILE),
        out_shape=(
            jax.ShapeDtypeStruct((b, s, d), BF16),
            jax.ShapeDtypeStruct((b, n_e, s), F32),
            jax.ShapeDtypeStruct((b, s, LANES), BF16),
            jax.ShapeDtypeStruct((b, s, LANES), BF16),
            jax.ShapeDtypeStruct((b, n_e, LANES), jnp.int32),
        ),
        grid=(b,),
        in_specs=[per_seq(s, d), _full_spec((1, d)), _full_spec((2 * n_e, d)), _full_spec((LANES, LANES))],
        out_specs=(per_seq(s, d), per_seq(n_e, s), per_seq(s, LANES), per_seq(s, LANES), per_seq(n_e, LANES)),
        compiler_params=_cparams("parallel"),
        name="route",
    )(h3, g.reshape(1, d), jnp.concatenate([rw_hi, rw_lo], axis=0), tri)


def _chunk_tables(cnt, cap):
    n_e = cnt.shape[1]
    kmax = n_e * (TOKEN_TILE // CHUNK + 1)
    kmax = -(-kmax // CHUNKS_PER_GROUP) * CHUNKS_PER_GROUP
    c0 = jnp.cumsum(cnt, axis=2) - cnt
    a0 = c0 // CHUNK
    a1 = jnp.where(cnt > 0, (c0 + cnt + CHUNK - 1) // CHUNK, a0)
    nch = (a1 - a0).transpose(0, 2, 1)
    a0 = a0.transpose(0, 2, 1)
    pos_end = jnp.cumsum(nch, axis=2)
    pos = pos_end - nch
    total = pos_end[..., -1]
    k = jnp.arange(kmax, dtype=jnp.int32)
    e_of_k = jnp.sum((k[None, None, :, None] >= pos_end[:, :, None, :]).astype(jnp.int32), axis=-1)
    e_of_k = jnp.minimum(e_of_k, n_e - 1)
    is_e = e_of_k[..., None] == jnp.arange(n_e, dtype=jnp.int32)
    q = k + jnp.sum(jnp.where(is_e, (a0 - pos)[:, :, None, :], 0), axis=-1)
    valid = k[None, None, :] < total[..., None]
    e_tab = jnp.where(valid, e_of_k, 0)
    slot_tab = jnp.where(valid, q * CHUNK, -(1 << 20))
    dst_tab = jnp.where(valid, e_of_k * cap + q * CHUNK, 0)
    groups = (total + CHUNKS_PER_GROUP - 1) // CHUNKS_PER_GROUP
    return jnp.concatenate([e_tab, slot_tab, dst_tab, groups[..., None]], axis=-1).astype(jnp.int32), kmax


def _gather_kernel(tab_ref, xn_ref, slotr_ref, xs_ref, *, kmax):
    @pl.when(pl.program_id(1) == 0)
    def _():
        xs_ref[...] = jnp.zeros_like(xs_ref)

    sub = lax.broadcasted_iota(jnp.int32, (CHUNK, TOKEN_TILE), 0).astype(F32)

    def gathered_rows(tile, g):
        tokens = slice(tile * TOKEN_TILE, (tile + 1) * TOKEN_TILE)
        base = g * CHUNKS_PER_GROUP
        onehot = []
        for c in range(CHUNKS_PER_GROUP):
            slots = slotr_ref[pl.ds(tab_ref[tile, base + c], 1), tokens]
            want = sub + tab_ref[tile, kmax + base + c].astype(F32)
            onehot.append(jnp.where(slots == want, 1.0, 0.0).astype(BF16))
        return _dot(jnp.concatenate(onehot, axis=0), xn_ref[tokens, :]).astype(BF16)

    def accumulate(tile, g, rows):
        base = g * CHUNKS_PER_GROUP
        for c0 in range(0, CHUNKS_PER_GROUP, 4):
            cs = range(c0, c0 + 4)
            dsts = [pl.ds(pl.multiple_of(tab_ref[tile, 2 * kmax + base + c], CHUNK), CHUNK) for c in cs]
            sums = [xs_ref[dst, :] + rows[c * CHUNK:(c + 1) * CHUNK, :] for c, dst in zip(cs, dsts)]
            for dst, total in reversed(list(zip(dsts, sums))):
                xs_ref[dst, :] = total

    pending = None
    for tile in range(TILES_PER_STEP):
        rows = gathered_rows(tile, 0)
        if pending is not None:
            accumulate(tile - 1, 0, pending)
        pending = rows
    accumulate(TILES_PER_STEP - 1, 0, pending)
    for tile in range(TILES_PER_STEP):
        def more(g, carry, tile=tile):
            accumulate(tile, g, gathered_rows(tile, g))
            return carry
        lax.fori_loop(1, tab_ref[tile, 3 * kmax], more, 0)


def _expert_kernel(xs_ref, wg_ref, wu_ref, wd_ref, y_ref, wg_bf, wu_bf, wd_bf):
    @pl.when(pl.program_id(1) == 0)
    def _():
        wg_bf[...] = wg_ref[...].astype(BF16)
        wu_bf[...] = wu_ref[...].astype(BF16)
        wd_bf[...] = wd_ref[...].astype(BF16)

    nb, cap, d = xs_ref.shape
    xs = xs_ref[...].reshape(nb * cap, d)
    hg = _dot(xs, wg_bf[...])
    hu = _dot(xs, wu_bf[...])
    hdn = (hg * jax.nn.sigmoid(hg) * hu).astype(BF16)
    y_ref[...] = _dot(hdn, wd_bf[...]).astype(BF16).reshape(nb, cap, d)


def _scatter_ple_kernel(tab_ref, y_ref, slotc_ref, gatec_ref, h_ref, p_ref, g_ref, wg_ref, wp_ref, o_ref,
                        acc_ref, *, kmax):
    acc_ref[...] = jnp.zeros_like(acc_ref)
    lane = lax.broadcasted_iota(jnp.int32, (1, GROUP), 1)
    sub = lax.broadcasted_iota(jnp.int32, (LANES, GROUP), 0)

    def group_operands(tile, g):
        tokens = slice(tile * TOKEN_TILE, (tile + 1) * TOKEN_TILE)
        base = g * CHUNKS_PER_GROUP
        expert = jnp.zeros((1, GROUP), jnp.int32)
        want = jnp.zeros((1, GROUP), jnp.int32)
        chunks = []
        for c in range(CHUNKS_PER_GROUP):
            in_chunk = (lane >= c * CHUNK) & (lane < (c + 1) * CHUNK)
            expert = jnp.where(in_chunk, tab_ref[tile, base + c], expert)
            want = jnp.where(in_chunk, tab_ref[tile, kmax + base + c] + lane - c * CHUNK, want)
            chunks.append(y_ref[pl.ds(pl.multiple_of(tab_ref[tile, 2 * kmax + base + c], CHUNK), CHUNK), :])
        pick = jnp.where(sub == expert, 1.0, 0.0).astype(BF16)
        both = _dot(jnp.concatenate([slotc_ref[tokens, :], gatec_ref[tokens, :]], axis=0), pick)
        weights = jnp.where(both[:TOKEN_TILE] == want.astype(F32), both[TOKEN_TILE:], 0.0).astype(BF16)
        return weights, jnp.concatenate(chunks, axis=0)

    def add_group(tile, operands):
        weights, rows = operands
        acc_ref[tile * TOKEN_TILE:(tile + 1) * TOKEN_TILE, :] += _dot(weights, rows)

    pending = None
    for tile in range(TILES_PER_STEP):
        operands = group_operands(tile, 0)
        if pending is not None:
            add_group(tile - 1, pending)
        pending = operands
    add_group(TILES_PER_STEP - 1, pending)
    for tile in range(TILES_PER_STEP):
        def more(g, carry, tile=tile):
            add_group(tile, group_operands(tile, g))
            return carry
        lax.fori_loop(1, tab_ref[tile, 3 * kmax], more, 0)
    h = h_ref[...] + acc_ref[...]
    xn = _rms(h, g_ref[...]).astype(BF16)
    gate = jax.nn.sigmoid(_dot(xn, wg_ref[...]))
    o_ref[...] = h + gate * _dot(p_ref[...].astype(BF16), wp_ref[...])


def _expert_ffn_ple(h3, xn3, slot_row, slot_col, gate_col, cnt, w_gate, w_up, w_down, cap, p3, layer, ple_g, ple_wg,
                    ple_wp, seqs_per_step=4):
    b, s, d = xn3.shape
    pd = p3.shape[2]
    _, n_e, _, ff = w_gate.shape
    step_rows = TILES_PER_STEP * TOKEN_TILE
    nt = s // step_rows
    tables, kmax = _chunk_tables(cnt[:, :, :s // TOKEN_TILE], cap)
    tables = tables.reshape(b * nt, TILES_PER_STEP, 3 * kmax + 1)
    tab_spec = pl.BlockSpec((None, TILES_PER_STEP, 3 * kmax + 1), lambda i, t: (i * nt + t, 0, 0),
                            memory_space=pltpu.SMEM)
    xs = pl.pallas_call(
        functools.partial(_gather_kernel, kmax=kmax),
        out_shape=jax.ShapeDtypeStruct((b, n_e * cap, d), BF16),
        grid=(b, nt),
        in_specs=[
            tab_spec,
            pl.BlockSpec((None, step_rows, d), lambda i, t: (i, t, 0)),
            pl.BlockSpec((None, n_e, step_rows), lambda i, t: (i, 0, t)),
        ],
        out_specs=pl.BlockSpec((None, n_e * cap, d), lambda i, t: (i, 0, 0)),
        compiler_params=_cparams("parallel", "arbitrary"),
        name="expert_gather",
    )(tables, xn3, slot_row)
    nb = seqs_per_step
    y = pl.pallas_call(
        _expert_kernel,
        out_shape=jax.ShapeDtypeStruct((b, n_e, cap, d), BF16),
        grid=(n_e, b // nb),
        in_specs=[
            pl.BlockSpec((nb, None, cap, d), lambda e, i: (i, e, 0, 0)),
            pl.BlockSpec((None, None, d, ff), lambda e, i: (layer, e, 0, 0)),
            pl.BlockSpec((None, None, d, ff), lambda e, i: (layer, e, 0, 0)),
            pl.BlockSpec((None, None, ff, d), lambda e, i: (layer, e, 0, 0)),
        ],
        out_specs=pl.BlockSpec((nb, None, cap, d), lambda e, i: (i, e, 0, 0)),
        scratch_shapes=[pltpu.VMEM((d, ff), BF16), pltpu.VMEM((d, ff), BF16), pltpu.VMEM((ff, d), BF16)],
        compiler_params=_cparams("parallel", "arbitrary"),
        name="expert_mlp",
    )(xs.reshape(b, n_e, cap, d), w_gate, w_up, w_down)
    tile_rows = lambda width: pl.BlockSpec((None, step_rows, width), lambda i, t: (i, t, 0))
    return pl.pallas_call(
        functools.partial(_scatter_ple_kernel, kmax=kmax),
        out_shape=jax.ShapeDtypeStruct((b, s, d), F32),
        grid=(b, nt),
        in_specs=[
            tab_spec,
            pl.BlockSpec((None, n_e * cap, d), lambda i, t: (i, 0, 0)),
            tile_rows(LANES), tile_rows(LANES), tile_rows(d),
            pl.BlockSpec((None, step_rows, pd), lambda i, t: (layer, i * nt + t, 0)),
            _full_spec((1, d)), _full_spec((d, d)), _full_spec((pd, d)),
        ],
        out_specs=tile_rows(d),
        scratch_shapes=[pltpu.VMEM((step_rows, d), F32)],
        compiler_params=_cparams("parallel", "arbitrary"),
        name="expert_scatter_ple",
    )(tables, y.reshape(b, n_e * cap, d), slot_col, gate_col, h3, p3, ple_g.reshape(1, d), ple_wg.astype(BF16),
      ple_wp.astype(BF16))


def kernel(x, p, norm_mix_g, norm_ffn_g, w_out, router_w, exp_w_gate, exp_w_up, exp_w_down, ple_norm_g, ple_gate_w, ple_proj_w, a_w_in, a_vnorm_g, a_w_s, a_b_s, b_w_in, b_qnorm_g, b_knorm_g, b_sink, c_w_in, c_qnorm_g, c_knorm_g, c_rpb):
    b, s, d = x.shape
    depth = norm_mix_g.shape[0]
    t = b * s
    cap = max(1, EC_CAPACITY_FACTOR * s // N_EXPERTS)
    scale = HEAD_DIM ** -0.5
    h = x.reshape(t, d)
    for i in range(depth):
        kind = i % N_MIXERS
        j = i // N_MIXERS
        if kind == 0:
            h = _mixer_a(h, norm_mix_g[i], a_w_in[j], a_vnorm_g[j], a_w_s[j], a_b_s[j], w_out[i])
        elif kind == 1:
            gain = jnp.concatenate([jnp.tile(b_qnorm_g[j] * scale, B_HEADS), jnp.tile(b_knorm_g[j], 2 * B_KV_HEADS)])
            qw, kw = B_HEADS * HEAD_DIM, B_KV_HEADS * HEAD_DIM
            w_b = jnp.concatenate([b_w_in[j][:, :qw], _dup_heads(b_w_in[j][:, qw:qw + kw], B_KV_HEADS),
                                   _dup_heads(b_w_in[j][:, qw + kw:], B_KV_HEADS)], axis=1)
            qkv = _norm_proj(h, norm_mix_g[i], w_b, gain, qw + 2 * kw)
            h = _attn_b(qkv.reshape(b, s, -1), b_sink[j], h.reshape(b, s, d), w_out[i]).reshape(t, d)
        else:
            gain = jnp.concatenate([jnp.tile(c_qnorm_g[j] * scale, C_HEADS), jnp.tile(c_knorm_g[j], C_HEADS)])
            qkv = _norm_proj(h, norm_mix_g[i], c_w_in[j], gain, 2 * C_HEADS * HEAD_DIM)
            h = _attn_c(qkv.reshape(b, s // GRID_W, GRID_W, -1), c_rpb[j], h.reshape(b, s // GRID_W, GRID_W, d),
                        w_out[i]).reshape(t, d)
        xn, slot_row, slot_col, gate_col, cnt = _route(h.reshape(b, s, d), norm_ffn_g[i], router_w[i], cap)
        h = _expert_ffn_ple(h.reshape(b, s, d), xn, slot_row, slot_col, gate_col, cnt, exp_w_gate, exp_w_up,
                            exp_w_down, cap, p.reshape(depth, t, -1), i, ple_norm_g[i], ple_gate_w[i],
                            ple_proj_w[i]).reshape(t, d)
    return h.reshape(b, s, d)
```

```python
import functools

import numpy as np
import jax
import jax.numpy as jnp
from jax import lax
from jax.experimental import pallas as pl
from jax.experimental.pallas import tpu as pltpu

F32 = jnp.float32
BF16 = jnp.bfloat16

RMS_EPS = 1e-6
LN_EPS = 1e-5
NEG = -1e30
HEAD_DIM = 64
GRID_W = 64
A_GROUPS = 8
A_CHUNK = 128
B_HEADS = 16
B_KV_HEADS = 4
B_BLOCK = 128
C_HEADS = 16
C_WIN_ROWS = 8
C_WIN_COLS = 16
N_EXPERTS = 16
EC_CAPACITY_FACTOR = 2
N_MIXERS = 3

LANES = 128
MXU_DIM = 256
VMEM_LIMIT_BYTES = 60 * 1024 * 1024

TOKEN_TILE = 256
TILES_PER_STEP = 4
A_SUBTILES = 2
PROJ_SUBTILES = 2
CHUNK = 16
GROUP = 4 * MXU_DIM
CHUNKS_PER_GROUP = GROUP // CHUNK


def _cparams(*sem):
    return pltpu.CompilerParams(dimension_semantics=sem, vmem_limit_bytes=VMEM_LIMIT_BYTES)


def _rms(xf, g):
    return xf * lax.rsqrt(jnp.mean(xf * xf, axis=-1, keepdims=True) + RMS_EPS) * g


def _dot(a, b):
    return jnp.dot(a, b, preferred_element_type=F32)


def _dot_nt(a, b):
    return lax.dot_general(a, b, (((1,), (1,)), ((), ())), preferred_element_type=F32)


def _full_spec(shape):
    nd = len(shape)
    return pl.BlockSpec(shape, lambda *_: (0,) * nd)


def _mixer_a_kernel(h_ref, g_ref, win_ref, vg_ref, ws_ref, bias_ref, wout_ref, o_ref, mix_ref):
    tm = h_ref.shape[0]
    width = vg_ref.shape[1]
    gw = width // A_GROUPS
    sub = tm // A_SUBTILES
    halves = [slice(i * sub, (i + 1) * sub) for i in range(A_SUBTILES)]
    zs = [_dot(_rms(h_ref[hs, :], g_ref[...]).astype(BF16), win_ref[...]) for hs in halves]
    us, vns = [], []
    for z in zs:
        z = 0.5 * z * (1.0 + jnp.tanh(np.sqrt(2.0 / np.pi).astype(np.float32) * (z + 0.044715 * (z * z * z))))
        v = z[:, width:]
        mu = jnp.mean(v, axis=-1, keepdims=True)
        vc = v - mu
        var = jnp.mean(vc * vc, axis=-1, keepdims=True)
        us.append(z[:, :width])
        vns.append((vc * lax.rsqrt(var + LN_EPS) * vg_ref[...]).astype(BF16))
    for hs, u, vn in zip(halves, us, vns):
        for c in range(sub // A_CHUNK):
            rows = slice(c * A_CHUNK, (c + 1) * A_CHUNK)
            out_rows = slice(hs.start + c * A_CHUNK, hs.start + (c + 1) * A_CHUNK)
            for g in range(A_GROUPS):
                cols = slice(g * gw, (g + 1) * gw)
                s = _dot(ws_ref[g], vn[rows, cols]) + bias_ref[:, cols]
                mix_ref[out_rows, cols] = (u[rows, cols] * s).astype(BF16)
    o_ref[...] = h_ref[...] + _dot(mix_ref[...], wout_ref[...])


def _mixer_a(h2, g, w_in, vnorm_g, w_s, b_s, w_out, tm=512):
    t, d = h2.shape
    width = vnorm_g.shape[0]
    gw = width // A_GROUPS
    bias = jnp.repeat(b_s.T.astype(F32), gw, axis=1)
    return pl.pallas_call(
        _mixer_a_kernel,
        out_shape=jax.ShapeDtypeStruct((t, d), F32),
        grid=(t // tm,),
        in_specs=[
            pl.BlockSpec((tm, d), lambda i: (i, 0)),
            _full_spec((1, d)),
            _full_spec((d, 2 * width)),
            _full_spec((1, width)),
            _full_spec((A_GROUPS, A_CHUNK, A_CHUNK)),
            _full_spec((A_CHUNK, width)),
            _full_spec((width, d)),
        ],
        out_specs=pl.BlockSpec((tm, d), lambda i: (i, 0)),
        scratch_shapes=[pltpu.VMEM((tm, width), BF16)],
        compiler_params=_cparams("parallel"),
        name="mixer_a",
    )(h2, g.reshape(1, d), w_in.astype(BF16), vnorm_g.reshape(1, width), w_s.astype(BF16), bias,
      w_out.astype(BF16))


def _norm_proj_kernel(h_ref, g_ref, w_ref, hg_ref, bd_ref, o_ref, *, n_norm_cols):
    sub = h_ref.shape[0] // PROJ_SUBTILES
    parts = [slice(i * sub, (i + 1) * sub) for i in range(PROJ_SUBTILES)]
    accs = [_dot(_rms(h_ref[rows, :], g_ref[...]).astype(BF16), w_ref[...]) for rows in parts]
    for rows, acc in zip(parts, accs):
        for j in range(n_norm_cols // MXU_DIM):
            cols = slice(j * MXU_DIM, (j + 1) * MXU_DIM)
            blk = acc[:, cols]
            ss = _dot((blk * blk).astype(BF16), bd_ref[...])
            o_ref[rows, cols] = (blk * lax.rsqrt(ss * (1.0 / HEAD_DIM) + RMS_EPS) * hg_ref[:, cols]).astype(BF16)
        o_ref[rows, n_norm_cols:] = acc[:, n_norm_cols:].astype(BF16)


def _norm_proj(h2, g, w, head_gain, n_norm_cols, tm=1024):
    t, d = h2.shape
    n = w.shape[1]
    blockdiag = jnp.asarray(np.kron(np.eye(MXU_DIM // HEAD_DIM), np.ones((HEAD_DIM, HEAD_DIM))), BF16)
    return pl.pallas_call(
        functools.partial(_norm_proj_kernel, n_norm_cols=n_norm_cols),
        out_shape=jax.ShapeDtypeStruct((t, n), BF16),
        grid=(t // tm,),
        in_specs=[
            pl.BlockSpec((tm, d), lambda i: (i, 0)),
            _full_spec((1, d)),
            _full_spec((d, n)),
            _full_spec((1, n_norm_cols)),
            _full_spec((MXU_DIM, MXU_DIM)),
        ],
        out_specs=pl.BlockSpec((tm, n), lambda i: (i, 0)),
        compiler_params=_cparams("parallel"),
        name="norm_proj",
    )(h2, g.reshape(1, d), w.astype(BF16), head_gain.reshape(1, n_norm_cols).astype(F32), blockdiag)


def _alibi_slopes(n):
    return np.array([2.0 ** (-8.0 * (h + 1) / n) for h in range(n)], dtype=np.float32)


def _attn_b_tables():
    span = 3 * B_BLOCK
    rel = np.arange(span)[None, :] - B_BLOCK - np.arange(B_BLOCK)[:, None]
    in_window = np.abs(rel) <= B_BLOCK
    alibi = (-_alibi_slopes(B_HEADS)[:, None, None] * np.abs(rel)[None]).astype(np.float32)
    kblk = np.arange(span) // B_BLOCK
    tabs = []
    for kind in range(3):
        valid = in_window & ~((kind == 0) & (kblk == 0))[None, :] & ~((kind == 2) & (kblk == 2))[None, :]
        tabs.append(np.where(valid[None], alibi, np.float32(NEG)))
    return np.stack(tabs).astype(np.float32)


def _attn_b_kernel(sink_ref, q_ref, kp_ref, kc_ref, kn_ref, vp_ref, vc_ref, vn_ref, tab_ref, h_ref, wout_ref, o_ref,
                   mix_ref):
    grp = B_HEADS // B_KV_HEADS
    nq = q_ref.shape[1]
    lane = lax.broadcasted_iota(jnp.int32, (1, LANES), 1)
    low = lane < HEAD_DIM
    span = kp_ref.shape[1] + kc_ref.shape[1] + kn_ref.shape[1]
    heads = [slice(h * nq, (h + 1) * nq) for h in range(B_HEADS)]
    sum_low = jnp.broadcast_to(jnp.where(low, 1.0, 0.0), (span, LANES)).astype(BF16)
    sum_high = jnp.broadcast_to(jnp.where(low, 0.0, 1.0), (span, LANES)).astype(BF16)

    def half_lanes(x):
        zero = jnp.zeros_like(x)
        return jnp.where(low, x, zero), jnp.where(low, zero, x)

    def all_scores(i):
        scores = []
        for kh in range(B_KV_HEADS):
            tile = slice(kh * LANES, (kh + 1) * LANES)
            kt = jnp.concatenate([kp_ref[i, :, tile], kc_ref[i, :, tile], kn_ref[i, :, tile]], axis=0)
            kk = jnp.concatenate(half_lanes(kt), axis=0)
            for local in range(grp // 2):
                qt = kh * grp // 2 + local
                s2 = _dot_nt(q_ref[i, :, qt * LANES:(qt + 1) * LANES], kk)
                scores.append(s2[:, :span] + tab_ref[kh, (2 * local) * nq:(2 * local + 1) * nq, :])
                scores.append(s2[:, span:] + tab_ref[kh, (2 * local + 1) * nq:(2 * local + 2) * nq, :])
        return jnp.concatenate(scores, axis=0)

    def softmax_numerators(sh):
        rowmax = jnp.broadcast_to(jnp.max(sh, axis=-1, keepdims=True), (sh.shape[0], LANES))
        m = jnp.concatenate([jnp.maximum(rowmax[hs], sink_ref[h]) for h, hs in enumerate(heads)], axis=0)
        sink_term = jnp.concatenate([jnp.exp(sink_ref[h] - m[hs]) for h, hs in enumerate(heads)], axis=0)
        return jnp.exp(sh - jnp.concatenate([m] * (span // LANES), axis=1)).astype(BF16), sink_term

    def outputs(i, pe, sink_term):
        seq_rows = pl.ds(pl.multiple_of(i * nq, nq), nq)
        for kh in range(B_KV_HEADS):
            tile = slice(kh * LANES, (kh + 1) * LANES)
            vt = jnp.concatenate([vp_ref[i, :, tile], vc_ref[i, :, tile], vn_ref[i, :, tile]], axis=0)
            v_low, v_high = half_lanes(vt)
            vv = jnp.concatenate([jnp.concatenate([v_low, sum_low], axis=1),
                                  jnp.concatenate([v_high, sum_high], axis=1)], axis=0)
            for local in range(grp // 2):
                qt = kh * grp // 2 + local
                even, odd = heads[2 * qt], heads[2 * qt + 1]
                od = _dot(jnp.concatenate([pe[even], pe[odd]], axis=1), vv)
                sink = jnp.where(low, sink_term[even], sink_term[odd])
                mix_ref[seq_rows, qt * LANES:(qt + 1) * LANES] = (od[:, :LANES] / (od[:, LANES:] + sink)).astype(BF16)

    def two_sequences(j, carry):
        first, second = 2 * j, 2 * j + 1
        sh_a = all_scores(first)
        sh_b = all_scores(second)
        pe_a, sink_a = softmax_numerators(sh_a)
        pe_b, sink_b = softmax_numerators(sh_b)
        outputs(first, pe_a, sink_a)
        outputs(second, pe_b, sink_b)
        return carry

    n_seq = q_ref.shape[0]
    lax.fori_loop(0, n_seq // 2, two_sequences, 0)
    proj = _dot(mix_ref[...], wout_ref[...])
    o_ref[...] = h_ref[...] + proj.reshape(n_seq, nq, proj.shape[1])


def _dup_heads(w, n_heads):
    d = w.shape[0]
    return jnp.repeat(w.reshape(d, n_heads, 1, HEAD_DIM), 2, axis=2).reshape(d, n_heads * LANES)


def _attn_b(qkv3, sink, h3, w_out, seqs_per_step=4):
    b, s, d = h3.shape
    nb = s // B_BLOCK
    ns = seqs_per_step
    grp = B_HEADS // B_KV_HEADS
    qw = B_HEADS * HEAD_DIM
    kvw = B_KV_HEADS * LANES
    kcol = qw // kvw
    vcol = kcol + 1
    tabs = jnp.asarray(_attn_b_tables().reshape(3, B_KV_HEADS, grp * B_BLOCK, 3 * B_BLOCK))
    prev = lambda n: jnp.maximum(n - 1, 0)
    nxt = lambda n: jnp.minimum(n + 1, nb - 1)
    kind = lambda n: jnp.where(n == 0, 0, jnp.where(n == nb - 1, 2, 1))
    kv = lambda col, f: pl.BlockSpec((ns, B_BLOCK, kvw), lambda bi, n: (bi, f(n), col))
    same = lambda n: n
    return pl.pallas_call(
        _attn_b_kernel,
        out_shape=jax.ShapeDtypeStruct((b, s, d), F32),
        grid=(b // ns, nb),
        in_specs=[
            pl.BlockSpec(memory_space=pltpu.SMEM),
            pl.BlockSpec((ns, B_BLOCK, qw), lambda bi, n: (bi, n, 0)),
            kv(kcol, prev), kv(kcol, same), kv(kcol, nxt),
            kv(vcol, prev), kv(vcol, same), kv(vcol, nxt),
            pl.BlockSpec((None, B_KV_HEADS, grp * B_BLOCK, 3 * B_BLOCK), lambda bi, n: (kind(n), 0, 0, 0)),
            pl.BlockSpec((ns, B_BLOCK, d), lambda bi, n: (bi, n, 0)),
            _full_spec((qw, d)),
        ],
        out_specs=pl.BlockSpec((ns, B_BLOCK, d), lambda bi, n: (bi, n, 0)),
        scratch_shapes=[pltpu.VMEM((ns * B_BLOCK, qw), BF16)],
        compiler_params=_cparams("parallel", "arbitrary"),
        name="attn_b",
    )(sink.astype(F32), qkv3, qkv3, qkv3, qkv3, qkv3, qkv3, qkv3, tabs, h3, w_out.astype(BF16))


def _attn_c_table(rpb, rows):
    w = GRID_W
    kr_n = min(C_WIN_ROWS, rows)
    c = np.arange(w)
    cs = np.clip(c - C_WIN_COLS // 2, 0, w - C_WIN_COLS)
    kcol = np.arange(w)
    colmask = (kcol[None, :] >= cs[:, None]) & (kcol[None, :] < cs[:, None] + C_WIN_COLS)
    n_off = 2 * C_WIN_ROWS - kr_n
    n_h, n_ri, _ = rpb.shape
    pad = w - C_WIN_COLS
    padded = jnp.pad(rpb.astype(F32), ((0, 0), (0, 0), (pad, pad)))
    bias = jnp.stack([padded[:, :, w - 1 - qc:2 * w - 1 - qc] for qc in range(w)], axis=1)
    bias = jnp.where(jnp.asarray(colmask)[None, :, None, :], bias, NEG)
    bias = bias.reshape(n_h, w, n_ri * w)
    slabs = jnp.stack([bias[:, :, n * w:(n + kr_n) * w] for n in range(n_off)])
    return slabs.reshape(n_off, n_h // 2, 2 * w, kr_n * w)


def _attn_c_kernel(q_ref, k_ref, v_ref, tab_ref, h_ref, wout_ref, o_ref, mix_ref):
    nk = k_ref.shape[1] * k_ref.shape[2]
    nq = q_ref.shape[1]
    lane = lax.broadcasted_iota(jnp.int32, (1, LANES), 1)
    low = lane < HEAD_DIM
    ones = jnp.ones((nk, LANES), BF16)

    def all_scores(i):
        scores = []
        for t in range(C_HEADS // 2):
            cols = slice(t * LANES, (t + 1) * LANES)
            q2 = q_ref[i, :, cols]
            zero = jnp.zeros_like(q2)
            qq = jnp.concatenate([jnp.where(low, q2, zero), jnp.where(low, zero, q2)], axis=0)
            kt = k_ref[i, :, :, cols].reshape(nk, LANES)
            scores.append(_dot_nt(qq, kt) + tab_ref[t])
        return jnp.concatenate(scores, axis=0)

    def softmax_numerators(sh):
        m = jnp.broadcast_to(jnp.max(sh, axis=-1, keepdims=True), (sh.shape[0], LANES))
        return jnp.exp(sh - jnp.concatenate([m] * (nk // LANES), axis=1)).astype(BF16)

    def outputs(i, pe):
        for t in range(C_HEADS // 2):
            cols = slice(t * LANES, (t + 1) * LANES)
            rows = slice(2 * t * nq, 2 * (t + 1) * nq)
            vt = v_ref[i, :, :, cols].reshape(nk, LANES)
            od = _dot(pe[rows], jnp.concatenate([vt, ones], axis=1))
            o = od[:, :LANES] / od[:, LANES:]
            mix_ref[pl.ds(pl.multiple_of(i * nq, nq), nq), cols] = jnp.where(low, o[:nq], o[nq:]).astype(BF16)

    def two_sequences(j, carry):
        first, second = 2 * j, 2 * j + 1
        sh_a = all_scores(first)
        sh_b = all_scores(second)
        pe_a = softmax_numerators(sh_a)
        pe_b = softmax_numerators(sh_b)
        outputs(first, pe_a)
        outputs(second, pe_b)
        return carry

    n_seq = q_ref.shape[0]
    lax.fori_loop(0, n_seq // 2, two_sequences, 0)
    proj = _dot(mix_ref[...], wout_ref[...])
    o_ref[...] = h_ref[...] + proj.reshape(n_seq, nq, proj.shape[1])


def _attn_c(qkv4, rpb, h4, w_out, seqs_per_step=4):
    b, rows, w, d = h4.shape
    hw = C_HEADS * HEAD_DIM
    kr_n = min(C_WIN_ROWS, rows)
    nb = seqs_per_step
    table = _attn_c_table(rpb, rows)
    row_start = lambda r: jnp.clip(r - kr_n // 2, 0, rows - kr_n)
    el = pl.Element
    return pl.pallas_call(
        _attn_c_kernel,
        out_shape=jax.ShapeDtypeStruct((b, rows, w, d), F32),
        grid=(rows, b // nb),
        in_specs=[
            pl.BlockSpec((nb, None, w, hw), lambda r, bi: (bi, r, 0, 0)),
            pl.BlockSpec((el(nb), el(kr_n), el(w), el(hw)), lambda r, bi: (bi * nb, row_start(r), 0, hw)),
            pl.BlockSpec((el(nb), el(kr_n), el(w), el(hw)), lambda r, bi: (bi * nb, row_start(r), 0, 2 * hw)),
            pl.BlockSpec((None, C_HEADS // 2, 2 * w, kr_n * w),
                         lambda r, bi: (row_start(r) - r + C_WIN_ROWS - 1, 0, 0, 0)),
            pl.BlockSpec((nb, None, w, d), lambda r, bi: (bi, r, 0, 0)),
            _full_spec((hw, d)),
        ],
        out_specs=pl.BlockSpec((nb, None, w, d), lambda r, bi: (bi, r, 0, 0)),
        scratch_shapes=[pltpu.VMEM((nb * w, hw), BF16)],
        compiler_params=_cparams("arbitrary", "arbitrary"),
        name="attn_c",
    )(qkv4, qkv4, qkv4, table, h4, w_out.astype(BF16))


def _lane_cumsum(x, tri):
    e, s = x.shape
    off = jnp.zeros((e, 1), F32)
    outs = []
    for j in range(s // LANES):
        blk = x[:, j * LANES:(j + 1) * LANES]
        outs.append(_dot(blk.astype(BF16), tri) + off)
        off = off + jnp.sum(blk, axis=1, keepdims=True)
    return jnp.concatenate(outs, axis=1)


def _route_kernel(h_ref, g_ref, rwt_ref, tri_ref, xn_ref, slotr_ref, slotc_ref, gatec_ref, cnt_ref,
                  *, cap, chunk, tile):
    s = h_ref.shape[0]
    n_e = rwt_ref.shape[0] // 2
    logits = []
    for c in range(s // chunk):
        rows = slice(c * chunk, (c + 1) * chunk)
        xn = _rms(h_ref[rows, :], g_ref[...])
        xn_hi = xn.astype(BF16)
        xn_ref[rows, :] = xn_hi
        xn_lo = (xn - xn_hi.astype(F32)).astype(BF16)
        both = _dot_nt(rwt_ref[...], xn_hi)
        logits.append(both[:n_e] + both[n_e:] + _dot_nt(rwt_ref[:n_e, :], xn_lo))
    lg = jnp.concatenate(logits, axis=1)
    ex = jnp.exp(lg - jnp.max(lg, axis=0, keepdims=True))
    aff = ex / jnp.sum(ex, axis=0, keepdims=True)
    bits = pltpu.bitcast(aff, jnp.int32)
    thr = jnp.zeros((n_e, 1), jnp.int32)
    for shift in range(27, -1, -3):
        digit = jnp.zeros((n_e, 1), jnp.int32)
        for j in range(1, 8):
            cnt = jnp.sum(jnp.where(bits >= (thr | (j << shift)), 1.0, 0.0), axis=1, keepdims=True)
            digit = digit + jnp.where(cnt >= cap, 1, 0)
        thr = thr | (digit * (1 << shift))
    gt = bits > thr
    eq = bits == thr
    need = cap - jnp.sum(jnp.where(gt, 1.0, 0.0), axis=1, keepdims=True)
    eqf = jnp.where(eq, 1.0, 0.0)
    eq_rank = _lane_cumsum(eqf, tri_ref[...]) - eqf
    sel = gt | (eq & (eq_rank < need))
    self_ = jnp.where(sel, 1.0, 0.0)
    slot = jnp.where(sel, _lane_cumsum(self_, tri_ref[...]) - self_, -1.0)
    lane = lax.broadcasted_iota(jnp.int32, (n_e, LANES), 1)
    counts = jnp.zeros((n_e, LANES), F32)
    for i in range(s // tile):
        counts = jnp.where(lane == i, jnp.sum(self_[:, i * tile:(i + 1) * tile], axis=1, keepdims=True), counts)
    cnt_ref[...] = counts.astype(jnp.int32)
    pad_rows = LANES - n_e
    slot_p = jnp.concatenate([slot, jnp.full((pad_rows, s), -1.0, F32)], axis=0)
    gate_p = jnp.concatenate([jnp.where(sel, aff, 0.0), jnp.zeros((pad_rows, s), F32)], axis=0)
    slotr_ref[...] = slot
    for j in range(s // LANES):
        cols = slice(j * LANES, (j + 1) * LANES)
        slotc_ref[cols, :] = slot_p[:, cols].T.astype(BF16)
        gatec_ref[cols, :] = gate_p[:, cols].T.astype(BF16)


def _route(h3, g, router_w, cap):
    b, s, d = h3.shape
    n_e = router_w.shape[1]
    tri = jnp.asarray(np.triu(np.ones((LANES, LANES))), BF16)
    rw_t = router_w.T.astype(F32)
    rw_hi = rw_t.astype(BF16)
    rw_lo = (rw_t - rw_hi.astype(F32)).astype(BF16)
    per_seq = lambda *shape: pl.BlockSpec((None,) + shape, lambda i: (i,) + (0,) * len(shape))
    return pl.pallas_call(
        functools.partial(_route_kernel, cap=cap, chunk=256, tile=TOKEN_TILE),
        out_shape=(
            jax.ShapeDtypeStruct((b, s, d), BF16),
            jax.ShapeDtypeStruct((b, n_e, s), F32),
            jax.ShapeDtypeStruct((b, s, LANES), BF16),
            jax.ShapeDtypeStruct((b, s, LANES), BF16),
            jax.ShapeDtypeStruct((b, n_e, LANES), jnp.int32),
        ),
        grid=(b,),
        in_specs=[per_seq(s, d), _full_spec((1, d)), _full_spec((2 * n_e, d)), _full_spec((LANES, LANES))],
        out_specs=(per_seq(s, d), per_seq(n_e, s), per_seq(s, LANES), per_seq(s, LANES), per_seq(n_e, LANES)),
        compiler_params=_cparams("parallel"),
        name="route",
    )(h3, g.reshape(1, d), jnp.concatenate([rw_hi, rw_lo], axis=0), tri)


def _chunk_tables(cnt, cap):
    n_e = cnt.shape[1]
    kmax = n_e * (TOKEN_TILE // CHUNK + 1)
    kmax = -(-kmax // CHUNKS_PER_GROUP) * CHUNKS_PER_GROUP
    c0 = jnp.cumsum(cnt, axis=2) - cnt
    a0 = c0 // CHUNK
    a1 = jnp.where(cnt > 0, (c0 + cnt + CHUNK - 1) // CHUNK, a0)
    nch = (a1 - a0).transpose(0, 2, 1)
    a0 = a0.transpose(0, 2, 1)
    pos_end = jnp.cumsum(nch, axis=2)
    pos = pos_end - nch
    total = pos_end[..., -1]
    k = jnp.arange(kmax, dtype=jnp.int32)
    e_of_k = jnp.sum((k[None, None, :, None] >= pos_end[:, :, None, :]).astype(jnp.int32), axis=-1)
    e_of_k = jnp.minimum(e_of_k, n_e - 1)
    is_e = e_of_k[..., None] == jnp.arange(n_e, dtype=jnp.int32)
    q = k + jnp.sum(jnp.where(is_e, (a0 - pos)[:, :, None, :], 0), axis=-1)
    valid = k[None, None, :] < total[..., None]
    e_tab = jnp.where(valid, e_of_k, 0)
    slot_tab = jnp.where(valid, q * CHUNK, -(1 << 20))
    dst_tab = jnp.where(valid, e_of_k * cap + q * CHUNK, 0)
    groups = (total + CHUNKS_PER_GROUP - 1) // CHUNKS_PER_GROUP
    return jnp.concatenate([e_tab, slot_tab, dst_tab, groups[..., None]], axis=-1).astype(jnp.int32), kmax


def _gather_kernel(tab_ref, xn_ref, slotr_ref, xs_ref, *, kmax):
    @pl.when(pl.program_id(1) == 0)
    def _():
        xs_ref[...] = jnp.zeros_like(xs_ref)

    sub = lax.broadcasted_iota(jnp.int32, (CHUNK, TOKEN_TILE), 0).astype(F32)

    def gathered_rows(tile, g):
        tokens = slice(tile * TOKEN_TILE, (tile + 1) * TOKEN_TILE)
        base = g * CHUNKS_PER_GROUP
        onehot = []
        for c in range(CHUNKS_PER_GROUP):
            slots = slotr_ref[pl.ds(tab_ref[tile, base + c], 1), tokens]
            want = sub + tab_ref[tile, kmax + base + c].astype(F32)
            onehot.append(jnp.where(slots == want, 1.0, 0.0).astype(BF16))
        return _dot(jnp.concatenate(onehot, axis=0), xn_ref[tokens, :]).astype(BF16)

    def accumulate(tile, g, rows):
        base = g * CHUNKS_PER_GROUP
        for c0 in range(0, CHUNKS_PER_GROUP, 4):
            cs = range(c0, c0 + 4)
            dsts = [pl.ds(pl.multiple_of(tab_ref[tile, 2 * kmax + base + c], CHUNK), CHUNK) for c in cs]
            sums = [xs_ref[dst, :] + rows[c * CHUNK:(c + 1) * CHUNK, :] for c, dst in zip(cs, dsts)]
            for dst, total in reversed(list(zip(dsts, sums))):
                xs_ref[dst, :] = total

    pending = None
    for tile in range(TILES_PER_STEP):
        rows = gathered_rows(tile, 0)
        if pending is not None:
            accumulate(tile - 1, 0, pending)
        pending = rows
    accumulate(TILES_PER_STEP - 1, 0, pending)
    for tile in range(TILES_PER_STEP):
        def more(g, carry, tile=tile):
            accumulate(tile, g, gathered_rows(tile, g))
            return carry
        lax.fori_loop(1, tab_ref[tile, 3 * kmax], more, 0)


def _expert_kernel(xs_ref, wg_ref, wu_ref, wd_ref, y_ref, wg_bf, wu_bf, wd_bf):
    @pl.when(pl.program_id(1) == 0)
    def _():
        wg_bf[...] = wg_ref[...].astype(BF16)
        wu_bf[...] = wu_ref[...].astype(BF16)
        wd_bf[...] = wd_ref[...].astype(BF16)

    nb, cap, d = xs_ref.shape
    xs = xs_ref[...].reshape(nb * cap, d)
    hg = _dot(xs, wg_bf[...])
    hu = _dot(xs, wu_bf[...])
    hdn = (hg * jax.nn.sigmoid(hg) * hu).astype(BF16)
    y_ref[...] = _dot(hdn, wd_bf[...]).astype(BF16).reshape(nb, cap, d)


def _scatter_ple_kernel(tab_ref, y_ref, slotc_ref, gatec_ref, h_ref, p_ref, g_ref, wg_ref, wp_ref, o_ref,
                        acc_ref, *, kmax):
    acc_ref[...] = jnp.zeros_like(acc_ref)
    lane = lax.broadcasted_iota(jnp.int32, (1, GROUP), 1)
    sub = lax.broadcasted_iota(jnp.int32, (LANES, GROUP), 0)

    def group_operands(tile, g):
        tokens = slice(tile * TOKEN_TILE, (tile + 1) * TOKEN_TILE)
        base = g * CHUNKS_PER_GROUP
        expert = jnp.zeros((1, GROUP), jnp.int32)
        want = jnp.zeros((1, GROUP), jnp.int32)
        chunks = []
        for c in range(CHUNKS_PER_GROUP):
            in_chunk = (lane >= c * CHUNK) & (lane < (c + 1) * CHUNK)
            expert = jnp.where(in_chunk, tab_ref[tile, base + c], expert)
            want = jnp.where(in_chunk, tab_ref[tile, kmax + base + c] + lane - c * CHUNK, want)
            chunks.append(y_ref[pl.ds(pl.multiple_of(tab_ref[tile, 2 * kmax + base + c], CHUNK), CHUNK), :])
        pick = jnp.where(sub == expert, 1.0, 0.0).astype(BF16)
        both = _dot(jnp.concatenate([slotc_ref[tokens, :], gatec_ref[tokens, :]], axis=0), pick)
        weights = jnp.where(both[:TOKEN_TILE] == want.astype(F32), both[TOKEN_TILE:], 0.0).astype(BF16)
        return weights, jnp.concatenate(chunks, axis=0)

    def add_group(tile, operands):
        weights, rows = operands
        acc_ref[tile * TOKEN_TILE:(tile + 1) * TOKEN_TILE, :] += _dot(weights, rows)

    pending = None
    for tile in range(TILES_PER_STEP):
        operands = group_operands(tile, 0)
        if pending is not None:
            add_group(tile - 1, pending)
        pending = operands
    add_group(TILES_PER_STEP - 1, pending)
    for tile in range(TILES_PER_STEP):
        def more(g, carry, tile=tile):
            add_group(tile, group_operands(tile, g))
            return carry
        lax.fori_loop(1, tab_ref[tile, 3 * kmax], more, 0)
    h = h_ref[...] + acc_ref[...]
    xn = _rms(h, g_ref[...]).astype(BF16)
    gate = jax.nn.sigmoid(_dot(xn, wg_ref[...]))
    o_ref[...] = h + gate * _dot(p_ref[...].astype(BF16), wp_ref[...])


def _expert_ffn_ple(h3, xn3, slot_row, slot_col, gate_col, cnt, w_gate, w_up, w_down, cap, p3, layer, ple_g, ple_wg,
                    ple_wp, seqs_per_step=4):
    b, s, d = xn3.shape
    pd = p3.shape[2]
    _, n_e, _, ff = w_gate.shape
    step_rows = TILES_PER_STEP * TOKEN_TILE
    nt = s // step_rows
    tables, kmax = _chunk_tables(cnt[:, :, :s // TOKEN_TILE], cap)
    tables = tables.reshape(b * nt, TILES_PER_STEP, 3 * kmax + 1)
    tab_spec = pl.BlockSpec((None, TILES_PER_STEP, 3 * kmax + 1), lambda i, t: (i * nt + t, 0, 0),
                            memory_space=pltpu.SMEM)
    xs = pl.pallas_call(
        functools.partial(_gather_kernel, kmax=kmax),
        out_shape=jax.ShapeDtypeStruct((b, n_e * cap, d), BF16),
        grid=(b, nt),
        in_specs=[
            tab_spec,
            pl.BlockSpec((None, step_rows, d), lambda i, t: (i, t, 0)),
            pl.BlockSpec((None, n_e, step_rows), lambda i, t: (i, 0, t)),
        ],
        out_specs=pl.BlockSpec((None, n_e * cap, d), lambda i, t: (i, 0, 0)),
        compiler_params=_cparams("parallel", "arbitrary"),
        name="expert_gather",
    )(tables, xn3, slot_row)
    nb = seqs_per_step
    y = pl.pallas_call(
        _expert_kernel,
        out_shape=jax.ShapeDtypeStruct((b, n_e, cap, d), BF16),
        grid=(n_e, b // nb),
        in_specs=[
            pl.BlockSpec((nb, None, cap, d), lambda e, i: (i, e, 0, 0)),
            pl.BlockSpec((None, None, d, ff), lambda e, i: (layer, e, 0, 0)),
            pl.BlockSpec((None, None, d, ff), lambda e, i: (layer, e, 0, 0)),
            pl.BlockSpec((None, None, ff, d), lambda e, i: (layer, e, 0, 0)),
        ],
        out_specs=pl.BlockSpec((nb, None, cap, d), lambda e, i: (i, e, 0, 0)),
        scratch_shapes=[pltpu.VMEM((d, ff), BF16), pltpu.VMEM((d, ff), BF16), pltpu.VMEM((ff, d), BF16)],
        compiler_params=_cparams("parallel", "arbitrary"),
        name="expert_mlp",
    )(xs.reshape(b, n_e, cap, d), w_gate, w_up, w_down)
    tile_rows = lambda width: pl.BlockSpec((None, step_rows, width), lambda i, t: (i, t, 0))
    return pl.pallas_call(
        functools.partial(_scatter_ple_kernel, kmax=kmax),
        out_shape=jax.ShapeDtypeStruct((b, s, d), F32),
        grid=(b, nt),
        in_specs=[
            tab_spec,
            pl.BlockSpec((None, n_e * cap, d), lambda i, t: (i, 0, 0)),
            tile_rows(LANES), tile_rows(LANES), tile_rows(d),
            pl.BlockSpec((None, step_rows, pd), lambda i, t: (layer, i * nt + t, 0)),
            _full_spec((1, d)), _full_spec((d, d)), _full_spec((pd, d)),
        ],
        out_specs=tile_rows(d),
        scratch_shapes=[pltpu.VMEM((step_rows, d), F32)],
        compiler_params=_cparams("parallel", "arbitrary"),
        name="expert_scatter_ple",
    )(tables, y.reshape(b, n_e * cap, d), slot_col, gate_col, h3, p3, ple_g.reshape(1, d), ple_wg.astype(BF16),
      ple_wp.astype(BF16))


def kernel(x, p, norm_mix_g, norm_ffn_g, w_out, router_w, exp_w_gate, exp_w_up, exp_w_down, ple_norm_g, ple_gate_w, ple_proj_w, a_w_in, a_vnorm_g, a_w_s, a_b_s, b_w_in, b_qnorm_g, b_knorm_g, b_sink, c_w_in, c_qnorm_g, c_knorm_g, c_rpb):
    b, s, d = x.shape
    depth = norm_mix_g.shape[0]
    t = b * s
    cap = max(1, EC_CAPACITY_FACTOR * s // N_EXPERTS)
    scale = HEAD_DIM ** -0.5
    h = x.reshape(t, d)
    for i in range(depth):
        kind = i % N_MIXERS
        j = i // N_MIXERS
        if kind == 0:
            h = _mixer_a(h, norm_mix_g[i], a_w_in[j], a_vnorm_g[j], a_w_s[j], a_b_s[j], w_out[i])
        elif kind == 1:
            gain = jnp.concatenate([jnp.tile(b_qnorm_g[j] * scale, B_HEADS), jnp.tile(b_knorm_g[j], 2 * B_KV_HEADS)])
            qw, kw = B_HEADS * HEAD_DIM, B_KV_HEADS * HEAD_DIM
            w_b = jnp.concatenate([b_w_in[j][:, :qw], _dup_heads(b_w_in[j][:, qw:qw + kw], B_KV_HEADS),
                                   _dup_heads(b_w_in[j][:, qw + kw:], B_KV_HEADS)], axis=1)
            qkv = _norm_proj(h, norm_mix_g[i], w_b, gain, qw + 2 * kw)
            h = _attn_b(qkv.reshape(b, s, -1), b_sink[j], h.reshape(b, s, d), w_out[i]).reshape(t, d)
        else:
            gain = jnp.concatenate([jnp.tile(c_qnorm_g[j] * scale, C_HEADS), jnp.tile(c_knorm_g[j], C_HEADS)])
            qkv = _norm_proj(h, norm_mix_g[i], c_w_in[j], gain, 2 * C_HEADS * HEAD_DIM)
            h = _attn_c(qkv.reshape(b, s // GRID_W, GRID_W, -1), c_rpb[j], h.reshape(b, s // GRID_W, GRID_W, d),
                        w_out[i]).reshape(t, d)
        xn, slot_row, slot_col, gate_col, cnt = _route(h.reshape(b, s, d), norm_ffn_g[i], router_w[i], cap)
        h = _expert_ffn_ple(h.reshape(b, s, d), xn, slot_row, slot_col, gate_col, cnt, exp_w_gate, exp_w_up,
                            exp_w_down, cap, p.reshape(depth, t, -1), i, ple_norm_g[i], ple_gate_w[i],
                            ple_proj_w[i]).reshape(t, d)
    return h.reshape(b, s, d)
```

```python
import functools

import numpy as np
import jax
import jax.numpy as jnp
from jax import lax
from jax.experimental import pallas as pl
from jax.experimental.pallas import tpu as pltpu

F32 = jnp.float32
BF16 = jnp.bfloat16

RMS_EPS = 1e-6
LN_EPS = 1e-5
NEG = -1e30
HEAD_DIM = 64
GRID_W = 64
A_GROUPS = 8
A_CHUNK = 128
B_HEADS = 16
B_KV_HEADS = 4
B_BLOCK = 128
C_HEADS = 16
C_WIN_ROWS = 8
C_WIN_COLS = 16
N_EXPERTS = 16
EC_CAPACITY_FACTOR = 2
N_MIXERS = 3

LANES = 128
MXU_DIM = 256
VMEM_LIMIT_BYTES = 60 * 1024 * 1024

TOKEN_TILE = 256
TILES_PER_STEP = 4
A_SUBTILES = 2
PROJ_SUBTILES = 2
CHUNK = 16
GROUP = 4 * MXU_DIM
CHUNKS_PER_GROUP = GROUP // CHUNK


def _cparams(*sem):
    return pltpu.CompilerParams(dimension_semantics=sem, vmem_limit_bytes=VMEM_LIMIT_BYTES)


def _rms(xf, g):
    return xf * lax.rsqrt(jnp.mean(xf * xf, axis=-1, keepdims=True) + RMS_EPS) * g


def _dot(a, b):
    return jnp.dot(a, b, preferred_element_type=F32)


def _dot_nt(a, b):
    return lax.dot_general(a, b, (((1,), (1,)), ((), ())), preferred_element_type=F32)


def _full_spec(shape):
    nd = len(shape)
    return pl.BlockSpec(shape, lambda *_: (0,) * nd)


def _mixer_a_kernel(h_ref, g_ref, win_ref, vg_ref, ws_ref, bias_ref, wout_ref, o_ref, mix_ref):
    tm = h_ref.shape[0]
    width = vg_ref.shape[1]
    gw = width // A_GROUPS
    sub = tm // A_SUBTILES
    halves = [slice(i * sub, (i + 1) * sub) for i in range(A_SUBTILES)]
    zs = [_dot(_rms(h_ref[hs, :], g_ref[...]).astype(BF16), win_ref[...]) for hs in halves]
    us, vns = [], []
    for z in zs:
        z = 0.5 * z * (1.0 + jnp.tanh(np.sqrt(2.0 / np.pi).astype(np.float32) * (z + 0.044715 * (z * z * z))))
        v = z[:, width:]
        mu = jnp.mean(v, axis=-1, keepdims=True)
        vc = v - mu
        var = jnp.mean(vc * vc, axis=-1, keepdims=True)
        us.append(z[:, :width])
        vns.append((vc * lax.rsqrt(var + LN_EPS) * vg_ref[...]).astype(BF16))
    for hs, u, vn in zip(halves, us, vns):
        for c in range(sub // A_CHUNK):
            rows = slice(c * A_CHUNK, (c + 1) * A_CHUNK)
            out_rows = slice(hs.start + c * A_CHUNK, hs.start + (c + 1) * A_CHUNK)
            for g in range(A_GROUPS):
                cols = slice(g * gw, (g + 1) * gw)
                s = _dot(ws_ref[g], vn[rows, cols]) + bias_ref[:, cols]
                mix_ref[out_rows, cols] = (u[rows, cols] * s).astype(BF16)
    o_ref[...] = h_ref[...] + _dot(mix_ref[...], wout_ref[...])


def _mixer_a(h2, g, w_in, vnorm_g, w_s, b_s, w_out, tm=1024):
    t, d = h2.shape
    width = vnorm_g.shape[0]
    gw = width // A_GROUPS
    bias = jnp.repeat(b_s.T.astype(F32), gw, axis=1)
    return pl.pallas_call(
        _mixer_a_kernel,
        out_shape=jax.ShapeDtypeStruct((t, d), F32),
        grid=(t // tm,),
        in_specs=[
            pl.BlockSpec((tm, d), lambda i: (i, 0)),
            _full_spec((1, d)),
            _full_spec((d, 2 * width)),
            _full_spec((1, width)),
            _full_spec((A_GROUPS, A_CHUNK, A_CHUNK)),
            _full_spec((A_CHUNK, width)),
            _full_spec((width, d)),
        ],
        out_specs=pl.BlockSpec((tm, d), lambda i: (i, 0)),
        scratch_shapes=[pltpu.VMEM((tm, width), BF16)],
        compiler_params=_cparams("parallel"),
        name="mixer_a",
    )(h2, g.reshape(1, d), w_in.astype(BF16), vnorm_g.reshape(1, width), w_s.astype(BF16), bias,
      w_out.astype(BF16))


def _norm_proj_kernel(h_ref, g_ref, w_ref, hg_ref, bd_ref, o_ref, *, n_norm_cols):
    sub = h_ref.shape[0] // PROJ_SUBTILES
    parts = [slice(i * sub, (i + 1) * sub) for i in range(PROJ_SUBTILES)]
    accs = [_dot(_rms(h_ref[rows, :], g_ref[...]).astype(BF16), w_ref[...]) for rows in parts]
    for rows, acc in zip(parts, accs):
        for j in range(n_norm_cols // MXU_DIM):
            cols = slice(j * MXU_DIM, (j + 1) * MXU_DIM)
            blk = acc[:, cols]
            ss = _dot((blk * blk).astype(BF16), bd_ref[...])
            o_ref[rows, cols] = (blk * lax.rsqrt(ss * (1.0 / HEAD_DIM) + RMS_EPS) * hg_ref[:, cols]).astype(BF16)
        o_ref[rows, n_norm_cols:] = acc[:, n_norm_cols:].astype(BF16)


def _norm_proj(h2, g, w, head_gain, n_norm_cols, tm=1024):
    t, d = h2.shape
    n = w.shape[1]
    blockdiag = jnp.asarray(np.kron(np.eye(MXU_DIM // HEAD_DIM), np.ones((HEAD_DIM, HEAD_DIM))), BF16)
    return pl.pallas_call(
        functools.partial(_norm_proj_kernel, n_norm_cols=n_norm_cols),
        out_shape=jax.ShapeDtypeStruct((t, n), BF16),
        grid=(t // tm,),
        in_specs=[
            pl.BlockSpec((tm, d), lambda i: (i, 0)),
            _full_spec((1, d)),
            _full_spec((d, n)),
            _full_spec((1, n_norm_cols)),
            _full_spec((MXU_DIM, MXU_DIM)),
        ],
        out_specs=pl.BlockSpec((tm, n), lambda i: (i, 0)),
        compiler_params=_cparams("parallel"),
        name="norm_proj",
    )(h2, g.reshape(1, d), w.astype(BF16), head_gain.reshape(1, n_norm_cols).astype(F32), blockdiag)


def _alibi_slopes(n):
    return np.array([2.0 ** (-8.0 * (h + 1) / n) for h in range(n)], dtype=np.float32)


def _attn_b_tables():
    span = 3 * B_BLOCK
    rel = np.arange(span)[None, :] - B_BLOCK - np.arange(B_BLOCK)[:, None]
    in_window = np.abs(rel) <= B_BLOCK
    alibi = (-_alibi_slopes(B_HEADS)[:, None, None] * np.abs(rel)[None]).astype(np.float32)
    kblk = np.arange(span) // B_BLOCK
    tabs = []
    for kind in range(3):
        valid = in_window & ~((kind == 0) & (kblk == 0))[None, :] & ~((kind == 2) & (kblk == 2))[None, :]
        tabs.append(np.where(valid[None], alibi, np.float32(NEG)))
    return np.stack(tabs).astype(np.float32)


def _attn_b_kernel(sink_ref, q_ref, kp_ref, kc_ref, kn_ref, vp_ref, vc_ref, vn_ref, tab_ref, h_ref, wout_ref, o_ref,
                   mix_ref):
    grp = B_HEADS // B_KV_HEADS
    nq = q_ref.shape[1]
    lane = lax.broadcasted_iota(jnp.int32, (1, LANES), 1)
    low = lane < HEAD_DIM
    span = kp_ref.shape[1] + kc_ref.shape[1] + kn_ref.shape[1]
    heads = [slice(h * nq, (h + 1) * nq) for h in range(B_HEADS)]
    sum_low = jnp.broadcast_to(jnp.where(low, 1.0, 0.0), (span, LANES)).astype(BF16)
    sum_high = jnp.broadcast_to(jnp.where(low, 0.0, 1.0), (span, LANES)).astype(BF16)

    def half_lanes(x):
        zero = jnp.zeros_like(x)
        return jnp.where(low, x, zero), jnp.where(low, zero, x)

    def all_scores(i):
        scores = []
        for kh in range(B_KV_HEADS):
            tile = slice(kh * LANES, (kh + 1) * LANES)
            kt = jnp.concatenate([kp_ref[i, :, tile], kc_ref[i, :, tile], kn_ref[i, :, tile]], axis=0)
            kk = jnp.concatenate(half_lanes(kt), axis=0)
            for local in range(grp // 2):
                qt = kh * grp // 2 + local
                s2 = _dot_nt(q_ref[i, :, qt * LANES:(qt + 1) * LANES], kk)
                scores.append(s2[:, :span] + tab_ref[kh, (2 * local) * nq:(2 * local + 1) * nq, :])
                scores.append(s2[:, span:] + tab_ref[kh, (2 * local + 1) * nq:(2 * local + 2) * nq, :])
        return jnp.concatenate(scores, axis=0)

    def softmax_numerators(sh):
        rowmax = jnp.broadcast_to(jnp.max(sh, axis=-1, keepdims=True), (sh.shape[0], LANES))
        m = jnp.concatenate([jnp.maximum(rowmax[hs], sink_ref[h]) for h, hs in enumerate(heads)], axis=0)
        sink_term = jnp.concatenate([jnp.exp(sink_ref[h] - m[hs]) for h, hs in enumerate(heads)], axis=0)
        return jnp.exp(sh - jnp.concatenate([m] * (span // LANES), axis=1)).astype(BF16), sink_term

    def outputs(i, pe, sink_term):
        seq_rows = pl.ds(pl.multiple_of(i * nq, nq), nq)
        for kh in range(B_KV_HEADS):
            tile = slice(kh * LANES, (kh + 1) * LANES)
            vt = jnp.concatenate([vp_ref[i, :, tile], vc_ref[i, :, tile], vn_ref[i, :, tile]], axis=0)
            v_low, v_high = half_lanes(vt)
            vv = jnp.concatenate([jnp.concatenate([v_low, sum_low], axis=1),
                                  jnp.concatenate([v_high, sum_high], axis=1)], axis=0)
            for local in range(grp // 2):
                qt = kh * grp // 2 + local
                even, odd = heads[2 * qt], heads[2 * qt + 1]
                od = _dot(jnp.concatenate([pe[even], pe[odd]], axis=1), vv)
                sink = jnp.where(low, sink_term[even], sink_term[odd])
                mix_ref[seq_rows, qt * LANES:(qt + 1) * LANES] = (od[:, :LANES] / (od[:, LANES:] + sink)).astype(BF16)

    def two_sequences(j, carry):
        first, second = 2 * j, 2 * j + 1
        sh_a = all_scores(first)
        sh_b = all_scores(second)
        pe_a, sink_a = softmax_numerators(sh_a)
        pe_b, sink_b = softmax_numerators(sh_b)
        outputs(first, pe_a, sink_a)
        outputs(second, pe_b, sink_b)
        return carry

    n_seq = q_ref.shape[0]
    lax.fori_loop(0, n_seq // 2, two_sequences, 0)
    proj = _dot(mix_ref[...], wout_ref[...])
    o_ref[...] = h_ref[...] + proj.reshape(n_seq, nq, proj.shape[1])


def _dup_heads(w, n_heads):
    d = w.shape[0]
    return jnp.repeat(w.reshape(d, n_heads, 1, HEAD_DIM), 2, axis=2).reshape(d, n_heads * LANES)


def _attn_b(qkv3, sink, h3, w_out, seqs_per_step=4):
    b, s, d = h3.shape
    nb = s // B_BLOCK
    ns = seqs_per_step
    grp = B_HEADS // B_KV_HEADS
    qw = B_HEADS * HEAD_DIM
    kvw = B_KV_HEADS * LANES
    kcol = qw // kvw
    vcol = kcol + 1
    tabs = jnp.asarray(_attn_b_tables().reshape(3, B_KV_HEADS, grp * B_BLOCK, 3 * B_BLOCK))
    prev = lambda n: jnp.maximum(n - 1, 0)
    nxt = lambda n: jnp.minimum(n + 1, nb - 1)
    kind = lambda n: jnp.where(n == 0, 0, jnp.where(n == nb - 1, 2, 1))
    kv = lambda col, f: pl.BlockSpec((ns, B_BLOCK, kvw), lambda bi, n: (bi, f(n), col))
    same = lambda n: n
    return pl.pallas_call(
        _attn_b_kernel,
        out_shape=jax.ShapeDtypeStruct((b, s, d), F32),
        grid=(b // ns, nb),
        in_specs=[
            pl.BlockSpec(memory_space=pltpu.SMEM),
            pl.BlockSpec((ns, B_BLOCK, qw), lambda bi, n: (bi, n, 0)),
            kv(kcol, prev), kv(kcol, same), kv(kcol, nxt),
            kv(vcol, prev), kv(vcol, same), kv(vcol, nxt),
            pl.BlockSpec((None, B_KV_HEADS, grp * B_BLOCK, 3 * B_BLOCK), lambda bi, n: (kind(n), 0, 0, 0)),
            pl.BlockSpec((ns, B_BLOCK, d), lambda bi, n: (bi, n, 0)),
            _full_spec((qw, d)),
        ],
        out_specs=pl.BlockSpec((ns, B_BLOCK, d), lambda bi, n: (bi, n, 0)),
        scratch_shapes=[pltpu.VMEM((ns * B_BLOCK, qw), BF16)],
        compiler_params=_cparams("parallel", "arbitrary"),
        name="attn_b",
    )(sink.astype(F32), qkv3, qkv3, qkv3, qkv3, qkv3, qkv3, qkv3, tabs, h3, w_out.astype(BF16))


def _attn_c_table(rpb, rows):
    w = GRID_W
    kr_n = min(C_WIN_ROWS, rows)
    c = np.arange(w)
    cs = np.clip(c - C_WIN_COLS // 2, 0, w - C_WIN_COLS)
    kcol = np.arange(w)
    colmask = (kcol[None, :] >= cs[:, None]) & (kcol[None, :] < cs[:, None] + C_WIN_COLS)
    n_off = 2 * C_WIN_ROWS - kr_n
    n_h, n_ri, _ = rpb.shape
    pad = w - C_WIN_COLS
    padded = jnp.pad(rpb.astype(F32), ((0, 0), (0, 0), (pad, pad)))
    bias = jnp.stack([padded[:, :, w - 1 - qc:2 * w - 1 - qc] for qc in range(w)], axis=1)
    bias = jnp.where(jnp.asarray(colmask)[None, :, None, :], bias, NEG)
    bias = bias.reshape(n_h, w, n_ri * w)
    slabs = jnp.stack([bias[:, :, n * w:(n + kr_n) * w] for n in range(n_off)])
    return slabs.reshape(n_off, n_h // 2, 2 * w, kr_n * w)


def _attn_c_kernel(q_ref, k_ref, v_ref, tab_ref, h_ref, wout_ref, o_ref, mix_ref):
    nk = k_ref.shape[1] * k_ref.shape[2]
    nq = q_ref.shape[1]
    lane = lax.broadcasted_iota(jnp.int32, (1, LANES), 1)
    low = lane < HEAD_DIM
    ones = jnp.ones((nk, LANES), BF16)

    def all_scores(i):
        scores = []
        for t in range(C_HEADS // 2):
            cols = slice(t * LANES, (t + 1) * LANES)
            q2 = q_ref[i, :, cols]
            zero = jnp.zeros_like(q2)
            qq = jnp.concatenate([jnp.where(low, q2, zero), jnp.where(low, zero, q2)], axis=0)
            kt = k_ref[i, :, :, cols].reshape(nk, LANES)
            scores.append(_dot_nt(qq, kt) + tab_ref[t])
        return jnp.concatenate(scores, axis=0)

    def softmax_numerators(sh):
        m = jnp.broadcast_to(jnp.max(sh, axis=-1, keepdims=True), (sh.shape[0], LANES))
        return jnp.exp(sh - jnp.concatenate([m] * (nk // LANES), axis=1)).astype(BF16)

    def outputs(i, pe):
        for t in range(C_HEADS // 2):
            cols = slice(t * LANES, (t + 1) * LANES)
            rows = slice(2 * t * nq, 2 * (t + 1) * nq)
            vt = v_ref[i, :, :, cols].reshape(nk, LANES)
            od = _dot(pe[rows], jnp.concatenate([vt, ones], axis=1))
            o = od[:, :LANES] / od[:, LANES:]
            mix_ref[pl.ds(pl.multiple_of(i * nq, nq), nq), cols] = jnp.where(low, o[:nq], o[nq:]).astype(BF16)

    def two_sequences(j, carry):
        first, second = 2 * j, 2 * j + 1
        sh_a = all_scores(first)
        sh_b = all_scores(second)
        pe_a = softmax_numerators(sh_a)
        pe_b = softmax_numerators(sh_b)
        outputs(first, pe_a)
        outputs(second, pe_b)
        return carry

    n_seq = q_ref.shape[0]
    lax.fori_loop(0, n_seq // 2, two_sequences, 0)
    proj = _dot(mix_ref[...], wout_ref[...])
    o_ref[...] = h_ref[...] + proj.reshape(n_seq, nq, proj.shape[1])


def _attn_c(qkv4, rpb, h4, w_out, seqs_per_step=4):
    b, rows, w, d = h4.shape
    hw = C_HEADS * HEAD_DIM
    kr_n = min(C_WIN_ROWS, rows)
    nb = seqs_per_step
    table = _attn_c_table(rpb, rows)
    row_start = lambda r: jnp.clip(r - kr_n // 2, 0, rows - kr_n)
    el = pl.Element
    return pl.pallas_call(
        _attn_c_kernel,
        out_shape=jax.ShapeDtypeStruct((b, rows, w, d), F32),
        grid=(rows, b // nb),
        in_specs=[
            pl.BlockSpec((nb, None, w, hw), lambda r, bi: (bi, r, 0, 0)),
            pl.BlockSpec((el(nb), el(kr_n), el(w), el(hw)), lambda r, bi: (bi * nb, row_start(r), 0, hw)),
            pl.BlockSpec((el(nb), el(kr_n), el(w), el(hw)), lambda r, bi: (bi * nb, row_start(r), 0, 2 * hw)),
            pl.BlockSpec((None, C_HEADS // 2, 2 * w, kr_n * w),
                         lambda r, bi: (row_start(r) - r + C_WIN_ROWS - 1, 0, 0, 0)),
            pl.BlockSpec((nb, None, w, d), lambda r, bi: (bi, r, 0, 0)),
            _full_spec((hw, d)),
        ],
        out_specs=pl.BlockSpec((nb, None, w, d), lambda r, bi: (bi, r, 0, 0)),
        scratch_shapes=[pltpu.VMEM((nb * w, hw), BF16)],
        compiler_params=_cparams("arbitrary", "arbitrary"),
        name="attn_c",
    )(qkv4, qkv4, qkv4, table, h4, w_out.astype(BF16))


def _lane_cumsum(x, tri):
    e, s = x.shape
    off = jnp.zeros((e, 1), F32)
    outs = []
    for j in range(s // LANES):
        blk = x[:, j * LANES:(j + 1) * LANES]
        outs.append(_dot(blk.astype(BF16), tri) + off)
        off = off + jnp.sum(blk, axis=1, keepdims=True)
    return jnp.concatenate(outs, axis=1)


def _route_kernel(h_ref, g_ref, rwt_ref, tri_ref, xn_ref, slotr_ref, slotc_ref, gatec_ref, cnt_ref,
                  *, cap, chunk, tile):
    s = h_ref.shape[0]
    n_e = rwt_ref.shape[0] // 2
    logits = []
    for c in range(s // chunk):
        rows = slice(c * chunk, (c + 1) * chunk)
        xn = _rms(h_ref[rows, :], g_ref[...])
        xn_hi = xn.astype(BF16)
        xn_ref[rows, :] = xn_hi
        xn_lo = (xn - xn_hi.astype(F32)).astype(BF16)
        both = _dot_nt(rwt_ref[...], xn_hi)
        logits.append(both[:n_e] + both[n_e:] + _dot_nt(rwt_ref[:n_e, :], xn_lo))
    lg = jnp.concatenate(logits, axis=1)
    ex = jnp.exp(lg - jnp.max(lg, axis=0, keepdims=True))
    aff = ex / jnp.sum(ex, axis=0, keepdims=True)
    bits = pltpu.bitcast(aff, jnp.int32)
    thr = jnp.zeros((n_e, 1), jnp.int32)
    for shift in range(27, -1, -3):
        digit = jnp.zeros((n_e, 1), jnp.int32)
        for j in range(1, 8):
            cnt = jnp.sum(jnp.where(bits >= (thr | (j << shift)), 1.0, 0.0), axis=1, keepdims=True)
            digit = digit + jnp.where(cnt >= cap, 1, 0)
        thr = thr | (digit * (1 << shift))
    gt = bits > thr
    eq = bits == thr
    need = cap - jnp.sum(jnp.where(gt, 1.0, 0.0), axis=1, keepdims=True)
    eqf = jnp.where(eq, 1.0, 0.0)
    eq_rank = _lane_cumsum(eqf, tri_ref[...]) - eqf
    sel = gt | (eq & (eq_rank < need))
    self_ = jnp.where(sel, 1.0, 0.0)
    slot = jnp.where(sel, _lane_cumsum(self_, tri_ref[...]) - self_, -1.0)
    lane = lax.broadcasted_iota(jnp.int32, (n_e, LANES), 1)
    counts = jnp.zeros((n_e, LANES), F32)
    for i in range(s // tile):
        counts = jnp.where(lane == i, jnp.sum(self_[:, i * tile:(i + 1) * tile], axis=1, keepdims=True), counts)
    cnt_ref[...] = counts.astype(jnp.int32)
    pad_rows = LANES - n_e
    slot_p = jnp.concatenate([slot, jnp.full((pad_rows, s), -1.0, F32)], axis=0)
    gate_p = jnp.concatenate([jnp.where(sel, aff, 0.0), jnp.zeros((pad_rows, s), F32)], axis=0)
    slotr_ref[...] = slot
    for j in range(s // LANES):
        cols = slice(j * LANES, (j + 1) * LANES)
        slotc_ref[cols, :] = slot_p[:, cols].T.astype(BF16)
        gatec_ref[cols, :] = gate_p[:, cols].T.astype(BF16)


def _route(h3, g, router_w, cap):
    b, s, d = h3.shape
    n_e = router_w.shape[1]
    tri = jnp.asarray(np.triu(np.ones((LANES, LANES))), BF16)
    rw_t = router_w.T.astype(F32)
    rw_hi = rw_t.astype(BF16)
    rw_lo = (rw_t - rw_hi.astype(F32)).astype(BF16)
    per_seq = lambda *shape: pl.BlockSpec((None,) + shape, lambda i: (i,) + (0,) * len(shape))
    return pl.pallas_call(
        functools.partial(_route_kernel, cap=cap, chunk=256, tile=TOKEN_TILE),
        out_shape=(
            jax.ShapeDtypeStruct((b, s, d), BF16),
            jax.ShapeDtypeStruct((b, n_e, s), F32),
            jax.ShapeDtypeStruct((b, s, LANES), BF16),
            jax.ShapeDtypeStruct((b, s, LANES), BF16),
            jax.ShapeDtypeStruct((b, n_e, LANES), jnp.int32),
        ),
        grid=(b,),
        in_specs=[per_seq(s, d), _full_spec((1, d)), _full_spec((2 * n_e, d)), _full_spec((LANES, LANES))],
        out_specs=(per_seq(s, d), per_seq(n_e, s), per_seq(s, LANES), per_seq(s, LANES), per_seq(n_e, LANES)),
        compiler_params=_cparams("parallel"),
        name="route",
    )(h3, g.reshape(1, d), jnp.concatenate([rw_hi, rw_lo], axis=0), tri)


def _chunk_tables(cnt, cap):
    n_e = cnt.shape[1]
    kmax = n_e * (TOKEN_TILE // CHUNK + 1)
    kmax = -(-kmax // CHUNKS_PER_GROUP) * CHUNKS_PER_GROUP
    c0 = jnp.cumsum(cnt, axis=2) - cnt
    a0 = c0 // CHUNK
    a1 = jnp.where(cnt > 0, (c0 + cnt + CHUNK - 1) // CHUNK, a0)
    nch = (a1 - a0).transpose(0, 2, 1)
    a0 = a0.transpose(0, 2, 1)
    pos_end = jnp.cumsum(nch, axis=2)
    pos = pos_end - nch
    total = pos_end[..., -1]
    k = jnp.arange(kmax, dtype=jnp.int32)
    e_of_k = jnp.sum((k[None, None, :, None] >= pos_end[:, :, None, :]).astype(jnp.int32), axis=-1)
    e_of_k = jnp.minimum(e_of_k, n_e - 1)
    is_e = e_of_k[..., None] == jnp.arange(n_e, dtype=jnp.int32)
    q = k + jnp.sum(jnp.where(is_e, (a0 - pos)[:, :, None, :], 0), axis=-1)
    valid = k[None, None, :] < total[..., None]
    e_tab = jnp.where(valid, e_of_k, 0)
    slot_tab = jnp.where(valid, q * CHUNK, -(1 << 20))
    dst_tab = jnp.where(valid, e_of_k * cap + q * CHUNK, 0)
    groups = (total + CHUNKS_PER_GROUP - 1) // CHUNKS_PER_GROUP
    return jnp.concatenate([e_tab, slot_tab, dst_tab, groups[..., None]], axis=-1).astype(jnp.int32), kmax


def _gather_kernel(tab_ref, xn_ref, slotr_ref, xs_ref, *, kmax):
    @pl.when(pl.program_id(1) == 0)
    def _():
        xs_ref[...] = jnp.zeros_like(xs_ref)

    sub = lax.broadcasted_iota(jnp.int32, (CHUNK, TOKEN_TILE), 0).astype(F32)

    def gathered_rows(tile, g):
        tokens = slice(tile * TOKEN_TILE, (tile + 1) * TOKEN_TILE)
        base = g * CHUNKS_PER_GROUP
        onehot = []
        for c in range(CHUNKS_PER_GROUP):
            slots = slotr_ref[pl.ds(tab_ref[tile, base + c], 1), tokens]
            want = sub + tab_ref[tile, kmax + base + c].astype(F32)
            onehot.append(jnp.where(slots == want, 1.0, 0.0).astype(BF16))
        return _dot(jnp.concatenate(onehot, axis=0), xn_ref[tokens, :]).astype(BF16)

    def accumulate(tile, g, rows):
        base = g * CHUNKS_PER_GROUP
        for c0 in range(0, CHUNKS_PER_GROUP, 4):
            cs = range(c0, c0 + 4)
            dsts = [pl.ds(pl.multiple_of(tab_ref[tile, 2 * kmax + base + c], CHUNK), CHUNK) for c in cs]
            sums = [xs_ref[dst, :] + rows[c * CHUNK:(c + 1) * CHUNK, :] for c, dst in zip(cs, dsts)]
            for dst, total in reversed(list(zip(dsts, sums))):
                xs_ref[dst, :] = total

    pending = None
    for tile in range(TILES_PER_STEP):
        rows = gathered_rows(tile, 0)
        if pending is not None:
            accumulate(tile - 1, 0, pending)
        pending = rows
    accumulate(TILES_PER_STEP - 1, 0, pending)
    for tile in range(TILES_PER_STEP):
        def more(g, carry, tile=tile):
            accumulate(tile, g, gathered_rows(tile, g))
            return carry
        lax.fori_loop(1, tab_ref[tile, 3 * kmax], more, 0)


def _expert_kernel(xs_ref, wg_ref, wu_ref, wd_ref, y_ref, wg_bf, wu_bf, wd_bf):
    @pl.when(pl.program_id(1) == 0)
    def _():
        wg_bf[...] = wg_ref[...].astype(BF16)
        wu_bf[...] = wu_ref[...].astype(BF16)
        wd_bf[...] = wd_ref[...].astype(BF16)

    nb, cap, d = xs_ref.shape
    xs = xs_ref[...].reshape(nb * cap, d)
    hg = _dot(xs, wg_bf[...])
    hu = _dot(xs, wu_bf[...])
    hdn = (hg * jax.nn.sigmoid(hg) * hu).astype(BF16)
    y_ref[...] = _dot(hdn, wd_bf[...]).astype(BF16).reshape(nb, cap, d)


def _scatter_ple_kernel(tab_ref, y_ref, slotc_ref, gatec_ref, h_ref, p_ref, g_ref, wg_ref, wp_ref, o_ref,
                        acc_ref, *, kmax):
    acc_ref[...] = jnp.zeros_like(acc_ref)
    lane = lax.broadcasted_iota(jnp.int32, (1, GROUP), 1)
    sub = lax.broadcasted_iota(jnp.int32, (LANES, GROUP), 0)

    def group_operands(tile, g):
        tokens = slice(tile * TOKEN_TILE, (tile + 1) * TOKEN_TILE)
        base = g * CHUNKS_PER_GROUP
        expert = jnp.zeros((1, GROUP), jnp.int32)
        want = jnp.zeros((1, GROUP), jnp.int32)
        chunks = []
        for c in range(CHUNKS_PER_GROUP):
            in_chunk = (lane >= c * CHUNK) & (lane < (c + 1) * CHUNK)
            expert = jnp.where(in_chunk, tab_ref[tile, base + c], expert)
            want = jnp.where(in_chunk, tab_ref[tile, kmax + base + c] + lane - c * CHUNK, want)
            chunks.append(y_ref[pl.ds(pl.multiple_of(tab_ref[tile, 2 * kmax + base + c], CHUNK), CHUNK), :])
        pick = jnp.where(sub == expert, 1.0, 0.0).astype(BF16)
        both = _dot(jnp.concatenate([slotc_ref[tokens, :], gatec_ref[tokens, :]], axis=0), pick)
        weights = jnp.where(both[:TOKEN_TILE] == want.astype(F32), both[TOKEN_TILE:], 0.0).astype(BF16)
        return weights, jnp.concatenate(chunks, axis=0)

    def add_group(tile, operands):
        weights, rows = operands
        acc_ref[tile * TOKEN_TILE:(tile + 1) * TOKEN_TILE, :] += _dot(weights, rows)

    pending = None
    for tile in range(TILES_PER_STEP):
        operands = group_operands(tile, 0)
        if pending is not None:
            add_group(tile - 1, pending)
        pending = operands
    add_group(TILES_PER_STEP - 1, pending)
    for tile in range(TILES_PER_STEP):
        def more(g, carry, tile=tile):
            add_group(tile, group_operands(tile, g))
            return carry
        lax.fori_loop(1, tab_ref[tile, 3 * kmax], more, 0)
    h = h_ref[...] + acc_ref[...]
    xn = _rms(h, g_ref[...]).astype(BF16)
    gate = jax.nn.sigmoid(_dot(xn, wg_ref[...]))
    o_ref[...] = h + gate * _dot(p_ref[...].astype(BF16), wp_ref[...])


def _expert_ffn_ple(h3, xn3, slot_row, slot_col, gate_col, cnt, w_gate, w_up, w_down, cap, p3, layer, ple_g, ple_wg,
                    ple_wp, seqs_per_step=4):
    b, s, d = xn3.shape
    pd = p3.shape[2]
    _, n_e, _, ff = w_gate.shape
    step_rows = TILES_PER_STEP * TOKEN_TILE
    nt = s // step_rows
    tables, kmax = _chunk_tables(cnt[:, :, :s // TOKEN_TILE], cap)
    tables = tables.reshape(b * nt, TILES_PER_STEP, 3 * kmax + 1)
    tab_spec = pl.BlockSpec((None, TILES_PER_STEP, 3 * kmax + 1), lambda i, t: (i * nt + t, 0, 0),
                            memory_space=pltpu.SMEM)
    xs = pl.pallas_call(
        functools.partial(_gather_kernel, kmax=kmax),
        out_shape=jax.ShapeDtypeStruct((b, n_e * cap, d), BF16),
        grid=(b, nt),
        in_specs=[
            tab_spec,
            pl.BlockSpec((None, step_rows, d), lambda i, t: (i, t, 0)),
            pl.BlockSpec((None, n_e, step_rows), lambda i, t: (i, 0, t)),
        ],
        out_specs=pl.BlockSpec((None, n_e * cap, d), lambda i, t: (i, 0, 0)),
        compiler_params=_cparams("parallel", "arbitrary"),
        name="expert_gather",
    )(tables, xn3, slot_row)
    nb = seqs_per_step
    y = pl.pallas_call(
        _expert_kernel,
        out_shape=jax.ShapeDtypeStruct((b, n_e, cap, d), BF16),
        grid=(n_e, b // nb),
        in_specs=[
            pl.BlockSpec((nb, None, cap, d), lambda e, i: (i, e, 0, 0)),
            pl.BlockSpec((None, None, d, ff), lambda e, i: (layer, e, 0, 0)),
            pl.BlockSpec((None, None, d, ff), lambda e, i: (layer, e, 0, 0)),
            pl.BlockSpec((None, None, ff, d), lambda e, i: (layer, e, 0, 0)),
        ],
        out_specs=pl.BlockSpec((nb, None, cap, d), lambda e, i: (i, e, 0, 0)),
        scratch_shapes=[pltpu.VMEM((d, ff), BF16), pltpu.VMEM((d, ff), BF16), pltpu.VMEM((ff, d), BF16)],
        compiler_params=_cparams("parallel", "arbitrary"),
        name="expert_mlp",
    )(xs.reshape(b, n_e, cap, d), w_gate, w_up, w_down)
    tile_rows = lambda width: pl.BlockSpec((None, step_rows, width), lambda i, t: (i, t, 0))
    return pl.pallas_call(
        functools.partial(_scatter_ple_kernel, kmax=kmax),
        out_shape=jax.ShapeDtypeStruct((b, s, d), F32),
        grid=(b, nt),
        in_specs=[
            tab_spec,
            pl.BlockSpec((None, n_e * cap, d), lambda i, t: (i, 0, 0)),
            tile_rows(LANES), tile_rows(LANES), tile_rows(d),
            pl.BlockSpec((None, step_rows, pd), lambda i, t: (layer, i * nt + t, 0)),
            _full_spec((1, d)), _full_spec((d, d)), _full_spec((pd, d)),
        ],
        out_specs=tile_rows(d),
        scratch_shapes=[pltpu.VMEM((step_rows, d), F32)],
        compiler_params=_cparams("parallel", "arbitrary"),
        name="expert_scatter_ple",
    )(tables, y.reshape(b, n_e * cap, d), slot_col, gate_col, h3, p3, ple_g.reshape(1, d), ple_wg.astype(BF16),
      ple_wp.astype(BF16))


def kernel(x, p, norm_mix_g, norm_ffn_g, w_out, router_w, exp_w_gate, exp_w_up, exp_w_down, ple_norm_g, ple_gate_w, ple_proj_w, a_w_in, a_vnorm_g, a_w_s, a_b_s, b_w_in, b_qnorm_g, b_knorm_g, b_sink, c_w_in, c_qnorm_g, c_knorm_g, c_rpb):
    b, s, d = x.shape
    depth = norm_mix_g.shape[0]
    t = b * s
    cap = max(1, EC_CAPACITY_FACTOR * s // N_EXPERTS)
    scale = HEAD_DIM ** -0.5
    h = x.reshape(t, d)
    for i in range(depth):
        kind = i % N_MIXERS
        j = i // N_MIXERS
        if kind == 0:
            h = _mixer_a(h, norm_mix_g[i], a_w_in[j], a_vnorm_g[j], a_w_s[j], a_b_s[j], w_out[i])
        elif kind == 1:
            gain = jnp.concatenate([jnp.tile(b_qnorm_g[j] * scale, B_HEADS), jnp.tile(b_knorm_g[j], 2 * B_KV_HEADS)])
            qw, kw = B_HEADS * HEAD_DIM, B_KV_HEADS * HEAD_DIM
            w_b = jnp.concatenate([b_w_in[j][:, :qw], _dup_heads(b_w_in[j][:, qw:qw + kw], B_KV_HEADS),
                                   _dup_heads(b_w_in[j][:, qw + kw:], B_KV_HEADS)], axis=1)
            qkv = _norm_proj(h, norm_mix_g[i], w_b, gain, qw + 2 * kw)
            h = _attn_b(qkv.reshape(b, s, -1), b_sink[j], h.reshape(b, s, d), w_out[i]).reshape(t, d)
        else:
            gain = jnp.concatenate([jnp.tile(c_qnorm_g[j] * scale, C_HEADS), jnp.tile(c_knorm_g[j], C_HEADS)])
            qkv = _norm_proj(h, norm_mix_g[i], c_w_in[j], gain, 2 * C_HEADS * HEAD_DIM)
            h = _attn_c(qkv.reshape(b, s // GRID_W, GRID_W, -1), c_rpb[j], h.reshape(b, s // GRID_W, GRID_W, d),
                        w_out[i]).reshape(t, d)
        xn, slot_row, slot_col, gate_col, cnt = _route(h.reshape(b, s, d), norm_ffn_g[i], router_w[i], cap)
        h = _expert_ffn_ple(h.reshape(b, s, d), xn, slot_row, slot_col, gate_col, cnt, exp_w_gate, exp_w_up,
                            exp_w_down, cap, p.reshape(depth, t, -1), i, ple_norm_g[i], ple_gate_w[i],
                            ple_proj_w[i]).reshape(t, d)
    return h.reshape(b, s, d)
```

```python
import functools

import numpy as np
import jax
import jax.numpy as jnp
from jax import lax
from jax.experimental import pallas as pl
from jax.experimental.pallas import tpu as pltpu

F32 = jnp.float32
BF16 = jnp.bfloat16

RMS_EPS = 1e-6
LN_EPS = 1e-5
NEG = -1e30
HEAD_DIM = 64
GRID_W = 64
A_GROUPS = 8
A_CHUNK = 128
B_HEADS = 16
B_KV_HEADS = 4
B_BLOCK = 128
C_HEADS = 16
C_WIN_ROWS = 8
C_WIN_COLS = 16
N_EXPERTS = 16
EC_CAPACITY_FACTOR = 2
N_MIXERS = 3

LANES = 128
MXU_DIM = 256
VMEM_LIMIT_BYTES = 60 * 1024 * 1024

TOKEN_TILE = 256
TILES_PER_STEP = 4
A_SUBTILES = 2
PROJ_SUBTILES = 2
CHUNK = 16
GROUP = 4 * MXU_DIM
CHUNKS_PER_GROUP = GROUP // CHUNK


def _cparams(*sem):
    return pltpu.CompilerParams(dimension_semantics=sem, vmem_limit_bytes=VMEM_LIMIT_BYTES)


def _rms(xf, g):
    return xf * lax.rsqrt(jnp.mean(xf * xf, axis=-1, keepdims=True) + RMS_EPS) * g


def _dot(a, b):
    return jnp.dot(a, b, preferred_element_type=F32)


def _dot_nt(a, b):
    return lax.dot_general(a, b, (((1,), (1,)), ((), ())), preferred_element_type=F32)


def _full_spec(shape):
    nd = len(shape)
    return pl.BlockSpec(shape, lambda *_: (0,) * nd)


def _mixer_a_kernel(h_ref, g_ref, win_ref, vg_ref, ws_ref, bias_ref, wout_ref, o_ref, mix_ref):
    tm = h_ref.shape[0]
    width = vg_ref.shape[1]
    gw = width // A_GROUPS
    sub = tm // A_SUBTILES
    halves = [slice(i * sub, (i + 1) * sub) for i in range(A_SUBTILES)]
    zs = [_dot(_rms(h_ref[hs, :], g_ref[...]).astype(BF16), win_ref[...]) for hs in halves]
    us, vns = [], []
    for z in zs:
        z = 0.5 * z * (1.0 + jnp.tanh(np.sqrt(2.0 / np.pi).astype(np.float32) * (z + 0.044715 * (z * z * z))))
        v = z[:, width:]
        mu = jnp.mean(v, axis=-1, keepdims=True)
        vc = v - mu
        var = jnp.mean(vc * vc, axis=-1, keepdims=True)
        us.append(z[:, :width])
        vns.append((vc * lax.rsqrt(var + LN_EPS) * vg_ref[...]).astype(BF16))
    for hs, u, vn in zip(halves, us, vns):
        for c in range(sub // A_CHUNK):
            rows = slice(c * A_CHUNK, (c + 1) * A_CHUNK)
            out_rows = slice(hs.start + c * A_CHUNK, hs.start + (c + 1) * A_CHUNK)
            for g in range(A_GROUPS):
                cols = slice(g * gw, (g + 1) * gw)
                s = _dot(ws_ref[g], vn[rows, cols]) + bias_ref[:, cols]
                mix_ref[out_rows, cols] = (u[rows, cols] * s).astype(BF16)
    o_ref[...] = h_ref[...] + _dot(mix_ref[...], wout_ref[...])


def _mixer_a(h2, g, w_in, vnorm_g, w_s, b_s, w_out, tm=1024):
    t, d = h2.shape
    width = vnorm_g.shape[0]
    gw = width // A_GROUPS
    bias = jnp.repeat(b_s.T.astype(F32), gw, axis=1)
    return pl.pallas_call(
        _mixer_a_kernel,
        out_shape=jax.ShapeDtypeStruct((t, d), F32),
        grid=(t // tm,),
        in_specs=[
            pl.BlockSpec((tm, d), lambda i: (i, 0)),
            _full_spec((1, d)),
            _full_spec((d, 2 * width)),
            _full_spec((1, width)),
            _full_spec((A_GROUPS, A_CHUNK, A_CHUNK)),
            _full_spec((A_CHUNK, width)),
            _full_spec((width, d)),
        ],
        out_specs=pl.BlockSpec((tm, d), lambda i: (i, 0)),
        scratch_shapes=[pltpu.VMEM((tm, width), BF16)],
        compiler_params=_cparams("parallel"),
        name="mixer_a",
    )(h2, g.reshape(1, d), w_in.astype(BF16), vnorm_g.reshape(1, width), w_s.astype(BF16), bias,
      w_out.astype(BF16))


def _norm_proj_kernel(h_ref, g_ref, w_ref, hg_ref, bd_ref, o_ref, *, n_norm_cols):
    sub = h_ref.shape[0] // PROJ_SUBTILES
    parts = [slice(i * sub, (i + 1) * sub) for i in range(PROJ_SUBTILES)]
    accs = [_dot(_rms(h_ref[rows, :], g_ref[...]).astype(BF16), w_ref[...]) for rows in parts]
    for rows, acc in zip(parts, accs):
        for j in range(n_norm_cols // MXU_DIM):
            cols = slice(j * MXU_DIM, (j + 1) * MXU_DIM)
            blk = acc[:, cols]
            ss = _dot((blk * blk).astype(BF16), bd_ref[...])
            o_ref[rows, cols] = (blk * lax.rsqrt(ss * (1.0 / HEAD_DIM) + RMS_EPS) * hg_ref[:, cols]).astype(BF16)
        o_ref[rows, n_norm_cols:] = acc[:, n_norm_cols:].astype(BF16)


def _norm_proj(h2, g, w, head_gain, n_norm_cols, tm=1024):
    t, d = h2.shape
    n = w.shape[1]
    blockdiag = jnp.asarray(np.kron(np.eye(MXU_DIM // HEAD_DIM), np.ones((HEAD_DIM, HEAD_DIM))), BF16)
    return pl.pallas_call(
        functools.partial(_norm_proj_kernel, n_norm_cols=n_norm_cols),
        out_shape=jax.ShapeDtypeStruct((t, n), BF16),
        grid=(t // tm,),
        in_specs=[
            pl.BlockSpec((tm, d), lambda i: (i, 0)),
            _full_spec((1, d)),
            _full_spec((d, n)),
            _full_spec((1, n_norm_cols)),
            _full_spec((MXU_DIM, MXU_DIM)),
        ],
        out_specs=pl.BlockSpec((tm, n), lambda i: (i, 0)),
        compiler_params=_cparams("parallel"),
        name="norm_proj",
    )(h2, g.reshape(1, d), w.astype(BF16), head_gain.reshape(1, n_norm_cols).astype(F32), blockdiag)


def _alibi_slopes(n):
    return np.array([2.0 ** (-8.0 * (h + 1) / n) for h in range(n)], dtype=np.float32)


def _attn_b_tables():
    span = 3 * B_BLOCK
    rel = np.arange(span)[None, :] - B_BLOCK - np.arange(B_BLOCK)[:, None]
    in_window = np.abs(rel) <= B_BLOCK
    alibi = (-_alibi_slopes(B_HEADS)[:, None, None] * np.abs(rel)[None]).astype(np.float32)
    kblk = np.arange(span) // B_BLOCK
    tabs = []
    for kind in range(3):
        valid = in_window & ~((kind == 0) & (kblk == 0))[None, :] & ~((kind == 2) & (kblk == 2))[None, :]
        tabs.append(np.where(valid[None], alibi, np.float32(NEG)))
    return np.stack(tabs).astype(np.float32)


def _attn_b_kernel(sink_ref, q_ref, kp_ref, kc_ref, kn_ref, vp_ref, vc_ref, vn_ref, tab_ref, h_ref, wout_ref, o_ref,
                   mix_ref):
    grp = B_HEADS // B_KV_HEADS
    nq = q_ref.shape[1]
    lane = lax.broadcasted_iota(jnp.int32, (1, LANES), 1)
    low = lane < HEAD_DIM
    span = kp_ref.shape[1] + kc_ref.shape[1] + kn_ref.shape[1]
    heads = [slice(h * nq, (h + 1) * nq) for h in range(B_HEADS)]
    sum_low = jnp.broadcast_to(jnp.where(low, 1.0, 0.0), (span, LANES)).astype(BF16)
    sum_high = jnp.broadcast_to(jnp.where(low, 0.0, 1.0), (span, LANES)).astype(BF16)

    def half_lanes(x):
        zero = jnp.zeros_like(x)
        return jnp.where(low, x, zero), jnp.where(low, zero, x)

    def all_scores(i):
        scores = []
        for kh in range(B_KV_HEADS):
            tile = slice(kh * LANES, (kh + 1) * LANES)
            kt = jnp.concatenate([kp_ref[i, :, tile], kc_ref[i, :, tile], kn_ref[i, :, tile]], axis=0)
            kk = jnp.concatenate(half_lanes(kt), axis=0)
            for local in range(grp // 2):
                qt = kh * grp // 2 + local
                s2 = _dot_nt(q_ref[i, :, qt * LANES:(qt + 1) * LANES], kk)
                scores.append(s2[:, :span] + tab_ref[kh, (2 * local) * nq:(2 * local + 1) * nq, :])
                scores.append(s2[:, span:] + tab_ref[kh, (2 * local + 1) * nq:(2 * local + 2) * nq, :])
        return jnp.concatenate(scores, axis=0)

    def softmax_numerators(sh):
        rowmax = jnp.broadcast_to(jnp.max(sh, axis=-1, keepdims=True), (sh.shape[0], LANES))
        m = jnp.concatenate([jnp.maximum(rowmax[hs], sink_ref[h]) for h, hs in enumerate(heads)], axis=0)
        sink_term = jnp.concatenate([jnp.exp(sink_ref[h] - m[hs]) for h, hs in enumerate(heads)], axis=0)
        return jnp.exp(sh - jnp.concatenate([m] * (span // LANES), axis=1)).astype(BF16), sink_term

    def outputs(i, pe, sink_term):
        seq_rows = pl.ds(pl.multiple_of(i * nq, nq), nq)
        for kh in range(B_KV_HEADS):
            tile = slice(kh * LANES, (kh + 1) * LANES)
            vt = jnp.concatenate([vp_ref[i, :, tile], vc_ref[i, :, tile], vn_ref[i, :, tile]], axis=0)
            v_low, v_high = half_lanes(vt)
            vv = jnp.concatenate([jnp.concatenate([v_low, sum_low], axis=1),
                                  jnp.concatenate([v_high, sum_high], axis=1)], axis=0)
            for local in range(grp // 2):
                qt = kh * grp // 2 + local
                even, odd = heads[2 * qt], heads[2 * qt + 1]
                od = _dot(jnp.concatenate([pe[even], pe[odd]], axis=1), vv)
                sink = jnp.where(low, sink_term[even], sink_term[odd])
                mix_ref[seq_rows, qt * LANES:(qt + 1) * LANES] = (od[:, :LANES] / (od[:, LANES:] + sink)).astype(BF16)

    def two_sequences(j, carry):
        first, second = 2 * j, 2 * j + 1
        sh_a = all_scores(first)
        sh_b = all_scores(second)
        pe_a, sink_a = softmax_numerators(sh_a)
        pe_b, sink_b = softmax_numerators(sh_b)
        outputs(first, pe_a, sink_a)
        outputs(second, pe_b, sink_b)
        return carry

    n_seq = q_ref.shape[0]
    lax.fori_loop(0, n_seq // 2, two_sequences, 0)
    proj = _dot(mix_ref[...], wout_ref[...])
    o_ref[...] = h_ref[...] + proj.reshape(n_seq, nq, proj.shape[1])


def _dup_heads(w, n_heads):
    d = w.shape[0]
    return jnp.repeat(w.reshape(d, n_heads, 1, HEAD_DIM), 2, axis=2).reshape(d, n_heads * LANES)


def _attn_b(qkv3, sink, h3, w_out, seqs_per_step=4):
    b, s, d = h3.shape
    nb = s // B_BLOCK
    ns = seqs_per_step
    grp = B_HEADS // B_KV_HEADS
    qw = B_HEADS * HEAD_DIM
    kvw = B_KV_HEADS * LANES
    kcol = qw // kvw
    vcol = kcol + 1
    tabs = jnp.asarray(_attn_b_tables().reshape(3, B_KV_HEADS, grp * B_BLOCK, 3 * B_BLOCK))
    prev = lambda n: jnp.maximum(n - 1, 0)
    nxt = lambda n: jnp.minimum(n + 1, nb - 1)
    kind = lambda n: jnp.where(n == 0, 0, jnp.where(n == nb - 1, 2, 1))
    kv = lambda col, f: pl.BlockSpec((ns, B_BLOCK, kvw), lambda bi, n: (bi, f(n), col))
    same = lambda n: n
    return pl.pallas_call(
        _attn_b_kernel,
        out_shape=jax.ShapeDtypeStruct((b, s, d), F32),
        grid=(b // ns, nb),
        in_specs=[
            pl.BlockSpec(memory_space=pltpu.SMEM),
            pl.BlockSpec((ns, B_BLOCK, qw), lambda bi, n: (bi, n, 0)),
            kv(kcol, prev), kv(kcol, same), kv(kcol, nxt),
            kv(vcol, prev), kv(vcol, same), kv(vcol, nxt),
            pl.BlockSpec((None, B_KV_HEADS, grp * B_BLOCK, 3 * B_BLOCK), lambda bi, n: (kind(n), 0, 0, 0)),
            pl.BlockSpec((ns, B_BLOCK, d), lambda bi, n: (bi, n, 0)),
            _full_spec((qw, d)),
        ],
        out_specs=pl.BlockSpec((ns, B_BLOCK, d), lambda bi, n: (bi, n, 0)),
        scratch_shapes=[pltpu.VMEM((ns * B_BLOCK, qw), BF16)],
        compiler_params=_cparams("parallel", "arbitrary"),
        name="attn_b",
    )(sink.astype(F32), qkv3, qkv3, qkv3, qkv3, qkv3, qkv3, qkv3, tabs, h3, w_out.astype(BF16))


def _attn_c_table(rpb, rows):
    w = GRID_W
    kr_n = min(C_WIN_ROWS, rows)
    c = np.arange(w)
    cs = np.clip(c - C_WIN_COLS // 2, 0, w - C_WIN_COLS)
    kcol = np.arange(w)
    colmask = (kcol[None, :] >= cs[:, None]) & (kcol[None, :] < cs[:, None] + C_WIN_COLS)
    n_off = 2 * C_WIN_ROWS - kr_n
    n_h, n_ri, _ = rpb.shape
    pad = w - C_WIN_COLS
    padded = jnp.pad(rpb.astype(F32), ((0, 0), (0, 0), (pad, pad)))
    bias = jnp.stack([padded[:, :, w - 1 - qc:2 * w - 1 - qc] for qc in range(w)], axis=1)
    bias = jnp.where(jnp.asarray(colmask)[None, :, None, :], bias, NEG)
    bias = bias.reshape(n_h, w, n_ri * w)
    slabs = jnp.stack([bias[:, :, n * w:(n + kr_n) * w] for n in range(n_off)])
    return slabs.reshape(n_off, n_h // 2, 2 * w, kr_n * w)


def _attn_c_kernel(q_ref, k_ref, v_ref, tab_ref, h_ref, wout_ref, o_ref, mix_ref):
    nk = k_ref.shape[1] * k_ref.shape[2]
    nq = q_ref.shape[1]
    lane = lax.broadcasted_iota(jnp.int32, (1, LANES), 1)
    low = lane < HEAD_DIM
    ones = jnp.ones((nk, LANES), BF16)

    def all_scores(i):
        scores = []
        for t in range(C_HEADS // 2):
            cols = slice(t * LANES, (t + 1) * LANES)
            q2 = q_ref[i, :, cols]
            zero = jnp.zeros_like(q2)
            qq = jnp.concatenate([jnp.where(low, q2, zero), jnp.where(low, zero, q2)], axis=0)
            kt = k_ref[i, :, :, cols].reshape(nk, LANES)
            scores.append(_dot_nt(qq, kt) + tab_ref[t])
        return jnp.concatenate(scores, axis=0)

    def softmax_numerators(sh):
        m = jnp.broadcast_to(jnp.max(sh, axis=-1, keepdims=True), (sh.shape[0], LANES))
        return jnp.exp(sh - jnp.concatenate([m] * (nk // LANES), axis=1)).astype(BF16)

    def outputs(i, pe):
        for t in range(C_HEADS // 2):
            cols = slice(t * LANES, (t + 1) * LANES)
            rows = slice(2 * t * nq, 2 * (t + 1) * nq)
            vt = v_ref[i, :, :, cols].reshape(nk, LANES)
            od = _dot(pe[rows], jnp.concatenate([vt, ones], axis=1))
            o = od[:, :LANES] / od[:, LANES:]
            mix_ref[pl.ds(pl.multiple_of(i * nq, nq), nq), cols] = jnp.where(low, o[:nq], o[nq:]).astype(BF16)

    def two_sequences(j, carry):
        first, second = 2 * j, 2 * j + 1
        sh_a = all_scores(first)
        sh_b = all_scores(second)
        pe_a = softmax_numerators(sh_a)
        pe_b = softmax_numerators(sh_b)
        outputs(first, pe_a)
        outputs(second, pe_b)
        return carry

    n_seq = q_ref.shape[0]
    lax.fori_loop(0, n_seq // 2, two_sequences, 0)
    proj = _dot(mix_ref[...], wout_ref[...])
    o_ref[...] = h_ref[...] + proj.reshape(n_seq, nq, proj.shape[1])


def _attn_c(qkv4, rpb, h4, w_out, seqs_per_step=4):
    b, rows, w, d = h4.shape
    hw = C_HEADS * HEAD_DIM
    kr_n = min(C_WIN_ROWS, rows)
    nb = seqs_per_step
    table = _attn_c_table(rpb, rows)
    row_start = lambda r: jnp.clip(r - kr_n // 2, 0, rows - kr_n)
    el = pl.Element
    return pl.pallas_call(
        _attn_c_kernel,
        out_shape=jax.ShapeDtypeStruct((b, rows, w, d), F32),
        grid=(rows, b // nb),
        in_specs=[
            pl.BlockSpec((nb, None, w, hw), lambda r, bi: (bi, r, 0, 0)),
            pl.BlockSpec((el(nb), el(kr_n), el(w), el(hw)), lambda r, bi: (bi * nb, row_start(r), 0, hw)),
            pl.BlockSpec((el(nb), el(kr_n), el(w), el(hw)), lambda r, bi: (bi * nb, row_start(r), 0, 2 * hw)),
            pl.BlockSpec((None, C_HEADS // 2, 2 * w, kr_n * w),
                         lambda r, bi: (row_start(r) - r + C_WIN_ROWS - 1, 0, 0, 0)),
            pl.BlockSpec((nb, None, w, d), lambda r, bi: (bi, r, 0, 0)),
            _full_spec((hw, d)),
        ],
        out_specs=pl.BlockSpec((nb, None, w, d), lambda r, bi: (bi, r, 0, 0)),
        scratch_shapes=[pltpu.VMEM((nb * w, hw), BF16)],
        compiler_params=_cparams("arbitrary", "arbitrary"),
        name="attn_c",
    )(qkv4, qkv4, qkv4, table, h4, w_out.astype(BF16))


def _lane_cumsum(x, tri):
    e, s = x.shape
    off = jnp.zeros((e, 1), F32)
    outs = []
    for j in range(s // LANES):
        blk = x[:, j * LANES:(j + 1) * LANES]
        outs.append(_dot(blk.astype(BF16), tri) + off)
        off = off + jnp.sum(blk, axis=1, keepdims=True)
    return jnp.concatenate(outs, axis=1)


def _route_kernel(h_ref, g_ref, rwt_ref, tri_ref, xn_ref, slotr_ref, slotc_ref, gatec_ref, cnt_ref,
                  *, cap, chunk, tile):
    s = h_ref.shape[0]
    n_e = rwt_ref.shape[0] // 2
    logits = []
    for c in range(s // chunk):
        rows = slice(c * chunk, (c + 1) * chunk)
        xn = _rms(h_ref[rows, :], g_ref[...])
        xn_hi = xn.astype(BF16)
        xn_ref[rows, :] = xn_hi
        xn_lo = (xn - xn_hi.astype(F32)).astype(BF16)
        both = _dot_nt(rwt_ref[...], xn_hi)
        logits.append(both[:n_e] + both[n_e:] + _dot_nt(rwt_ref[:n_e, :], xn_lo))
    lg = jnp.concatenate(logits, axis=1)
    ex = jnp.exp(lg - jnp.max(lg, axis=0, keepdims=True))
    aff = ex / jnp.sum(ex, axis=0, keepdims=True)
    bits = pltpu.bitcast(aff, jnp.int32)
    thr = jnp.zeros((n_e, 1), jnp.int32)
    for shift in range(27, -1, -3):
        digit = jnp.zeros((n_e, 1), jnp.int32)
        for j in range(1, 8):
            cnt = jnp.sum(jnp.where(bits >= (thr | (j << shift)), 1.0, 0.0), axis=1, keepdims=True)
            digit = digit + jnp.where(cnt >= cap, 1, 0)
        thr = thr | (digit * (1 << shift))
    gt = bits > thr
    eq = bits == thr
    need = cap - jnp.sum(jnp.where(gt, 1.0, 0.0), axis=1, keepdims=True)
    eqf = jnp.where(eq, 1.0, 0.0)
    eq_rank = _lane_cumsum(eqf, tri_ref[...]) - eqf
    sel = gt | (eq & (eq_rank < need))
    self_ = jnp.where(sel, 1.0, 0.0)
    slot = jnp.where(sel, _lane_cumsum(self_, tri_ref[...]) - self_, -1.0)
    lane = lax.broadcasted_iota(jnp.int32, (n_e, LANES), 1)
    counts = jnp.zeros((n_e, LANES), F32)
    for i in range(s // tile):
        counts = jnp.where(lane == i, jnp.sum(self_[:, i * tile:(i + 1) * tile], axis=1, keepdims=True), counts)
    cnt_ref[...] = counts.astype(jnp.int32)
    pad_rows = LANES - n_e
    slot_p = jnp.concatenate([slot, jnp.full((pad_rows, s), -1.0, F32)], axis=0)
    gate_p = jnp.concatenate([jnp.where(sel, aff, 0.0), jnp.zeros((pad_rows, s), F32)], axis=0)
    slotr_ref[...] = slot
    for j in range(s // LANES):
        cols = slice(j * LANES, (j + 1) * LANES)
        slotc_ref[cols, :] = slot_p[:, cols].T.astype(BF16)
        gatec_ref[cols, :] = gate_p[:, cols].T.astype(BF16)


def _route(h3, g, router_w, cap):
    b, s, d = h3.shape
    n_e = router_w.shape[1]
    tri = jnp.asarray(np.triu(np.ones((LANES, LANES))), BF16)
    rw_t = router_w.T.astype(F32)
    rw_hi = rw_t.astype(BF16)
    rw_lo = (rw_t - rw_hi.astype(F32)).astype(BF16)
    per_seq = lambda *shape: pl.BlockSpec((None,) + shape, lambda i: (i,) + (0,) * len(shape))
    return pl.pallas_call(
        functools.partial(_route_kernel, cap=cap, chunk=256, tile=TOKEN_TILE),
        out_shape=(
            jax.ShapeDtypeStruct((b, s, d), BF16),
            jax.ShapeDtypeStruct((b, n_e, s), F32),
            jax.ShapeDtypeStruct((b, s, LANES), BF16),
            jax.ShapeDtypeStruct((b, s, LANES), BF16),
            jax.ShapeDtypeStruct((b, n_e, LANES), jnp.int32),
        ),
        grid=(b,),
        in_specs=[per_seq(s, d), _full_spec((1, d)), _full_spec((2 * n_e, d)), _full_spec((LANES, LANES))],
        out_specs=(per_seq(s, d), per_seq(n_e, s), per_seq(s, LANES), per_seq(s, LANES), per_seq(n_e, LANES)),
        compiler_params=_cparams("parallel"),
        name="route",
    )(h3, g.reshape(1, d), jnp.concatenate([rw_hi, rw_lo], axis=0), tri)


def _chunk_tables(cnt, cap):
    n_e = cnt.shape[1]
    kmax = n_e * (TOKEN_TILE // CHUNK + 1)
    kmax = -(-kmax // CHUNKS_PER_GROUP) * CHUNKS_PER_GROUP
    c0 = jnp.cumsum(cnt, axis=2) - cnt
    a0 = c0 // CHUNK
    a1 = jnp.where(cnt > 0, (c0 + cnt + CHUNK - 1) // CHUNK, a0)
    nch = (a1 - a0).transpose(0, 2, 1)
    a0 = a0.transpose(0, 2, 1)
    pos_end = jnp.cumsum(nch, axis=2)
    pos = pos_end - nch
    total = pos_end[..., -1]
    k = jnp.arange(kmax, dtype=jnp.int32)
    e_of_k = jnp.sum((k[None, None, :, None] >= pos_end[:, :, None, :]).astype(jnp.int32), axis=-1)
    e_of_k = jnp.minimum(e_of_k, n_e - 1)
    is_e = e_of_k[..., None] == jnp.arange(n_e, dtype=jnp.int32)
    q = k + jnp.sum(jnp.where(is_e, (a0 - pos)[:, :, None, :], 0), axis=-1)
    valid = k[None, None, :] < total[..., None]
    e_tab = jnp.where(valid, e_of_k, 0)
    slot_tab = jnp.where(valid, q * CHUNK, -(1 << 20))
    dst_tab = jnp.where(valid, e_of_k * cap + q * CHUNK, 0)
    groups = (total + CHUNKS_PER_GROUP - 1) // CHUNKS_PER_GROUP
    return jnp.concatenate([e_tab, slot_tab, dst_tab, groups[..., None]], axis=-1).astype(jnp.int32), kmax


def _gather_kernel(tab_ref, xn_ref, slotr_ref, xs_ref, *, kmax):
    @pl.when(pl.program_id(1) == 0)
    def _():
        xs_ref[...] = jnp.zeros_like(xs_ref)

    sub = lax.broadcasted_iota(jnp.int32, (CHUNK, TOKEN_TILE), 0).astype(F32)

    def gathered_rows(tile, g):
        tokens = slice(tile * TOKEN_TILE, (tile + 1) * TOKEN_TILE)
        base = g * CHUNKS_PER_GROUP
        onehot = []
        for c in range(CHUNKS_PER_GROUP):
            slots = slotr_ref[pl.ds(tab_ref[tile, base + c], 1), tokens]
            want = sub + tab_ref[tile, kmax + base + c].astype(F32)
            onehot.append(jnp.where(slots == want, 1.0, 0.0).astype(BF16))
        return _dot(jnp.concatenate(onehot, axis=0), xn_ref[tokens, :]).astype(BF16)

    def accumulate(tile, g, rows):
        base = g * CHUNKS_PER_GROUP
        for c0 in range(0, CHUNKS_PER_GROUP, 4):
            cs = range(c0, c0 + 4)
            dsts = [pl.ds(pl.multiple_of(tab_ref[tile, 2 * kmax + base + c], CHUNK), CHUNK) for c in cs]
            sums = [xs_ref[dst, :] + rows[c * CHUNK:(c + 1) * CHUNK, :] for c, dst in zip(cs, dsts)]
            for dst, total in reversed(list(zip(dsts, sums))):
                xs_ref[dst, :] = total

    pending = None
    for tile in range(TILES_PER_STEP):
        rows = gathered_rows(tile, 0)
        if pending is not None:
            accumulate(tile - 1, 0, pending)
        pending = rows
    accumulate(TILES_PER_STEP - 1, 0, pending)
    for tile in range(TILES_PER_STEP):
        def more(g, carry, tile=tile):
            accumulate(tile, g, gathered_rows(tile, g))
            return carry
        lax.fori_loop(1, tab_ref[tile, 3 * kmax], more, 0)


def _expert_kernel(xs_ref, wg_ref, wu_ref, wd_ref, y_ref, wg_bf, wu_bf, wd_bf):
    @pl.when(pl.program_id(1) == 0)
    def _():
        wg_bf[...] = wg_ref[...].astype(BF16)
        wu_bf[...] = wu_ref[...].astype(BF16)
        wd_bf[...] = wd_ref[...].astype(BF16)

    nb, cap, d = xs_ref.shape
    xs = xs_ref[...].reshape(nb * cap, d)
    hg = _dot(xs, wg_bf[...])
    hu = _dot(xs, wu_bf[...])
    hdn = (hg * jax.nn.sigmoid(hg) * hu).astype(BF16)
    y_ref[...] = _dot(hdn, wd_bf[...]).astype(BF16).reshape(nb, cap, d)


def _scatter_ple_kernel(tab_ref, y_ref, slotc_ref, gatec_ref, h_ref, p_ref, g_ref, wg_ref, wp_ref, o_ref,
                        acc_ref, *, kmax):
    lane = lax.broadcasted_iota(jnp.int32, (1, GROUP), 1)
    sub = lax.broadcasted_iota(jnp.int32, (LANES, GROUP), 0)

    def group_operands(tile, g):
        tokens = slice(tile * TOKEN_TILE, (tile + 1) * TOKEN_TILE)
        base = g * CHUNKS_PER_GROUP
        expert = jnp.zeros((1, GROUP), jnp.int32)
        want = jnp.zeros((1, GROUP), jnp.int32)
        chunks = []
        for c in range(CHUNKS_PER_GROUP):
            in_chunk = (lane >= c * CHUNK) & (lane < (c + 1) * CHUNK)
            expert = jnp.where(in_chunk, tab_ref[tile, base + c], expert)
            want = jnp.where(in_chunk, tab_ref[tile, kmax + base + c] + lane - c * CHUNK, want)
            chunks.append(y_ref[pl.ds(pl.multiple_of(tab_ref[tile, 2 * kmax + base + c], CHUNK), CHUNK), :])
        pick = jnp.where(sub == expert, 1.0, 0.0).astype(BF16)
        both = _dot(jnp.concatenate([slotc_ref[tokens, :], gatec_ref[tokens, :]], axis=0), pick)
        weights = jnp.where(both[:TOKEN_TILE] == want.astype(F32), both[TOKEN_TILE:], 0.0).astype(BF16)
        return weights, jnp.concatenate(chunks, axis=0)

    def add_group(tile, operands, first=False):
        weights, rows = operands
        tokens = slice(tile * TOKEN_TILE, (tile + 1) * TOKEN_TILE)
        acc_ref[tokens, :] = (h_ref[tokens, :] if first else acc_ref[tokens, :]) + _dot(weights, rows)

    pending = None
    for tile in range(TILES_PER_STEP):
        operands = group_operands(tile, 0)
        if pending is not None:
            add_group(tile - 1, pending, first=True)
        pending = operands
    add_group(TILES_PER_STEP - 1, pending, first=True)
    for tile in range(TILES_PER_STEP):
        def more(g, carry, tile=tile):
            add_group(tile, group_operands(tile, g))
            return carry
        lax.fori_loop(1, tab_ref[tile, 3 * kmax], more, 0)
    h = acc_ref[...]
    xn = _rms(h, g_ref[...]).astype(BF16)
    gate = jax.nn.sigmoid(_dot(xn, wg_ref[...]))
    o_ref[...] = h + gate * _dot(p_ref[...].astype(BF16), wp_ref[...])


def _expert_ffn_ple(h3, xn3, slot_row, slot_col, gate_col, cnt, w_gate, w_up, w_down, cap, p3, layer, ple_g, ple_wg,
                    ple_wp, seqs_per_step=4):
    b, s, d = xn3.shape
    pd = p3.shape[2]
    _, n_e, _, ff = w_gate.shape
    step_rows = TILES_PER_STEP * TOKEN_TILE
    nt = s // step_rows
    tables, kmax = _chunk_tables(cnt[:, :, :s // TOKEN_TILE], cap)
    tables = tables.reshape(b * nt, TILES_PER_STEP, 3 * kmax + 1)
    tab_spec = pl.BlockSpec((None, TILES_PER_STEP, 3 * kmax + 1), lambda i, t: (i * nt + t, 0, 0),
                            memory_space=pltpu.SMEM)
    xs = pl.pallas_call(
        functools.partial(_gather_kernel, kmax=kmax),
        out_shape=jax.ShapeDtypeStruct((b, n_e * cap, d), BF16),
        grid=(b, nt),
        in_specs=[
            tab_spec,
            pl.BlockSpec((None, step_rows, d), lambda i, t: (i, t, 0)),
            pl.BlockSpec((None, n_e, step_rows), lambda i, t: (i, 0, t)),
        ],
        out_specs=pl.BlockSpec((None, n_e * cap, d), lambda i, t: (i, 0, 0)),
        compiler_params=_cparams("parallel", "arbitrary"),
        name="expert_gather",
    )(tables, xn3, slot_row)
    nb = seqs_per_step
    y = pl.pallas_call(
        _expert_kernel,
        out_shape=jax.ShapeDtypeStruct((b, n_e, cap, d), BF16),
        grid=(n_e, b // nb),
        in_specs=[
            pl.BlockSpec((nb, None, cap, d), lambda e, i: (i, e, 0, 0)),
            pl.BlockSpec((None, None, d, ff), lambda e, i: (layer, e, 0, 0)),
            pl.BlockSpec((None, None, d, ff), lambda e, i: (layer, e, 0, 0)),
            pl.BlockSpec((None, None, ff, d), lambda e, i: (layer, e, 0, 0)),
        ],
        out_specs=pl.BlockSpec((nb, None, cap, d), lambda e, i: (i, e, 0, 0)),
        scratch_shapes=[pltpu.VMEM((d, ff), BF16), pltpu.VMEM((d, ff), BF16), pltpu.VMEM((ff, d), BF16)],
        compiler_params=_cparams("parallel", "arbitrary"),
        name="expert_mlp",
    )(xs.reshape(b, n_e, cap, d), w_gate, w_up, w_down)
    tile_rows = lambda width: pl.BlockSpec((None, step_rows, width), lambda i, t: (i, t, 0))
    return pl.pallas_call(
        functools.partial(_scatter_ple_kernel, kmax=kmax),
        out_shape=jax.ShapeDtypeStruct((b, s, d), F32),
        grid=(b, nt),
        in_specs=[
            tab_spec,
            pl.BlockSpec((None, n_e * cap, d), lambda i, t: (i, 0, 0)),
            tile_rows(LANES), tile_rows(LANES), tile_rows(d),
            pl.BlockSpec((None, step_rows, pd), lambda i, t: (layer, i * nt + t, 0)),
            _full_spec((1, d)), _full_spec((d, d)), _full_spec((pd, d)),
        ],
        out_specs=tile_rows(d),
        scratch_shapes=[pltpu.VMEM((step_rows, d), F32)],
        compiler_params=_cparams("parallel", "arbitrary"),
        name="expert_scatter_ple",
    )(tables, y.reshape(b, n_e * cap, d), slot_col, gate_col, h3, p3, ple_g.reshape(1, d), ple_wg.astype(BF16),
      ple_wp.astype(BF16))


def kernel(x, p, norm_mix_g, norm_ffn_g, w_out, router_w, exp_w_gate, exp_w_up, exp_w_down, ple_norm_g, ple_gate_w, ple_proj_w, a_w_in, a_vnorm_g, a_w_s, a_b_s, b_w_in, b_qnorm_g, b_knorm_g, b_sink, c_w_in, c_qnorm_g, c_knorm_g, c_rpb):
    b, s, d = x.shape
    depth = norm_mix_g.shape[0]
    t = b * s
    cap = max(1, EC_CAPACITY_FACTOR * s // N_EXPERTS)
    scale = HEAD_DIM ** -0.5
    h = x.reshape(t, d)
    for i in range(depth):
        kind = i % N_MIXERS
        j = i // N_MIXERS
        if kind == 0:
            h = _mixer_a(h, norm_mix_g[i], a_w_in[j], a_vnorm_g[j], a_w_s[j], a_b_s[j], w_out[i])
        elif kind == 1:
            gain = jnp.concatenate([jnp.tile(b_qnorm_g[j] * scale, B_HEADS), jnp.tile(b_knorm_g[j], 2 * B_KV_HEADS)])
            qw, kw = B_HEADS * HEAD_DIM, B_KV_HEADS * HEAD_DIM
            w_b = jnp.concatenate([b_w_in[j][:, :qw], _dup_heads(b_w_in[j][:, qw:qw + kw], B_KV_HEADS),
                                   _dup_heads(b_w_in[j][:, qw + kw:], B_KV_HEADS)], axis=1)
            qkv = _norm_proj(h, norm_mix_g[i], w_b, gain, qw + 2 * kw)
            h = _attn_b(qkv.reshape(b, s, -1), b_sink[j], h.reshape(b, s, d), w_out[i]).reshape(t, d)
        else:
            gain = jnp.concatenate([jnp.tile(c_qnorm_g[j] * scale, C_HEADS), jnp.tile(c_knorm_g[j], C_HEADS)])
            qkv = _norm_proj(h, norm_mix_g[i], c_w_in[j], gain, 2 * C_HEADS * HEAD_DIM)
            h = _attn_c(qkv.reshape(b, s // GRID_W, GRID_W, -1), c_rpb[j], h.reshape(b, s // GRID_W, GRID_W, d),
                        w_out[i]).reshape(t, d)
        xn, slot_row, slot_col, gate_col, cnt = _route(h.reshape(b, s, d), norm_ffn_g[i], router_w[i], cap)
        h = _expert_ffn_ple(h.reshape(b, s, d), xn, slot_row, slot_col, gate_col, cnt, exp_w_gate, exp_w_up,
                            exp_w_down, cap, p.reshape(depth, t, -1), i, ple_norm_g[i], ple_gate_w[i],
                            ple_proj_w[i]).reshape(t, d)
    return h.reshape(b, s, d)
```

```python
import functools

import numpy as np
import jax
import jax.numpy as jnp
from jax import lax
from jax.experimental import pallas as pl
from jax.experimental.pallas import tpu as pltpu

F32 = jnp.float32
BF16 = jnp.bfloat16

RMS_EPS = 1e-6
LN_EPS = 1e-5
NEG = -1e30
HEAD_DIM = 64
GRID_W = 64
A_GROUPS = 8
A_CHUNK = 128
B_HEADS = 16
B_KV_HEADS = 4
B_BLOCK = 128
C_HEADS = 16
C_WIN_ROWS = 8
C_WIN_COLS = 16
N_EXPERTS = 16
EC_CAPACITY_FACTOR = 2
N_MIXERS = 3

LANES = 128
MXU_DIM = 256
VMEM_LIMIT_BYTES = 60 * 1024 * 1024

TOKEN_TILE = 256
TILES_PER_STEP = 4
A_SUBTILES = 2
PROJ_SUBTILES = 2
CHUNK = 16
GROUP = 4 * MXU_DIM
CHUNKS_PER_GROUP = GROUP // CHUNK


def _cparams(*sem):
    return pltpu.CompilerParams(dimension_semantics=sem, vmem_limit_bytes=VMEM_LIMIT_BYTES)


def _rms(xf, g):
    return xf * lax.rsqrt(jnp.mean(xf * xf, axis=-1, keepdims=True) + RMS_EPS) * g


def _dot(a, b):
    return jnp.dot(a, b, preferred_element_type=F32)


def _dot_nt(a, b):
    return lax.dot_general(a, b, (((1,), (1,)), ((), ())), preferred_element_type=F32)


def _full_spec(shape):
    nd = len(shape)
    return pl.BlockSpec(shape, lambda *_: (0,) * nd)


def _mixer_a_kernel(h_ref, g_ref, win_ref, vg_ref, ws_ref, bias_ref, wout_ref, o_ref, mix_ref):
    tm = h_ref.shape[0]
    width = vg_ref.shape[1]
    gw = width // A_GROUPS
    sub = tm // A_SUBTILES
    halves = [slice(i * sub, (i + 1) * sub) for i in range(A_SUBTILES)]
    zs = [_dot(_rms(h_ref[hs, :], g_ref[...]).astype(BF16), win_ref[...]) for hs in halves]
    us, vns = [], []
    for z in zs:
        z = 0.5 * z * (1.0 + jnp.tanh(np.sqrt(2.0 / np.pi).astype(np.float32) * (z + 0.044715 * (z * z * z))))
        v = z[:, width:]
        mu = jnp.mean(v, axis=-1, keepdims=True)
        vc = v - mu
        var = jnp.mean(vc * vc, axis=-1, keepdims=True)
        us.append(z[:, :width])
        vns.append((vc * lax.rsqrt(var + LN_EPS) * vg_ref[...]).astype(BF16))
    for hs, u, vn in zip(halves, us, vns):
        for c in range(sub // A_CHUNK):
            rows = slice(c * A_CHUNK, (c + 1) * A_CHUNK)
            out_rows = slice(hs.start + c * A_CHUNK, hs.start + (c + 1) * A_CHUNK)
            for g in range(A_GROUPS):
                cols = slice(g * gw, (g + 1) * gw)
                s = _dot(ws_ref[g], vn[rows, cols]) + bias_ref[:, cols]
                mix_ref[out_rows, cols] = (u[rows, cols] * s).astype(BF16)
    o_ref[...] = h_ref[...] + _dot(mix_ref[...], wout_ref[...])


def _mixer_a(h2, g, w_in, vnorm_g, w_s, b_s, w_out, tm=1024):
    t, d = h2.shape
    width = vnorm_g.shape[0]
    gw = width // A_GROUPS
    bias = jnp.repeat(b_s.T.astype(F32), gw, axis=1)
    return pl.pallas_call(
        _mixer_a_kernel,
        out_shape=jax.ShapeDtypeStruct((t, d), F32),
        grid=(t // tm,),
        in_specs=[
            pl.BlockSpec((tm, d), lambda i: (i, 0)),
            _full_spec((1, d)),
            _full_spec((d, 2 * width)),
            _full_spec((1, width)),
            _full_spec((A_GROUPS, A_CHUNK, A_CHUNK)),
            _full_spec((A_CHUNK, width)),
            _full_spec((width, d)),
        ],
        out_specs=pl.BlockSpec((tm, d), lambda i: (i, 0)),
        scratch_shapes=[pltpu.VMEM((tm, width), BF16)],
        compiler_params=_cparams("parallel"),
        name="mixer_a",
    )(h2, g.reshape(1, d), w_in.astype(BF16), vnorm_g.reshape(1, width), w_s.astype(BF16), bias,
      w_out.astype(BF16))


def _norm_proj_kernel(h_ref, g_ref, w_ref, hg_ref, bd_ref, o_ref, *, n_norm_cols):
    sub = h_ref.shape[0] // PROJ_SUBTILES
    parts = [slice(i * sub, (i + 1) * sub) for i in range(PROJ_SUBTILES)]
    accs = [_dot(_rms(h_ref[rows, :], g_ref[...]).astype(BF16), w_ref[...]) for rows in parts]
    for rows, acc in zip(parts, accs):
        for j in range(n_norm_cols // MXU_DIM):
            cols = slice(j * MXU_DIM, (j + 1) * MXU_DIM)
            blk = acc[:, cols]
            ss = _dot((blk * blk).astype(BF16), bd_ref[...])
            o_ref[rows, cols] = (blk * lax.rsqrt(ss * (1.0 / HEAD_DIM) + RMS_EPS) * hg_ref[:, cols]).astype(BF16)
        o_ref[rows, n_norm_cols:] = acc[:, n_norm_cols:].astype(BF16)


def _norm_proj(h2, g, w, head_gain, n_norm_cols, tm=1024):
    t, d = h2.shape
    n = w.shape[1]
    blockdiag = jnp.asarray(np.kron(np.eye(MXU_DIM // HEAD_DIM), np.ones((HEAD_DIM, HEAD_DIM))), BF16)
    return pl.pallas_call(
        functools.partial(_norm_proj_kernel, n_norm_cols=n_norm_cols),
        out_shape=jax.ShapeDtypeStruct((t, n), BF16),
        grid=(t // tm,),
        in_specs=[
            pl.BlockSpec((tm, d), lambda i: (i, 0)),
            _full_spec((1, d)),
            _full_spec((d, n)),
            _full_spec((1, n_norm_cols)),
            _full_spec((MXU_DIM, MXU_DIM)),
        ],
        out_specs=pl.BlockSpec((tm, n), lambda i: (i, 0)),
        compiler_params=_cparams("parallel"),
        name="norm_proj",
    )(h2, g.reshape(1, d), w.astype(BF16), head_gain.reshape(1, n_norm_cols).astype(F32), blockdiag)


def _alibi_slopes(n):
    return np.array([2.0 ** (-8.0 * (h + 1) / n) for h in range(n)], dtype=np.float32)


def _attn_b_tables():
    span = 3 * B_BLOCK
    rel = np.arange(span)[None, :] - B_BLOCK - np.arange(B_BLOCK)[:, None]
    in_window = np.abs(rel) <= B_BLOCK
    alibi = (-_alibi_slopes(B_HEADS)[:, None, None] * np.abs(rel)[None]).astype(np.float32)
    kblk = np.arange(span) // B_BLOCK
    tabs = []
    for kind in range(3):
        valid = in_window & ~((kind == 0) & (kblk == 0))[None, :] & ~((kind == 2) & (kblk == 2))[None, :]
        tabs.append(np.where(valid[None], alibi, np.float32(NEG)))
    return np.stack(tabs).astype(np.float32)


def _attn_b_kernel(sink_ref, q_ref, kp_ref, kc_ref, kn_ref, vp_ref, vc_ref, vn_ref, tab_ref, h_ref, wout_ref, o_ref,
                   mix_ref):
    grp = B_HEADS // B_KV_HEADS
    nq = q_ref.shape[1]
    lane = lax.broadcasted_iota(jnp.int32, (1, LANES), 1)
    low = lane < HEAD_DIM
    span = kp_ref.shape[1] + kc_ref.shape[1] + kn_ref.shape[1]
    heads = [slice(h * nq, (h + 1) * nq) for h in range(B_HEADS)]
    sum_low = jnp.broadcast_to(jnp.where(low, 1.0, 0.0), (span, LANES)).astype(BF16)
    sum_high = jnp.broadcast_to(jnp.where(low, 0.0, 1.0), (span, LANES)).astype(BF16)

    def half_lanes(x):
        zero = jnp.zeros_like(x)
        return jnp.where(low, x, zero), jnp.where(low, zero, x)

    def all_scores(i):
        scores = []
        for kh in range(B_KV_HEADS):
            tile = slice(kh * LANES, (kh + 1) * LANES)
            kt = jnp.concatenate([kp_ref[i, :, tile], kc_ref[i, :, tile], kn_ref[i, :, tile]], axis=0)
            kk = jnp.concatenate(half_lanes(kt), axis=0)
            for local in range(grp // 2):
                qt = kh * grp // 2 + local
                s2 = _dot_nt(q_ref[i, :, qt * LANES:(qt + 1) * LANES], kk)
                scores.append(s2[:, :span] + tab_ref[kh, (2 * local) * nq:(2 * local + 1) * nq, :])
                scores.append(s2[:, span:] + tab_ref[kh, (2 * local + 1) * nq:(2 * local + 2) * nq, :])
        return jnp.concatenate(scores, axis=0)

    def softmax_numerators(sh):
        rowmax = jnp.broadcast_to(jnp.max(sh, axis=-1, keepdims=True), (sh.shape[0], LANES))
        m = jnp.concatenate([jnp.maximum(rowmax[hs], sink_ref[h]) for h, hs in enumerate(heads)], axis=0)
        sink_term = jnp.concatenate([jnp.exp(sink_ref[h] - m[hs]) for h, hs in enumerate(heads)], axis=0)
        return jnp.exp(sh - jnp.concatenate([m] * (span // LANES), axis=1)).astype(BF16), sink_term

    def outputs(i, pe, sink_term):
        seq_rows = slice(i * nq, (i + 1) * nq)
        for kh in range(B_KV_HEADS):
            tile = slice(kh * LANES, (kh + 1) * LANES)
            vt = jnp.concatenate([vp_ref[i, :, tile], vc_ref[i, :, tile], vn_ref[i, :, tile]], axis=0)
            v_low, v_high = half_lanes(vt)
            vv = jnp.concatenate([jnp.concatenate([v_low, sum_low], axis=1),
                                  jnp.concatenate([v_high, sum_high], axis=1)], axis=0)
            for local in range(grp // 2):
                qt = kh * grp // 2 + local
                even, odd = heads[2 * qt], heads[2 * qt + 1]
                od = _dot(jnp.concatenate([pe[even], pe[odd]], axis=1), vv)
                sink = jnp.where(low, sink_term[even], sink_term[odd])
                mix_ref[seq_rows, qt * LANES:(qt + 1) * LANES] = (od[:, :LANES] / (od[:, LANES:] + sink)).astype(BF16)

    n_seq = q_ref.shape[0]
    sh, pe = {}, {}
    for step in range(n_seq + 2):
        if step < n_seq:
            sh[step] = all_scores(step)
        if 1 <= step <= n_seq:
            pe[step - 1] = softmax_numerators(sh.pop(step - 1))
        if step >= 2:
            outputs(step - 2, *pe.pop(step - 2))
    proj = _dot(mix_ref[...], wout_ref[...])
    o_ref[...] = h_ref[...] + proj.reshape(n_seq, nq, proj.shape[1])


def _dup_heads(w, n_heads):
    d = w.shape[0]
    return jnp.repeat(w.reshape(d, n_heads, 1, HEAD_DIM), 2, axis=2).reshape(d, n_heads * LANES)


def _attn_b(qkv3, sink, h3, w_out, seqs_per_step=4):
    b, s, d = h3.shape
    nb = s // B_BLOCK
    ns = seqs_per_step
    grp = B_HEADS // B_KV_HEADS
    qw = B_HEADS * HEAD_DIM
    kvw = B_KV_HEADS * LANES
    kcol = qw // kvw
    vcol = kcol + 1
    tabs = jnp.asarray(_attn_b_tables().reshape(3, B_KV_HEADS, grp * B_BLOCK, 3 * B_BLOCK))
    prev = lambda n: jnp.maximum(n - 1, 0)
    nxt = lambda n: jnp.minimum(n + 1, nb - 1)
    kind = lambda n: jnp.where(n == 0, 0, jnp.where(n == nb - 1, 2, 1))
    kv = lambda col, f: pl.BlockSpec((ns, B_BLOCK, kvw), lambda bi, n: (bi, f(n), col))
    same = lambda n: n
    return pl.pallas_call(
        _attn_b_kernel,
        out_shape=jax.ShapeDtypeStruct((b, s, d), F32),
        grid=(b // ns, nb),
        in_specs=[
            pl.BlockSpec(memory_space=pltpu.SMEM),
            pl.BlockSpec((ns, B_BLOCK, qw), lambda bi, n: (bi, n, 0)),
            kv(kcol, prev), kv(kcol, same), kv(kcol, nxt),
            kv(vcol, prev), kv(vcol, same), kv(vcol, nxt),
            pl.BlockSpec((None, B_KV_HEADS, grp * B_BLOCK, 3 * B_BLOCK), lambda bi, n: (kind(n), 0, 0, 0)),
            pl.BlockSpec((ns, B_BLOCK, d), lambda bi, n: (bi, n, 0)),
            _full_spec((qw, d)),
        ],
        out_specs=pl.BlockSpec((ns, B_BLOCK, d), lambda bi, n: (bi, n, 0)),
        scratch_shapes=[pltpu.VMEM((ns * B_BLOCK, qw), BF16)],
        compiler_params=_cparams("parallel", "arbitrary"),
        name="attn_b",
    )(sink.astype(F32), qkv3, qkv3, qkv3, qkv3, qkv3, qkv3, qkv3, tabs, h3, w_out.astype(BF16))


def _attn_c_table(rpb, rows):
    w = GRID_W
    kr_n = min(C_WIN_ROWS, rows)
    c = np.arange(w)
    cs = np.clip(c - C_WIN_COLS // 2, 0, w - C_WIN_COLS)
    kcol = np.arange(w)
    colmask = (kcol[None, :] >= cs[:, None]) & (kcol[None, :] < cs[:, None] + C_WIN_COLS)
    n_off = 2 * C_WIN_ROWS - kr_n
    n_h, n_ri, _ = rpb.shape
    pad = w - C_WIN_COLS
    padded = jnp.pad(rpb.astype(F32), ((0, 0), (0, 0), (pad, pad)))
    bias = jnp.stack([padded[:, :, w - 1 - qc:2 * w - 1 - qc] for qc in range(w)], axis=1)
    bias = jnp.where(jnp.asarray(colmask)[None, :, None, :], bias, NEG)
    bias = bias.reshape(n_h, w, n_ri * w)
    slabs = jnp.stack([bias[:, :, n * w:(n + kr_n) * w] for n in range(n_off)])
    return slabs.reshape(n_off, n_h // 2, 2 * w, kr_n * w)


def _attn_c_kernel(q_ref, k_ref, v_ref, tab_ref, h_ref, wout_ref, o_ref, mix_ref):
    nk = k_ref.shape[1] * k_ref.shape[2]
    nq = q_ref.shape[1]
    lane = lax.broadcasted_iota(jnp.int32, (1, LANES), 1)
    low = lane < HEAD_DIM
    ones = jnp.ones((nk, LANES), BF16)

    def all_scores(i):
        scores = []
        for t in range(C_HEADS // 2):
            cols = slice(t * LANES, (t + 1) * LANES)
            q2 = q_ref[i, :, cols]
            zero = jnp.zeros_like(q2)
            qq = jnp.concatenate([jnp.where(low, q2, zero), jnp.where(low, zero, q2)], axis=0)
            kt = k_ref[i, :, :, cols].reshape(nk, LANES)
            scores.append(_dot_nt(qq, kt) + tab_ref[t])
        return jnp.concatenate(scores, axis=0)

    def softmax_numerators(sh):
        m = jnp.broadcast_to(jnp.max(sh, axis=-1, keepdims=True), (sh.shape[0], LANES))
        return jnp.exp(sh - jnp.concatenate([m] * (nk // LANES), axis=1)).astype(BF16)

    def outputs(i, pe):
        for t in range(C_HEADS // 2):
            cols = slice(t * LANES, (t + 1) * LANES)
            rows = slice(2 * t * nq, 2 * (t + 1) * nq)
            vt = v_ref[i, :, :, cols].reshape(nk, LANES)
            od = _dot(pe[rows], jnp.concatenate([vt, ones], axis=1))
            o = od[:, :LANES] / od[:, LANES:]
            mix_ref[i * nq:(i + 1) * nq, cols] = jnp.where(low, o[:nq], o[nq:]).astype(BF16)

    n_seq = q_ref.shape[0]
    sh, pe = {}, {}
    for step in range(n_seq + 2):
        if step < n_seq:
            sh[step] = all_scores(step)
        if 1 <= step <= n_seq:
            pe[step - 1] = softmax_numerators(sh.pop(step - 1))
        if step >= 2:
            outputs(step - 2, pe.pop(step - 2))
    proj = _dot(mix_ref[...], wout_ref[...])
    o_ref[...] = h_ref[...] + proj.reshape(n_seq, nq, proj.shape[1])


def _attn_c(qkv4, rpb, h4, w_out, seqs_per_step=4):
    b, rows, w, d = h4.shape
    hw = C_HEADS * HEAD_DIM
    kr_n = min(C_WIN_ROWS, rows)
    nb = seqs_per_step
    table = _attn_c_table(rpb, rows)
    row_start = lambda r: jnp.clip(r - kr_n // 2, 0, rows - kr_n)
    el = pl.Element
    return pl.pallas_call(
        _attn_c_kernel,
        out_shape=jax.ShapeDtypeStruct((b, rows, w, d), F32),
        grid=(rows, b // nb),
        in_specs=[
            pl.BlockSpec((nb, None, w, hw), lambda r, bi: (bi, r, 0, 0)),
            pl.BlockSpec((el(nb), el(kr_n), el(w), el(hw)), lambda r, bi: (bi * nb, row_start(r), 0, hw)),
            pl.BlockSpec((el(nb), el(kr_n), el(w), el(hw)), lambda r, bi: (bi * nb, row_start(r), 0, 2 * hw)),
            pl.BlockSpec((None, C_HEADS // 2, 2 * w, kr_n * w),
                         lambda r, bi: (row_start(r) - r + C_WIN_ROWS - 1, 0, 0, 0)),
            pl.BlockSpec((nb, None, w, d), lambda r, bi: (bi, r, 0, 0)),
            _full_spec((hw, d)),
        ],
        out_specs=pl.BlockSpec((nb, None, w, d), lambda r, bi: (bi, r, 0, 0)),
        scratch_shapes=[pltpu.VMEM((nb * w, hw), BF16)],
        compiler_params=_cparams("arbitrary", "arbitrary"),
        name="attn_c",
    )(qkv4, qkv4, qkv4, table, h4, w_out.astype(BF16))


def _lane_cumsum(x, tri):
    e, s = x.shape
    off = jnp.zeros((e, 1), F32)
    outs = []
    for j in range(s // LANES):
        blk = x[:, j * LANES:(j + 1) * LANES]
        outs.append(_dot(blk.astype(BF16), tri) + off)
        off = off + jnp.sum(blk, axis=1, keepdims=True)
    return jnp.concatenate(outs, axis=1)


def _route_kernel(h_ref, g_ref, rwt_ref, tri_ref, xn_ref, slotr_ref, slotc_ref, gatec_ref, cnt_ref,
                  *, cap, chunk, tile):
    s = h_ref.shape[0]
    n_e = rwt_ref.shape[0] // 2
    logits = []
    for c in range(s // chunk):
        rows = slice(c * chunk, (c + 1) * chunk)
        xn = _rms(h_ref[rows, :], g_ref[...])
        xn_hi = xn.astype(BF16)
        xn_ref[rows, :] = xn_hi
        xn_lo = (xn - xn_hi.astype(F32)).astype(BF16)
        both = _dot_nt(rwt_ref[...], xn_hi)
        logits.append(both[:n_e] + both[n_e:] + _dot_nt(rwt_ref[:n_e, :], xn_lo))
    lg = jnp.concatenate(logits, axis=1)
    ex = jnp.exp(lg - jnp.max(lg, axis=0, keepdims=True))
    aff = ex / jnp.sum(ex, axis=0, keepdims=True)
    bits = pltpu.bitcast(aff, jnp.int32)
    thr = jnp.zeros((n_e, 1), jnp.int32)
    for shift in range(27, -1, -3):
        digit = jnp.zeros((n_e, 1), jnp.int32)
        for j in range(1, 8):
            cnt = jnp.sum(jnp.where(bits >= (thr | (j << shift)), 1.0, 0.0), axis=1, keepdims=True)
            digit = digit + jnp.where(cnt >= cap, 1, 0)
        thr = thr | (digit * (1 << shift))
    gt = bits > thr
    eq = bits == thr
    need = cap - jnp.sum(jnp.where(gt, 1.0, 0.0), axis=1, keepdims=True)
    eqf = jnp.where(eq, 1.0, 0.0)
    eq_rank = _lane_cumsum(eqf, tri_ref[...]) - eqf
    sel = gt | (eq & (eq_rank < need))
    self_ = jnp.where(sel, 1.0, 0.0)
    slot = jnp.where(sel, _lane_cumsum(self_, tri_ref[...]) - self_, -1.0)
    lane = lax.broadcasted_iota(jnp.int32, (n_e, LANES), 1)
    counts = jnp.zeros((n_e, LANES), F32)
    for i in range(s // tile):
        counts = jnp.where(lane == i, jnp.sum(self_[:, i * tile:(i + 1) * tile], axis=1, keepdims=True), counts)
    cnt_ref[...] = counts.astype(jnp.int32)
    pad_rows = LANES - n_e
    slot_p = jnp.concatenate([slot, jnp.full((pad_rows, s), -1.0, F32)], axis=0)
    gate_p = jnp.concatenate([jnp.where(sel, aff, 0.0), jnp.zeros((pad_rows, s), F32)], axis=0)
    slotr_ref[...] = slot
    for j in range(s // LANES):
        cols = slice(j * LANES, (j + 1) * LANES)
        slotc_ref[cols, :] = slot_p[:, cols].T.astype(BF16)
        gatec_ref[cols, :] = gate_p[:, cols].T.astype(BF16)


def _route(h3, g, router_w, cap):
    b, s, d = h3.shape
    n_e = router_w.shape[1]
    tri = jnp.asarray(np.triu(np.ones((LANES, LANES))), BF16)
    rw_t = router_w.T.astype(F32)
    rw_hi = rw_t.astype(BF16)
    rw_lo = (rw_t - rw_hi.astype(F32)).astype(BF16)
    per_seq = lambda *shape: pl.BlockSpec((None,) + shape, lambda i: (i,) + (0,) * len(shape))
    return pl.pallas_call(
        functools.partial(_route_kernel, cap=cap, chunk=256, tile=TOKEN_TILE),
        out_shape=(
            jax.ShapeDtypeStruct((b, s, d), BF16),
            jax.ShapeDtypeStruct((b, n_e, s), F32),
            jax.ShapeDtypeStruct((b, s, LANES), BF16),
            jax.ShapeDtypeStruct((b, s, LANES), BF16),
            jax.ShapeDtypeStruct((b, n_e, LANES), jnp.int32),
        ),
        grid=(b,),
        in_specs=[per_seq(s, d), _full_spec((1, d)), _full_spec((2 * n_e, d)), _full_spec((LANES, LANES))],
        out_specs=(per_seq(s, d), per_seq(n_e, s), per_seq(s, LANES), per_seq(s, LANES), per_seq(n_e, LANES)),
        compiler_params=_cparams("parallel"),
        name="route",
    )(h3, g.reshape(1, d), jnp.concatenate([rw_hi, rw_lo], axis=0), tri)


def _chunk_tables(cnt, cap):
    n_e = cnt.shape[1]
    kmax = n_e * (TOKEN_TILE // CHUNK + 1)
    kmax = -(-kmax // CHUNKS_PER_GROUP) * CHUNKS_PER_GROUP
    c0 = jnp.cumsum(cnt, axis=2) - cnt
    a0 = c0 // CHUNK
    a1 = jnp.where(cnt > 0, (c0 + cnt + CHUNK - 1) // CHUNK, a0)
    nch = (a1 - a0).transpose(0, 2, 1)
    a0 = a0.transpose(0, 2, 1)
    pos_end = jnp.cumsum(nch, axis=2)
    pos = pos_end - nch
    total = pos_end[..., -1]
    k = jnp.arange(kmax, dtype=jnp.int32)
    e_of_k = jnp.sum((k[None, None, :, None] >= pos_end[:, :, None, :]).astype(jnp.int32), axis=-1)
    e_of_k = jnp.minimum(e_of_k, n_e - 1)
    is_e = e_of_k[..., None] == jnp.arange(n_e, dtype=jnp.int32)
    q = k + jnp.sum(jnp.where(is_e, (a0 - pos)[:, :, None, :], 0), axis=-1)
    valid = k[None, None, :] < total[..., None]
    e_tab = jnp.where(valid, e_of_k, 0)
    slot_tab = jnp.where(valid, q * CHUNK, -(1 << 20))
    dst_tab = jnp.where(valid, e_of_k * cap + q * CHUNK, 0)
    groups = (total + CHUNKS_PER_GROUP - 1) // CHUNKS_PER_GROUP
    return jnp.concatenate([e_tab, slot_tab, dst_tab, groups[..., None]], axis=-1).astype(jnp.int32), kmax


def _gather_kernel(tab_ref, xn_ref, slotr_ref, xs_ref, *, kmax):
    @pl.when(pl.program_id(1) == 0)
    def _():
        xs_ref[...] = jnp.zeros_like(xs_ref)

    sub = lax.broadcasted_iota(jnp.int32, (CHUNK, TOKEN_TILE), 0).astype(F32)

    def gathered_rows(tile, g):
        tokens = slice(tile * TOKEN_TILE, (tile + 1) * TOKEN_TILE)
        base = g * CHUNKS_PER_GROUP
        onehot = []
        for c in range(CHUNKS_PER_GROUP):
            slots = slotr_ref[pl.ds(tab_ref[tile, base + c], 1), tokens]
            want = sub + tab_ref[tile, kmax + base + c].astype(F32)
            onehot.append(jnp.where(slots == want, 1.0, 0.0).astype(BF16))
        return _dot(jnp.concatenate(onehot, axis=0), xn_ref[tokens, :]).astype(BF16)

    def accumulate(tile, g, rows):
        base = g * CHUNKS_PER_GROUP
        for c0 in range(0, CHUNKS_PER_GROUP, 4):
            cs = range(c0, c0 + 4)
            dsts = [pl.ds(pl.multiple_of(tab_ref[tile, 2 * kmax + base + c], CHUNK), CHUNK) for c in cs]
            sums = [xs_ref[dst, :] + rows[c * CHUNK:(c + 1) * CHUNK, :] for c, dst in zip(cs, dsts)]
            for dst, total in reversed(list(zip(dsts, sums))):
                xs_ref[dst, :] = total

    pending = None
    for tile in range(TILES_PER_STEP):
        rows = gathered_rows(tile, 0)
        if pending is not None:
            accumulate(tile - 1, 0, pending)
        pending = rows
    accumulate(TILES_PER_STEP - 1, 0, pending)
    for tile in range(TILES_PER_STEP):
        def more(g, carry, tile=tile):
            accumulate(tile, g, gathered_rows(tile, g))
            return carry
        lax.fori_loop(1, tab_ref[tile, 3 * kmax], more, 0)


def _expert_kernel(xs_ref, wg_ref, wu_ref, wd_ref, y_ref, wg_bf, wu_bf, wd_bf):
    @pl.when(pl.program_id(1) == 0)
    def _():
        wg_bf[...] = wg_ref[...].astype(BF16)
        wu_bf[...] = wu_ref[...].astype(BF16)
        wd_bf[...] = wd_ref[...].astype(BF16)

    nb, cap, d = xs_ref.shape
    xs = xs_ref[...].reshape(nb * cap, d)
    hg = _dot(xs, wg_bf[...])
    hu = _dot(xs, wu_bf[...])
    hdn = (hg * jax.nn.sigmoid(hg) * hu).astype(BF16)
    y_ref[...] = _dot(hdn, wd_bf[...]).astype(BF16).reshape(nb, cap, d)


def _scatter_ple_kernel(tab_ref, y_ref, slotc_ref, gatec_ref, h_ref, p_ref, g_ref, wg_ref, wp_ref, o_ref,
                        acc_ref, *, kmax):
    lane = lax.broadcasted_iota(jnp.int32, (1, GROUP), 1)
    sub = lax.broadcasted_iota(jnp.int32, (LANES, GROUP), 0)

    def group_operands(tile, g):
        tokens = slice(tile * TOKEN_TILE, (tile + 1) * TOKEN_TILE)
        base = g * CHUNKS_PER_GROUP
        expert = jnp.zeros((1, GROUP), jnp.int32)
        want = jnp.zeros((1, GROUP), jnp.int32)
        chunks = []
        for c in range(CHUNKS_PER_GROUP):
            in_chunk = (lane >= c * CHUNK) & (lane < (c + 1) * CHUNK)
            expert = jnp.where(in_chunk, tab_ref[tile, base + c], expert)
            want = jnp.where(in_chunk, tab_ref[tile, kmax + base + c] + lane - c * CHUNK, want)
            chunks.append(y_ref[pl.ds(pl.multiple_of(tab_ref[tile, 2 * kmax + base + c], CHUNK), CHUNK), :])
        pick = jnp.where(sub == expert, 1.0, 0.0).astype(BF16)
        both = _dot(jnp.concatenate([slotc_ref[tokens, :], gatec_ref[tokens, :]], axis=0), pick)
        weights = jnp.where(both[:TOKEN_TILE] == want.astype(F32), both[TOKEN_TILE:], 0.0).astype(BF16)
        return weights, jnp.concatenate(chunks, axis=0)

    def add_group(tile, operands, first=False):
        weights, rows = operands
        tokens = slice(tile * TOKEN_TILE, (tile + 1) * TOKEN_TILE)
        acc_ref[tokens, :] = (h_ref[tokens, :] if first else acc_ref[tokens, :]) + _dot(weights, rows)

    pending = None
    for tile in range(TILES_PER_STEP):
        operands = group_operands(tile, 0)
        if pending is not None:
            add_group(tile - 1, pending, first=True)
        pending = operands
    add_group(TILES_PER_STEP - 1, pending, first=True)
    for tile in range(TILES_PER_STEP):
        def more(g, carry, tile=tile):
            add_group(tile, group_operands(tile, g))
            return carry
        lax.fori_loop(1, tab_ref[tile, 3 * kmax], more, 0)
    h = acc_ref[...]
    xn = _rms(h, g_ref[...]).astype(BF16)
    gate = jax.nn.sigmoid(_dot(xn, wg_ref[...]))
    o_ref[...] = h + gate * _dot(p_ref[...].astype(BF16), wp_ref[...])


def _expert_ffn_ple(h3, xn3, slot_row, slot_col, gate_col, cnt, w_gate, w_up, w_down, cap, p3, layer, ple_g, ple_wg,
                    ple_wp, seqs_per_step=4):
    b, s, d = xn3.shape
    pd = p3.shape[2]
    _, n_e, _, ff = w_gate.shape
    step_rows = TILES_PER_STEP * TOKEN_TILE
    nt = s // step_rows
    tables, kmax = _chunk_tables(cnt[:, :, :s // TOKEN_TILE], cap)
    tables = tables.reshape(b * nt, TILES_PER_STEP, 3 * kmax + 1)
    tab_spec = pl.BlockSpec((None, TILES_PER_STEP, 3 * kmax + 1), lambda i, t: (i * nt + t, 0, 0),
                            memory_space=pltpu.SMEM)
    xs = pl.pallas_call(
        functools.partial(_gather_kernel, kmax=kmax),
        out_shape=jax.ShapeDtypeStruct((b, n_e * cap, d), BF16),
        grid=(b, nt),
        in_specs=[
            tab_spec,
            pl.BlockSpec((None, step_rows, d), lambda i, t: (i, t, 0)),
            pl.BlockSpec((None, n_e, step_rows), lambda i, t: (i, 0, t)),
        ],
        out_specs=pl.BlockSpec((None, n_e * cap, d), lambda i, t: (i, 0, 0)),
        compiler_params=_cparams("parallel", "arbitrary"),
        name="expert_gather",
    )(tables, xn3, slot_row)
    nb = seqs_per_step
    y = pl.pallas_call(
        _expert_kernel,
        out_shape=jax.ShapeDtypeStruct((b, n_e, cap, d), BF16),
        grid=(n_e, b // nb),
        in_specs=[
            pl.BlockSpec((nb, None, cap, d), lambda e, i: (i, e, 0, 0)),
            pl.BlockSpec((None, None, d, ff), lambda e, i: (layer, e, 0, 0)),
            pl.BlockSpec((None, None, d, ff), lambda e, i: (layer, e, 0, 0)),
            pl.BlockSpec((None, None, ff, d), lambda e, i: (layer, e, 0, 0)),
        ],
        out_specs=pl.BlockSpec((nb, None, cap, d), lambda e, i: (i, e, 0, 0)),
        scratch_shapes=[pltpu.VMEM((d, ff), BF16), pltpu.VMEM((d, ff), BF16), pltpu.VMEM((ff, d), BF16)],
        compiler_params=_cparams("parallel", "arbitrary"),
        name="expert_mlp",
    )(xs.reshape(b, n_e, cap, d), w_gate, w_up, w_down)
    tile_rows = lambda width: pl.BlockSpec((None, step_rows, width), lambda i, t: (i, t, 0))
    return pl.pallas_call(
        functools.partial(_scatter_ple_kernel, kmax=kmax),
        out_shape=jax.ShapeDtypeStruct((b, s, d), F32),
        grid=(b, nt),
        in_specs=[
            tab_spec,
            pl.BlockSpec((None, n_e * cap, d), lambda i, t: (i, 0, 0)),
            tile_rows(LANES), tile_rows(LANES), tile_rows(d),
            pl.BlockSpec((None, step_rows, pd), lambda i, t: (layer, i * nt + t, 0)),
            _full_spec((1, d)), _full_spec((d, d)), _full_spec((pd, d)),
        ],
        out_specs=tile_rows(d),
        scratch_shapes=[pltpu.VMEM((step_rows, d), F32)],
        compiler_params=_cparams("parallel", "arbitrary"),
        name="expert_scatter_ple",
    )(tables, y.reshape(b, n_e * cap, d), slot_col, gate_col, h3, p3, ple_g.reshape(1, d), ple_wg.astype(BF16),
      ple_wp.astype(BF16))


def kernel(x, p, norm_mix_g, norm_ffn_g, w_out, router_w, exp_w_gate, exp_w_up, exp_w_down, ple_norm_g, ple_gate_w, ple_proj_w, a_w_in, a_vnorm_g, a_w_s, a_b_s, b_w_in, b_qnorm_g, b_knorm_g, b_sink, c_w_in, c_qnorm_g, c_knorm_g, c_rpb):
    b, s, d = x.shape
    depth = norm_mix_g.shape[0]
    t = b * s
    cap = max(1, EC_CAPACITY_FACTOR * s // N_EXPERTS)
    scale = HEAD_DIM ** -0.5
    h = x.reshape(t, d)
    for i in range(depth):
        kind = i % N_MIXERS
        j = i // N_MIXERS
        if kind == 0:
            h = _mixer_a(h, norm_mix_g[i], a_w_in[j], a_vnorm_g[j], a_w_s[j], a_b_s[j], w_out[i])
        elif kind == 1:
            gain = jnp.concatenate([jnp.tile(b_qnorm_g[j] * scale, B_HEADS), jnp.tile(b_knorm_g[j], 2 * B_KV_HEADS)])
            qw, kw = B_HEADS * HEAD_DIM, B_KV_HEADS * HEAD_DIM
            w_b = jnp.concatenate([b_w_in[j][:, :qw], _dup_heads(b_w_in[j][:, qw:qw + kw], B_KV_HEADS),
                                   _dup_heads(b_w_in[j][:, qw + kw:], B_KV_HEADS)], axis=1)
            qkv = _norm_proj(h, norm_mix_g[i], w_b, gain, qw + 2 * kw)
            h = _attn_b(qkv.reshape(b, s, -1), b_sink[j], h.reshape(b, s, d), w_out[i]).reshape(t, d)
        else:
            gain = jnp.concatenate([jnp.tile(c_qnorm_g[j] * scale, C_HEADS), jnp.tile(c_knorm_g[j], C_HEADS)])
            qkv = _norm_proj(h, norm_mix_g[i], c_w_in[j], gain, 2 * C_HEADS * HEAD_DIM)
            h = _attn_c(qkv.reshape(b, s // GRID_W, GRID_W, -1), c_rpb[j], h.reshape(b, s // GRID_W, GRID_W, d),
                        w_out[i]).reshape(t, d)
        xn, slot_row, slot_col, gate_col, cnt = _route(h.reshape(b, s, d), norm_ffn_g[i], router_w[i], cap)
        h = _expert_ffn_ple(h.reshape(b, s, d), xn, slot_row, slot_col, gate_col, cnt, exp_w_gate, exp_w_up,
                            exp_w_down, cap, p.reshape(depth, t, -1), i, ple_norm_g[i], ple_gate_w[i],
                            ple_proj_w[i]).reshape(t, d)
    return h.reshape(b, s, d)
```

```python
import functools

import numpy as np
import jax
import jax.numpy as jnp
from jax import lax
from jax.experimental import pallas as pl
from jax.experimental.pallas import tpu as pltpu

F32 = jnp.float32
BF16 = jnp.bfloat16

RMS_EPS = 1e-6
LN_EPS = 1e-5
NEG = -1e30
LOG2E = 1.4426950408889634
HEAD_DIM = 64
GRID_W = 64
A_GROUPS = 8
A_CHUNK = 128
B_HEADS = 16
B_KV_HEADS = 4
B_BLOCK = 128
C_HEADS = 16
C_WIN_ROWS = 8
C_WIN_COLS = 16
N_EXPERTS = 16
EC_CAPACITY_FACTOR = 2
N_MIXERS = 3

LANES = 128
MXU_DIM = 256
VMEM_LIMIT_BYTES = 60 * 1024 * 1024

TOKEN_TILE = 256
TILES_PER_STEP = 4
A_SUBTILES = 2
PROJ_SUBTILES = 2
CHUNK = 16
GROUP = 4 * MXU_DIM
CHUNKS_PER_GROUP = GROUP // CHUNK


def _cparams(*sem):
    return pltpu.CompilerParams(dimension_semantics=sem, vmem_limit_bytes=VMEM_LIMIT_BYTES)


def _rms(xf, g):
    return xf * lax.rsqrt(jnp.mean(xf * xf, axis=-1, keepdims=True) + RMS_EPS) * g


def _dot(a, b):
    return jnp.dot(a, b, preferred_element_type=F32)


def _dot_nt(a, b):
    return lax.dot_general(a, b, (((1,), (1,)), ((), ())), preferred_element_type=F32)


def _full_spec(shape):
    nd = len(shape)
    return pl.BlockSpec(shape, lambda *_: (0,) * nd)


def _mixer_a_kernel(h_ref, g_ref, win_ref, vg_ref, ws_ref, bias_ref, wout_ref, o_ref, mix_ref):
    tm = h_ref.shape[0]
    width = vg_ref.shape[1]
    gw = width // A_GROUPS
    sub = tm // A_SUBTILES
    halves = [slice(i * sub, (i + 1) * sub) for i in range(A_SUBTILES)]
    zs = [_dot(_rms(h_ref[hs, :], g_ref[...]).astype(BF16), win_ref[...]) for hs in halves]
    us, vns = [], []
    for z in zs:
        z = 0.5 * z * (1.0 + jnp.tanh(np.sqrt(2.0 / np.pi).astype(np.float32) * (z + 0.044715 * (z * z * z))))
        v = z[:, width:]
        mu = jnp.mean(v, axis=-1, keepdims=True)
        vc = v - mu
        var = jnp.mean(vc * vc, axis=-1, keepdims=True)
        us.append(z[:, :width])
        vns.append((vc * lax.rsqrt(var + LN_EPS) * vg_ref[...]).astype(BF16))
    for hs, u, vn in zip(halves, us, vns):
        for c in range(sub // A_CHUNK):
            rows = slice(c * A_CHUNK, (c + 1) * A_CHUNK)
            out_rows = slice(hs.start + c * A_CHUNK, hs.start + (c + 1) * A_CHUNK)
            for g in range(A_GROUPS):
                cols = slice(g * gw, (g + 1) * gw)
                s = _dot(ws_ref[g], vn[rows, cols]) + bias_ref[:, cols]
                mix_ref[out_rows, cols] = (u[rows, cols] * s).astype(BF16)
    o_ref[...] = h_ref[...] + _dot(mix_ref[...], wout_ref[...])


def _mixer_a(h2, g, w_in, vnorm_g, w_s, b_s, w_out, tm=1024):
    t, d = h2.shape
    width = vnorm_g.shape[0]
    gw = width // A_GROUPS
    bias = jnp.repeat(b_s.T.astype(F32), gw, axis=1)
    return pl.pallas_call(
        _mixer_a_kernel,
        out_shape=jax.ShapeDtypeStruct((t, d), F32),
        grid=(t // tm,),
        in_specs=[
            pl.BlockSpec((tm, d), lambda i: (i, 0)),
            _full_spec((1, d)),
            _full_spec((d, 2 * width)),
            _full_spec((1, width)),
            _full_spec((A_GROUPS, A_CHUNK, A_CHUNK)),
            _full_spec((A_CHUNK, width)),
            _full_spec((width, d)),
        ],
        out_specs=pl.BlockSpec((tm, d), lambda i: (i, 0)),
        scratch_shapes=[pltpu.VMEM((tm, width), BF16)],
        compiler_params=_cparams("parallel"),
        name="mixer_a",
    )(h2, g.reshape(1, d), w_in.astype(BF16), vnorm_g.reshape(1, width), w_s.astype(BF16), bias,
      w_out.astype(BF16))


def _norm_proj_kernel(h_ref, g_ref, w_ref, hg_ref, bd_ref, o_ref, *, n_norm_cols):
    sub = h_ref.shape[0] // PROJ_SUBTILES
    parts = [slice(i * sub, (i + 1) * sub) for i in range(PROJ_SUBTILES)]
    accs = [_dot(_rms(h_ref[rows, :], g_ref[...]).astype(BF16), w_ref[...]) for rows in parts]
    for rows, acc in zip(parts, accs):
        for j in range(n_norm_cols // MXU_DIM):
            cols = slice(j * MXU_DIM, (j + 1) * MXU_DIM)
            blk = acc[:, cols]
            ss = _dot((blk * blk).astype(BF16), bd_ref[...])
            o_ref[rows, cols] = (blk * lax.rsqrt(ss * (1.0 / HEAD_DIM) + RMS_EPS) * hg_ref[:, cols]).astype(BF16)
        o_ref[rows, n_norm_cols:] = acc[:, n_norm_cols:].astype(BF16)


def _norm_proj(h2, g, w, head_gain, n_norm_cols, tm=1024):
    t, d = h2.shape
    n = w.shape[1]
    blockdiag = jnp.asarray(np.kron(np.eye(MXU_DIM // HEAD_DIM), np.ones((HEAD_DIM, HEAD_DIM))), BF16)
    return pl.pallas_call(
        functools.partial(_norm_proj_kernel, n_norm_cols=n_norm_cols),
        out_shape=jax.ShapeDtypeStruct((t, n), BF16),
        grid=(t // tm,),
        in_specs=[
            pl.BlockSpec((tm, d), lambda i: (i, 0)),
            _full_spec((1, d)),
            _full_spec((d, n)),
            _full_spec((1, n_norm_cols)),
            _full_spec((MXU_DIM, MXU_DIM)),
        ],
        out_specs=pl.BlockSpec((tm, n), lambda i: (i, 0)),
        compiler_params=_cparams("parallel"),
        name="norm_proj",
    )(h2, g.reshape(1, d), w.astype(BF16), head_gain.reshape(1, n_norm_cols).astype(F32), blockdiag)


def _alibi_slopes(n):
    return np.array([2.0 ** (-8.0 * (h + 1) / n) for h in range(n)], dtype=np.float32)


def _attn_b_tables():
    span = 3 * B_BLOCK
    rel = np.arange(span)[None, :] - B_BLOCK - np.arange(B_BLOCK)[:, None]
    in_window = np.abs(rel) <= B_BLOCK
    alibi = (-_alibi_slopes(B_HEADS)[:, None, None] * np.abs(rel)[None]).astype(np.float32)
    kblk = np.arange(span) // B_BLOCK
    tabs = []
    for kind in range(3):
        valid = in_window & ~((kind == 0) & (kblk == 0))[None, :] & ~((kind == 2) & (kblk == 2))[None, :]
        tabs.append(np.where(valid[None], alibi * np.float32(LOG2E), np.float32(NEG)))
    return np.stack(tabs).astype(np.float32)


def _attn_b_kernel(sink_ref, q_ref, kp_ref, kc_ref, kn_ref, vp_ref, vc_ref, vn_ref, tab_ref, h_ref, wout_ref, o_ref,
                   mix_ref):
    grp = B_HEADS // B_KV_HEADS
    nq = q_ref.shape[1]
    lane = lax.broadcasted_iota(jnp.int32, (1, LANES), 1)
    low = lane < HEAD_DIM
    span = kp_ref.shape[1] + kc_ref.shape[1] + kn_ref.shape[1]
    heads = [slice(h * nq, (h + 1) * nq) for h in range(B_HEADS)]
    sum_low = jnp.broadcast_to(jnp.where(low, 1.0, 0.0), (span, LANES)).astype(BF16)
    sum_high = jnp.broadcast_to(jnp.where(low, 0.0, 1.0), (span, LANES)).astype(BF16)

    def half_lanes(x):
        zero = jnp.zeros_like(x)
        return jnp.where(low, x, zero), jnp.where(low, zero, x)

    def all_scores(i):
        scores = []
        for kh in range(B_KV_HEADS):
            tile = slice(kh * LANES, (kh + 1) * LANES)
            kt = jnp.concatenate([kp_ref[i, :, tile], kc_ref[i, :, tile], kn_ref[i, :, tile]], axis=0)
            kk = jnp.concatenate(half_lanes(kt), axis=0)
            for local in range(grp // 2):
                qt = kh * grp // 2 + local
                s2 = _dot_nt(q_ref[i, :, qt * LANES:(qt + 1) * LANES], kk)
                scores.append(s2[:, :span] + tab_ref[kh, (2 * local) * nq:(2 * local + 1) * nq, :])
                scores.append(s2[:, span:] + tab_ref[kh, (2 * local + 1) * nq:(2 * local + 2) * nq, :])
        return jnp.concatenate(scores, axis=0)

    def softmax_numerators(sh):
        rowmax = jnp.broadcast_to(jnp.max(sh, axis=-1, keepdims=True), (sh.shape[0], LANES))
        m = jnp.concatenate([jnp.maximum(rowmax[hs], sink_ref[h]) for h, hs in enumerate(heads)], axis=0)
        sink_term = jnp.concatenate([jnp.exp2(sink_ref[h] - m[hs]) for h, hs in enumerate(heads)], axis=0)
        return jnp.exp2(sh - jnp.concatenate([m] * (span // LANES), axis=1)).astype(BF16), sink_term

    def outputs(i, pe, sink_term):
        seq_rows = slice(i * nq, (i + 1) * nq)
        for kh in range(B_KV_HEADS):
            tile = slice(kh * LANES, (kh + 1) * LANES)
            vt = jnp.concatenate([vp_ref[i, :, tile], vc_ref[i, :, tile], vn_ref[i, :, tile]], axis=0)
            v_low, v_high = half_lanes(vt)
            vv = jnp.concatenate([jnp.concatenate([v_low, sum_low], axis=1),
                                  jnp.concatenate([v_high, sum_high], axis=1)], axis=0)
            for local in range(grp // 2):
                qt = kh * grp // 2 + local
                even, odd = heads[2 * qt], heads[2 * qt + 1]
                od = _dot(jnp.concatenate([pe[even], pe[odd]], axis=1), vv)
                sink = jnp.where(low, sink_term[even], sink_term[odd])
                mix_ref[seq_rows, qt * LANES:(qt + 1) * LANES] = (od[:, :LANES] / (od[:, LANES:] + sink)).astype(BF16)

    n_seq = q_ref.shape[0]
    sh, pe = {}, {}
    for step in range(n_seq + 2):
        if step < n_seq:
            sh[step] = all_scores(step)
        if 1 <= step <= n_seq:
            pe[step - 1] = softmax_numerators(sh.pop(step - 1))
        if step >= 2:
            outputs(step - 2, *pe.pop(step - 2))
    proj = _dot(mix_ref[...], wout_ref[...])
    o_ref[...] = h_ref[...] + proj.reshape(n_seq, nq, proj.shape[1])


def _dup_heads(w, n_heads):
    d = w.shape[0]
    return jnp.repeat(w.reshape(d, n_heads, 1, HEAD_DIM), 2, axis=2).reshape(d, n_heads * LANES)


def _attn_b(qkv3, sink, h3, w_out, seqs_per_step=4):
    b, s, d = h3.shape
    nb = s // B_BLOCK
    ns = seqs_per_step
    grp = B_HEADS // B_KV_HEADS
    qw = B_HEADS * HEAD_DIM
    kvw = B_KV_HEADS * LANES
    kcol = qw // kvw
    vcol = kcol + 1
    tabs = jnp.asarray(_attn_b_tables().reshape(3, B_KV_HEADS, grp * B_BLOCK, 3 * B_BLOCK))
    prev = lambda n: jnp.maximum(n - 1, 0)
    nxt = lambda n: jnp.minimum(n + 1, nb - 1)
    kind = lambda n: jnp.where(n == 0, 0, jnp.where(n == nb - 1, 2, 1))
    kv = lambda col, f: pl.BlockSpec((ns, B_BLOCK, kvw), lambda bi, n: (bi, f(n), col))
    same = lambda n: n
    return pl.pallas_call(
        _attn_b_kernel,
        out_shape=jax.ShapeDtypeStruct((b, s, d), F32),
        grid=(b // ns, nb),
        in_specs=[
            pl.BlockSpec(memory_space=pltpu.SMEM),
            pl.BlockSpec((ns, B_BLOCK, qw), lambda bi, n: (bi, n, 0)),
            kv(kcol, prev), kv(kcol, same), kv(kcol, nxt),
            kv(vcol, prev), kv(vcol, same), kv(vcol, nxt),
            pl.BlockSpec((None, B_KV_HEADS, grp * B_BLOCK, 3 * B_BLOCK), lambda bi, n: (kind(n), 0, 0, 0)),
            pl.BlockSpec((ns, B_BLOCK, d), lambda bi, n: (bi, n, 0)),
            _full_spec((qw, d)),
        ],
        out_specs=pl.BlockSpec((ns, B_BLOCK, d), lambda bi, n: (bi, n, 0)),
        scratch_shapes=[pltpu.VMEM((ns * B_BLOCK, qw), BF16)],
        compiler_params=_cparams("parallel", "arbitrary"),
        name="attn_b",
    )(sink.astype(F32) * LOG2E, qkv3, qkv3, qkv3, qkv3, qkv3, qkv3, qkv3, tabs, h3, w_out.astype(BF16))


def _attn_c_table(rpb, rows):
    w = GRID_W
    kr_n = min(C_WIN_ROWS, rows)
    c = np.arange(w)
    cs = np.clip(c - C_WIN_COLS // 2, 0, w - C_WIN_COLS)
    kcol = np.arange(w)
    colmask = (kcol[None, :] >= cs[:, None]) & (kcol[None, :] < cs[:, None] + C_WIN_COLS)
    n_off = 2 * C_WIN_ROWS - kr_n
    n_h, n_ri, _ = rpb.shape
    pad = w - C_WIN_COLS
    padded = jnp.pad(rpb.astype(F32), ((0, 0), (0, 0), (pad, pad)))
    bias = jnp.stack([padded[:, :, w - 1 - qc:2 * w - 1 - qc] for qc in range(w)], axis=1)
    bias = jnp.where(jnp.asarray(colmask)[None, :, None, :], bias * LOG2E, NEG)
    bias = bias.reshape(n_h, w, n_ri * w)
    slabs = jnp.stack([bias[:, :, n * w:(n + kr_n) * w] for n in range(n_off)])
    return slabs.reshape(n_off, n_h // 2, 2 * w, kr_n * w)


def _attn_c_kernel(q_ref, k_ref, v_ref, tab_ref, h_ref, wout_ref, o_ref, mix_ref):
    nk = k_ref.shape[1] * k_ref.shape[2]
    nq = q_ref.shape[1]
    lane = lax.broadcasted_iota(jnp.int32, (1, LANES), 1)
    low = lane < HEAD_DIM
    ones = jnp.ones((nk, LANES), BF16)

    def all_scores(i):
        scores = []
        for t in range(C_HEADS // 2):
            cols = slice(t * LANES, (t + 1) * LANES)
            q2 = q_ref[i, :, cols]
            zero = jnp.zeros_like(q2)
            qq = jnp.concatenate([jnp.where(low, q2, zero), jnp.where(low, zero, q2)], axis=0)
            kt = k_ref[i, :, :, cols].reshape(nk, LANES)
            scores.append(_dot_nt(qq, kt) + tab_ref[t])
        return jnp.concatenate(scores, axis=0)

    def softmax_numerators(sh):
        m = jnp.broadcast_to(jnp.max(sh, axis=-1, keepdims=True), (sh.shape[0], LANES))
        return jnp.exp2(sh - jnp.concatenate([m] * (nk // LANES), axis=1)).astype(BF16)

    def outputs(i, pe):
        for t in range(C_HEADS // 2):
            cols = slice(t * LANES, (t + 1) * LANES)
            rows = slice(2 * t * nq, 2 * (t + 1) * nq)
            vt = v_ref[i, :, :, cols].reshape(nk, LANES)
            od = _dot(pe[rows], jnp.concatenate([vt, ones], axis=1))
            o = od[:, :LANES] / od[:, LANES:]
            mix_ref[i * nq:(i + 1) * nq, cols] = jnp.where(low, o[:nq], o[nq:]).astype(BF16)

    n_seq = q_ref.shape[0]
    sh, pe = {}, {}
    for step in range(n_seq + 2):
        if step < n_seq:
            sh[step] = all_scores(step)
        if 1 <= step <= n_seq:
            pe[step - 1] = softmax_numerators(sh.pop(step - 1))
        if step >= 2:
            outputs(step - 2, pe.pop(step - 2))
    proj = _dot(mix_ref[...], wout_ref[...])
    o_ref[...] = h_ref[...] + proj.reshape(n_seq, nq, proj.shape[1])


def _attn_c(qkv4, rpb, h4, w_out, seqs_per_step=4):
    b, rows, w, d = h4.shape
    hw = C_HEADS * HEAD_DIM
    kr_n = min(C_WIN_ROWS, rows)
    nb = seqs_per_step
    table = _attn_c_table(rpb, rows)
    row_start = lambda r: jnp.clip(r - kr_n // 2, 0, rows - kr_n)
    el = pl.Element
    return pl.pallas_call(
        _attn_c_kernel,
        out_shape=jax.ShapeDtypeStruct((b, rows, w, d), F32),
        grid=(rows, b // nb),
        in_specs=[
            pl.BlockSpec((nb, None, w, hw), lambda r, bi: (bi, r, 0, 0)),
            pl.BlockSpec((el(nb), el(kr_n), el(w), el(hw)), lambda r, bi: (bi * nb, row_start(r), 0, hw)),
            pl.BlockSpec((el(nb), el(kr_n), el(w), el(hw)), lambda r, bi: (bi * nb, row_start(r), 0, 2 * hw)),
            pl.BlockSpec((None, C_HEADS // 2, 2 * w, kr_n * w),
                         lambda r, bi: (row_start(r) - r + C_WIN_ROWS - 1, 0, 0, 0)),
            pl.BlockSpec((nb, None, w, d), lambda r, bi: (bi, r, 0, 0)),
            _full_spec((hw, d)),
        ],
        out_specs=pl.BlockSpec((nb, None, w, d), lambda r, bi: (bi, r, 0, 0)),
        scratch_shapes=[pltpu.VMEM((nb * w, hw), BF16)],
        compiler_params=_cparams("arbitrary", "arbitrary"),
        name="attn_c",
    )(qkv4, qkv4, qkv4, table, h4, w_out.astype(BF16))


def _lane_cumsum(x, tri):
    e, s = x.shape
    off = jnp.zeros((e, 1), F32)
    outs = []
    for j in range(s // LANES):
        blk = x[:, j * LANES:(j + 1) * LANES]
        outs.append(_dot(blk.astype(BF16), tri) + off)
        off = off + jnp.sum(blk, axis=1, keepdims=True)
    return jnp.concatenate(outs, axis=1)


def _route_kernel(h_ref, g_ref, rwt_ref, tri_ref, xn_ref, slotr_ref, slotc_ref, gatec_ref, cnt_ref,
                  *, cap, chunk, tile):
    s = h_ref.shape[0]
    n_e = rwt_ref.shape[0] // 2
    logits = []
    for c in range(s // chunk):
        rows = slice(c * chunk, (c + 1) * chunk)
        xn = _rms(h_ref[rows, :], g_ref[...])
        xn_hi = xn.astype(BF16)
        xn_ref[rows, :] = xn_hi
        xn_lo = (xn - xn_hi.astype(F32)).astype(BF16)
        both = _dot_nt(rwt_ref[...], xn_hi)
        logits.append(both[:n_e] + both[n_e:] + _dot_nt(rwt_ref[:n_e, :], xn_lo))
    lg = jnp.concatenate(logits, axis=1)
    ex = jnp.exp(lg - jnp.max(lg, axis=0, keepdims=True))
    aff = ex / jnp.sum(ex, axis=0, keepdims=True)
    bits = pltpu.bitcast(aff, jnp.int32)
    thr = jnp.zeros((n_e, 1), jnp.int32)
    for shift in range(27, -1, -3):
        digit = jnp.zeros((n_e, 1), jnp.int32)
        for j in range(1, 8):
            cnt = jnp.sum(jnp.where(bits >= (thr | (j << shift)), 1.0, 0.0), axis=1, keepdims=True)
            digit = digit + jnp.where(cnt >= cap, 1, 0)
        thr = thr | (digit * (1 << shift))
    gt = bits > thr
    eq = bits == thr
    need = cap - jnp.sum(jnp.where(gt, 1.0, 0.0), axis=1, keepdims=True)
    eqf = jnp.where(eq, 1.0, 0.0)
    eq_rank = _lane_cumsum(eqf, tri_ref[...]) - eqf
    sel = gt | (eq & (eq_rank < need))
    self_ = jnp.where(sel, 1.0, 0.0)
    slot = jnp.where(sel, _lane_cumsum(self_, tri_ref[...]) - self_, -1.0)
    lane = lax.broadcasted_iota(jnp.int32, (n_e, LANES), 1)
    counts = jnp.zeros((n_e, LANES), F32)
    for i in range(s // tile):
        counts = jnp.where(lane == i, jnp.sum(self_[:, i * tile:(i + 1) * tile], axis=1, keepdims=True), counts)
    cnt_ref[...] = counts.astype(jnp.int32)
    pad_rows = LANES - n_e
    slot_p = jnp.concatenate([slot, jnp.full((pad_rows, s), -1.0, F32)], axis=0)
    gate_p = jnp.concatenate([jnp.where(sel, aff, 0.0), jnp.zeros((pad_rows, s), F32)], axis=0)
    slotr_ref[...] = slot
    for j in range(s // LANES):
        cols = slice(j * LANES, (j + 1) * LANES)
        slotc_ref[cols, :] = slot_p[:, cols].T.astype(BF16)
        gatec_ref[cols, :] = gate_p[:, cols].T.astype(BF16)


def _route(h3, g, router_w, cap):
    b, s, d = h3.shape
    n_e = router_w.shape[1]
    tri = jnp.asarray(np.triu(np.ones((LANES, LANES))), BF16)
    rw_t = router_w.T.astype(F32)
    rw_hi = rw_t.astype(BF16)
    rw_lo = (rw_t - rw_hi.astype(F32)).astype(BF16)
    per_seq = lambda *shape: pl.BlockSpec((None,) + shape, lambda i: (i,) + (0,) * len(shape))
    return pl.pallas_call(
        functools.partial(_route_kernel, cap=cap, chunk=256, tile=TOKEN_TILE),
        out_shape=(
            jax.ShapeDtypeStruct((b, s, d), BF16),
            jax.ShapeDtypeStruct((b, n_e, s), F32),
            jax.ShapeDtypeStruct((b, s, LANES), BF16),
            jax.ShapeDtypeStruct((b, s, LANES), BF16),
            jax.ShapeDtypeStruct((b, n_e, LANES), jnp.int32),
        ),
        grid=(b,),
        in_specs=[per_seq(s, d), _full_spec((1, d)), _full_spec((2 * n_e, d)), _full_spec((LANES, LANES))],
        out_specs=(per_seq(s, d), per_seq(n_e, s), per_seq(s, LANES), per_seq(s, LANES), per_seq(n_e, LANES)),
        compiler_params=_cparams("parallel"),
        name="route",
    )(h3, g.reshape(1, d), jnp.concatenate([rw_hi, rw_lo], axis=0), tri)


def _chunk_tables(cnt, cap):
    n_e = cnt.shape[1]
    kmax = n_e * (TOKEN_TILE // CHUNK + 1)
    kmax = -(-kmax // CHUNKS_PER_GROUP) * CHUNKS_PER_GROUP
    c0 = jnp.cumsum(cnt, axis=2) - cnt
    a0 = c0 // CHUNK
    a1 = jnp.where(cnt > 0, (c0 + cnt + CHUNK - 1) // CHUNK, a0)
    nch = (a1 - a0).transpose(0, 2, 1)
    a0 = a0.transpose(0, 2, 1)
    pos_end = jnp.cumsum(nch, axis=2)
    pos = pos_end - nch
    total = pos_end[..., -1]
    k = jnp.arange(kmax, dtype=jnp.int32)
    e_of_k = jnp.sum((k[None, None, :, None] >= pos_end[:, :, None, :]).astype(jnp.int32), axis=-1)
    e_of_k = jnp.minimum(e_of_k, n_e - 1)
    is_e = e_of_k[..., None] == jnp.arange(n_e, dtype=jnp.int32)
    q = k + jnp.sum(jnp.where(is_e, (a0 - pos)[:, :, None, :], 0), axis=-1)
    valid = k[None, None, :] < total[..., None]
    e_tab = jnp.where(valid, e_of_k, 0)
    slot_tab = jnp.where(valid, q * CHUNK, -(1 << 20))
    dst_tab = jnp.where(valid, e_of_k * cap + q * CHUNK, 0)
    groups = (total + CHUNKS_PER_GROUP - 1) // CHUNKS_PER_GROUP
    return jnp.concatenate([e_tab, slot_tab, dst_tab, groups[..., None]], axis=-1).astype(jnp.int32), kmax


def _gather_kernel(tab_ref, xn_ref, slotr_ref, xs_ref, *, kmax):
    @pl.when(pl.program_id(1) == 0)
    def _():
        xs_ref[...] = jnp.zeros_like(xs_ref)

    sub = lax.broadcasted_iota(jnp.int32, (CHUNK, TOKEN_TILE), 0).astype(F32)

    def gathered_rows(tile, g):
        tokens = slice(tile * TOKEN_TILE, (tile + 1) * TOKEN_TILE)
        base = g * CHUNKS_PER_GROUP
        onehot = []
        for c in range(CHUNKS_PER_GROUP):
            slots = slotr_ref[pl.ds(tab_ref[tile, base + c], 1), tokens]
            want = sub + tab_ref[tile, kmax + base + c].astype(F32)
            onehot.append(jnp.where(slots == want, 1.0, 0.0).astype(BF16))
        return _dot(jnp.concatenate(onehot, axis=0), xn_ref[tokens, :]).astype(BF16)

    def accumulate(tile, g, rows):
        base = g * CHUNKS_PER_GROUP
        for c0 in range(0, CHUNKS_PER_GROUP, 4):
            cs = range(c0, c0 + 4)
            dsts = [pl.ds(pl.multiple_of(tab_ref[tile, 2 * kmax + base + c], CHUNK), CHUNK) for c in cs]
            sums = [xs_ref[dst, :] + rows[c * CHUNK:(c + 1) * CHUNK, :] for c, dst in zip(cs, dsts)]
            for dst, total in reversed(list(zip(dsts, sums))):
                xs_ref[dst, :] = total

    pending = None
    for tile in range(TILES_PER_STEP):
        rows = gathered_rows(tile, 0)
        if pending is not None:
            accumulate(tile - 1, 0, pending)
        pending = rows
    accumulate(TILES_PER_STEP - 1, 0, pending)
    for tile in range(TILES_PER_STEP):
        def more(g, carry, tile=tile):
            accumulate(tile, g, gathered_rows(tile, g))
            return carry
        lax.fori_loop(1, tab_ref[tile, 3 * kmax], more, 0)


def _expert_kernel(xs_ref, wg_ref, wu_ref, wd_ref, y_ref, wg_bf, wu_bf, wd_bf):
    @pl.when(pl.program_id(1) == 0)
    def _():
        wg_bf[...] = wg_ref[...].astype(BF16)
        wu_bf[...] = wu_ref[...].astype(BF16)
        wd_bf[...] = wd_ref[...].astype(BF16)

    nb, cap, d = xs_ref.shape
    xs = xs_ref[...].reshape(nb * cap, d)
    hg = _dot(xs, wg_bf[...])
    hu = _dot(xs, wu_bf[...])
    hdn = (hg * jax.nn.sigmoid(hg) * hu).astype(BF16)
    y_ref[...] = _dot(hdn, wd_bf[...]).astype(BF16).reshape(nb, cap, d)


def _scatter_ple_kernel(tab_ref, y_ref, slotc_ref, gatec_ref, h_ref, p_ref, g_ref, wg_ref, wp_ref, o_ref,
                        acc_ref, *, kmax):
    lane = lax.broadcasted_iota(jnp.int32, (1, GROUP), 1)
    sub = lax.broadcasted_iota(jnp.int32, (LANES, GROUP), 0)

    def group_operands(tile, g):
        tokens = slice(tile * TOKEN_TILE, (tile + 1) * TOKEN_TILE)
        base = g * CHUNKS_PER_GROUP
        expert = jnp.zeros((1, GROUP), jnp.int32)
        want = jnp.zeros((1, GROUP), jnp.int32)
        chunks = []
        for c in range(CHUNKS_PER_GROUP):
            in_chunk = (lane >= c * CHUNK) & (lane < (c + 1) * CHUNK)
            expert = jnp.where(in_chunk, tab_ref[tile, base + c], expert)
            want = jnp.where(in_chunk, tab_ref[tile, kmax + base + c] + lane - c * CHUNK, want)
            chunks.append(y_ref[pl.ds(pl.multiple_of(tab_ref[tile, 2 * kmax + base + c], CHUNK), CHUNK), :])
        pick = jnp.where(sub == expert, 1.0, 0.0).astype(BF16)
        both = _dot(jnp.concatenate([slotc_ref[tokens, :], gatec_ref[tokens, :]], axis=0), pick)
        weights = jnp.where(both[:TOKEN_TILE] == want.astype(F32), both[TOKEN_TILE:], 0.0).astype(BF16)
        return weights, jnp.concatenate(chunks, axis=0)

    def add_group(tile, operands, first=False):
        weights, rows = operands
        tokens = slice(tile * TOKEN_TILE, (tile + 1) * TOKEN_TILE)
        acc_ref[tokens, :] = (h_ref[tokens, :] if first else acc_ref[tokens, :]) + _dot(weights, rows)

    pending = None
    for tile in range(TILES_PER_STEP):
        operands = group_operands(tile, 0)
        if pending is not None:
            add_group(tile - 1, pending, first=True)
        pending = operands
    add_group(TILES_PER_STEP - 1, pending, first=True)
    for tile in range(TILES_PER_STEP):
        def more(g, carry, tile=tile):
            add_group(tile, group_operands(tile, g))
            return carry
        lax.fori_loop(1, tab_ref[tile, 3 * kmax], more, 0)
    h = acc_ref[...]
    xn = _rms(h, g_ref[...]).astype(BF16)
    gate = jax.nn.sigmoid(_dot(xn, wg_ref[...]))
    o_ref[...] = h + gate * _dot(p_ref[...].astype(BF16), wp_ref[...])


def _expert_ffn_ple(h3, xn3, slot_row, slot_col, gate_col, cnt, w_gate, w_up, w_down, cap, p3, layer, ple_g, ple_wg,
                    ple_wp, seqs_per_step=4):
    b, s, d = xn3.shape
    pd = p3.shape[2]
    _, n_e, _, ff = w_gate.shape
    step_rows = TILES_PER_STEP * TOKEN_TILE
    nt = s // step_rows
    tables, kmax = _chunk_tables(cnt[:, :, :s // TOKEN_TILE], cap)
    tables = tables.reshape(b * nt, TILES_PER_STEP, 3 * kmax + 1)
    tab_spec = pl.BlockSpec((None, TILES_PER_STEP, 3 * kmax + 1), lambda i, t: (i * nt + t, 0, 0),
                            memory_space=pltpu.SMEM)
    xs = pl.pallas_call(
        functools.partial(_gather_kernel, kmax=kmax),
        out_shape=jax.ShapeDtypeStruct((b, n_e * cap, d), BF16),
        grid=(b, nt),
        in_specs=[
            tab_spec,
            pl.BlockSpec((None, step_rows, d), lambda i, t: (i, t, 0)),
            pl.BlockSpec((None, n_e, step_rows), lambda i, t: (i, 0, t)),
        ],
        out_specs=pl.BlockSpec((None, n_e * cap, d), lambda i, t: (i, 0, 0)),
        compiler_params=_cparams("parallel", "arbitrary"),
        name="expert_gather",
    )(tables, xn3, slot_row)
    nb = seqs_per_step
    y = pl.pallas_call(
        _expert_kernel,
        out_shape=jax.ShapeDtypeStruct((b, n_e, cap, d), BF16),
        grid=(n_e, b // nb),
        in_specs=[
            pl.BlockSpec((nb, None, cap, d), lambda e, i: (i, e, 0, 0)),
            pl.BlockSpec((None, None, d, ff), lambda e, i: (layer, e, 0, 0)),
            pl.BlockSpec((None, None, d, ff), lambda e, i: (layer, e, 0, 0)),
            pl.BlockSpec((None, None, ff, d), lambda e, i: (layer, e, 0, 0)),
        ],
        out_specs=pl.BlockSpec((nb, None, cap, d), lambda e, i: (i, e, 0, 0)),
        scratch_shapes=[pltpu.VMEM((d, ff), BF16), pltpu.VMEM((d, ff), BF16), pltpu.VMEM((ff, d), BF16)],
        compiler_params=_cparams("parallel", "arbitrary"),
        name="expert_mlp",
    )(xs.reshape(b, n_e, cap, d), w_gate, w_up, w_down)
    tile_rows = lambda width: pl.BlockSpec((None, step_rows, width), lambda i, t: (i, t, 0))
    return pl.pallas_call(
        functools.partial(_scatter_ple_kernel, kmax=kmax),
        out_shape=jax.ShapeDtypeStruct((b, s, d), F32),
        grid=(b, nt),
        in_specs=[
            tab_spec,
            pl.BlockSpec((None, n_e * cap, d), lambda i, t: (i, 0, 0)),
            tile_rows(LANES), tile_rows(LANES), tile_rows(d),
            pl.BlockSpec((None, step_rows, pd), lambda i, t: (layer, i * nt + t, 0)),
            _full_spec((1, d)), _full_spec((d, d)), _full_spec((pd, d)),
        ],
        out_specs=tile_rows(d),
        scratch_shapes=[pltpu.VMEM((step_rows, d), F32)],
        compiler_params=_cparams("parallel", "arbitrary"),
        name="expert_scatter_ple",
    )(tables, y.reshape(b, n_e * cap, d), slot_col, gate_col, h3, p3, ple_g.reshape(1, d), ple_wg.astype(BF16),
      ple_wp.astype(BF16))


def kernel(x, p, norm_mix_g, norm_ffn_g, w_out, router_w, exp_w_gate, exp_w_up, exp_w_down, ple_norm_g, ple_gate_w, ple_proj_w, a_w_in, a_vnorm_g, a_w_s, a_b_s, b_w_in, b_qnorm_g, b_knorm_g, b_sink, c_w_in, c_qnorm_g, c_knorm_g, c_rpb):
    b, s, d = x.shape
    depth = norm_mix_g.shape[0]
    t = b * s
    cap = max(1, EC_CAPACITY_FACTOR * s // N_EXPERTS)
    scale = HEAD_DIM ** -0.5 * LOG2E
    h = x.reshape(t, d)
    for i in range(depth):
        kind = i % N_MIXERS
        j = i // N_MIXERS
        if kind == 0:
            h = _mixer_a(h, norm_mix_g[i], a_w_in[j], a_vnorm_g[j], a_w_s[j], a_b_s[j], w_out[i])
        elif kind == 1:
            gain = jnp.concatenate([jnp.tile(b_qnorm_g[j] * scale, B_HEADS), jnp.tile(b_knorm_g[j], 2 * B_KV_HEADS)])
            qw, kw = B_HEADS * HEAD_DIM, B_KV_HEADS * HEAD_DIM
            w_b = jnp.concatenate([b_w_in[j][:, :qw], _dup_heads(b_w_in[j][:, qw:qw + kw], B_KV_HEADS),
                                   _dup_heads(b_w_in[j][:, qw + kw:], B_KV_HEADS)], axis=1)
            qkv = _norm_proj(h, norm_mix_g[i], w_b, gain, qw + 2 * kw)
            h = _attn_b(qkv.reshape(b, s, -1), b_sink[j], h.reshape(b, s, d), w_out[i]).reshape(t, d)
        else:
            gain = jnp.concatenate([jnp.tile(c_qnorm_g[j] * scale, C_HEADS), jnp.tile(c_knorm_g[j], C_HEADS)])
            qkv = _norm_proj(h, norm_mix_g[i], c_w_in[j], gain, 2 * C_HEADS * HEAD_DIM)
            h = _attn_c(qkv.reshape(b, s // GRID_W, GRID_W, -1), c_rpb[j], h.reshape(b, s // GRID_W, GRID_W, d),
                        w_out[i]).reshape(t, d)
        xn, slot_row, slot_col, gate_col, cnt = _route(h.reshape(b, s, d), norm_ffn_g[i], router_w[i], cap)
        h = _expert_ffn_ple(h.reshape(b, s, d), xn, slot_row, slot_col, gate_col, cnt, exp_w_gate, exp_w_up,
                            exp_w_down, cap, p.reshape(depth, t, -1), i, ple_norm_g[i], ple_gate_w[i],
                            ple_proj_w[i]).reshape(t, d)
    return h.reshape(b, s, d)
```
